```python
import jax, jax.numpy as jnp
from jax import lax
import numpy as np

D_MODEL = 1024
BATCH = 8
SEQ = 2048
DEPTH = 1
DEC_BATCH = 128
DEC_SEQ = 8
PAST_LEN = 8192
PAGE_SIZE = 128

CONV_CH = D_MODEL // 2
CONV_WIDTH = 31
N_HEADS = 8
QK_NOPE = 64
QK_ROPE = 32
V_DIM = 64
Q_RANK = (3 * D_MODEL) // 8
KV_RANK = D_MODEL // 4
ROPE_THETA = 10000.0
MLA_SCALE = (QK_NOPE + QK_ROPE) ** -0.5
Q_BLOCK = 128
IN_COLS = 2 * CONV_CH + Q_RANK + KV_RANK + QK_ROPE
MIX_WIDTH = CONV_CH + N_HEADS * V_DIM
MEM_TOKENS = 256
MEM_HEADS = 4
MEM_HD = 128
MEM_SCALE = MEM_HD ** -0.5
N_EXPERTS = 32
TOP_K = 4
D_FF = D_MODEL
SWIGLU_LIMIT = 7.0
SWIGLU_ALPHA = 1.702
MOE_BLOCK = 128
EPS = 1e-6
NEG_INF = -1e30

kernel_name = 'hybrid_conv_mla_memory_moe_step'


def rmsnorm(x, g):
    xf = x.astype(jnp.float32)
    y = xf * lax.rsqrt(jnp.mean(xf * xf, axis=-1, keepdims=True) + EPS)
    return (y * g.astype(jnp.float32)).astype(x.dtype)


def layernorm(x, g, b):
    xf = x.astype(jnp.float32)
    mu = jnp.mean(xf, axis=-1, keepdims=True)
    var = jnp.mean(jnp.square(xf - mu), axis=-1, keepdims=True)
    y = (xf - mu) * lax.rsqrt(var + EPS) * g.astype(jnp.float32) + b.astype(jnp.float32)
    return y.astype(x.dtype)


def rope(x, pos):
    half = QK_ROPE // 2
    inv = ROPE_THETA ** (-jnp.arange(half, dtype=jnp.float32) / half)
    ang = pos.astype(jnp.float32)[:, None] * inv[None, :]
    shape = (1, ang.shape[0]) + (1,) * (x.ndim - 3) + (half,)
    cos = jnp.cos(ang).reshape(shape)
    sin = jnp.sin(ang).reshape(shape)
    xf = x.astype(jnp.float32)
    x1, x2 = xf[..., :half], xf[..., half:]
    return jnp.concatenate([x1 * cos - x2 * sin, x2 * cos + x1 * sin], axis=-1).astype(x.dtype)


def mla_attend(q_lat, q_pe, ckv, kpe, pos_q, pos_k):
    b, t, h, c = q_lat.shape
    qb = min(Q_BLOCK, t)
    n_blocks = t // qb

    def one_block(i):
        start = i * qb
        ql = lax.dynamic_slice_in_dim(q_lat, start, qb, axis=1)
        qp = lax.dynamic_slice_in_dim(q_pe, start, qb, axis=1)
        pq = lax.dynamic_slice_in_dim(pos_q, start, qb, axis=0)
        s = (jnp.einsum('bqhc,bkc->bhqk', ql, ckv).astype(jnp.float32)
             + jnp.einsum('bqhr,bkr->bhqk', qp, kpe).astype(jnp.float32)) * MLA_SCALE
        causal = pos_k[None, :] <= pq[:, None]
        s = jnp.where(causal[None, None], s, NEG_INF)
        p = jax.nn.softmax(s, axis=-1).astype(ckv.dtype)
        return jnp.einsum('bhqk,bkc->bqhc', p, ckv)

    o = lax.map(one_block, jnp.arange(n_blocks))
    return jnp.moveaxis(o, 0, 1).reshape(b, t, h, c)


def mixer_block(h, conv_prev, past_ckv, past_kpe, pos_q, pos_k,
                w_in, conv_w, conv_b, conv_ln_g, conv_ln_b,
                q_norm_g, w_q_up, kv_norm_g, w_kv_up, w_out):
    b, t, _ = h.shape
    proj = h @ w_in
    c0 = CONV_CH
    c1 = 2 * CONV_CH
    c2 = c1 + Q_RANK
    c3 = c2 + KV_RANK
    u_val, u_gate = proj[..., :c0], proj[..., c0:c1]
    q_in, ckv_in, kpe_in = proj[..., c1:c2], proj[..., c2:c3], proj[..., c3:]

    u = u_val * jax.nn.sigmoid(u_gate)
    ucat = jnp.concatenate([conv_prev, u], axis=1)
    conv = lax.conv_general_dilated(
        ucat, conv_w[:, None, :], window_strides=(1,), padding='VALID',
        dimension_numbers=('NWC', 'WIO', 'NWC'), feature_group_count=CONV_CH) + conv_b
    conv_out = jax.nn.silu(layernorm(conv, conv_ln_g, conv_ln_b))
    conv_tail = ucat[:, -(CONV_WIDTH - 1):]

    q = (rmsnorm(q_in, q_norm_g) @ w_q_up).reshape(b, t, N_HEADS, QK_NOPE + QK_ROPE)
    q_nope, q_pe = q[..., :QK_NOPE], rope(q[..., QK_NOPE:], pos_q)
    ckv_new = rmsnorm(ckv_in, kv_norm_g)
    kpe_new = rope(kpe_in, pos_q)
    ckv_all = jnp.concatenate([past_ckv, ckv_new], axis=1)
    kpe_all = jnp.concatenate([past_kpe, kpe_new], axis=1)
    w_uk = w_kv_up[:, :, :QK_NOPE]
    w_uv = w_kv_up[:, :, QK_NOPE:]
    q_lat = jnp.einsum('bthd,chd->bthc', q_nope, w_uk)
    o_lat = mla_attend(q_lat, q_pe, ckv_all, kpe_all, pos_q, pos_k)
    attn_out = jnp.einsum('bthc,chd->bthd', o_lat, w_uv).reshape(b, t, N_HEADS * V_DIM)

    out = jnp.concatenate([conv_out, attn_out], axis=-1) @ w_out
    return out, conv_tail, ckv_new, kpe_new


def memory_kv(mem, g_mem_kv, w_mk, w_mv):
    b, n, _ = mem.shape
    m = rmsnorm(mem, g_mem_kv)
    k = (m @ w_mk).reshape(b, n, MEM_HEADS, MEM_HD)
    v = (m @ w_mv).reshape(b, n, MEM_HEADS, MEM_HD)
    return k, v


def memory_attend(h, mem_k, mem_v, w_mq, w_mo):
    b, t, _ = h.shape
    q = (h @ w_mq).reshape(b, t, MEM_HEADS, MEM_HD)
    s = jnp.einsum('bthd,bmhd->bhtm', q, mem_k).astype(jnp.float32) * MEM_SCALE
    p = jax.nn.softmax(s, axis=-1).astype(mem_v.dtype)
    o = jnp.einsum('bhtm,bmhd->bthd', p, mem_v).reshape(b, t, MEM_HEADS * MEM_HD)
    return o @ w_mo


def moe_ffn(h, w_router, b_router, w_gate, b_gate, w_up, b_up, w_down, b_down):
    lead = h.shape[:-1]
    xt = h.reshape(-1, D_MODEL)
    n_tok = xt.shape[0]
    n_assign = n_tok * TOP_K
    logits = (xt @ w_router).astype(jnp.float32) + b_router.astype(jnp.float32)
    top_val, top_idx = lax.top_k(logits, TOP_K)
    gates = jax.nn.softmax(top_val, axis=-1)
    flat_e = top_idx.reshape(-1)
    flat_tok = jnp.repeat(jnp.arange(n_tok, dtype=jnp.int32), TOP_K)
    flat_g = gates.reshape(-1)
    order = jnp.argsort(flat_e)
    e_sorted = flat_e[order]
    counts = jnp.bincount(flat_e, length=N_EXPERTS)
    padded = ((counts + MOE_BLOCK - 1) // MOE_BLOCK) * MOE_BLOCK
    start = jnp.cumsum(counts) - counts
    pad_end = jnp.cumsum(padded)
    pad_start = pad_end - padded
    dest = pad_start[e_sorted] + (jnp.arange(n_assign) - start[e_sorted])
    n_blocks = (n_assign + N_EXPERTS * (MOE_BLOCK - 1) + MOE_BLOCK - 1) // MOE_BLOCK
    n_rows = n_blocks * MOE_BLOCK
    row_tok = jnp.zeros((n_rows,), jnp.int32).at[dest].set(flat_tok[order])
    row_gate = jnp.zeros((n_rows,), jnp.float32).at[dest].set(flat_g[order])
    block_expert = jnp.minimum(
        jnp.searchsorted(pad_end, jnp.arange(n_blocks) * MOE_BLOCK, side='right'), N_EXPERTS - 1)
    xb = xt[row_tok].reshape(n_blocks, MOE_BLOCK, D_MODEL)

    def run_block(args):
        xblk, e = args
        g = xblk @ w_gate[e] + b_gate[e]
        u = xblk @ w_up[e] + b_up[e]
        g = jnp.minimum(g, SWIGLU_LIMIT)
        u = jnp.clip(u, -SWIGLU_LIMIT, SWIGLU_LIMIT)
        a = (u + 1.0) * (g * jax.nn.sigmoid(SWIGLU_ALPHA * g))
        return a @ w_down[e] + b_down[e]

    yb = lax.map(run_block, (xb, block_expert)).reshape(n_rows, D_MODEL)
    y = jax.ops.segment_sum(yb * row_gate[:, None].astype(yb.dtype), row_tok, num_segments=n_tok)
    return y.reshape(lead + (D_MODEL,)).astype(h.dtype)


def layer(x, conv_prev, past_ckv, past_kpe, mem_k, mem_v, pos_q, pos_k,
          g_mix, w_in, conv_w, conv_b, conv_ln_g, conv_ln_b, q_norm_g, w_q_up, kv_norm_g, w_kv_up, w_out,
          g_mem_q, w_mq, w_mo, g_ffn, w_router, b_router, w_gate, b_gate, w_up, b_up, w_down, b_down):
    mix, conv_tail, ckv_new, kpe_new = mixer_block(
        rmsnorm(x, g_mix), conv_prev, past_ckv, past_kpe, pos_q, pos_k,
        w_in, conv_w, conv_b, conv_ln_g, conv_ln_b, q_norm_g, w_q_up, kv_norm_g, w_kv_up, w_out)
    x = x + mix
    x = x + memory_attend(rmsnorm(x, g_mem_q), mem_k, mem_v, w_mq, w_mo)
    x = x + moe_ffn(rmsnorm(x, g_ffn), w_router, b_router, w_gate, b_gate, w_up, b_up, w_down, b_down)
    return x, conv_tail, ckv_new, kpe_new


def setup_inputs(seed: int = 0) -> dict:
    key = jax.random.key(seed)
    keys = list(jax.random.split(key, 40))

    def nrm(shape, scale):
        return scale * jax.random.normal(keys.pop(), shape, jnp.float32)

    def gain(shape):
        return 1.0 + nrm(shape, 0.02)

    n_pages = PAST_LEN // PAGE_SIZE
    n_phys = (5 * DEC_BATCH * n_pages) // 4
    perm = jax.random.permutation(keys.pop(), n_phys)[:DEC_BATCH * n_pages]
    page_table = perm.reshape(DEC_BATCH, n_pages).astype(jnp.int32)
    L, D, E = DEPTH, D_MODEL, N_EXPERTS
    return {
        'x_prompt': nrm((BATCH, SEQ, D), 1.0),
        'x_sample': nrm((DEC_BATCH, DEC_SEQ, D), 1.0),
        'mem_prompt': nrm((BATCH, MEM_TOKENS, D), 1.0),
        'cache_ckv': nrm((L, n_phys, PAGE_SIZE, KV_RANK), 1.0),
        'cache_kpe': nrm((L, n_phys, PAGE_SIZE, QK_ROPE), 1.0),
        'page_table': page_table,
        'cache_mem_k': nrm((L, DEC_BATCH, MEM_TOKENS, MEM_HEADS, MEM_HD), 1.0),
        'cache_mem_v': nrm((L, DEC_BATCH, MEM_TOKENS, MEM_HEADS, MEM_HD), 1.0),
        'state_conv': nrm((L, DEC_BATCH, CONV_WIDTH - 1, CONV_CH), 0.5),
        'g_mix': gain((L, D)),
        'w_in': nrm((L, D, IN_COLS), D ** -0.5),
        'conv_w': nrm((L, CONV_WIDTH, CONV_CH), CONV_WIDTH ** -0.5),
        'conv_b': nrm((L, CONV_CH), 0.02),
        'conv_ln_g': gain((L, CONV_CH)),
        'conv_ln_b': nrm((L, CONV_CH), 0.02),
        'q_norm_g': gain((L, Q_RANK)),
        'w_q_up': nrm((L, Q_RANK, N_HEADS * (QK_NOPE + QK_ROPE)), Q_RANK ** -0.5),
        'kv_norm_g': gain((L, KV_RANK)),
        'w_kv_up': nrm((L, KV_RANK, N_HEADS, QK_NOPE + V_DIM), KV_RANK ** -0.5),
        'w_out': nrm((L, MIX_WIDTH, D), MIX_WIDTH ** -0.5),
        'g_mem_q': gain((L, D)),
        'g_mem_kv': gain((L, D)),
        'w_mq': nrm((L, D, MEM_HEADS * MEM_HD), D ** -0.5),
        'w_mk': nrm((L, D, MEM_HEADS * MEM_HD), D ** -0.5),
        'w_mv': nrm((L, D, MEM_HEADS * MEM_HD), D ** -0.5),
        'w_mo': nrm((L, MEM_HEADS * MEM_HD, D), (MEM_HEADS * MEM_HD) ** -0.5),
        'g_ffn': gain((L, D)),
        'w_router': nrm((L, D, E), D ** -0.5),
        'b_router': nrm((L, E), 0.01),
        'w_gate': nrm((L, E, D, D_FF), D ** -0.5),
        'b_gate': nrm((L, E, D_FF), 0.02),
        'w_up': nrm((L, E, D, D_FF), D ** -0.5),
        'b_up': nrm((L, E, D_FF), 0.02),
        'w_down': nrm((L, E, D_FF, D), D_FF ** -0.5),
        'b_down': nrm((L, E, D), 0.02),
        'g_final': gain((D,)),
    }


def reference(x_prompt, x_sample, mem_prompt, cache_ckv, cache_kpe, page_table, cache_mem_k, cache_mem_v,
              state_conv, g_mix, w_in, conv_w, conv_b, conv_ln_g, conv_ln_b, q_norm_g, w_q_up, kv_norm_g,
              w_kv_up, w_out, g_mem_q, g_mem_kv, w_mq, w_mk, w_mv, w_mo, g_ffn, w_router, b_router,
              w_gate, b_gate, w_up, b_up, w_down, b_down, g_final):
    b_p, s_p, _ = x_prompt.shape
    b_s, t_s, _ = x_sample.shape
    past = page_table.shape[1] * PAGE_SIZE
    pos_prompt = jnp.arange(s_p, dtype=jnp.int32)
    pos_sample_q = past + jnp.arange(t_s, dtype=jnp.int32)
    pos_sample_k = jnp.arange(past + t_s, dtype=jnp.int32)

    xp, xs = x_prompt, x_sample
    ckv_p, kpe_p, mk_p, mv_p, conv_p = [], [], [], [], []
    ckv_s, kpe_s, conv_s = [], [], []
    for l in range(DEPTH):
        wl = (g_mix[l], w_in[l], conv_w[l], conv_b[l], conv_ln_g[l], conv_ln_b[l], q_norm_g[l], w_q_up[l],
              kv_norm_g[l], w_kv_up[l], w_out[l], g_mem_q[l], w_mq[l], w_mo[l], g_ffn[l], w_router[l],
              b_router[l], w_gate[l], b_gate[l], w_up[l], b_up[l], w_down[l], b_down[l])
        mk, mv = memory_kv(mem_prompt, g_mem_kv[l], w_mk[l], w_mv[l])
        conv0 = jnp.zeros((b_p, CONV_WIDTH - 1, CONV_CH), xp.dtype)
        empty_ckv = jnp.zeros((b_p, 0, KV_RANK), xp.dtype)
        empty_kpe = jnp.zeros((b_p, 0, QK_ROPE), xp.dtype)
        xp, cp, ckp, kpp = layer(xp, conv0, empty_ckv, empty_kpe, mk, mv, pos_prompt, pos_prompt, *wl)
        past_ckv = cache_ckv[l][page_table].reshape(b_s, past, KV_RANK)
        past_kpe = cache_kpe[l][page_table].reshape(b_s, past, QK_ROPE)
        xs, cs, cks, kps = layer(xs, state_conv[l], past_ckv, past_kpe, cache_mem_k[l], cache_mem_v[l],
                                 pos_sample_q, pos_sample_k, *wl)
        ckv_p.append(ckp)
        kpe_p.append(kpp)
        mk_p.append(mk)
        mv_p.append(mv)
        conv_p.append(cp)
        ckv_s.append(cks)
        kpe_s.append(kps)
        conv_s.append(cs)

    y_prompt = rmsnorm(xp, g_final)
    y_sample = rmsnorm(xs, g_final)
    return (y_prompt, y_sample,
            jnp.stack(ckv_p), jnp.stack(kpe_p), jnp.stack(mk_p), jnp.stack(mv_p), jnp.stack(conv_p),
            jnp.stack(ckv_s), jnp.stack(kpe_s), jnp.stack(conv_s))
```

```python
import functools

import jax
import jax.numpy as jnp
from jax import lax
from jax.experimental import pallas as pl
from jax.experimental.pallas import tpu as pltpu

F32 = jnp.float32
BF16 = jnp.bfloat16

D_MODEL = 1024
PAGE_SIZE = 128
CONV_CH = 512
CONV_WIDTH = 31
N_HEADS = 8
QK_NOPE = 64
QK_ROPE = 32
V_DIM = 64
Q_RANK = 384
KV_RANK = 256
ROPE_THETA = 10000.0
MLA_SCALE = (QK_NOPE + QK_ROPE) ** -0.5
MEM_TOKENS = 256
MEM_HEADS = 4
MEM_HD = 128
MEM_SCALE = MEM_HD ** -0.5
N_EXPERTS = 32
TOP_K = 4
SWIGLU_LIMIT = 7.0
SWIGLU_ALPHA = 1.702
EPS = 1e-6
NEG_INF = -1e30

LANES = 128
HEAD_PAD = 128
C_VAL, C_GATE, C_Q, C_CKV, C_KPE, C_KPE_SW, C_END = 0, 512, 1024, 1408, 1664, 1792, 1920
TOKEN_TILE = 256
ATTN_TILE = 256
CONV_TILE = 256
CONV_HALO = 32
CONV_CHUNK = 32
PAGES_PER_STEP = 8
MOE_ROWS = 256
VMEM_LIMIT = 48 * 1024 * 1024


def _rms(x, g):
    return x * lax.rsqrt(jnp.mean(x * x, axis=-1, keepdims=True) + EPS) * g


def _dot(a, b):
    return jnp.dot(a, b, preferred_element_type=F32)


def _dot_t(a, b):
    return lax.dot_general(a, b, (((1,), (1,)), ((), ())), preferred_element_type=F32)


def _params(*sem, vmem=None):
    return pltpu.CompilerParams(dimension_semantics=sem, vmem_limit_bytes=vmem)


def _const_spec(shape):
    nd = len(shape)
    return pl.BlockSpec(shape, lambda *_: (0,) * nd)


def _inproj_common(x_ref, gmix_ref, win_ref, qg_ref, wq_ref, wqsw_ref, kvg_ref, tab_ref,
                   u_ref, ckv_ref, kpe_ref):
    h = _rms(x_ref[...], gmix_ref[...]).astype(BF16)
    proj = _dot(h, win_ref[...])
    u_ref[...] = proj[:, C_VAL:C_GATE] * jax.nn.sigmoid(proj[:, C_GATE:C_Q])
    qn = _rms(proj[:, C_Q:C_CKV], qg_ref[...]).astype(BF16)
    ckv = _rms(proj[:, C_CKV:C_KPE], kvg_ref[...])
    ckv_ref[...] = ckv
    cq, sq, ck, sk = tab_ref[0], tab_ref[1], tab_ref[2], tab_ref[3]
    kpe_rot = proj[:, C_KPE:C_KPE_SW] * ck + proj[:, C_KPE_SW:C_END] * sk
    kpe_ref[...] = kpe_rot[:, :QK_ROPE]
    q = _dot(qn, wq_ref[...])
    qs = _dot(qn, wqsw_ref[...])
    q_heads = []
    for hd in range(N_HEADS):
        sl = slice(hd * HEAD_PAD, (hd + 1) * HEAD_PAD)
        q_heads.append(q[:, sl] * cq + qs[:, sl] * sq)
    return ckv, kpe_rot, q_heads


def _inproj_prompt_kernel(x_ref, gmix_ref, win_ref, qg_ref, wq_ref, wqsw_ref, kvg_ref, tab_ref,
                          wuk_ref, wuv_ref,
                          u_ref, ckv_ref, kpe_ref, q_ref, k_ref, v_ref):
    ckv, kpe_rot, q_heads = _inproj_common(x_ref, gmix_ref, win_ref, qg_ref, wq_ref, wqsw_ref,
                                           kvg_ref, tab_ref, u_ref, ckv_ref, kpe_ref)
    ckv_b = ckv.astype(BF16)
    k_nope = _dot(ckv_b, wuk_ref[...])
    for hd in range(N_HEADS):
        sl = slice(hd * HEAD_PAD, (hd + 1) * HEAD_PAD)
        q_ref[:, sl] = q_heads[hd].astype(BF16)
        k_ref[:, sl] = (k_nope[:, sl] + kpe_rot).astype(BF16)
    v_ref[...] = _dot(ckv_b, wuv_ref[...]).astype(BF16)


def _inproj_sample_kernel(x_ref, gmix_ref, win_ref, qg_ref, wq_ref, wqsw_ref, kvg_ref, tab_ref,
                          wukt_ref,
                          u_ref, ckv_ref, kpe_ref, qlat_ref, qpe_ref):
    _, _, q_heads = _inproj_common(x_ref, gmix_ref, win_ref, qg_ref, wq_ref, wqsw_ref,
                                   kvg_ref, tab_ref, u_ref, ckv_ref, kpe_ref)
    for hd in range(N_HEADS):
        qlat_ref[hd] = _dot(q_heads[hd].astype(BF16), wukt_ref[hd])
        qpe_ref[hd] = q_heads[hd][:, :QK_ROPE]


def _inproj(x2d, tab, w, sample):
    t = x2d.shape[0]
    tm = TOKEN_TILE
    n_tab = tab.shape[1] // tm
    row = lambda n: pl.BlockSpec((tm, n), lambda i: (i, 0))
    in_specs = [row(D_MODEL), _const_spec((1, D_MODEL)), _const_spec((D_MODEL, C_END)),
                _const_spec((1, Q_RANK)), _const_spec((Q_RANK, N_HEADS * HEAD_PAD)),
                _const_spec((Q_RANK, N_HEADS * HEAD_PAD)), _const_spec((1, KV_RANK)),
                pl.BlockSpec((4, tm, LANES), lambda i: (0, i % n_tab, 0))]
    args = [x2d, w['g_mix'], w['w_in_ext'], w['q_norm_g'], w['wq'], w['wq_sw'], w['kv_norm_g'], tab]
    out_shape = [jax.ShapeDtypeStruct((t, CONV_CH), F32), jax.ShapeDtypeStruct((t, KV_RANK), F32),
                 jax.ShapeDtypeStruct((t, QK_ROPE), F32)]
    out_specs = [row(CONV_CH), row(KV_RANK), row(QK_ROPE)]
    if sample:
        body = _inproj_sample_kernel
        in_specs += [_const_spec((N_HEADS, HEAD_PAD, KV_RANK))]
        args += [w['wukt_pad']]
        out_shape += [jax.ShapeDtypeStruct((N_HEADS, t, KV_RANK), F32),
                      jax.ShapeDtypeStruct((N_HEADS, t, QK_ROPE), F32)]
        out_specs += [pl.BlockSpec((N_HEADS, tm, KV_RANK), lambda i: (0, i, 0)),
                      pl.BlockSpec((N_HEADS, tm, QK_ROPE), lambda i: (0, i, 0))]
    else:
        body = _inproj_prompt_kernel
        in_specs += [_const_spec((KV_RANK, N_HEADS * HEAD_PAD)), _const_spec((KV_RANK, N_HEADS * V_DIM))]
        args += [w['wuk_pad'], w['wuv']]
        out_shape += [jax.ShapeDtypeStruct((t, N_HEADS * HEAD_PAD), BF16),
                      jax.ShapeDtypeStruct((t, N_HEADS * HEAD_PAD), BF16),
                      jax.ShapeDtypeStruct((t, N_HEADS * V_DIM), BF16)]
        out_specs += [row(N_HEADS * HEAD_PAD), row(N_HEADS * HEAD_PAD), row(N_HEADS * V_DIM)]
    return pl.pallas_call(
        body, grid=(t // tm,), in_specs=in_specs, out_specs=out_specs, out_shape=out_shape,
        compiler_params=_params("parallel", vmem=VMEM_LIMIT),
        name="inproj_sample" if sample else "inproj_prompt")(*args)


def _ln_swish(conv, g, b):
    mu = jnp.mean(conv, axis=-1, keepdims=True)
    xc = conv - mu
    var = jnp.mean(xc * xc, axis=-1, keepdims=True)
    y = xc * lax.rsqrt(var + EPS) * g + b
    return y * jax.nn.sigmoid(y)


def _conv_prompt_kernel(main_ref, halo_ref, w_ref, b_ref, g_ref, lb_ref, o_ref, win_ref):
    tt = main_ref.shape[1]
    win_ref[0:tt, :] = main_ref[0]
    win_ref[tt:tt + CONV_HALO, :] = halo_ref[0]
    for c in range(tt // CONV_CHUNK):
        base = c * CONV_CHUNK + 2
        acc = win_ref[base:base + CONV_CHUNK, :] * w_ref[0:1, :]
        for j in range(1, CONV_WIDTH):
            acc = acc + win_ref[base + j:base + j + CONV_CHUNK, :] * w_ref[j:j + 1, :]
        y = _ln_swish(acc + b_ref[...], g_ref[...], lb_ref[...])
        o_ref[0, c * CONV_CHUNK:(c + 1) * CONV_CHUNK, :] = y


def _conv_prompt(upad, w):
    b, s_pad, _ = upad.shape
    s = s_pad - CONV_HALO
    tt = CONV_TILE
    ratio = tt // CONV_HALO
    return pl.pallas_call(
        _conv_prompt_kernel, grid=(b, s // tt),
        in_specs=[pl.BlockSpec((1, tt, CONV_CH), lambda bi, i: (bi, i, 0)),
                  pl.BlockSpec((1, CONV_HALO, CONV_CH), lambda bi, i: (bi, (i + 1) * ratio, 0)),
                  _const_spec((CONV_HALO, CONV_CH)), _const_spec((1, CONV_CH)),
                  _const_spec((1, CONV_CH)), _const_spec((1, CONV_CH))],
        out_specs=pl.BlockSpec((1, tt, CONV_CH), lambda bi, i: (bi, i, 0)),
        out_shape=jax.ShapeDtypeStruct((b, s, CONV_CH), F32),
        scratch_shapes=[pltpu.VMEM((tt + CONV_HALO, CONV_CH), F32)],
        compiler_params=_params("parallel", "parallel"),
        name="conv_prompt")(upad, upad, w['conv_w'], w['conv_b'], w['conv_ln_g'], w['conv_ln_b'])


def _conv_sample_kernel(win_ref, w_ref, b_ref, g_ref, lb_ref, o_ref):
    t = o_ref.shape[1]
    acc = win_ref[:, 2:2 + t, :] * w_ref[0:1, :]
    for j in range(1, CONV_WIDTH):
        acc = acc + win_ref[:, 2 + j:2 + j + t, :] * w_ref[j:j + 1, :]
    y = _ln_swish(acc + b_ref[...], g_ref[...], lb_ref[...])
    o_ref[...] = y


def _conv_sample(upad, w):
    b, s_pad, _ = upad.shape
    t = s_pad - CONV_HALO
    bb = 8
    return pl.pallas_call(
        _conv_sample_kernel, grid=(b // bb,),
        in_specs=[pl.BlockSpec((bb, s_pad, CONV_CH), lambda i: (i, 0, 0)),
                  _const_spec((CONV_HALO, CONV_CH)), _const_spec((1, CONV_CH)),
                  _const_spec((1, CONV_CH)), _const_spec((1, CONV_CH))],
        out_specs=pl.BlockSpec((bb, t, CONV_CH), lambda i: (i, 0, 0)),
        out_shape=jax.ShapeDtypeStruct((b, t, CONV_CH), F32),
        compiler_params=_params("parallel"),
        name="conv_sample")(upad, w['conv_w'], w['conv_b'], w['conv_ln_g'], w['conv_ln_b'])


def _softmax_step(s, m, l):
    m_new = jnp.maximum(m, jnp.max(s, axis=-1, keepdims=True))
    alpha = jnp.exp(m - m_new)
    p = jnp.exp(s - m_new)
    return m_new, alpha, p, alpha * l + jnp.sum(p, axis=-1, keepdims=True)


def _attn_prompt_kernel(q_ref, k_ref, v_ref, o_ref):
    tq = q_ref.shape[1]
    qi = pl.program_id(2)
    outs = []
    for hd in range(2):
        sl = slice(hd * HEAD_PAD, (hd + 1) * HEAD_PAD)
        qh = q_ref[0, :, sl]

        def tile(j, carry, masked):
            m, l, acc = carry
            start = pl.multiple_of(j * tq, tq)
            s = _dot_t(qh, k_ref[0, pl.ds(start, tq), sl])
            if masked:
                rows = lax.broadcasted_iota(jnp.int32, s.shape, 0)
                cols = lax.broadcasted_iota(jnp.int32, s.shape, 1)
                s = jnp.where(cols <= rows, s, NEG_INF)
            m, alpha, p, l = _softmax_step(s, m, l)
            acc = alpha * acc + _dot(p.astype(BF16), v_ref[0, pl.ds(start, tq), :])
            return m, l, acc

        init = (jnp.full((tq, 1), NEG_INF, F32), jnp.zeros((tq, 1), F32), jnp.zeros((tq, LANES), F32))
        carry = lax.fori_loop(0, qi, functools.partial(tile, masked=False), init)
        m, l, acc = tile(qi, carry, True)
        outs.append(acc / l)
    lane = lax.broadcasted_iota(jnp.int32, outs[0].shape, 1)
    o_ref[0] = jnp.where(lane < V_DIM, outs[0], outs[1]).astype(BF16)


def _attn_prompt(q, k, v):
    b, s, _ = q.shape
    tq = ATTN_TILE
    return pl.pallas_call(
        _attn_prompt_kernel, grid=(b, N_HEADS // 2, s // tq),
        in_specs=[pl.BlockSpec((1, tq, 2 * HEAD_PAD), lambda bi, hp, qi: (bi, qi, hp)),
                  pl.BlockSpec((1, s, 2 * HEAD_PAD), lambda bi, hp, qi: (bi, 0, hp)),
                  pl.BlockSpec((1, s, 2 * V_DIM), lambda bi, hp, qi: (bi, 0, hp))],
        out_specs=pl.BlockSpec((1, tq, 2 * V_DIM), lambda bi, hp, qi: (bi, qi, hp)),
        out_shape=jax.ShapeDtypeStruct((b, s, N_HEADS * V_DIM), BF16),
        compiler_params=_params("parallel", "parallel", "arbitrary"),
        name="attn_prompt")(q, k, v)


def _attn_sample_kernel(pt_ref, ql_ref, qp_ref, cn_ref, kn_ref, *rest):
    n = PAGES_PER_STEP
    ckv_refs, kpe_refs = rest[:n], rest[n:2 * n]
    o_ref, m_ref, l_ref, acc_ref = rest[2 * n:]
    c = pl.program_id(1)
    t_new = cn_ref.shape[1]
    rows_q = N_HEADS * t_new

    @pl.when(c == 0)
    def _():
        m_ref[...] = jnp.full(m_ref.shape, NEG_INF, F32)
        l_ref[...] = jnp.zeros(l_ref.shape, F32)
        acc_ref[...] = jnp.zeros(acc_ref.shape, F32)

    ql = ql_ref[:, 0].reshape(rows_q, KV_RANK).astype(BF16)
    qp = qp_ref[:, 0].reshape(rows_q, QK_ROPE).astype(BF16)

    def update(s_list, v_list):
        s = jnp.concatenate(s_list, axis=1) if len(s_list) > 1 else s_list[0]
        m, alpha, p, l = _softmax_step(s, m_ref[...], l_ref[...])
        pv = None
        for i, vi in enumerate(v_list):
            d = _dot(p[:, i * PAGE_SIZE:(i + 1) * PAGE_SIZE].astype(BF16), vi)
            pv = d if pv is None else pv + d
        m_ref[...] = m
        l_ref[...] = l
        acc_ref[...] = alpha * acc_ref[...] + pv

    s_list, v_list = [], []
    for i in range(n):
        kc = ckv_refs[i][0].astype(BF16)
        kp = kpe_refs[i][0].astype(BF16)
        s_list.append(_dot_t(ql, kc) + _dot_t(qp, kp))
        v_list.append(kc)
    update(s_list, v_list)

    @pl.when(c == pl.num_programs(1) - 1)
    def _():
        pad = PAGE_SIZE - t_new
        kc = jnp.concatenate([cn_ref[0], jnp.zeros((pad, KV_RANK), F32)], axis=0).astype(BF16)
        kp = jnp.concatenate([kn_ref[0], jnp.zeros((pad, QK_ROPE), F32)], axis=0).astype(BF16)
        s = _dot_t(ql, kc) + _dot_t(qp, kp)
        t_q = lax.broadcasted_iota(jnp.int32, s.shape, 0) % t_new
        cols = lax.broadcasted_iota(jnp.int32, s.shape, 1)
        s = jnp.where(cols <= t_q, s, NEG_INF)
        update([s], [kc])
        o = acc_ref[...] / l_ref[...]
        o_ref[:, 0] = o.reshape(N_HEADS, t_new, KV_RANK)


def _attn_sample(page_table, qlat, qpe, ckv_new, kpe_new, cache_ckv, cache_kpe):
    bs, n_pages = page_table.shape
    t_new = ckv_new.shape[1]
    n = PAGES_PER_STEP
    qlat4 = qlat.reshape(N_HEADS, bs, t_new, KV_RANK)
    qpe4 = qpe.reshape(N_HEADS, bs, t_new, QK_ROPE)

    def page_spec(width, i):
        return pl.BlockSpec((1, PAGE_SIZE, width),
                            lambda b, c, pt: (pt[b * n_pages + c * n + i], 0, 0))

    in_specs = [pl.BlockSpec((N_HEADS, 1, t_new, KV_RANK), lambda b, c, pt: (0, b, 0, 0)),
                pl.BlockSpec((N_HEADS, 1, t_new, QK_ROPE), lambda b, c, pt: (0, b, 0, 0)),
                pl.BlockSpec((1, t_new, KV_RANK), lambda b, c, pt: (b, 0, 0)),
                pl.BlockSpec((1, t_new, QK_ROPE), lambda b, c, pt: (b, 0, 0))]
    in_specs += [page_spec(KV_RANK, i) for i in range(n)]
    in_specs += [page_spec(QK_ROPE, i) for i in range(n)]
    rows_q = N_HEADS * t_new
    grid_spec = pltpu.PrefetchScalarGridSpec(
        num_scalar_prefetch=1, grid=(bs, n_pages // n), in_specs=in_specs,
        out_specs=pl.BlockSpec((N_HEADS, 1, t_new, KV_RANK), lambda b, c, pt: (0, b, 0, 0)),
        scratch_shapes=[pltpu.VMEM((rows_q, 1), F32), pltpu.VMEM((rows_q, 1), F32),
                        pltpu.VMEM((rows_q, KV_RANK), F32)])
    o = pl.pallas_call(
        _attn_sample_kernel, grid_spec=grid_spec,
        out_shape=jax.ShapeDtypeStruct((N_HEADS, bs, t_new, KV_RANK), F32),
        compiler_params=_params("parallel", "arbitrary"),
        name="attn_sample")(page_table.reshape(-1), qlat4, qpe4, ckv_new, kpe_new,
                            *([cache_ckv] * n), *([cache_kpe] * n))
    return o.reshape(N_HEADS, bs * t_new, KV_RANK)


def _post_a_kernel(sample, x_ref, conv_ref, attn_ref, *rest):
    if sample:
        wuv_ref, wout_ref, g_ref, wmq_ref, x1_ref, qm_ref = rest
        attn = None
        for hd in range(N_HEADS):
            d = _dot(attn_ref[hd].astype(BF16), wuv_ref[hd])
            attn = d if attn is None else attn + d
        attn = attn.astype(BF16)
    else:
        wout_ref, g_ref, wmq_ref, x1_ref, qm_ref = rest
        attn = attn_ref[...]
    mix = _dot(conv_ref[...].astype(BF16), wout_ref[0:CONV_CH, :]) + _dot(attn, wout_ref[CONV_CH:, :])
    x1 = x_ref[...] + mix
    x1_ref[...] = x1
    qm_ref[...] = _dot(_rms(x1, g_ref[...]).astype(BF16), wmq_ref[...])


def _post_a(x2d, conv2d, attn, w, sample):
    t = x2d.shape[0]
    tm = TOKEN_TILE
    row = lambda n: pl.BlockSpec((tm, n), lambda i: (i, 0))
    mem_w = MEM_HEADS * MEM_HD
    if sample:
        in_specs = [row(D_MODEL), row(CONV_CH), pl.BlockSpec((N_HEADS, tm, KV_RANK), lambda i: (0, i, 0)),
                    _const_spec((N_HEADS, KV_RANK, N_HEADS * V_DIM))]
        args = [x2d, conv2d, attn, w['wuv_pad']]
    else:
        in_specs = [row(D_MODEL), row(CONV_CH), row(N_HEADS * V_DIM)]
        args = [x2d, conv2d, attn]
    in_specs += [_const_spec((D_MODEL, D_MODEL)), _const_spec((1, D_MODEL)), _const_spec((D_MODEL, mem_w))]
    args += [w['w_out'], w['g_mem_q'], w['w_mq']]
    return pl.pallas_call(
        functools.partial(_post_a_kernel, sample), grid=(t // tm,), in_specs=in_specs,
        out_specs=[row(D_MODEL), row(mem_w)],
        out_shape=[jax.ShapeDtypeStruct((t, D_MODEL), F32), jax.ShapeDtypeStruct((t, mem_w), F32)],
        compiler_params=_params("parallel", vmem=VMEM_LIMIT),
        name="post_a_sample" if sample else "post_a_prompt")(*args)


def _mem_kv_kernel(m_ref, g_ref, wk_ref, wv_ref, k_ref, v_ref):
    m = _rms(m_ref[...], g_ref[...]).astype(BF16)
    k_ref[...] = _dot(m, wk_ref[...])
    v_ref[...] = _dot(m, wv_ref[...])


def _mem_kv(mem2d, w):
    t = mem2d.shape[0]
    tm = TOKEN_TILE
    mem_w = MEM_HEADS * MEM_HD
    row = lambda n: pl.BlockSpec((tm, n), lambda i: (i, 0))
    return pl.pallas_call(
        _mem_kv_kernel, grid=(t // tm,),
        in_specs=[row(D_MODEL), _const_spec((1, D_MODEL)), _const_spec((D_MODEL, mem_w)),
                  _const_spec((D_MODEL, mem_w))],
        out_specs=[row(mem_w), row(mem_w)],
        out_shape=[jax.ShapeDtypeStruct((t, mem_w), F32)] * 2,
        compiler_params=_params("parallel"),
        name="mem_kv")(mem2d, w['g_mem_kv'], w['w_mk'], w['w_mv'])


def _mem_attn_kernel(q_ref, k_ref, v_ref, o_ref):
    for hd in range(MEM_HEADS):
        sl = slice(hd * MEM_HD, (hd + 1) * MEM_HD)
        s = _dot_t(q_ref[0, :, sl].astype(BF16), k_ref[0, :, sl].astype(BF16)) * MEM_SCALE
        p = jnp.exp(s - jnp.max(s, axis=-1, keepdims=True))
        p = p / jnp.sum(p, axis=-1, keepdims=True)
        o_ref[0, :, sl] = _dot(p.astype(BF16), v_ref[0, :, sl].astype(BF16))


def _mem_attn(qm, mem_k, mem_v):
    b, s, mem_w = qm.shape
    tq = min(TOKEN_TILE, s)
    return pl.pallas_call(
        _mem_attn_kernel, grid=(b, s // tq),
        in_specs=[pl.BlockSpec((1, tq, mem_w), lambda bi, i: (bi, i, 0)),
                  pl.BlockSpec((1, MEM_TOKENS, mem_w), lambda bi, i: (bi, 0, 0)),
                  pl.BlockSpec((1, MEM_TOKENS, mem_w), lambda bi, i: (bi, 0, 0))],
        out_specs=pl.BlockSpec((1, tq, mem_w), lambda bi, i: (bi, i, 0)),
        out_shape=jax.ShapeDtypeStruct((b, s, mem_w), F32),
        compiler_params=_params("parallel", "parallel"),
        name="mem_attn")(qm, mem_k, mem_v)


def _post_b_kernel(x1_ref, om_ref, wmo_ref, g_ref, wr_ref, br_ref,
                   x2_ref, h3_ref, idx_ref, gate_ref, rank_ref, cnt_ref, carry_ref):
    tm = x1_ref.shape[0]

    @pl.when(pl.program_id(0) == 0)
    def _():
        carry_ref[...] = jnp.zeros(carry_ref.shape, F32)

    x2 = x1_ref[...] + _dot(om_ref[...].astype(BF16), wmo_ref[...])
    x2_ref[...] = x2
    h3 = _rms(x2, g_ref[...])
    h3_ref[...] = h3
    logits = _dot(h3.astype(BF16), wr_ref[...]) + br_ref[...]
    lane = lax.broadcasted_iota(jnp.int32, logits.shape, 1)
    lane_f = lane.astype(F32)
    vals, hots = [], []
    idx_out = jnp.zeros(logits.shape, F32)
    for kk in range(TOP_K):
        mx = jnp.max(logits, axis=-1, keepdims=True)
        first = jnp.min(jnp.where(logits == mx, lane_f, float(LANES)), axis=-1, keepdims=True)
        hot = lane_f == first
        logits = jnp.where(hot, -jnp.inf, logits)
        vals.append(mx)
        hots.append(hot)
        idx_out = jnp.where(lane == kk, first, idx_out)
    exps = [jnp.exp(v - vals[0]) for v in vals]
    denom = exps[0] + exps[1] + exps[2] + exps[3]
    chosen = jnp.zeros(logits.shape, F32)
    gate_out = jnp.zeros(logits.shape, F32)
    for kk in range(TOP_K):
        chosen = chosen + hots[kk].astype(F32)
        gate_out = jnp.where(lane == kk, exps[kk] / denom, gate_out)
    r_i = lax.broadcasted_iota(jnp.int32, (tm, tm), 0)
    c_i = lax.broadcasted_iota(jnp.int32, (tm, tm), 1)
    tril = (c_i < r_i).astype(BF16)
    before = _dot(tril, chosen.astype(BF16)) + carry_ref[...]
    rank_out = jnp.zeros(logits.shape, F32)
    for kk in range(TOP_K):
        rk = jnp.sum(jnp.where(hots[kk], before, 0.0), axis=-1, keepdims=True)
        rank_out = jnp.where(lane == kk, rk, rank_out)
    carry = carry_ref[...] + jnp.sum(chosen, axis=0, keepdims=True)
    carry_ref[...] = carry
    idx_ref[...] = idx_out.astype(jnp.int32)
    gate_ref[...] = gate_out
    rank_ref[...] = rank_out.astype(jnp.int32)
    cnt_ref[...] = carry.astype(jnp.int32)


def _post_b(x1, om2d, w):
    t = x1.shape[0]
    tm = TOKEN_TILE
    mem_w = MEM_HEADS * MEM_HD
    row = lambda n: pl.BlockSpec((tm, n), lambda i: (i, 0))
    return pl.pallas_call(
        _post_b_kernel, grid=(t // tm,),
        in_specs=[row(D_MODEL), row(mem_w), _const_spec((mem_w, D_MODEL)), _const_spec((1, D_MODEL)),
                  _const_spec((D_MODEL, LANES)), _const_spec((1, LANES))],
        out_specs=[row(D_MODEL), row(D_MODEL), row(LANES), row(LANES), row(LANES), _const_spec((1, LANES))],
        out_shape=[jax.ShapeDtypeStruct((t, D_MODEL), F32), jax.ShapeDtypeStruct((t, D_MODEL), F32),
                   jax.ShapeDtypeStruct((t, LANES), jnp.int32), jax.ShapeDtypeStruct((t, LANES), F32),
                   jax.ShapeDtypeStruct((t, LANES), jnp.int32), jax.ShapeDtypeStruct((1, LANES), jnp.int32)],
        scratch_shapes=[pltpu.VMEM((1, LANES), F32)],
        compiler_params=_params("arbitrary", vmem=VMEM_LIMIT),
        name="post_b")(x1, om2d, w['w_mo'], w['g_ffn'], w['w_router'], w['b_router'])


def _row_copy(src, src_row, dst, dst_row, sem):
    return pltpu.make_async_copy(src.at[pl.ds(src_row, 1)], dst.at[pl.ds(dst_row, 1)], sem)


def _dispatch_kernel(lo_ref, hi_ref, dest_ref, h_hbm, xs_hbm, zero_ref, sem):
    i = pl.program_id(0)
    n = dest_ref.shape[0]
    base_tok = i * (n // TOP_K)

    def issue(r, _):
        _row_copy(h_hbm, base_tok + r // TOP_K, xs_hbm, dest_ref[r], sem).start()
        return 0

    def drain(r, _):
        _row_copy(h_hbm, 0, xs_hbm, 0, sem).wait()
        return 0

    lax.fori_loop(0, n, issue, 0)
    lax.fori_loop(0, n, drain, 0)

    @pl.when(i == pl.num_programs(0) - 1)
    def _():
        zero_ref[...] = jnp.zeros(zero_ref.shape, F32)
        for e in range(N_EXPERTS):
            lo, hi = lo_ref[e], hi_ref[e]

            def zissue(r, _):
                _row_copy(zero_ref, 0, xs_hbm, r, sem).start()
                return 0

            def zdrain(r, _):
                _row_copy(zero_ref, 0, xs_hbm, 0, sem).wait()
                return 0

            lax.fori_loop(lo, hi, zissue, 0)
            lax.fori_loop(lo, hi, zdrain, 0)

        bm = zero_ref.shape[0]

        def tail(blk, _):
            cp = pltpu.make_async_copy(zero_ref, xs_hbm.at[pl.ds(pl.multiple_of(blk * bm, bm), bm)], sem)
            cp.start()
            cp.wait()
            return 0

        lax.fori_loop(hi_ref[N_EXPERTS - 1] // bm, xs_hbm.shape[0] // bm, tail, 0)


def _dispatch(h3, dest_flat, pad_lo, pad_hi, n_rows):
    t = h3.shape[0]
    tm = TOKEN_TILE
    grid_spec = pltpu.PrefetchScalarGridSpec(
        num_scalar_prefetch=2, grid=(t // tm,),
        in_specs=[pl.BlockSpec((tm * TOP_K,), lambda i, lo, hi: (i,), memory_space=pltpu.SMEM),
                  pl.BlockSpec(memory_space=pl.ANY)],
        out_specs=pl.BlockSpec(memory_space=pl.ANY),
        scratch_shapes=[pltpu.VMEM((MOE_ROWS, D_MODEL), F32), pltpu.SemaphoreType.DMA])
    return pl.pallas_call(
        _dispatch_kernel, grid_spec=grid_spec,
        out_shape=jax.ShapeDtypeStruct((n_rows, D_MODEL), F32),
        compiler_params=pltpu.CompilerParams(dimension_semantics=("arbitrary",), has_side_effects=True),
        name="moe_dispatch")(pad_lo, pad_hi, dest_flat, h3)


def _expert_kernel(src_ref, exp_ref, first_ref, valid_ref, x_ref, wg_ref, wu_ref, wd_ref,
                   bg_ref, bu_ref, bd_ref, y_ref, wgb_ref, wub_ref, wdb_ref):
    b = pl.program_id(0)

    @pl.when(first_ref[b] == 1)
    def _():
        wgb_ref[...] = wg_ref[0].astype(BF16)
        wub_ref[...] = wu_ref[0].astype(BF16)
        wdb_ref[...] = wd_ref[0].astype(BF16)

    @pl.when(valid_ref[b] == 1)
    def _():
        x = x_ref[...].astype(BF16)
        g = _dot(x, wgb_ref[...]) + bg_ref[0]
        u = _dot(x, wub_ref[...]) + bu_ref[0]
        g = jnp.minimum(g, SWIGLU_LIMIT)
        u = jnp.clip(u, -SWIGLU_LIMIT, SWIGLU_LIMIT)
        a = (u + 1.0) * (g * jax.nn.sigmoid(SWIGLU_ALPHA * g))
        y_ref[...] = _dot(a.astype(BF16), wdb_ref[...]) + bd_ref[0]

    @pl.when(valid_ref[b] == 0)
    def _():
        y_ref[...] = jnp.zeros(y_ref.shape, F32)


def _experts(xs, blk_src, blk_exp, blk_first, blk_valid, w):
    n_rows = xs.shape[0]
    bm = MOE_ROWS
    rows = pl.BlockSpec((bm, D_MODEL), lambda b, src, ex, fi, va: (src[b], 0))
    wspec = pl.BlockSpec((1, D_MODEL, D_MODEL), lambda b, src, ex, fi, va: (ex[b], 0, 0))
    bspec = pl.BlockSpec((1, 1, D_MODEL), lambda b, src, ex, fi, va: (ex[b], 0, 0))
    grid_spec = pltpu.PrefetchScalarGridSpec(
        num_scalar_prefetch=4, grid=(n_rows // bm,),
        in_specs=[rows, wspec, wspec, wspec, bspec, bspec, bspec],
        out_specs=pl.BlockSpec((bm, D_MODEL), lambda b, src, ex, fi, va: (b, 0)),
        scratch_shapes=[pltpu.VMEM((D_MODEL, D_MODEL), BF16)] * 3)
    return pl.pallas_call(
        _expert_kernel, grid_spec=grid_spec,
        out_shape=jax.ShapeDtypeStruct((n_rows, D_MODEL), F32),
        compiler_params=_params("arbitrary", vmem=56 * 1024 * 1024),
        name="moe_experts")(blk_src, blk_exp, blk_first, blk_valid, xs,
                            w['w_gate'], w['w_up'], w['w_down'], w['b_gate'], w['b_up'], w['b_down'])


def _combine_kernel(dest_ref, yb_hbm, x2_ref, gate_ref, g_ref, o_ref, rows_ref, sem):
    tm = x2_ref.shape[0]
    n = dest_ref.shape[0]

    def issue(r, _):
        pltpu.make_async_copy(yb_hbm.at[pl.ds(dest_ref[r], 1)],
                              rows_ref.at[r % TOP_K, pl.ds(r // TOP_K, 1)], sem).start()
        return 0

    def drain(r, _):
        pltpu.make_async_copy(yb_hbm.at[pl.ds(0, 1)], rows_ref.at[0, pl.ds(0, 1)], sem).wait()
        return 0

    lax.fori_loop(0, n, issue, 0)
    lax.fori_loop(0, n, drain, 0)
    y = x2_ref[...]
    gate = gate_ref[...]
    for kk in range(TOP_K):
        y = y + rows_ref[kk] * gate[:, kk:kk + 1]
    o_ref[...] = _rms(y, g_ref[...])
    del tm


def _combine(yb, dest_flat, x2, gate128, g_final):
    t = x2.shape[0]
    tm = TOKEN_TILE
    row = lambda n: pl.BlockSpec((tm, n), lambda i: (i, 0))
    return pl.pallas_call(
        _combine_kernel, grid=(t // tm,),
        in_specs=[pl.BlockSpec((tm * TOP_K,), lambda i: (i,), memory_space=pltpu.SMEM),
                  pl.BlockSpec(memory_space=pl.ANY), row(D_MODEL), row(LANES), _const_spec((1, D_MODEL))],
        out_specs=row(D_MODEL),
        out_shape=jax.ShapeDtypeStruct((t, D_MODEL), F32),
        scratch_shapes=[pltpu.VMEM((TOP_K, tm, D_MODEL), F32), pltpu.SemaphoreType.DMA],
        compiler_params=_params("arbitrary"),
        name="moe_combine")(dest_flat, yb, x2, gate128, g_final)


def _moe_and_final(x2, h3, idx128, gate128, rank128, cnt128, w):
    t = x2.shape[0]
    bm = MOE_ROWS
    n_assign = t * TOP_K
    n_blocks = (n_assign + N_EXPERTS * (bm - 1) + bm - 1) // bm
    counts = cnt128[0, :N_EXPERTS]
    padded = ((counts + bm - 1) // bm) * bm
    pad_end = jnp.cumsum(padded)
    pad_start = pad_end - padded
    dest = (pad_start[idx128[:, :TOP_K]] + rank128[:, :TOP_K]).astype(jnp.int32).reshape(-1)
    blk_row0 = jnp.arange(n_blocks, dtype=jnp.int32) * bm
    blk_valid = (blk_row0 < pad_end[-1]).astype(jnp.int32)
    last_valid = jnp.maximum(pad_end[-1] // bm - 1, 0).astype(jnp.int32)
    blk_src = jnp.minimum(jnp.arange(n_blocks, dtype=jnp.int32), last_valid)
    blk_exp = jnp.minimum(jnp.searchsorted(pad_end, blk_src * bm, side='right'), N_EXPERTS - 1).astype(jnp.int32)
    blk_first = jnp.concatenate([jnp.ones((1,), jnp.int32),
                                 (blk_exp[1:] != blk_exp[:-1]).astype(jnp.int32)])
    xs = _dispatch(h3, dest, (pad_start + counts).astype(jnp.int32), pad_end.astype(jnp.int32), n_blocks * bm)
    yb = _experts(xs, blk_src, blk_exp, blk_first, blk_valid, w)
    return _combine(yb, dest, x2, gate128, w['g_final'])


def _swap_halves(wcols):
    half = QK_ROPE // 2
    return jnp.concatenate([-wcols[..., half:], wcols[..., :half]], axis=-1)


def _prep_weights(g_mix, w_in, conv_w, conv_b, conv_ln_g, conv_ln_b, q_norm_g, w_q_up, kv_norm_g, w_kv_up,
                  w_out, g_mem_q, g_mem_kv, w_mq, w_mk, w_mv, w_mo, g_ffn, w_router, b_router,
                  w_gate, b_gate, w_up, b_up, w_down, b_down, g_final):
    l = 0
    w_kpe = w_in[l][:, C_KPE:]
    zpad = jnp.zeros((D_MODEL, HEAD_PAD - QK_ROPE), F32)
    w_in_ext = jnp.concatenate([w_in[l][:, :C_KPE], w_kpe, zpad, _swap_halves(w_kpe), zpad], axis=1)
    wq3 = w_q_up[l].reshape(Q_RANK, N_HEADS, QK_NOPE + QK_ROPE)
    q_nope, q_rope = wq3[..., :QK_NOPE], wq3[..., QK_NOPE:]
    z32 = jnp.zeros((Q_RANK, N_HEADS, HEAD_PAD - QK_NOPE - QK_ROPE), F32)
    wq = jnp.concatenate([q_rope, q_nope, z32], axis=-1).reshape(Q_RANK, N_HEADS * HEAD_PAD)
    wq_sw = jnp.concatenate([_swap_halves(q_rope), jnp.zeros_like(q_nope), z32], axis=-1)
    wq_sw = wq_sw.reshape(Q_RANK, N_HEADS * HEAD_PAD)
    w_uk = w_kv_up[l][:, :, :QK_NOPE]
    w_uv = w_kv_up[l][:, :, QK_NOPE:]
    wuk_pad = jnp.concatenate([jnp.zeros((KV_RANK, N_HEADS, QK_ROPE), F32), w_uk,
                               jnp.zeros((KV_RANK, N_HEADS, HEAD_PAD - QK_NOPE - QK_ROPE), F32)], axis=-1)
    wukt_pad = jnp.transpose(wuk_pad, (1, 2, 0))
    eye = jnp.eye(N_HEADS, dtype=F32)
    wuv_pad = (w_uv.transpose(1, 0, 2)[:, :, None, :] * eye[:, None, :, None])
    conv_w_pad = jnp.concatenate([conv_w[l], jnp.zeros((CONV_HALO - CONV_WIDTH, CONV_CH), F32)], axis=0)
    w_router_pad = jnp.concatenate([w_router[l], jnp.zeros((D_MODEL, LANES - N_EXPERTS), F32)], axis=1)
    b_router_pad = jnp.concatenate([b_router[l], jnp.full((LANES - N_EXPERTS,), NEG_INF, F32)])
    return {
        'g_mix': g_mix[l][None], 'w_in_ext': w_in_ext.astype(BF16),
        'q_norm_g': q_norm_g[l][None], 'wq': wq.astype(BF16), 'wq_sw': wq_sw.astype(BF16),
        'kv_norm_g': kv_norm_g[l][None],
        'wuk_pad': wuk_pad.reshape(KV_RANK, N_HEADS * HEAD_PAD).astype(BF16),
        'wuv': w_uv.reshape(KV_RANK, N_HEADS * V_DIM).astype(BF16),
        'wukt_pad': wukt_pad.astype(BF16),
        'wuv_pad': wuv_pad.reshape(N_HEADS, KV_RANK, N_HEADS * V_DIM).astype(BF16),
        'conv_w': conv_w_pad, 'conv_b': conv_b[l][None],
        'conv_ln_g': conv_ln_g[l][None], 'conv_ln_b': conv_ln_b[l][None],
        'w_out': w_out[l].astype(BF16), 'g_mem_q': g_mem_q[l][None], 'w_mq': w_mq[l].astype(BF16),
        'g_mem_kv': g_mem_kv[l][None], 'w_mk': w_mk[l].astype(BF16), 'w_mv': w_mv[l].astype(BF16),
        'w_mo': w_mo[l].astype(BF16), 'g_ffn': g_ffn[l][None],
        'w_router': w_router_pad.astype(BF16), 'b_router': b_router_pad[None],
        'w_gate': w_gate[l], 'w_up': w_up[l], 'w_down': w_down[l],
        'b_gate': b_gate[l][:, None, :], 'b_up': b_up[l][:, None, :], 'b_down': b_down[l][:, None, :],
        'g_final': g_final[None],
    }


def _rope_table(pos):
    half = QK_ROPE // 2
    inv = ROPE_THETA ** (-jnp.arange(half, dtype=F32) / half)
    ang = pos.astype(F32)[:, None] * inv[None, :]
    cos, sin = jnp.cos(ang), jnp.sin(ang)
    n = pos.shape[0]
    ones = jnp.ones((n, QK_NOPE), F32)
    z = lambda k: jnp.zeros((n, k), F32)
    cq = MLA_SCALE * jnp.concatenate([cos, cos, ones, z(HEAD_PAD - QK_NOPE - QK_ROPE)], axis=1)
    sq = MLA_SCALE * jnp.concatenate([sin, sin, z(HEAD_PAD - QK_ROPE)], axis=1)
    ck = jnp.concatenate([cos, cos, z(HEAD_PAD - QK_ROPE)], axis=1)
    sk = jnp.concatenate([sin, sin, z(HEAD_PAD - QK_ROPE)], axis=1)
    return jnp.stack([cq, sq, ck, sk])


def _layer(x, conv_prev, mem_k, mem_v, pos, w, paged):
    b, s, _ = x.shape
    t = b * s
    x2d = x.reshape(t, D_MODEL)
    sample = paged is not None
    if sample:
        tab = _rope_table(jnp.tile(pos, TOKEN_TILE // s))
    else:
        tab = _rope_table(pos)
    outs = _inproj(x2d, tab, w, sample)
    u, ckv, kpe = outs[0], outs[1], outs[2]
    u3 = u.reshape(b, s, CONV_CH)
    ucat = jnp.concatenate([conv_prev, u3], axis=1)
    conv_tail = ucat[:, -(CONV_WIDTH - 1):]
    upad = jnp.concatenate([jnp.zeros((b, CONV_HALO - CONV_WIDTH + 1, CONV_CH), F32), ucat], axis=1)
    ckv3 = ckv.reshape(b, s, KV_RANK)
    kpe3 = kpe.reshape(b, s, QK_ROPE)
    if sample:
        page_table, cache_ckv, cache_kpe = paged
        conv_out = _conv_sample(upad, w)
        attn = _attn_sample(page_table, outs[3], outs[4], ckv3, kpe3, cache_ckv, cache_kpe)
    else:
        conv_out = _conv_prompt(upad, w)
        q, k, v = (a.reshape(b, s, -1) for a in outs[3:6])
        attn = _attn_prompt(q, k, v).reshape(t, N_HEADS * V_DIM)
    x1, qm = _post_a(x2d, conv_out.reshape(t, CONV_CH), attn, w, sample)
    om = _mem_attn(qm.reshape(b, s, -1), mem_k, mem_v)
    x2, h3, idx128, gate128, rank128, cnt128 = _post_b(x1, om.reshape(t, -1), w)
    y = _moe_and_final(x2, h3, idx128, gate128, rank128, cnt128, w)
    return y.reshape(b, s, D_MODEL), conv_tail, ckv3, kpe3


def kernel(x_prompt, x_sample, mem_prompt, cache_ckv, cache_kpe, page_table, cache_mem_k, cache_mem_v, state_conv, g_mix, w_in, conv_w, conv_b, conv_ln_g, conv_ln_b, q_norm_g, w_q_up, kv_norm_g, w_kv_up, w_out, g_mem_q, g_mem_kv, w_mq, w_mk, w_mv, w_mo, g_ffn, w_router, b_router, w_gate, b_gate, w_up, b_up, w_down, b_down, g_final):
    assert g_mix.shape[0] == 1, "single-layer step"
    w = _prep_weights(g_mix, w_in, conv_w, conv_b, conv_ln_g, conv_ln_b, q_norm_g, w_q_up, kv_norm_g, w_kv_up,
                      w_out, g_mem_q, g_mem_kv, w_mq, w_mk, w_mv, w_mo, g_ffn, w_router, b_router,
                      w_gate, b_gate, w_up, b_up, w_down, b_down, g_final)
    b_p, s_p, _ = x_prompt.shape
    b_s, t_s, _ = x_sample.shape
    past = page_table.shape[1] * PAGE_SIZE
    mem_w = MEM_HEADS * MEM_HD

    mk, mv = _mem_kv(mem_prompt.reshape(-1, D_MODEL), w)
    mk = mk.reshape(b_p, MEM_TOKENS, mem_w)
    mv = mv.reshape(b_p, MEM_TOKENS, mem_w)
    conv0 = jnp.zeros((b_p, CONV_WIDTH - 1, CONV_CH), F32)
    y_p, conv_p, ckv_p, kpe_p = _layer(x_prompt, conv0, mk, mv, jnp.arange(s_p, dtype=jnp.int32), w, None)

    y_s, conv_s, ckv_s, kpe_s = _layer(
        x_sample, state_conv[0], cache_mem_k[0].reshape(b_s, MEM_TOKENS, mem_w),
        cache_mem_v[0].reshape(b_s, MEM_TOKENS, mem_w), past + jnp.arange(t_s, dtype=jnp.int32), w,
        (page_table, cache_ckv[0], cache_kpe[0]))

    return (y_p, y_s, ckv_p[None], kpe_p[None],
            mk.reshape(1, b_p, MEM_TOKENS, MEM_HEADS, MEM_HD), mv.reshape(1, b_p, MEM_TOKENS, MEM_HEADS, MEM_HD),
            conv_p[None], ckv_s[None], kpe_s[None], conv_s[None])
```

```python
import functools

import jax
import jax.numpy as jnp
from jax import lax
from jax.experimental import pallas as pl
from jax.experimental.pallas import tpu as pltpu

F32 = jnp.float32
BF16 = jnp.bfloat16

D_MODEL = 1024
PAGE_SIZE = 128
CONV_CH = 512
CONV_WIDTH = 31
N_HEADS = 8
QK_NOPE = 64
QK_ROPE = 32
V_DIM = 64
Q_RANK = 384
KV_RANK = 256
ROPE_THETA = 10000.0
MLA_SCALE = (QK_NOPE + QK_ROPE) ** -0.5
MEM_TOKENS = 256
MEM_HEADS = 4
MEM_HD = 128
MEM_SCALE = MEM_HD ** -0.5
N_EXPERTS = 32
TOP_K = 4
SWIGLU_LIMIT = 7.0
SWIGLU_ALPHA = 1.702
EPS = 1e-6
NEG_INF = -1e30

LANES = 128
HEAD_PAD = 128
C_VAL, C_GATE, C_Q, C_CKV, C_KPE, C_KPE_SW, C_END = 0, 512, 1024, 1408, 1664, 1792, 1920
TOKEN_TILE = 256
ATTN_TILE = 256
CONV_TILE = 256
CONV_HALO = 32
CONV_CHUNK = 32
PAGES_PER_CHUNK = 8
SLAB = 8
MOE_ROWS = 256
VMEM_LIMIT = 48 * 1024 * 1024


def _rms(x, g):
    return x * lax.rsqrt(jnp.mean(x * x, axis=-1, keepdims=True) + EPS) * g


def _dot(a, b):
    return jnp.dot(a, b, preferred_element_type=F32)


def _dot_t(a, b):
    return lax.dot_general(a, b, (((1,), (1,)), ((), ())), preferred_element_type=F32)


def _params(*sem, vmem=None):
    return pltpu.CompilerParams(dimension_semantics=sem, vmem_limit_bytes=vmem)


def _const_spec(shape):
    nd = len(shape)
    return pl.BlockSpec(shape, lambda *_: (0,) * nd)


def _inproj_common(x_ref, gmix_ref, win_ref, qg_ref, wq_ref, wqsw_ref, kvg_ref, tab_ref,
                   u_ref, ckv_ref, kpe_ref):
    h = _rms(x_ref[...], gmix_ref[...]).astype(BF16)
    proj = _dot(h, win_ref[...])
    u_ref[...] = proj[:, C_VAL:C_GATE] * jax.nn.sigmoid(proj[:, C_GATE:C_Q])
    qn = _rms(proj[:, C_Q:C_CKV], qg_ref[...]).astype(BF16)
    ckv = _rms(proj[:, C_CKV:C_KPE], kvg_ref[...])
    ckv_ref[...] = ckv
    cq, sq, ck, sk = tab_ref[0], tab_ref[1], tab_ref[2], tab_ref[3]
    kpe_rot = proj[:, C_KPE:C_KPE_SW] * ck + proj[:, C_KPE_SW:C_END] * sk
    kpe_ref[...] = kpe_rot[:, :QK_ROPE]
    q = _dot(qn, wq_ref[...])
    qs = _dot(qn, wqsw_ref[...])
    q_heads = []
    for hd in range(N_HEADS):
        sl = slice(hd * HEAD_PAD, (hd + 1) * HEAD_PAD)
        q_heads.append(q[:, sl] * cq + qs[:, sl] * sq)
    return ckv, kpe_rot, q_heads


def _inproj_prompt_kernel(x_ref, gmix_ref, win_ref, qg_ref, wq_ref, wqsw_ref, kvg_ref, tab_ref,
                          wuk_ref, wuv_ref,
                          u_ref, ckv_ref, kpe_ref, q_ref, k_ref, v_ref):
    ckv, kpe_rot, q_heads = _inproj_common(x_ref, gmix_ref, win_ref, qg_ref, wq_ref, wqsw_ref,
                                           kvg_ref, tab_ref, u_ref, ckv_ref, kpe_ref)
    ckv_b = ckv.astype(BF16)
    k_nope = _dot(ckv_b, wuk_ref[...])
    for hd in range(N_HEADS):
        sl = slice(hd * HEAD_PAD, (hd + 1) * HEAD_PAD)
        q_ref[:, sl] = q_heads[hd].astype(BF16)
        k_ref[:, sl] = (k_nope[:, sl] + kpe_rot).astype(BF16)
    v_ref[...] = _dot(ckv_b, wuv_ref[...]).astype(BF16)


def _inproj_sample_kernel(x_ref, gmix_ref, win_ref, qg_ref, wq_ref, wqsw_ref, kvg_ref, tab_ref,
                          wukt_ref,
                          u_ref, ckv_ref, kpe_ref, qlat_ref, qpe_ref):
    _, _, q_heads = _inproj_common(x_ref, gmix_ref, win_ref, qg_ref, wq_ref, wqsw_ref,
                                   kvg_ref, tab_ref, u_ref, ckv_ref, kpe_ref)
    for hd in range(N_HEADS):
        qlat_ref[hd] = _dot(q_heads[hd].astype(BF16), wukt_ref[hd])
        qpe_ref[hd] = q_heads[hd][:, :QK_ROPE]


def _inproj(x2d, tab, w, sample):
    t = x2d.shape[0]
    tm = TOKEN_TILE
    n_tab = tab.shape[1] // tm
    row = lambda n: pl.BlockSpec((tm, n), lambda i: (i, 0))
    in_specs = [row(D_MODEL), _const_spec((1, D_MODEL)), _const_spec((D_MODEL, C_END)),
                _const_spec((1, Q_RANK)), _const_spec((Q_RANK, N_HEADS * HEAD_PAD)),
                _const_spec((Q_RANK, N_HEADS * HEAD_PAD)), _const_spec((1, KV_RANK)),
                pl.BlockSpec((4, tm, LANES), lambda i: (0, i % n_tab, 0))]
    args = [x2d, w['g_mix'], w['w_in_ext'], w['q_norm_g'], w['wq'], w['wq_sw'], w['kv_norm_g'], tab]
    out_shape = [jax.ShapeDtypeStruct((t, CONV_CH), F32), jax.ShapeDtypeStruct((t, KV_RANK), F32),
                 jax.ShapeDtypeStruct((t, QK_ROPE), F32)]
    out_specs = [row(CONV_CH), row(KV_RANK), row(QK_ROPE)]
    if sample:
        body = _inproj_sample_kernel
        in_specs += [_const_spec((N_HEADS, HEAD_PAD, KV_RANK))]
        args += [w['wukt_pad']]
        out_shape += [jax.ShapeDtypeStruct((N_HEADS, t, KV_RANK), F32),
                      jax.ShapeDtypeStruct((N_HEADS, t, QK_ROPE), F32)]
        out_specs += [pl.BlockSpec((N_HEADS, tm, KV_RANK), lambda i: (0, i, 0)),
                      pl.BlockSpec((N_HEADS, tm, QK_ROPE), lambda i: (0, i, 0))]
    else:
        body = _inproj_prompt_kernel
        in_specs += [_const_spec((KV_RANK, N_HEADS * HEAD_PAD)), _const_spec((KV_RANK, N_HEADS * V_DIM))]
        args += [w['wuk_pad'], w['wuv']]
        out_shape += [jax.ShapeDtypeStruct((t, N_HEADS * HEAD_PAD), BF16),
                      jax.ShapeDtypeStruct((t, N_HEADS * HEAD_PAD), BF16),
                      jax.ShapeDtypeStruct((t, N_HEADS * V_DIM), BF16)]
        out_specs += [row(N_HEADS * HEAD_PAD), row(N_HEADS * HEAD_PAD), row(N_HEADS * V_DIM)]
    return pl.pallas_call(
        body, grid=(t // tm,), in_specs=in_specs, out_specs=out_specs, out_shape=out_shape,
        compiler_params=_params("parallel", vmem=VMEM_LIMIT),
        name="inproj_sample" if sample else "inproj_prompt")(*args)


def _ln_swish(conv, g, b):
    mu = jnp.mean(conv, axis=-1, keepdims=True)
    xc = conv - mu
    var = jnp.mean(xc * xc, axis=-1, keepdims=True)
    y = xc * lax.rsqrt(var + EPS) * g + b
    return y * jax.nn.sigmoid(y)


def _conv_prompt_kernel(main_ref, halo_ref, w_ref, b_ref, g_ref, lb_ref, o_ref, win_ref):
    tt = main_ref.shape[1]
    win_ref[0:tt, :] = main_ref[0]
    win_ref[tt:tt + CONV_HALO, :] = halo_ref[0]
    for c in range(tt // CONV_CHUNK):
        base = c * CONV_CHUNK + 2
        acc = win_ref[base:base + CONV_CHUNK, :] * w_ref[0:1, :]
        for j in range(1, CONV_WIDTH):
            acc = acc + win_ref[base + j:base + j + CONV_CHUNK, :] * w_ref[j:j + 1, :]
        y = _ln_swish(acc + b_ref[...], g_ref[...], lb_ref[...])
        o_ref[0, c * CONV_CHUNK:(c + 1) * CONV_CHUNK, :] = y


def _conv_prompt(upad, w):
    b, s_pad, _ = upad.shape
    s = s_pad - CONV_HALO
    tt = CONV_TILE
    ratio = tt // CONV_HALO
    return pl.pallas_call(
        _conv_prompt_kernel, grid=(b, s // tt),
        in_specs=[pl.BlockSpec((1, tt, CONV_CH), lambda bi, i: (bi, i, 0)),
                  pl.BlockSpec((1, CONV_HALO, CONV_CH), lambda bi, i: (bi, (i + 1) * ratio, 0)),
                  _const_spec((CONV_HALO, CONV_CH)), _const_spec((1, CONV_CH)),
                  _const_spec((1, CONV_CH)), _const_spec((1, CONV_CH))],
        out_specs=pl.BlockSpec((1, tt, CONV_CH), lambda bi, i: (bi, i, 0)),
        out_shape=jax.ShapeDtypeStruct((b, s, CONV_CH), F32),
        scratch_shapes=[pltpu.VMEM((tt + CONV_HALO, CONV_CH), F32)],
        compiler_params=_params("parallel", "parallel"),
        name="conv_prompt")(upad, upad, w['conv_w'], w['conv_b'], w['conv_ln_g'], w['conv_ln_b'])


def _conv_sample_kernel(win_ref, w_ref, b_ref, g_ref, lb_ref, o_ref):
    t = o_ref.shape[1]
    acc = win_ref[:, 2:2 + t, :] * w_ref[0:1, :]
    for j in range(1, CONV_WIDTH):
        acc = acc + win_ref[:, 2 + j:2 + j + t, :] * w_ref[j:j + 1, :]
    y = _ln_swish(acc + b_ref[...], g_ref[...], lb_ref[...])
    o_ref[...] = y


def _conv_sample(upad, w):
    b, s_pad, _ = upad.shape
    t = s_pad - CONV_HALO
    bb = 8
    return pl.pallas_call(
        _conv_sample_kernel, grid=(b // bb,),
        in_specs=[pl.BlockSpec((bb, s_pad, CONV_CH), lambda i: (i, 0, 0)),
                  _const_spec((CONV_HALO, CONV_CH)), _const_spec((1, CONV_CH)),
                  _const_spec((1, CONV_CH)), _const_spec((1, CONV_CH))],
        out_specs=pl.BlockSpec((bb, t, CONV_CH), lambda i: (i, 0, 0)),
        out_shape=jax.ShapeDtypeStruct((b, t, CONV_CH), F32),
        compiler_params=_params("parallel"),
        name="conv_sample")(upad, w['conv_w'], w['conv_b'], w['conv_ln_g'], w['conv_ln_b'])


def _softmax_step(s, m, l):
    m_new = jnp.maximum(m, jnp.max(s, axis=-1, keepdims=True))
    alpha = jnp.exp(m - m_new)
    p = jnp.exp(s - m_new)
    return m_new, alpha, p, alpha * l + jnp.sum(p, axis=-1, keepdims=True)


def _attn_prompt_kernel(q_ref, k_ref, v_ref, o_ref):
    tq = q_ref.shape[1]
    qi = pl.program_id(2)
    outs = []
    for hd in range(2):
        sl = slice(hd * HEAD_PAD, (hd + 1) * HEAD_PAD)
        qh = q_ref[0, :, sl]

        def tile(j, carry, masked):
            m, l, acc = carry
            start = pl.multiple_of(j * tq, tq)
            s = _dot_t(qh, k_ref[0, pl.ds(start, tq), sl])
            if masked:
                rows = lax.broadcasted_iota(jnp.int32, s.shape, 0)
                cols = lax.broadcasted_iota(jnp.int32, s.shape, 1)
                s = jnp.where(cols <= rows, s, NEG_INF)
            m, alpha, p, l = _softmax_step(s, m, l)
            acc = alpha * acc + _dot(p.astype(BF16), v_ref[0, pl.ds(start, tq), :])
            return m, l, acc

        init = (jnp.full((tq, 1), NEG_INF, F32), jnp.zeros((tq, 1), F32), jnp.zeros((tq, LANES), F32))
        carry = lax.fori_loop(0, qi, functools.partial(tile, masked=False), init)
        m, l, acc = tile(qi, carry, True)
        outs.append(acc / l)
    lane = lax.broadcasted_iota(jnp.int32, outs[0].shape, 1)
    o_ref[0] = jnp.where(lane < V_DIM, outs[0], outs[1]).astype(BF16)


def _attn_prompt(q, k, v):
    b, s, _ = q.shape
    tq = ATTN_TILE
    return pl.pallas_call(
        _attn_prompt_kernel, grid=(b, N_HEADS // 2, s // tq),
        in_specs=[pl.BlockSpec((1, tq, 2 * HEAD_PAD), lambda bi, hp, qi: (bi, qi, hp)),
                  pl.BlockSpec((1, s, 2 * HEAD_PAD), lambda bi, hp, qi: (bi, 0, hp)),
                  pl.BlockSpec((1, s, 2 * V_DIM), lambda bi, hp, qi: (bi, 0, hp))],
        out_specs=pl.BlockSpec((1, tq, 2 * V_DIM), lambda bi, hp, qi: (bi, qi, hp)),
        out_shape=jax.ShapeDtypeStruct((b, s, N_HEADS * V_DIM), BF16),
        compiler_params=_params("parallel", "parallel", "arbitrary"),
        name="attn_prompt")(q, k, v)


def _attn_sample_kernel(pt_ref, ql_ref, qp_ref, cn_ref, kn_ref, ckv_hbm, kpe_hbm, o_ref,
                        ckv_buf, kpe_buf, kb_ref, s_ref, sem):
    b = pl.program_id(0)
    n_pages = kpe_buf.shape[1]
    n_chunks = n_pages // PAGES_PER_CHUNK
    chunk = PAGES_PER_CHUNK * PAGE_SIZE
    t_new = cn_ref.shape[1]
    rows_q = N_HEADS * t_new

    def fetch(batch, slot):
        def one(p, _):
            page = pt_ref[batch * n_pages + p]
            rows = pl.ds(pl.multiple_of(p * PAGE_SIZE, PAGE_SIZE), PAGE_SIZE)
            pltpu.make_async_copy(ckv_hbm.at[page], ckv_buf.at[slot, rows], sem.at[0, slot]).start()
            pltpu.make_async_copy(kpe_hbm.at[page], kpe_buf.at[slot, p], sem.at[1, slot]).start()
            return 0
        lax.fori_loop(0, n_pages, one, 0)

    slot = b % 2

    @pl.when(b == 0)
    def _():
        fetch(0, 0)

    @pl.when(b + 1 < pl.num_programs(0))
    def _():
        fetch(b + 1, 1 - slot)

    pltpu.make_async_copy(ckv_buf.at[slot], ckv_buf.at[slot], sem.at[0, slot]).wait()
    pltpu.make_async_copy(kpe_buf.at[slot], kpe_buf.at[slot], sem.at[1, slot]).wait()

    ql = ql_ref[:, 0].reshape(rows_q, KV_RANK).astype(BF16)
    qp = qp_ref[:, 0].reshape(rows_q, QK_ROPE).astype(BF16)

    pad = PAGE_SIZE - t_new
    kc_new = jnp.concatenate([cn_ref[0], jnp.zeros((pad, KV_RANK), F32)], axis=0).astype(BF16)
    kp_new = jnp.concatenate([kn_ref[0], jnp.zeros((pad, QK_ROPE), F32)], axis=0).astype(BF16)
    s_new = _dot_t(ql, kc_new) + _dot_t(qp, kp_new)
    t_q = lax.broadcasted_iota(jnp.int32, s_new.shape, 0) % t_new
    cols = lax.broadcasted_iota(jnp.int32, s_new.shape, 1)
    s_new = jnp.where(cols <= t_q, s_new, NEG_INF)
    m = jnp.max(s_new, axis=-1, keepdims=True)

    for c in range(n_chunks):
        kc = ckv_buf[slot, c * chunk:(c + 1) * chunk, :].astype(BF16)
        kb_ref[c * chunk:(c + 1) * chunk, :] = kc
        kpt = jnp.concatenate([kpe_buf[slot, c * PAGES_PER_CHUNK + i] for i in range(PAGES_PER_CHUNK)],
                              axis=1).astype(BF16)
        s = _dot_t(ql, kc) + _dot(qp, kpt)
        s_ref[c] = s
        m = jnp.maximum(m, jnp.max(s, axis=-1, keepdims=True))

    p_new = jnp.exp(s_new - m)
    l = jnp.sum(p_new, axis=-1, keepdims=True)
    acc = _dot(p_new.astype(BF16), kc_new)
    for c in range(n_chunks):
        p = jnp.exp(s_ref[c] - m)
        l = l + jnp.sum(p, axis=-1, keepdims=True)
        acc = acc + _dot(p.astype(BF16), kb_ref[c * chunk:(c + 1) * chunk, :])
    o_ref[:, 0] = (acc / l).reshape(N_HEADS, t_new, KV_RANK)


def _attn_sample(page_table, qlat, qpe, ckv_new, kpe_new, cache_ckv, cache_kpe_t):
    bs, n_pages = page_table.shape
    t_new = ckv_new.shape[1]
    qlat4 = qlat.reshape(N_HEADS, bs, t_new, KV_RANK)
    qpe4 = qpe.reshape(N_HEADS, bs, t_new, QK_ROPE)
    rows_q = N_HEADS * t_new
    past = n_pages * PAGE_SIZE
    in_specs = [pl.BlockSpec((N_HEADS, 1, t_new, KV_RANK), lambda b, pt: (0, b, 0, 0)),
                pl.BlockSpec((N_HEADS, 1, t_new, QK_ROPE), lambda b, pt: (0, b, 0, 0)),
                pl.BlockSpec((1, t_new, KV_RANK), lambda b, pt: (b, 0, 0)),
                pl.BlockSpec((1, t_new, QK_ROPE), lambda b, pt: (b, 0, 0)),
                pl.BlockSpec(memory_space=pl.ANY), pl.BlockSpec(memory_space=pl.ANY)]
    grid_spec = pltpu.PrefetchScalarGridSpec(
        num_scalar_prefetch=1, grid=(bs,), in_specs=in_specs,
        out_specs=pl.BlockSpec((N_HEADS, 1, t_new, KV_RANK), lambda b, pt: (0, b, 0, 0)),
        scratch_shapes=[pltpu.VMEM((2, past, KV_RANK), F32),
                        pltpu.VMEM((2, n_pages, QK_ROPE, PAGE_SIZE), F32),
                        pltpu.VMEM((past, KV_RANK), BF16),
                        pltpu.VMEM((n_pages // PAGES_PER_CHUNK, rows_q, PAGES_PER_CHUNK * PAGE_SIZE), F32),
                        pltpu.SemaphoreType.DMA((2, 2))])
    o = pl.pallas_call(
        _attn_sample_kernel, grid_spec=grid_spec,
        out_shape=jax.ShapeDtypeStruct((N_HEADS, bs, t_new, KV_RANK), F32),
        compiler_params=_params("arbitrary", vmem=VMEM_LIMIT),
        name="attn_sample")(page_table.reshape(-1), qlat4, qpe4, ckv_new, kpe_new, cache_ckv, cache_kpe_t)
    return o.reshape(N_HEADS, bs * t_new, KV_RANK)


def _post_a_kernel(sample, x_ref, conv_ref, attn_ref, *rest):
    if sample:
        wuv_ref, wout_ref, g_ref, wmq_ref, x1_ref, qm_ref = rest
        attn = None
        for hd in range(N_HEADS):
            d = _dot(attn_ref[hd].astype(BF16), wuv_ref[hd])
            attn = d if attn is None else attn + d
        attn = attn.astype(BF16)
    else:
        wout_ref, g_ref, wmq_ref, x1_ref, qm_ref = rest
        attn = attn_ref[...]
    mix = _dot(conv_ref[...].astype(BF16), wout_ref[0:CONV_CH, :]) + _dot(attn, wout_ref[CONV_CH:, :])
    x1 = x_ref[...] + mix
    x1_ref[...] = x1
    qm_ref[...] = _dot(_rms(x1, g_ref[...]).astype(BF16), wmq_ref[...])


def _post_a(x2d, conv2d, attn, w, sample):
    t = x2d.shape[0]
    tm = TOKEN_TILE
    row = lambda n: pl.BlockSpec((tm, n), lambda i: (i, 0))
    mem_w = MEM_HEADS * MEM_HD
    if sample:
        in_specs = [row(D_MODEL), row(CONV_CH), pl.BlockSpec((N_HEADS, tm, KV_RANK), lambda i: (0, i, 0)),
                    _const_spec((N_HEADS, KV_RANK, N_HEADS * V_DIM))]
        args = [x2d, conv2d, attn, w['wuv_pad']]
    else:
        in_specs = [row(D_MODEL), row(CONV_CH), row(N_HEADS * V_DIM)]
        args = [x2d, conv2d, attn]
    in_specs += [_const_spec((D_MODEL, D_MODEL)), _const_spec((1, D_MODEL)), _const_spec((D_MODEL, mem_w))]
    args += [w['w_out'], w['g_mem_q'], w['w_mq']]
    return pl.pallas_call(
        functools.partial(_post_a_kernel, sample), grid=(t // tm,), in_specs=in_specs,
        out_specs=[row(D_MODEL), row(mem_w)],
        out_shape=[jax.ShapeDtypeStruct((t, D_MODEL), F32), jax.ShapeDtypeStruct((t, mem_w), F32)],
        compiler_params=_params("parallel", vmem=VMEM_LIMIT),
        name="post_a_sample" if sample else "post_a_prompt")(*args)


def _mem_kv_kernel(m_ref, g_ref, wk_ref, wv_ref, k_ref, v_ref):
    tm = m_ref.shape[0]
    m = _rms(m_ref[...], g_ref[...]).astype(BF16)
    k = _dot(m, wk_ref[...])
    v = _dot(m, wv_ref[...])
    for hd in range(MEM_HEADS):
        sl = slice(hd * MEM_HD, (hd + 1) * MEM_HD)
        k_ref[pl.ds(hd, tm, stride=MEM_HEADS), :] = k[:, sl]
        v_ref[pl.ds(hd, tm, stride=MEM_HEADS), :] = v[:, sl]


def _mem_kv(mem2d, w):
    t = mem2d.shape[0]
    tm = TOKEN_TILE
    mem_w = MEM_HEADS * MEM_HD
    out = pl.BlockSpec((tm * MEM_HEADS, MEM_HD), lambda i: (i, 0))
    return pl.pallas_call(
        _mem_kv_kernel, grid=(t // tm,),
        in_specs=[pl.BlockSpec((tm, D_MODEL), lambda i: (i, 0)), _const_spec((1, D_MODEL)),
                  _const_spec((D_MODEL, mem_w)), _const_spec((D_MODEL, mem_w))],
        out_specs=[out, out],
        out_shape=[jax.ShapeDtypeStruct((t * MEM_HEADS, MEM_HD), F32)] * 2,
        compiler_params=_params("parallel"),
        name="mem_kv")(mem2d, w['g_mem_kv'], w['w_mk'], w['w_mv'])


def _mem_attn_kernel(q_ref, k_ref, v_ref, o_ref):
    for hd in range(MEM_HEADS):
        sl = slice(hd * MEM_HD, (hd + 1) * MEM_HD)
        kh = k_ref[pl.ds(hd, MEM_TOKENS, stride=MEM_HEADS), :].astype(BF16)
        vh = v_ref[pl.ds(hd, MEM_TOKENS, stride=MEM_HEADS), :].astype(BF16)
        s = _dot_t(q_ref[0, :, sl].astype(BF16), kh) * MEM_SCALE
        p = jnp.exp(s - jnp.max(s, axis=-1, keepdims=True))
        p = p / jnp.sum(p, axis=-1, keepdims=True)
        o_ref[0, :, sl] = _dot(p.astype(BF16), vh)


def _mem_attn(qm, mem_k, mem_v):
    b, s, mem_w = qm.shape
    tq = min(TOKEN_TILE, s)
    kv = pl.BlockSpec((MEM_TOKENS * MEM_HEADS, MEM_HD), lambda bi, i: (bi, 0))
    return pl.pallas_call(
        _mem_attn_kernel, grid=(b, s // tq),
        in_specs=[pl.BlockSpec((1, tq, mem_w), lambda bi, i: (bi, i, 0)), kv, kv],
        out_specs=pl.BlockSpec((1, tq, mem_w), lambda bi, i: (bi, i, 0)),
        out_shape=jax.ShapeDtypeStruct((b, s, mem_w), F32),
        compiler_params=_params("parallel", "parallel"),
        name="mem_attn")(qm, mem_k, mem_v)


def _slab_rows(j, n):
    return pl.ds(j, n, stride=SLAB)


def _post_b_kernel(n_p, x1p_ref, x1s_ref, omp_ref, oms_ref, wmo_ref, g_ref, wr_ref, br_ref,
                   x2_ref, h3_ref, idx_ref, gate_ref, rank_ref, cnt_ref, carry_ref):
    tm = x2_ref.shape[0]
    i = pl.program_id(0)

    @pl.when(i == 0)
    def _():
        carry_ref[...] = jnp.zeros(carry_ref.shape, F32)

    is_p = i < n_p
    x1 = jnp.where(is_p, x1p_ref[...], x1s_ref[...])
    om = jnp.where(is_p, omp_ref[...], oms_ref[...])
    x2 = x1 + _dot(om.astype(BF16), wmo_ref[...])
    x2_ref[...] = x2
    h3 = _rms(x2, g_ref[...])
    for j in range(SLAB):
        h3_ref[_slab_rows(j, tm), :] = h3[:, j * LANES:(j + 1) * LANES]
    logits = _dot(h3.astype(BF16), wr_ref[...]) + br_ref[...]
    lane = lax.broadcasted_iota(jnp.int32, logits.shape, 1)
    lane_f = lane.astype(F32)
    vals, hots = [], []
    idx_out = jnp.zeros(logits.shape, F32)
    for kk in range(TOP_K):
        mx = jnp.max(logits, axis=-1, keepdims=True)
        first = jnp.min(jnp.where(logits == mx, lane_f, float(LANES)), axis=-1, keepdims=True)
        hot = lane_f == first
        logits = jnp.where(hot, -jnp.inf, logits)
        vals.append(mx)
        hots.append(hot)
        idx_out = jnp.where(lane == kk, first, idx_out)
    exps = [jnp.exp(v - vals[0]) for v in vals]
    denom = exps[0] + exps[1] + exps[2] + exps[3]
    chosen = jnp.zeros(logits.shape, F32)
    gate_out = jnp.zeros(logits.shape, F32)
    for kk in range(TOP_K):
        chosen = chosen + hots[kk].astype(F32)
        gate_out = jnp.where(lane == kk, exps[kk] / denom, gate_out)
    r_i = lax.broadcasted_iota(jnp.int32, (tm, tm), 0)
    c_i = lax.broadcasted_iota(jnp.int32, (tm, tm), 1)
    tril = (c_i < r_i).astype(BF16)
    before = _dot(tril, chosen.astype(BF16)) + carry_ref[...]
    rank_out = jnp.zeros(logits.shape, F32)
    for kk in range(TOP_K):
        rk = jnp.sum(jnp.where(hots[kk], before, 0.0), axis=-1, keepdims=True)
        rank_out = jnp.where(lane == kk, rk, rank_out)
    carry = carry_ref[...] + jnp.sum(chosen, axis=0, keepdims=True)
    carry_ref[...] = carry
    idx_ref[...] = idx_out.astype(jnp.int32)
    gate_ref[...] = gate_out
    rank_ref[...] = rank_out.astype(jnp.int32)
    cnt_ref[...] = carry.astype(jnp.int32)


def _post_b(x1_p, om_p, x1_s, om_s, w):
    tm = TOKEN_TILE
    n_p, n_s = x1_p.shape[0] // tm, x1_s.shape[0] // tm
    t = (n_p + n_s) * tm
    mem_w = MEM_HEADS * MEM_HD
    row = lambda n: pl.BlockSpec((tm, n), lambda i: (i, 0))
    row_p = lambda n: pl.BlockSpec((tm, n), lambda i: (jnp.minimum(i, n_p - 1), 0))
    row_s = lambda n: pl.BlockSpec((tm, n), lambda i: (jnp.maximum(i - n_p, 0), 0))
    return pl.pallas_call(
        functools.partial(_post_b_kernel, n_p), grid=(n_p + n_s,),
        in_specs=[row_p(D_MODEL), row_s(D_MODEL), row_p(mem_w), row_s(mem_w),
                  _const_spec((mem_w, D_MODEL)), _const_spec((1, D_MODEL)),
                  _const_spec((D_MODEL, LANES)), _const_spec((1, LANES))],
        out_specs=[row(D_MODEL), pl.BlockSpec((tm * SLAB, LANES), lambda i: (i, 0)),
                   row(LANES), row(LANES), row(LANES), _const_spec((1, LANES))],
        out_shape=[jax.ShapeDtypeStruct((t, D_MODEL), F32), jax.ShapeDtypeStruct((t * SLAB, LANES), F32),
                   jax.ShapeDtypeStruct((t, LANES), jnp.int32), jax.ShapeDtypeStruct((t, LANES), F32),
                   jax.ShapeDtypeStruct((t, LANES), jnp.int32), jax.ShapeDtypeStruct((1, LANES), jnp.int32)],
        scratch_shapes=[pltpu.VMEM((1, LANES), F32)],
        compiler_params=_params("arbitrary", vmem=VMEM_LIMIT),
        name="post_b")(x1_p, x1_s, om_p, om_s, w['w_mo'], w['g_ffn'], w['w_router'], w['b_router'])


def _slab(row):
    return pl.ds(pl.multiple_of(row * SLAB, SLAB), SLAB)


def _dispatch_kernel(start_ref, lo_ref, hi_ref, idx_ref, rank_ref, h_ref, xs_hbm, zero_ref, sem):
    i = pl.program_id(0)
    tm = h_ref.shape[0] // SLAB

    def issue(t, _):
        for k in range(TOP_K):
            r = t * TOP_K + k
            d = start_ref[idx_ref[r]] + rank_ref[r]
            pltpu.make_async_copy(h_ref.at[_slab(t)], xs_hbm.at[_slab(d)], sem).start()
        return 0

    lax.fori_loop(0, tm, issue, 0, unroll=2)
    for _ in range(TOP_K):
        pltpu.make_async_copy(h_ref, xs_hbm.at[pl.ds(0, tm * SLAB)], sem).wait()

    @pl.when(i == pl.num_programs(0) - 1)
    def _():
        zero_ref[...] = jnp.zeros(zero_ref.shape, F32)
        for e in range(N_EXPERTS):
            lo, hi = lo_ref[e], hi_ref[e]

            def zissue(r, _):
                pltpu.make_async_copy(zero_ref.at[_slab(0)], xs_hbm.at[_slab(r)], sem).start()
                return 0

            def zdrain(r, _):
                pltpu.make_async_copy(zero_ref.at[_slab(0)], xs_hbm.at[_slab(0)], sem).wait()
                return 0

            lax.fori_loop(lo, hi, zissue, 0)
            lax.fori_loop(lo, hi, zdrain, 0)

        bm = zero_ref.shape[0] // SLAB

        def tail(blk, _):
            rows = pl.ds(pl.multiple_of(blk * bm * SLAB, SLAB), bm * SLAB)
            cp = pltpu.make_async_copy(zero_ref, xs_hbm.at[rows], sem)
            cp.start()
            cp.wait()
            return 0

        lax.fori_loop(hi_ref[N_EXPERTS - 1] // bm, xs_hbm.shape[0] // (bm * SLAB), tail, 0)


def _dispatch(h3s, idx_flat, rank_flat, pad_start, pad_lo, pad_hi, n_rows):
    tm = TOKEN_TILE
    t = h3s.shape[0] // SLAB
    grid_spec = pltpu.PrefetchScalarGridSpec(
        num_scalar_prefetch=3, grid=(t // tm,),
        in_specs=[pl.BlockSpec((tm * TOP_K,), lambda i, *_: (i,), memory_space=pltpu.SMEM),
                  pl.BlockSpec((tm * TOP_K,), lambda i, *_: (i,), memory_space=pltpu.SMEM),
                  pl.BlockSpec((tm * SLAB, LANES), lambda i, *_: (i, 0))],
        out_specs=pl.BlockSpec(memory_space=pl.ANY),
        scratch_shapes=[pltpu.VMEM((MOE_ROWS * SLAB, LANES), F32), pltpu.SemaphoreType.DMA])
    return pl.pallas_call(
        _dispatch_kernel, grid_spec=grid_spec,
        out_shape=jax.ShapeDtypeStruct((n_rows * SLAB, LANES), F32),
        compiler_params=pltpu.CompilerParams(dimension_semantics=("arbitrary",), has_side_effects=True),
        name="moe_dispatch")(pad_start, pad_lo, pad_hi, idx_flat, rank_flat, h3s)


def _expert_kernel(src_ref, exp_ref, first_ref, valid_ref, x_ref, wg_ref, wu_ref, wd_ref,
                   bg_ref, bu_ref, bd_ref, y_ref, wgb_ref, wub_ref, wdb_ref, xb_ref):
    b = pl.program_id(0)
    bm = xb_ref.shape[0]

    @pl.when(first_ref[b] == 1)
    def _():
        wgb_ref[...] = wg_ref[0].astype(BF16)
        wub_ref[...] = wu_ref[0].astype(BF16)
        wdb_ref[...] = wd_ref[0].astype(BF16)

    @pl.when(valid_ref[b] == 1)
    def _():
        for j in range(SLAB):
            xb_ref[:, j * LANES:(j + 1) * LANES] = x_ref[_slab_rows(j, bm), :].astype(BF16)
        x = xb_ref[...]
        g = _dot(x, wgb_ref[...]) + bg_ref[0]
        u = _dot(x, wub_ref[...]) + bu_ref[0]
        g = jnp.minimum(g, SWIGLU_LIMIT)
        u = jnp.clip(u, -SWIGLU_LIMIT, SWIGLU_LIMIT)
        a = (u + 1.0) * (g * jax.nn.sigmoid(SWIGLU_ALPHA * g))
        y = _dot(a.astype(BF16), wdb_ref[...]) + bd_ref[0]
        for j in range(SLAB):
            y_ref[_slab_rows(j, bm), :] = y[:, j * LANES:(j + 1) * LANES]

    @pl.when(valid_ref[b] == 0)
    def _():
        y_ref[...] = jnp.zeros(y_ref.shape, F32)


def _experts(xs, blk_src, blk_exp, blk_first, blk_valid, w):
    bm = MOE_ROWS
    n_blocks = xs.shape[0] // (bm * SLAB)
    wspec = pl.BlockSpec((1, D_MODEL, D_MODEL), lambda b, src, ex, fi, va: (ex[b], 0, 0))
    bspec = pl.BlockSpec((1, 1, D_MODEL), lambda b, src, ex, fi, va: (ex[b], 0, 0))
    grid_spec = pltpu.PrefetchScalarGridSpec(
        num_scalar_prefetch=4, grid=(n_blocks,),
        in_specs=[pl.BlockSpec((bm * SLAB, LANES), lambda b, src, ex, fi, va: (src[b], 0)),
                  wspec, wspec, wspec, bspec, bspec, bspec],
        out_specs=pl.BlockSpec((bm * SLAB, LANES), lambda b, src, ex, fi, va: (b, 0)),
        scratch_shapes=[pltpu.VMEM((D_MODEL, D_MODEL), BF16)] * 3 + [pltpu.VMEM((bm, D_MODEL), BF16)])
    return pl.pallas_call(
        _expert_kernel, grid_spec=grid_spec,
        out_shape=jax.ShapeDtypeStruct(xs.shape, F32),
        compiler_params=_params("arbitrary", vmem=56 * 1024 * 1024),
        name="moe_experts")(blk_src, blk_exp, blk_first, blk_valid, xs,
                            w['w_gate'], w['w_up'], w['w_down'], w['b_gate'], w['b_up'], w['b_down'])


def _combine_kernel(n_p, start_ref, idx_ref, rank_ref, yb_hbm, x2_ref, gate_ref, g_ref, yp_ref, ys_ref,
                    rows_ref, y_ref, sem):
    i = pl.program_id(0)
    tm = x2_ref.shape[0]

    def issue(t, _):
        for k in range(TOP_K):
            r = t * TOP_K + k
            d = start_ref[idx_ref[r]] + rank_ref[r]
            pltpu.make_async_copy(yb_hbm.at[_slab(d)], rows_ref.at[_slab(r)], sem).start()
        return 0

    lax.fori_loop(0, tm, issue, 0, unroll=2)
    for _ in range(TOP_K):
        pltpu.make_async_copy(yb_hbm.at[pl.ds(0, tm * SLAB)], rows_ref.at[pl.ds(0, tm * SLAB)], sem).wait()

    gate = gate_ref[...]
    ss = jnp.zeros((tm, 1), F32)
    for j in range(SLAB):
        y = x2_ref[:, j * LANES:(j + 1) * LANES]
        for k in range(TOP_K):
            y = y + rows_ref[pl.ds(k * SLAB + j, tm, stride=TOP_K * SLAB), :] * gate[:, k:k + 1]
        y_ref[:, j * LANES:(j + 1) * LANES] = y
        ss = ss + jnp.sum(y * y, axis=-1, keepdims=True)
    out = y_ref[...] * lax.rsqrt(ss * (1.0 / D_MODEL) + EPS) * g_ref[...]

    @pl.when(i < n_p)
    def _():
        yp_ref[...] = out

    @pl.when(i >= n_p)
    def _():
        ys_ref[...] = out


def _combine(yb, idx_flat, rank_flat, pad_start, x2, gate128, g_final, n_p):
    t = x2.shape[0]
    tm = TOKEN_TILE
    n_s = t // tm - n_p
    grid_spec = pltpu.PrefetchScalarGridSpec(
        num_scalar_prefetch=1, grid=(t // tm,),
        in_specs=[pl.BlockSpec((tm * TOP_K,), lambda i, *_: (i,), memory_space=pltpu.SMEM),
                  pl.BlockSpec((tm * TOP_K,), lambda i, *_: (i,), memory_space=pltpu.SMEM),
                  pl.BlockSpec(memory_space=pl.ANY),
                  pl.BlockSpec((tm, D_MODEL), lambda i, *_: (i, 0)),
                  pl.BlockSpec((tm, LANES), lambda i, *_: (i, 0)),
                  pl.BlockSpec((1, D_MODEL), lambda i, *_: (0, 0))],
        out_specs=[pl.BlockSpec((tm, D_MODEL), lambda i, *_: (jnp.minimum(i, n_p - 1), 0)),
                   pl.BlockSpec((tm, D_MODEL), lambda i, *_: (jnp.maximum(i - n_p, 0), 0))],
        scratch_shapes=[pltpu.VMEM((tm * TOP_K * SLAB, LANES), F32), pltpu.VMEM((tm, D_MODEL), F32),
                        pltpu.SemaphoreType.DMA])
    return pl.pallas_call(
        functools.partial(_combine_kernel, n_p), grid_spec=grid_spec,
        out_shape=[jax.ShapeDtypeStruct((n_p * tm, D_MODEL), F32), jax.ShapeDtypeStruct((n_s * tm, D_MODEL), F32)],
        compiler_params=_params("arbitrary"),
        name="moe_combine")(pad_start, idx_flat, rank_flat, yb, x2, gate128, g_final)


def _moe_and_final(x2, h3s, idx128, gate128, rank128, cnt128, w, n_p):
    t = x2.shape[0]
    bm = MOE_ROWS
    n_assign = t * TOP_K
    n_blocks = (n_assign + N_EXPERTS * (bm - 1) + bm - 1) // bm
    counts = cnt128[0, :N_EXPERTS]
    padded = ((counts + bm - 1) // bm) * bm
    pad_end = jnp.cumsum(padded).astype(jnp.int32)
    pad_start = pad_end - padded
    idx_flat = idx128[:, :TOP_K].reshape(-1)
    rank_flat = rank128[:, :TOP_K].reshape(-1)
    blk = jnp.arange(n_blocks, dtype=jnp.int32)
    blk_valid = (blk * bm < pad_end[-1]).astype(jnp.int32)
    blk_src = jnp.minimum(blk, jnp.maximum(pad_end[-1] // bm - 1, 0))
    blk_exp = jnp.sum((pad_end[None, :] <= (blk_src * bm)[:, None]).astype(jnp.int32), axis=1)
    blk_exp = jnp.minimum(blk_exp, N_EXPERTS - 1)
    blk_first = jnp.concatenate([jnp.ones((1,), jnp.int32),
                                 (blk_exp[1:] != blk_exp[:-1]).astype(jnp.int32)])
    xs = _dispatch(h3s, idx_flat, rank_flat, pad_start, pad_start + counts, pad_end, n_blocks * bm)
    yb = _experts(xs, blk_src, blk_exp, blk_first, blk_valid, w)
    return _combine(yb, idx_flat, rank_flat, pad_start, x2, gate128, w['g_final'], n_p)


def _swap_halves(wcols):
    half = QK_ROPE // 2
    return jnp.concatenate([-wcols[..., half:], wcols[..., :half]], axis=-1)


def _prep_weights(g_mix, w_in, conv_w, conv_b, conv_ln_g, conv_ln_b, q_norm_g, w_q_up, kv_norm_g, w_kv_up,
                  w_out, g_mem_q, g_mem_kv, w_mq, w_mk, w_mv, w_mo, g_ffn, w_router, b_router,
                  w_gate, b_gate, w_up, b_up, w_down, b_down, g_final):
    l = 0
    w_kpe = w_in[l][:, C_KPE:]
    zpad = jnp.zeros((D_MODEL, HEAD_PAD - QK_ROPE), F32)
    w_in_ext = jnp.concatenate([w_in[l][:, :C_KPE], w_kpe, zpad, _swap_halves(w_kpe), zpad], axis=1)
    wq3 = w_q_up[l].reshape(Q_RANK, N_HEADS, QK_NOPE + QK_ROPE)
    q_nope, q_rope = wq3[..., :QK_NOPE], wq3[..., QK_NOPE:]
    z32 = jnp.zeros((Q_RANK, N_HEADS, HEAD_PAD - QK_NOPE - QK_ROPE), F32)
    wq = jnp.concatenate([q_rope, q_nope, z32], axis=-1).reshape(Q_RANK, N_HEADS * HEAD_PAD)
    wq_sw = jnp.concatenate([_swap_halves(q_rope), jnp.zeros_like(q_nope), z32], axis=-1)
    wq_sw = wq_sw.reshape(Q_RANK, N_HEADS * HEAD_PAD)
    w_uk = w_kv_up[l][:, :, :QK_NOPE]
    w_uv = w_kv_up[l][:, :, QK_NOPE:]
    wuk_pad = jnp.concatenate([jnp.zeros((KV_RANK, N_HEADS, QK_ROPE), F32), w_uk,
                               jnp.zeros((KV_RANK, N_HEADS, HEAD_PAD - QK_NOPE - QK_ROPE), F32)], axis=-1)
    wukt_pad = jnp.transpose(wuk_pad, (1, 2, 0))
    eye = jnp.eye(N_HEADS, dtype=F32)
    wuv_pad = (w_uv.transpose(1, 0, 2)[:, :, None, :] * eye[:, None, :, None])
    conv_w_pad = jnp.concatenate([conv_w[l], jnp.zeros((CONV_HALO - CONV_WIDTH, CONV_CH), F32)], axis=0)
    w_router_pad = jnp.concatenate([w_router[l], jnp.zeros((D_MODEL, LANES - N_EXPERTS), F32)], axis=1)
    b_router_pad = jnp.concatenate([b_router[l], jnp.full((LANES - N_EXPERTS,), NEG_INF, F32)])
    return {
        'g_mix': g_mix[l][None], 'w_in_ext': w_in_ext.astype(BF16),
        'q_norm_g': q_norm_g[l][None], 'wq': wq.astype(BF16), 'wq_sw': wq_sw.astype(BF16),
        'kv_norm_g': kv_norm_g[l][None],
        'wuk_pad': wuk_pad.reshape(KV_RANK, N_HEADS * HEAD_PAD).astype(BF16),
        'wuv': w_uv.reshape(KV_RANK, N_HEADS * V_DIM).astype(BF16),
        'wukt_pad': wukt_pad.astype(BF16),
        'wuv_pad': wuv_pad.reshape(N_HEADS, KV_RANK, N_HEADS * V_DIM).astype(BF16),
        'conv_w': conv_w_pad, 'conv_b': conv_b[l][None],
        'conv_ln_g': conv_ln_g[l][None], 'conv_ln_b': conv_ln_b[l][None],
        'w_out': w_out[l].astype(BF16), 'g_mem_q': g_mem_q[l][None], 'w_mq': w_mq[l].astype(BF16),
        'g_mem_kv': g_mem_kv[l][None], 'w_mk': w_mk[l].astype(BF16), 'w_mv': w_mv[l].astype(BF16),
        'w_mo': w_mo[l].astype(BF16), 'g_ffn': g_ffn[l][None],
        'w_router': w_router_pad.astype(BF16), 'b_router': b_router_pad[None],
        'w_gate': w_gate[l], 'w_up': w_up[l], 'w_down': w_down[l],
        'b_gate': b_gate[l][:, None, :], 'b_up': b_up[l][:, None, :], 'b_down': b_down[l][:, None, :],
        'g_final': g_final[None],
    }


def _rope_table(pos):
    half = QK_ROPE // 2
    inv = ROPE_THETA ** (-jnp.arange(half, dtype=F32) / half)
    ang = pos.astype(F32)[:, None] * inv[None, :]
    cos, sin = jnp.cos(ang), jnp.sin(ang)
    n = pos.shape[0]
    ones = jnp.ones((n, QK_NOPE), F32)
    z = lambda k: jnp.zeros((n, k), F32)
    cq = MLA_SCALE * jnp.concatenate([cos, cos, ones, z(HEAD_PAD - QK_NOPE - QK_ROPE)], axis=1)
    sq = MLA_SCALE * jnp.concatenate([sin, sin, z(HEAD_PAD - QK_ROPE)], axis=1)
    ck = jnp.concatenate([cos, cos, z(HEAD_PAD - QK_ROPE)], axis=1)
    sk = jnp.concatenate([sin, sin, z(HEAD_PAD - QK_ROPE)], axis=1)
    return jnp.stack([cq, sq, ck, sk])


def _front(x, conv_prev, mem_k, mem_v, pos, w, paged):
    b, s, _ = x.shape
    t = b * s
    x2d = x.reshape(t, D_MODEL)
    sample = paged is not None
    if sample:
        tab = _rope_table(jnp.tile(pos, TOKEN_TILE // s))
    else:
        tab = _rope_table(pos)
    outs = _inproj(x2d, tab, w, sample)
    u, ckv, kpe = outs[0], outs[1], outs[2]
    u3 = u.reshape(b, s, CONV_CH)
    ucat = jnp.concatenate([conv_prev, u3], axis=1)
    conv_tail = ucat[:, -(CONV_WIDTH - 1):]
    upad = jnp.concatenate([jnp.zeros((b, CONV_HALO - CONV_WIDTH + 1, CONV_CH), F32), ucat], axis=1)
    ckv3 = ckv.reshape(b, s, KV_RANK)
    kpe3 = kpe.reshape(b, s, QK_ROPE)
    if sample:
        page_table, cache_ckv, cache_kpe_t = paged
        conv_out = _conv_sample(upad, w)
        attn = _attn_sample(page_table, outs[3], outs[4], ckv3, kpe3, cache_ckv, cache_kpe_t)
    else:
        conv_out = _conv_prompt(upad, w)
        q, k, v = (a.reshape(b, s, -1) for a in outs[3:6])
        attn = _attn_prompt(q, k, v).reshape(t, N_HEADS * V_DIM)
    x1, qm = _post_a(x2d, conv_out.reshape(t, CONV_CH), attn, w, sample)
    om = _mem_attn(qm.reshape(b, s, -1), mem_k, mem_v)
    return x1, om.reshape(t, -1), conv_tail, ckv3, kpe3


def kernel(x_prompt, x_sample, mem_prompt, cache_ckv, cache_kpe, page_table, cache_mem_k, cache_mem_v, state_conv, g_mix, w_in, conv_w, conv_b, conv_ln_g, conv_ln_b, q_norm_g, w_q_up, kv_norm_g, w_kv_up, w_out, g_mem_q, g_mem_kv, w_mq, w_mk, w_mv, w_mo, g_ffn, w_router, b_router, w_gate, b_gate, w_up, b_up, w_down, b_down, g_final):
    assert g_mix.shape[0] == 1, "single-layer step"
    w = _prep_weights(g_mix, w_in, conv_w, conv_b, conv_ln_g, conv_ln_b, q_norm_g, w_q_up, kv_norm_g, w_kv_up,
                      w_out, g_mem_q, g_mem_kv, w_mq, w_mk, w_mv, w_mo, g_ffn, w_router, b_router,
                      w_gate, b_gate, w_up, b_up, w_down, b_down, g_final)
    b_p, s_p, _ = x_prompt.shape
    b_s, t_s, _ = x_sample.shape
    past = page_table.shape[1] * PAGE_SIZE

    mk, mv = _mem_kv(mem_prompt.reshape(-1, D_MODEL), w)
    conv0 = jnp.zeros((b_p, CONV_WIDTH - 1, CONV_CH), F32)
    x1_p, om_p, conv_p, ckv_p, kpe_p = _front(x_prompt, conv0, mk, mv, jnp.arange(s_p, dtype=jnp.int32), w, None)
    x1_s, om_s, conv_s, ckv_s, kpe_s = _front(
        x_sample, state_conv[0], cache_mem_k[0].reshape(-1, MEM_HD), cache_mem_v[0].reshape(-1, MEM_HD),
        past + jnp.arange(t_s, dtype=jnp.int32), w,
        (page_table, cache_ckv[0], jnp.swapaxes(cache_kpe[0], 1, 2)))

    x2, h3s, idx128, gate128, rank128, cnt128 = _post_b(x1_p, om_p, x1_s, om_s, w)
    y_p, y_s = _moe_and_final(x2, h3s, idx128, gate128, rank128, cnt128, w, x1_p.shape[0] // TOKEN_TILE)

    mem_shape = (1, b_p, MEM_TOKENS, MEM_HEADS, MEM_HD)
    return (y_p.reshape(b_p, s_p, D_MODEL), y_s.reshape(b_s, t_s, D_MODEL), ckv_p[None], kpe_p[None],
            mk.reshape(mem_shape), mv.reshape(mem_shape), conv_p[None], ckv_s[None], kpe_s[None], conv_s[None])
```

```python
import functools

import jax
import jax.numpy as jnp
from jax import lax
from jax.experimental import pallas as pl
from jax.experimental.pallas import tpu as pltpu

F32 = jnp.float32
BF16 = jnp.bfloat16

D_MODEL = 1024
PAGE_SIZE = 128
CONV_CH = 512
CONV_WIDTH = 31
N_HEADS = 8
QK_NOPE = 64
QK_ROPE = 32
V_DIM = 64
Q_RANK = 384
KV_RANK = 256
ROPE_THETA = 10000.0
MLA_SCALE = (QK_NOPE + QK_ROPE) ** -0.5
MEM_TOKENS = 256
MEM_HEADS = 4
MEM_HD = 128
MEM_SCALE = MEM_HD ** -0.5
N_EXPERTS = 32
TOP_K = 4
SWIGLU_LIMIT = 7.0
SWIGLU_ALPHA = 1.702
EPS = 1e-6
NEG_INF = -1e30

LANES = 128
SUBLANES = 8
HEAD_PAD = 128
C_VAL, C_GATE, C_Q, C_CKV, C_KPE, C_KPE_SW, C_END = 0, 512, 1024, 1408, 1664, 1792, 1920
TOKEN_TILE = 256
ATTN_TILE = 256
ATTN_HEADS = 4
CONV_TILE = 256
CONV_HALO = 32
CONV_CHUNK = 32
MEM_ATTN_ROWS = 32
PAGES_PER_CHUNK = 8
SLAB = 8
MOE_ROWS = 256
VMEM_LIMIT = 48 * 1024 * 1024


def _rms(x, g):
    return x * lax.rsqrt(jnp.mean(x * x, axis=-1, keepdims=True) + EPS) * g


def _dot(a, b):
    return jnp.dot(a, b, preferred_element_type=F32)


def _dot_t(a, b):
    return lax.dot_general(a, b, (((1,), (1,)), ((), ())), preferred_element_type=F32)


def _params(*sem, vmem=None):
    return pltpu.CompilerParams(dimension_semantics=sem, vmem_limit_bytes=vmem)


def _const_spec(shape):
    nd = len(shape)
    return pl.BlockSpec(shape, lambda *_: (0,) * nd)


def _inproj_common(x_ref, gmix_ref, win_ref, qg_ref, wq_ref, wqsw_ref, kvg_ref, tab_ref,
                   u_ref, ckv_ref, kpe_ref):
    h = _rms(x_ref[...], gmix_ref[...]).astype(BF16)
    proj = _dot(h, win_ref[...])
    u_ref[...] = proj[:, C_VAL:C_GATE] * jax.nn.sigmoid(proj[:, C_GATE:C_Q])
    qn = _rms(proj[:, C_Q:C_CKV], qg_ref[...]).astype(BF16)
    ckv = _rms(proj[:, C_CKV:C_KPE], kvg_ref[...])
    ckv_ref[...] = ckv
    cq, sq, ck, sk = tab_ref[0], tab_ref[1], tab_ref[2], tab_ref[3]
    kpe_rot = proj[:, C_KPE:C_KPE_SW] * ck + proj[:, C_KPE_SW:C_END] * sk
    kpe_ref[...] = kpe_rot[:, :QK_ROPE]
    q = _dot(qn, wq_ref[...])
    qs = _dot(qn, wqsw_ref[...])
    q_heads = []
    for hd in range(N_HEADS):
        sl = slice(hd * HEAD_PAD, (hd + 1) * HEAD_PAD)
        q_heads.append(q[:, sl] * cq + qs[:, sl] * sq)
    return ckv, kpe_rot, q_heads


def _inproj_prompt_kernel(x_ref, gmix_ref, win_ref, qg_ref, wq_ref, wqsw_ref, kvg_ref, tab_ref,
                          wuk_ref, wuv_ref,
                          u_ref, ckv_ref, kpe_ref, q_ref, k_ref, v_ref):
    ckv, kpe_rot, q_heads = _inproj_common(x_ref, gmix_ref, win_ref, qg_ref, wq_ref, wqsw_ref,
                                           kvg_ref, tab_ref, u_ref, ckv_ref, kpe_ref)
    ckv_b = ckv.astype(BF16)
    k_nope = _dot(ckv_b, wuk_ref[...])
    for hd in range(N_HEADS):
        sl = slice(hd * HEAD_PAD, (hd + 1) * HEAD_PAD)
        q_ref[:, sl] = q_heads[hd].astype(BF16)
        k_ref[:, sl] = (k_nope[:, sl] + kpe_rot).astype(BF16)
    v_ref[...] = _dot(ckv_b, wuv_ref[...]).astype(BF16)


def _inproj_sample_kernel(x_ref, gmix_ref, win_ref, qg_ref, wq_ref, wqsw_ref, kvg_ref, tab_ref,
                          wukt_ref,
                          u_ref, ckv_ref, kpe_ref, qlat_ref, qpe_ref):
    _, _, q_heads = _inproj_common(x_ref, gmix_ref, win_ref, qg_ref, wq_ref, wqsw_ref,
                                   kvg_ref, tab_ref, u_ref, ckv_ref, kpe_ref)
    for hd in range(N_HEADS):
        qlat_ref[hd] = _dot(q_heads[hd].astype(BF16), wukt_ref[hd])
        qpe_ref[hd] = q_heads[hd][:, :QK_ROPE]


def _inproj(x2d, tab, w, sample):
    t = x2d.shape[0]
    tm = TOKEN_TILE
    n_tab = tab.shape[1] // tm
    row = lambda n: pl.BlockSpec((tm, n), lambda i: (i, 0))
    in_specs = [row(D_MODEL), _const_spec((1, D_MODEL)), _const_spec((D_MODEL, C_END)),
                _const_spec((1, Q_RANK)), _const_spec((Q_RANK, N_HEADS * HEAD_PAD)),
                _const_spec((Q_RANK, N_HEADS * HEAD_PAD)), _const_spec((1, KV_RANK)),
                pl.BlockSpec((4, tm, LANES), lambda i: (0, i % n_tab, 0))]
    args = [x2d, w['g_mix'], w['w_in_ext'], w['q_norm_g'], w['wq'], w['wq_sw'], w['kv_norm_g'], tab]
    out_shape = [jax.ShapeDtypeStruct((t, CONV_CH), F32), jax.ShapeDtypeStruct((t, KV_RANK), F32),
                 jax.ShapeDtypeStruct((t, QK_ROPE), F32)]
    out_specs = [row(CONV_CH), row(KV_RANK), row(QK_ROPE)]
    if sample:
        body = _inproj_sample_kernel
        in_specs += [_const_spec((N_HEADS, HEAD_PAD, KV_RANK))]
        args += [w['wukt_pad']]
        out_shape += [jax.ShapeDtypeStruct((N_HEADS, t, KV_RANK), F32),
                      jax.ShapeDtypeStruct((N_HEADS, t, QK_ROPE), F32)]
        out_specs += [pl.BlockSpec((N_HEADS, tm, KV_RANK), lambda i: (0, i, 0)),
                      pl.BlockSpec((N_HEADS, tm, QK_ROPE), lambda i: (0, i, 0))]
    else:
        body = _inproj_prompt_kernel
        in_specs += [_const_spec((KV_RANK, N_HEADS * HEAD_PAD)), _const_spec((KV_RANK, N_HEADS * V_DIM))]
        args += [w['wuk_pad'], w['wuv']]
        out_shape += [jax.ShapeDtypeStruct((t, N_HEADS * HEAD_PAD), BF16),
                      jax.ShapeDtypeStruct((t, N_HEADS * HEAD_PAD), BF16),
                      jax.ShapeDtypeStruct((t, N_HEADS * V_DIM), BF16)]
        out_specs += [row(N_HEADS * HEAD_PAD), row(N_HEADS * HEAD_PAD), row(N_HEADS * V_DIM)]
    return pl.pallas_call(
        body, grid=(t // tm,), in_specs=in_specs, out_specs=out_specs, out_shape=out_shape,
        compiler_params=_params("parallel", vmem=VMEM_LIMIT),
        name="inproj_sample" if sample else "inproj_prompt")(*args)


def _ln_swish(conv, g, b):
    mu = jnp.mean(conv, axis=-1, keepdims=True)
    xc = conv - mu
    var = jnp.mean(xc * xc, axis=-1, keepdims=True)
    y = xc * lax.rsqrt(var + EPS) * g + b
    return y * jax.nn.sigmoid(y)


def _conv_prompt_kernel(main_ref, halo_ref, prev_ref, w_ref, b_ref, g_ref, lb_ref, o_ref, win_ref):
    tt = main_ref.shape[1]
    first = pl.program_id(1) == 0
    win_ref[0, 0:CONV_HALO, :] = jnp.where(first, prev_ref[0], halo_ref[0])
    win_ref[0, CONV_HALO:CONV_HALO + tt, :] = main_ref[0]
    n = tt + CONV_HALO - SUBLANES
    for s in range(1, SUBLANES):
        win_ref[s, 0:n, :] = win_ref[0, s:s + n, :]
    for c in range(tt // CONV_CHUNK):
        acc = None
        for j in range(CONV_WIDTH):
            q, s = divmod(c * CONV_CHUNK + 2 + j, SUBLANES)
            term = win_ref[s, q * SUBLANES:q * SUBLANES + CONV_CHUNK, :] * w_ref[j:j + 1, :]
            acc = term if acc is None else acc + term
        y = _ln_swish(acc + b_ref[...], g_ref[...], lb_ref[...])
        o_ref[0, c * CONV_CHUNK:(c + 1) * CONV_CHUNK, :] = y


def _conv_prompt(u3, prev_pad, w):
    b, s, _ = u3.shape
    tt = CONV_TILE
    ratio = tt // CONV_HALO
    return pl.pallas_call(
        _conv_prompt_kernel, grid=(b, s // tt),
        in_specs=[pl.BlockSpec((1, tt, CONV_CH), lambda bi, i: (bi, i, 0)),
                  pl.BlockSpec((1, CONV_HALO, CONV_CH), lambda bi, i: (bi, jnp.maximum(i * ratio - 1, 0), 0)),
                  pl.BlockSpec((1, CONV_HALO, CONV_CH), lambda bi, i: (bi, 0, 0)),
                  _const_spec((CONV_HALO, CONV_CH)), _const_spec((1, CONV_CH)),
                  _const_spec((1, CONV_CH)), _const_spec((1, CONV_CH))],
        out_specs=pl.BlockSpec((1, tt, CONV_CH), lambda bi, i: (bi, i, 0)),
        out_shape=jax.ShapeDtypeStruct((b, s, CONV_CH), F32),
        scratch_shapes=[pltpu.VMEM((SUBLANES, tt + CONV_HALO, CONV_CH), F32)],
        compiler_params=_params("parallel", "parallel"),
        name="conv_prompt")(u3, u3, prev_pad, w['conv_w'], w['conv_b'], w['conv_ln_g'], w['conv_ln_b'])


def _conv_sample_kernel(win_ref, w_ref, b_ref, g_ref, lb_ref, o_ref):
    t = o_ref.shape[1]
    acc = win_ref[:, 2:2 + t, :] * w_ref[0:1, :]
    for j in range(1, CONV_WIDTH):
        acc = acc + win_ref[:, 2 + j:2 + j + t, :] * w_ref[j:j + 1, :]
    y = _ln_swish(acc + b_ref[...], g_ref[...], lb_ref[...])
    o_ref[...] = y


def _conv_sample(upad, w):
    b, s_pad, _ = upad.shape
    t = s_pad - CONV_HALO
    bb = 8
    return pl.pallas_call(
        _conv_sample_kernel, grid=(b // bb,),
        in_specs=[pl.BlockSpec((bb, s_pad, CONV_CH), lambda i: (i, 0, 0)),
                  _const_spec((CONV_HALO, CONV_CH)), _const_spec((1, CONV_CH)),
                  _const_spec((1, CONV_CH)), _const_spec((1, CONV_CH))],
        out_specs=pl.BlockSpec((bb, t, CONV_CH), lambda i: (i, 0, 0)),
        out_shape=jax.ShapeDtypeStruct((b, t, CONV_CH), F32),
        compiler_params=_params("parallel"),
        name="conv_sample")(upad, w['conv_w'], w['conv_b'], w['conv_ln_g'], w['conv_ln_b'])


def _softmax_step(s, m, l):
    m_new = jnp.maximum(m, jnp.max(s, axis=-1, keepdims=True))
    alpha = jnp.exp(m - m_new)
    p = jnp.exp(s - m_new)
    return m_new, alpha, p, alpha * l + jnp.sum(p, axis=-1, keepdims=True)


def _attn_prompt_kernel(q_ref, k_ref, v_ref, o_ref):
    tq = q_ref.shape[1]
    qi = pl.program_id(2)
    nh = ATTN_HEADS
    head = lambda hd: slice(hd * HEAD_PAD, (hd + 1) * HEAD_PAD)
    pair = lambda hd: slice((hd // 2) * LANES, (hd // 2 + 1) * LANES)
    qs = [q_ref[0, :, head(hd)] for hd in range(nh)]

    def tile(j, carry, masked):
        start = pl.multiple_of(j * tq, tq)
        if masked:
            rows = lax.broadcasted_iota(jnp.int32, (tq, tq), 0)
            cols = lax.broadcasted_iota(jnp.int32, (tq, tq), 1)
            keep = cols <= rows
        new = []
        for hd in range(nh):
            m, l, acc = carry[hd]
            s = _dot_t(qs[hd], k_ref[0, pl.ds(start, tq), head(hd)])
            if masked:
                s = jnp.where(keep, s, NEG_INF)
            m, alpha, p, l = _softmax_step(s, m, l)
            acc = alpha * acc + _dot(p.astype(BF16), v_ref[0, pl.ds(start, tq), pair(hd)])
            new.append((m, l, acc))
        return tuple(new)

    init = tuple((jnp.full((tq, 1), NEG_INF, F32), jnp.zeros((tq, 1), F32), jnp.zeros((tq, LANES), F32))
                 for _ in range(nh))
    carry = lax.fori_loop(0, qi, functools.partial(tile, masked=False), init)
    carry = tile(qi, carry, True)
    outs = [acc / l for (_, l, acc) in carry]
    lane = lax.broadcasted_iota(jnp.int32, outs[0].shape, 1)
    for hp in range(nh // 2):
        o_ref[0, :, hp * LANES:(hp + 1) * LANES] = jnp.where(
            lane < V_DIM, outs[2 * hp], outs[2 * hp + 1]).astype(BF16)


def _attn_prompt(q, k, v):
    b, s, _ = q.shape
    tq = ATTN_TILE
    nh = ATTN_HEADS
    return pl.pallas_call(
        _attn_prompt_kernel, grid=(b, N_HEADS // nh, s // tq),
        in_specs=[pl.BlockSpec((1, tq, nh * HEAD_PAD), lambda bi, hq, qi: (bi, qi, hq)),
                  pl.BlockSpec((1, s, nh * HEAD_PAD), lambda bi, hq, qi: (bi, 0, hq)),
                  pl.BlockSpec((1, s, nh * V_DIM), lambda bi, hq, qi: (bi, 0, hq))],
        out_specs=pl.BlockSpec((1, tq, nh * V_DIM), lambda bi, hq, qi: (bi, qi, hq)),
        out_shape=jax.ShapeDtypeStruct((b, s, N_HEADS * V_DIM), BF16),
        compiler_params=_params("parallel", "parallel", "arbitrary"),
        name="attn_prompt")(q, k, v)


def _attn_sample_kernel(pt_ref, ql_ref, qp_ref, cn_ref, kn_ref, ckv_hbm, kpe_hbm, o_ref,
                        ckv_buf, kpe_buf, kb_ref, s_ref, sem):
    b = pl.program_id(0)
    n_pages = kpe_buf.shape[1]
    n_chunks = n_pages // PAGES_PER_CHUNK
    chunk = PAGES_PER_CHUNK * PAGE_SIZE
    t_new = cn_ref.shape[1]
    rows_q = N_HEADS * t_new

    def fetch(batch, slot):
        def one(p, _):
            page = pt_ref[batch * n_pages + p]
            rows = pl.ds(pl.multiple_of(p * PAGE_SIZE, PAGE_SIZE), PAGE_SIZE)
            pltpu.make_async_copy(ckv_hbm.at[page], ckv_buf.at[slot, rows], sem.at[0, slot]).start()
            pltpu.make_async_copy(kpe_hbm.at[page], kpe_buf.at[slot, p], sem.at[1, slot]).start()
            return 0
        lax.fori_loop(0, n_pages, one, 0)

    slot = b % 2

    @pl.when(b == 0)
    def _():
        fetch(0, 0)

    @pl.when(b + 1 < pl.num_programs(0))
    def _():
        fetch(b + 1, 1 - slot)

    pltpu.make_async_copy(ckv_buf.at[slot], ckv_buf.at[slot], sem.at[0, slot]).wait()
    pltpu.make_async_copy(kpe_buf.at[slot], kpe_buf.at[slot], sem.at[1, slot]).wait()

    ql = ql_ref[:, 0].reshape(rows_q, KV_RANK).astype(BF16)
    qp = qp_ref[:, 0].reshape(rows_q, QK_ROPE).astype(BF16)

    pad = PAGE_SIZE - t_new
    kc_new = jnp.concatenate([cn_ref[0], jnp.zeros((pad, KV_RANK), F32)], axis=0).astype(BF16)
    kp_new = jnp.concatenate([kn_ref[0], jnp.zeros((pad, QK_ROPE), F32)], axis=0).astype(BF16)
    s_new = _dot_t(ql, kc_new) + _dot_t(qp, kp_new)
    t_q = lax.broadcasted_iota(jnp.int32, s_new.shape, 0) % t_new
    cols = lax.broadcasted_iota(jnp.int32, s_new.shape, 1)
    s_new = jnp.where(cols <= t_q, s_new, NEG_INF)
    m = jnp.max(s_new, axis=-1, keepdims=True)

    for c in range(n_chunks):
        kc = ckv_buf[slot, c * chunk:(c + 1) * chunk, :].astype(BF16)
        kb_ref[c * chunk:(c + 1) * chunk, :] = kc
        kpt = jnp.concatenate([kpe_buf[slot, c * PAGES_PER_CHUNK + i] for i in range(PAGES_PER_CHUNK)],
                              axis=1).astype(BF16)
        s = _dot_t(ql, kc) + _dot(qp, kpt)
        s_ref[c] = s
        m = jnp.maximum(m, jnp.max(s, axis=-1, keepdims=True))

    p_new = jnp.exp(s_new - m)
    l = jnp.sum(p_new, axis=-1, keepdims=True)
    acc = _dot(p_new.astype(BF16), kc_new)
    for c in range(n_chunks):
        p = jnp.exp(s_ref[c] - m)
        l = l + jnp.sum(p, axis=-1, keepdims=True)
        acc = acc + _dot(p.astype(BF16), kb_ref[c * chunk:(c + 1) * chunk, :])
    o_ref[:, 0] = (acc / l).reshape(N_HEADS, t_new, KV_RANK)


def _attn_sample(page_table, qlat, qpe, ckv_new, kpe_new, cache_ckv, cache_kpe_t):
    bs, n_pages = page_table.shape
    t_new = ckv_new.shape[1]
    qlat4 = qlat.reshape(N_HEADS, bs, t_new, KV_RANK)
    qpe4 = qpe.reshape(N_HEADS, bs, t_new, QK_ROPE)
    rows_q = N_HEADS * t_new
    past = n_pages * PAGE_SIZE
    in_specs = [pl.BlockSpec((N_HEADS, 1, t_new, KV_RANK), lambda b, pt: (0, b, 0, 0)),
                pl.BlockSpec((N_HEADS, 1, t_new, QK_ROPE), lambda b, pt: (0, b, 0, 0)),
                pl.BlockSpec((1, t_new, KV_RANK), lambda b, pt: (b, 0, 0)),
                pl.BlockSpec((1, t_new, QK_ROPE), lambda b, pt: (b, 0, 0)),
                pl.BlockSpec(memory_space=pl.ANY), pl.BlockSpec(memory_space=pl.ANY)]
    grid_spec = pltpu.PrefetchScalarGridSpec(
        num_scalar_prefetch=1, grid=(bs,), in_specs=in_specs,
        out_specs=pl.BlockSpec((N_HEADS, 1, t_new, KV_RANK), lambda b, pt: (0, b, 0, 0)),
        scratch_shapes=[pltpu.VMEM((2, past, KV_RANK), F32),
                        pltpu.VMEM((2, n_pages, QK_ROPE, PAGE_SIZE), F32),
                        pltpu.VMEM((past, KV_RANK), BF16),
                        pltpu.VMEM((n_pages // PAGES_PER_CHUNK, rows_q, PAGES_PER_CHUNK * PAGE_SIZE), F32),
                        pltpu.SemaphoreType.DMA((2, 2))])
    o = pl.pallas_call(
        _attn_sample_kernel, grid_spec=grid_spec,
        out_shape=jax.ShapeDtypeStruct((N_HEADS, bs, t_new, KV_RANK), F32),
        compiler_params=_params("arbitrary", vmem=VMEM_LIMIT),
        name="attn_sample")(page_table.reshape(-1), qlat4, qpe4, ckv_new, kpe_new, cache_ckv, cache_kpe_t)
    return o.reshape(N_HEADS, bs * t_new, KV_RANK)


def _post_a_kernel(sample, x_ref, conv_ref, attn_ref, *rest):
    if sample:
        wuv_ref, wout_ref, g_ref, wmq_ref, x1_ref, qm_ref = rest
        attn = None
        for hd in range(N_HEADS):
            d = _dot(attn_ref[hd].astype(BF16), wuv_ref[hd])
            attn = d if attn is None else attn + d
        attn = attn.astype(BF16)
    else:
        wout_ref, g_ref, wmq_ref, x1_ref, qm_ref = rest
        attn = attn_ref[...]
    mix = _dot(conv_ref[...].astype(BF16), wout_ref[0:CONV_CH, :]) + _dot(attn, wout_ref[CONV_CH:, :])
    x1 = x_ref[...] + mix
    x1_ref[...] = x1
    qm_ref[...] = _dot(_rms(x1, g_ref[...]).astype(BF16), wmq_ref[...])


def _post_a(x2d, conv2d, attn, w, sample):
    t = x2d.shape[0]
    tm = TOKEN_TILE
    row = lambda n: pl.BlockSpec((tm, n), lambda i: (i, 0))
    mem_w = MEM_HEADS * MEM_HD
    if sample:
        in_specs = [row(D_MODEL), row(CONV_CH), pl.BlockSpec((N_HEADS, tm, KV_RANK), lambda i: (0, i, 0)),
                    _const_spec((N_HEADS, KV_RANK, N_HEADS * V_DIM))]
        args = [x2d, conv2d, attn, w['wuv_pad']]
    else:
        in_specs = [row(D_MODEL), row(CONV_CH), row(N_HEADS * V_DIM)]
        args = [x2d, conv2d, attn]
    in_specs += [_const_spec((D_MODEL, D_MODEL)), _const_spec((1, D_MODEL)), _const_spec((D_MODEL, mem_w))]
    args += [w['w_out'], w['g_mem_q'], w['w_mq']]
    return pl.pallas_call(
        functools.partial(_post_a_kernel, sample), grid=(t // tm,), in_specs=in_specs,
        out_specs=[row(D_MODEL), row(mem_w)],
        out_shape=[jax.ShapeDtypeStruct((t, D_MODEL), F32), jax.ShapeDtypeStruct((t, mem_w), F32)],
        compiler_params=_params("parallel", vmem=VMEM_LIMIT),
        name="post_a_sample" if sample else "post_a_prompt")(*args)


def _mem_kv_kernel(m_ref, g_ref, wk_ref, wv_ref, k_ref, v_ref):
    tm = m_ref.shape[0]
    m = _rms(m_ref[...], g_ref[...]).astype(BF16)
    k = _dot(m, wk_ref[...])
    v = _dot(m, wv_ref[...])
    for hd in range(MEM_HEADS):
        sl = slice(hd * MEM_HD, (hd + 1) * MEM_HD)
        k_ref[pl.ds(hd, tm, stride=MEM_HEADS), :] = k[:, sl]
        v_ref[pl.ds(hd, tm, stride=MEM_HEADS), :] = v[:, sl]


def _mem_kv(mem2d, w):
    t = mem2d.shape[0]
    tm = TOKEN_TILE
    mem_w = MEM_HEADS * MEM_HD
    out = pl.BlockSpec((tm * MEM_HEADS, MEM_HD), lambda i: (i, 0))
    return pl.pallas_call(
        _mem_kv_kernel, grid=(t // tm,),
        in_specs=[pl.BlockSpec((tm, D_MODEL), lambda i: (i, 0)), _const_spec((1, D_MODEL)),
                  _const_spec((D_MODEL, mem_w)), _const_spec((D_MODEL, mem_w))],
        out_specs=[out, out],
        out_shape=[jax.ShapeDtypeStruct((t * MEM_HEADS, MEM_HD), F32)] * 2,
        compiler_params=_params("parallel"),
        name="mem_kv")(mem2d, w['g_mem_kv'], w['w_mk'], w['w_mv'])


def _mem_attn_kernel(q_ref, k_ref, v_ref, o_ref):
    rows = MEM_TOKENS * MEM_HEADS
    for bi in range(q_ref.shape[0]):
        for hd in range(MEM_HEADS):
            sl = slice(hd * MEM_HD, (hd + 1) * MEM_HD)
            kh = k_ref[pl.ds(bi * rows + hd, MEM_TOKENS, stride=MEM_HEADS), :].astype(BF16)
            vh = v_ref[pl.ds(bi * rows + hd, MEM_TOKENS, stride=MEM_HEADS), :].astype(BF16)
            s = _dot_t(q_ref[bi, :, sl].astype(BF16), kh) * MEM_SCALE
            p = jnp.exp(s - jnp.max(s, axis=-1, keepdims=True))
            p = p / jnp.sum(p, axis=-1, keepdims=True)
            o_ref[bi, :, sl] = _dot(p.astype(BF16), vh)


def _mem_attn(qm, mem_k, mem_v):
    b, s, mem_w = qm.shape
    tq = min(TOKEN_TILE, s)
    bb = max(1, MEM_ATTN_ROWS // tq)
    kv = pl.BlockSpec((bb * MEM_TOKENS * MEM_HEADS, MEM_HD), lambda bi, i: (bi, 0))
    return pl.pallas_call(
        _mem_attn_kernel, grid=(b // bb, s // tq),
        in_specs=[pl.BlockSpec((bb, tq, mem_w), lambda bi, i: (bi, i, 0)), kv, kv],
        out_specs=pl.BlockSpec((bb, tq, mem_w), lambda bi, i: (bi, i, 0)),
        out_shape=jax.ShapeDtypeStruct((b, s, mem_w), F32),
        compiler_params=_params("parallel", "parallel"),
        name="mem_attn")(qm, mem_k, mem_v)


def _slab_rows(j, n):
    return pl.ds(j, n, stride=SLAB)


def _post_b_kernel(n_p, x1p_ref, x1s_ref, omp_ref, oms_ref, wmo_ref, g_ref, wr_ref, br_ref,
                   x2_ref, h3_ref, idx_ref, gate_ref, rank_ref, cnt_ref, carry_ref):
    tm = x2_ref.shape[0]
    i = pl.program_id(0)

    @pl.when(i == 0)
    def _():
        carry_ref[...] = jnp.zeros(carry_ref.shape, F32)

    is_p = i < n_p
    x1 = jnp.where(is_p, x1p_ref[...], x1s_ref[...])
    om = jnp.where(is_p, omp_ref[...], oms_ref[...])
    x2 = x1 + _dot(om.astype(BF16), wmo_ref[...])
    x2_ref[...] = x2
    h3 = _rms(x2, g_ref[...])
    for j in range(SLAB):
        h3_ref[_slab_rows(j, tm), :] = h3[:, j * LANES:(j + 1) * LANES]
    logits = _dot(h3.astype(BF16), wr_ref[...]) + br_ref[...]
    lane = lax.broadcasted_iota(jnp.int32, logits.shape, 1)
    lane_f = lane.astype(F32)
    vals, hots = [], []
    idx_out = jnp.zeros(logits.shape, F32)
    for kk in range(TOP_K):
        mx = jnp.max(logits, axis=-1, keepdims=True)
        first = jnp.min(jnp.where(logits == mx, lane_f, float(LANES)), axis=-1, keepdims=True)
        hot = lane_f == first
        logits = jnp.where(hot, -jnp.inf, logits)
        vals.append(mx)
        hots.append(hot)
        idx_out = jnp.where(lane == kk, first, idx_out)
    exps = [jnp.exp(v - vals[0]) for v in vals]
    denom = exps[0] + exps[1] + exps[2] + exps[3]
    chosen = jnp.zeros(logits.shape, F32)
    gate_out = jnp.zeros(logits.shape, F32)
    for kk in range(TOP_K):
        chosen = chosen + hots[kk].astype(F32)
        gate_out = jnp.where(lane == kk, exps[kk] / denom, gate_out)
    r_i = lax.broadcasted_iota(jnp.int32, (tm, tm), 0)
    c_i = lax.broadcasted_iota(jnp.int32, (tm, tm), 1)
    tril = (c_i < r_i).astype(BF16)
    before = _dot(tril, chosen.astype(BF16)) + carry_ref[...]
    rank_out = jnp.zeros(logits.shape, F32)
    for kk in range(TOP_K):
        rk = jnp.sum(jnp.where(hots[kk], before, 0.0), axis=-1, keepdims=True)
        rank_out = jnp.where(lane == kk, rk, rank_out)
    carry = carry_ref[...] + jnp.sum(chosen, axis=0, keepdims=True)
    carry_ref[...] = carry
    idx_ref[...] = idx_out.astype(jnp.int32)
    gate_ref[...] = gate_out
    rank_ref[...] = rank_out.astype(jnp.int32)
    cnt_ref[...] = carry.astype(jnp.int32)


def _post_b(x1_p, om_p, x1_s, om_s, w):
    tm = TOKEN_TILE
    n_p, n_s = x1_p.shape[0] // tm, x1_s.shape[0] // tm
    t = (n_p + n_s) * tm
    mem_w = MEM_HEADS * MEM_HD
    row = lambda n: pl.BlockSpec((tm, n), lambda i: (i, 0))
    row_p = lambda n: pl.BlockSpec((tm, n), lambda i: (jnp.minimum(i, n_p - 1), 0))
    row_s = lambda n: pl.BlockSpec((tm, n), lambda i: (jnp.maximum(i - n_p, 0), 0))
    return pl.pallas_call(
        functools.partial(_post_b_kernel, n_p), grid=(n_p + n_s,),
        in_specs=[row_p(D_MODEL), row_s(D_MODEL), row_p(mem_w), row_s(mem_w),
                  _const_spec((mem_w, D_MODEL)), _const_spec((1, D_MODEL)),
                  _const_spec((D_MODEL, LANES)), _const_spec((1, LANES))],
        out_specs=[row(D_MODEL), pl.BlockSpec((tm * SLAB, LANES), lambda i: (i, 0)),
                   row(LANES), row(LANES), row(LANES), _const_spec((1, LANES))],
        out_shape=[jax.ShapeDtypeStruct((t, D_MODEL), F32), jax.ShapeDtypeStruct((t * SLAB, LANES), F32),
                   jax.ShapeDtypeStruct((t, LANES), jnp.int32), jax.ShapeDtypeStruct((t, LANES), F32),
                   jax.ShapeDtypeStruct((t, LANES), jnp.int32), jax.ShapeDtypeStruct((1, LANES), jnp.int32)],
        scratch_shapes=[pltpu.VMEM((1, LANES), F32)],
        compiler_params=_params("arbitrary", vmem=VMEM_LIMIT),
        name="post_b")(x1_p, x1_s, om_p, om_s, w['w_mo'], w['g_ffn'], w['w_router'], w['b_router'])


def _slab(row):
    return pl.ds(pl.multiple_of(row * SLAB, SLAB), SLAB)


def _dispatch_kernel(start_ref, lo_ref, hi_ref, idx_ref, rank_ref, h_ref, xs_hbm, zero_ref, sem):
    i = pl.program_id(0)
    tm = h_ref.shape[0] // SLAB

    def issue(t, _):
        for k in range(TOP_K):
            r = t * TOP_K + k
            d = start_ref[idx_ref[r]] + rank_ref[r]
            pltpu.make_async_copy(h_ref.at[_slab(t)], xs_hbm.at[_slab(d)], sem).start(priority=k % 2)
        return 0

    lax.fori_loop(0, tm, issue, 0, unroll=2)
    for _ in range(TOP_K):
        pltpu.make_async_copy(h_ref, xs_hbm.at[pl.ds(0, tm * SLAB)], sem).wait()

    @pl.when(i == pl.num_programs(0) - 1)
    def _():
        zero_ref[...] = jnp.zeros(zero_ref.shape, F32)
        for e in range(N_EXPERTS):
            lo, hi = lo_ref[e], hi_ref[e]

            def zissue(r, _):
                pltpu.make_async_copy(zero_ref.at[_slab(0)], xs_hbm.at[_slab(r)], sem).start()
                return 0

            def zdrain(r, _):
                pltpu.make_async_copy(zero_ref.at[_slab(0)], xs_hbm.at[_slab(0)], sem).wait()
                return 0

            lax.fori_loop(lo, hi, zissue, 0)
            lax.fori_loop(lo, hi, zdrain, 0)

        bm = zero_ref.shape[0] // SLAB

        def tail(blk, _):
            rows = pl.ds(pl.multiple_of(blk * bm * SLAB, SLAB), bm * SLAB)
            cp = pltpu.make_async_copy(zero_ref, xs_hbm.at[rows], sem)
            cp.start()
            cp.wait()
            return 0

        lax.fori_loop(hi_ref[N_EXPERTS - 1] // bm, xs_hbm.shape[0] // (bm * SLAB), tail, 0)


def _dispatch(h3s, idx_flat, rank_flat, pad_start, pad_lo, pad_hi, n_rows):
    tm = TOKEN_TILE
    t = h3s.shape[0] // SLAB
    grid_spec = pltpu.PrefetchScalarGridSpec(
        num_scalar_prefetch=3, grid=(t // tm,),
        in_specs=[pl.BlockSpec((tm * TOP_K,), lambda i, *_: (i,), memory_space=pltpu.SMEM),
                  pl.BlockSpec((tm * TOP_K,), lambda i, *_: (i,), memory_space=pltpu.SMEM),
                  pl.BlockSpec((tm * SLAB, LANES), lambda i, *_: (i, 0))],
        out_specs=pl.BlockSpec(memory_space=pl.ANY),
        scratch_shapes=[pltpu.VMEM((MOE_ROWS * SLAB, LANES), F32), pltpu.SemaphoreType.DMA])
    return pl.pallas_call(
        _dispatch_kernel, grid_spec=grid_spec,
        out_shape=jax.ShapeDtypeStruct((n_rows * SLAB, LANES), F32),
        compiler_params=pltpu.CompilerParams(dimension_semantics=("arbitrary",), has_side_effects=True),
        name="moe_dispatch")(pad_start, pad_lo, pad_hi, idx_flat, rank_flat, h3s)


def _expert_kernel(src_ref, exp_ref, first_ref, valid_ref, x_ref, wg_ref, wu_ref, wd_ref,
                   bg_ref, bu_ref, bd_ref, y_ref, wgb_ref, wub_ref, wdb_ref, xb_ref):
    b = pl.program_id(0)
    bm = xb_ref.shape[0]

    @pl.when(first_ref[b] == 1)
    def _():
        wgb_ref[...] = wg_ref[0].astype(BF16)
        wub_ref[...] = wu_ref[0].astype(BF16)
        wdb_ref[...] = wd_ref[0].astype(BF16)

    @pl.when(valid_ref[b] == 1)
    def _():
        for j in range(SLAB):
            xb_ref[:, j * LANES:(j + 1) * LANES] = x_ref[_slab_rows(j, bm), :].astype(BF16)
        x = xb_ref[...]
        g = _dot(x, wgb_ref[...]) + bg_ref[0]
        u = _dot(x, wub_ref[...]) + bu_ref[0]
        g = jnp.minimum(g, SWIGLU_LIMIT)
        u = jnp.clip(u, -SWIGLU_LIMIT, SWIGLU_LIMIT)
        a = (u + 1.0) * (g * jax.nn.sigmoid(SWIGLU_ALPHA * g))
        y = _dot(a.astype(BF16), wdb_ref[...]) + bd_ref[0]
        for j in range(SLAB):
            y_ref[_slab_rows(j, bm), :] = y[:, j * LANES:(j + 1) * LANES]

    @pl.when(valid_ref[b] == 0)
    def _():
        y_ref[...] = jnp.zeros(y_ref.shape, F32)


def _experts(xs, blk_src, blk_exp, blk_first, blk_valid, w):
    bm = MOE_ROWS
    n_blocks = xs.shape[0] // (bm * SLAB)
    wspec = pl.BlockSpec((1, D_MODEL, D_MODEL), lambda b, src, ex, fi, va: (ex[b], 0, 0))
    bspec = pl.BlockSpec((1, 1, D_MODEL), lambda b, src, ex, fi, va: (ex[b], 0, 0))
    grid_spec = pltpu.PrefetchScalarGridSpec(
        num_scalar_prefetch=4, grid=(n_blocks,),
        in_specs=[pl.BlockSpec((bm * SLAB, LANES), lambda b, src, ex, fi, va: (src[b], 0)),
                  wspec, wspec, wspec, bspec, bspec, bspec],
        out_specs=pl.BlockSpec((bm * SLAB, LANES), lambda b, src, ex, fi, va: (b, 0)),
        scratch_shapes=[pltpu.VMEM((D_MODEL, D_MODEL), BF16)] * 3 + [pltpu.VMEM((bm, D_MODEL), BF16)])
    return pl.pallas_call(
        _expert_kernel, grid_spec=grid_spec,
        out_shape=jax.ShapeDtypeStruct(xs.shape, F32),
        compiler_params=_params("arbitrary", vmem=56 * 1024 * 1024),
        name="moe_experts")(blk_src, blk_exp, blk_first, blk_valid, xs,
                            w['w_gate'], w['w_up'], w['w_down'], w['b_gate'], w['b_up'], w['b_down'])


def _combine_kernel(n_p, start_ref, idx_ref, rank_ref, yb_hbm, x2_ref, gate_ref, g_ref, yp_ref, ys_ref,
                    rows_ref, y_ref, sem):
    i = pl.program_id(0)
    tm = x2_ref.shape[0]

    def issue(t, _):
        for k in range(TOP_K):
            r = t * TOP_K + k
            d = start_ref[idx_ref[r]] + rank_ref[r]
            pltpu.make_async_copy(yb_hbm.at[_slab(d)], rows_ref.at[_slab(r)], sem).start(priority=k % 2)
        return 0

    lax.fori_loop(0, tm, issue, 0, unroll=2)
    for _ in range(TOP_K):
        pltpu.make_async_copy(yb_hbm.at[pl.ds(0, tm * SLAB)], rows_ref.at[pl.ds(0, tm * SLAB)], sem).wait()

    gate = gate_ref[...]
    ss = jnp.zeros((tm, 1), F32)
    for j in range(SLAB):
        y = x2_ref[:, j * LANES:(j + 1) * LANES]
        for k in range(TOP_K):
            y = y + rows_ref[pl.ds(k * SLAB + j, tm, stride=TOP_K * SLAB), :] * gate[:, k:k + 1]
        y_ref[:, j * LANES:(j + 1) * LANES] = y
        ss = ss + jnp.sum(y * y, axis=-1, keepdims=True)
    out = y_ref[...] * lax.rsqrt(ss * (1.0 / D_MODEL) + EPS) * g_ref[...]

    @pl.when(i < n_p)
    def _():
        yp_ref[...] = out

    @pl.when(i >= n_p)
    def _():
        ys_ref[...] = out


def _combine(yb, idx_flat, rank_flat, pad_start, x2, gate128, g_final, n_p):
    t = x2.shape[0]
    tm = TOKEN_TILE
    n_s = t // tm - n_p
    grid_spec = pltpu.PrefetchScalarGridSpec(
        num_scalar_prefetch=1, grid=(t // tm,),
        in_specs=[pl.BlockSpec((tm * TOP_K,), lambda i, *_: (i,), memory_space=pltpu.SMEM),
                  pl.BlockSpec((tm * TOP_K,), lambda i, *_: (i,), memory_space=pltpu.SMEM),
                  pl.BlockSpec(memory_space=pl.ANY),
                  pl.BlockSpec((tm, D_MODEL), lambda i, *_: (i, 0)),
                  pl.BlockSpec((tm, LANES), lambda i, *_: (i, 0)),
                  pl.BlockSpec((1, D_MODEL), lambda i, *_: (0, 0))],
        out_specs=[pl.BlockSpec((tm, D_MODEL), lambda i, *_: (jnp.minimum(i, n_p - 1), 0)),
                   pl.BlockSpec((tm, D_MODEL), lambda i, *_: (jnp.maximum(i - n_p, 0), 0))],
        scratch_shapes=[pltpu.VMEM((tm * TOP_K * SLAB, LANES), F32), pltpu.VMEM((tm, D_MODEL), F32),
                        pltpu.SemaphoreType.DMA])
    return pl.pallas_call(
        functools.partial(_combine_kernel, n_p), grid_spec=grid_spec,
        out_shape=[jax.ShapeDtypeStruct((n_p * tm, D_MODEL), F32), jax.ShapeDtypeStruct((n_s * tm, D_MODEL), F32)],
        compiler_params=_params("arbitrary"),
        name="moe_combine")(pad_start, idx_flat, rank_flat, yb, x2, gate128, g_final)


def _moe_and_final(x2, h3s, idx128, gate128, rank128, cnt128, w, n_p):
    t = x2.shape[0]
    bm = MOE_ROWS
    n_assign = t * TOP_K
    n_blocks = (n_assign + N_EXPERTS * (bm - 1) + bm - 1) // bm
    counts = cnt128[0, :N_EXPERTS]
    padded = ((counts + bm - 1) // bm) * bm
    pad_end = jnp.cumsum(padded).astype(jnp.int32)
    pad_start = pad_end - padded
    idx_flat = idx128[:, :TOP_K].reshape(-1)
    rank_flat = rank128[:, :TOP_K].reshape(-1)
    blk = jnp.arange(n_blocks, dtype=jnp.int32)
    blk_valid = (blk * bm < pad_end[-1]).astype(jnp.int32)
    blk_src = jnp.minimum(blk, jnp.maximum(pad_end[-1] // bm - 1, 0))
    blk_exp = jnp.sum((pad_end[None, :] <= (blk_src * bm)[:, None]).astype(jnp.int32), axis=1)
    blk_exp = jnp.minimum(blk_exp, N_EXPERTS - 1)
    blk_first = jnp.concatenate([jnp.ones((1,), jnp.int32),
                                 (blk_exp[1:] != blk_exp[:-1]).astype(jnp.int32)])
    xs = _dispatch(h3s, idx_flat, rank_flat, pad_start, pad_start + counts, pad_end, n_blocks * bm)
    yb = _experts(xs, blk_src, blk_exp, blk_first, blk_valid, w)
    return _combine(yb, idx_flat, rank_flat, pad_start, x2, gate128, w['g_final'], n_p)


def _swap_halves(wcols):
    half = QK_ROPE // 2
    return jnp.concatenate([-wcols[..., half:], wcols[..., :half]], axis=-1)


def _prep_weights(g_mix, w_in, conv_w, conv_b, conv_ln_g, conv_ln_b, q_norm_g, w_q_up, kv_norm_g, w_kv_up,
                  w_out, g_mem_q, g_mem_kv, w_mq, w_mk, w_mv, w_mo, g_ffn, w_router, b_router,
                  w_gate, b_gate, w_up, b_up, w_down, b_down, g_final):
    l = 0
    w_kpe = w_in[l][:, C_KPE:]
    zpad = jnp.zeros((D_MODEL, HEAD_PAD - QK_ROPE), F32)
    w_in_ext = jnp.concatenate([w_in[l][:, :C_KPE], w_kpe, zpad, _swap_halves(w_kpe), zpad], axis=1)
    wq3 = w_q_up[l].reshape(Q_RANK, N_HEADS, QK_NOPE + QK_ROPE)
    q_nope, q_rope = wq3[..., :QK_NOPE], wq3[..., QK_NOPE:]
    z32 = jnp.zeros((Q_RANK, N_HEADS, HEAD_PAD - QK_NOPE - QK_ROPE), F32)
    wq = jnp.concatenate([q_rope, q_nope, z32], axis=-1).reshape(Q_RANK, N_HEADS * HEAD_PAD)
    wq_sw = jnp.concatenate([_swap_halves(q_rope), jnp.zeros_like(q_nope), z32], axis=-1)
    wq_sw = wq_sw.reshape(Q_RANK, N_HEADS * HEAD_PAD)
    w_uk = w_kv_up[l][:, :, :QK_NOPE]
    w_uv = w_kv_up[l][:, :, QK_NOPE:]
    wuk_pad = jnp.concatenate([jnp.zeros((KV_RANK, N_HEADS, QK_ROPE), F32), w_uk,
                               jnp.zeros((KV_RANK, N_HEADS, HEAD_PAD - QK_NOPE - QK_ROPE), F32)], axis=-1)
    wukt_pad = jnp.transpose(wuk_pad, (1, 2, 0))
    eye = jnp.eye(N_HEADS, dtype=F32)
    wuv_pad = (w_uv.transpose(1, 0, 2)[:, :, None, :] * eye[:, None, :, None])
    conv_w_pad = jnp.concatenate([conv_w[l], jnp.zeros((CONV_HALO - CONV_WIDTH, CONV_CH), F32)], axis=0)
    w_router_pad = jnp.concatenate([w_router[l], jnp.zeros((D_MODEL, LANES - N_EXPERTS), F32)], axis=1)
    b_router_pad = jnp.concatenate([b_router[l], jnp.full((LANES - N_EXPERTS,), NEG_INF, F32)])
    return {
        'g_mix': g_mix[l][None], 'w_in_ext': w_in_ext.astype(BF16),
        'q_norm_g': q_norm_g[l][None], 'wq': wq.astype(BF16), 'wq_sw': wq_sw.astype(BF16),
        'kv_norm_g': kv_norm_g[l][None],
        'wuk_pad': wuk_pad.reshape(KV_RANK, N_HEADS * HEAD_PAD).astype(BF16),
        'wuv': w_uv.reshape(KV_RANK, N_HEADS * V_DIM).astype(BF16),
        'wukt_pad': wukt_pad.astype(BF16),
        'wuv_pad': wuv_pad.reshape(N_HEADS, KV_RANK, N_HEADS * V_DIM).astype(BF16),
        'conv_w': conv_w_pad, 'conv_b': conv_b[l][None],
        'conv_ln_g': conv_ln_g[l][None], 'conv_ln_b': conv_ln_b[l][None],
        'w_out': w_out[l].astype(BF16), 'g_mem_q': g_mem_q[l][None], 'w_mq': w_mq[l].astype(BF16),
        'g_mem_kv': g_mem_kv[l][None], 'w_mk': w_mk[l].astype(BF16), 'w_mv': w_mv[l].astype(BF16),
        'w_mo': w_mo[l].astype(BF16), 'g_ffn': g_ffn[l][None],
        'w_router': w_router_pad.astype(BF16), 'b_router': b_router_pad[None],
        'w_gate': w_gate[l], 'w_up': w_up[l], 'w_down': w_down[l],
        'b_gate': b_gate[l][:, None, :], 'b_up': b_up[l][:, None, :], 'b_down': b_down[l][:, None, :],
        'g_final': g_final[None],
    }


def _rope_table(pos):
    half = QK_ROPE // 2
    inv = ROPE_THETA ** (-jnp.arange(half, dtype=F32) / half)
    ang = pos.astype(F32)[:, None] * inv[None, :]
    cos, sin = jnp.cos(ang), jnp.sin(ang)
    n = pos.shape[0]
    ones = jnp.ones((n, QK_NOPE), F32)
    z = lambda k: jnp.zeros((n, k), F32)
    cq = MLA_SCALE * jnp.concatenate([cos, cos, ones, z(HEAD_PAD - QK_NOPE - QK_ROPE)], axis=1)
    sq = MLA_SCALE * jnp.concatenate([sin, sin, z(HEAD_PAD - QK_ROPE)], axis=1)
    ck = jnp.concatenate([cos, cos, z(HEAD_PAD - QK_ROPE)], axis=1)
    sk = jnp.concatenate([sin, sin, z(HEAD_PAD - QK_ROPE)], axis=1)
    return jnp.stack([cq, sq, ck, sk])


def _front(x, conv_prev, mem_k, mem_v, pos, w, paged):
    b, s, _ = x.shape
    t = b * s
    x2d = x.reshape(t, D_MODEL)
    sample = paged is not None
    if sample:
        tab = _rope_table(jnp.tile(pos, TOKEN_TILE // s))
    else:
        tab = _rope_table(pos)
    outs = _inproj(x2d, tab, w, sample)
    u, ckv, kpe = outs[0], outs[1], outs[2]
    u3 = u.reshape(b, s, CONV_CH)
    tail = CONV_WIDTH - 1
    if s >= tail:
        conv_tail = u3[:, s - tail:]
    else:
        conv_tail = jnp.concatenate([conv_prev[:, s:], u3], axis=1)
    prev_pad = jnp.concatenate([jnp.zeros((b, CONV_HALO - tail, CONV_CH), F32), conv_prev], axis=1)
    ckv3 = ckv.reshape(b, s, KV_RANK)
    kpe3 = kpe.reshape(b, s, QK_ROPE)
    if sample:
        page_table, cache_ckv, cache_kpe_t = paged
        conv_out = _conv_sample(jnp.concatenate([prev_pad, u3], axis=1), w)
        attn = _attn_sample(page_table, outs[3], outs[4], ckv3, kpe3, cache_ckv, cache_kpe_t)
    else:
        conv_out = _conv_prompt(u3, prev_pad, w)
        q, k, v = (a.reshape(b, s, -1) for a in outs[3:6])
        attn = _attn_prompt(q, k, v).reshape(t, N_HEADS * V_DIM)
    x1, qm = _post_a(x2d, conv_out.reshape(t, CONV_CH), attn, w, sample)
    om = _mem_attn(qm.reshape(b, s, -1), mem_k, mem_v)
    return x1, om.reshape(t, -1), conv_tail, ckv3, kpe3


def kernel(x_prompt, x_sample, mem_prompt, cache_ckv, cache_kpe, page_table, cache_mem_k, cache_mem_v, state_conv, g_mix, w_in, conv_w, conv_b, conv_ln_g, conv_ln_b, q_norm_g, w_q_up, kv_norm_g, w_kv_up, w_out, g_mem_q, g_mem_kv, w_mq, w_mk, w_mv, w_mo, g_ffn, w_router, b_router, w_gate, b_gate, w_up, b_up, w_down, b_down, g_final):
    assert g_mix.shape[0] == 1, "single-layer step"
    w = _prep_weights(g_mix, w_in, conv_w, conv_b, conv_ln_g, conv_ln_b, q_norm_g, w_q_up, kv_norm_g, w_kv_up,
                      w_out, g_mem_q, g_mem_kv, w_mq, w_mk, w_mv, w_mo, g_ffn, w_router, b_router,
                      w_gate, b_gate, w_up, b_up, w_down, b_down, g_final)
    b_p, s_p, _ = x_prompt.shape
    b_s, t_s, _ = x_sample.shape
    past = page_table.shape[1] * PAGE_SIZE

    mk, mv = _mem_kv(mem_prompt.reshape(-1, D_MODEL), w)
    conv0 = jnp.zeros((b_p, CONV_WIDTH - 1, CONV_CH), F32)
    x1_p, om_p, conv_p, ckv_p, kpe_p = _front(x_prompt, conv0, mk, mv, jnp.arange(s_p, dtype=jnp.int32), w, None)
    x1_s, om_s, conv_s, ckv_s, kpe_s = _front(
        x_sample, state_conv[0], cache_mem_k[0].reshape(-1, MEM_HD), cache_mem_v[0].reshape(-1, MEM_HD),
        past + jnp.arange(t_s, dtype=jnp.int32), w,
        (page_table, cache_ckv[0], jnp.swapaxes(cache_kpe[0], 1, 2)))

    x2, h3s, idx128, gate128, rank128, cnt128 = _post_b(x1_p, om_p, x1_s, om_s, w)
    y_p, y_s = _moe_and_final(x2, h3s, idx128, gate128, rank128, cnt128, w, x1_p.shape[0] // TOKEN_TILE)

    mem_shape = (1, b_p, MEM_TOKENS, MEM_HEADS, MEM_HD)
    return (y_p.reshape(b_p, s_p, D_MODEL), y_s.reshape(b_s, t_s, D_MODEL), ckv_p[None], kpe_p[None],
            mk.reshape(mem_shape), mv.reshape(mem_shape), conv_p[None], ckv_s[None], kpe_s[None], conv_s[None])
```

```python
import functools

import jax
import jax.numpy as jnp
from jax import lax
from jax.experimental import pallas as pl
from jax.experimental.pallas import tpu as pltpu

F32 = jnp.float32
BF16 = jnp.bfloat16

D_MODEL = 1024
PAGE_SIZE = 128
CONV_CH = 512
CONV_WIDTH = 31
N_HEADS = 8
QK_NOPE = 64
QK_ROPE = 32
V_DIM = 64
Q_RANK = 384
KV_RANK = 256
ROPE_THETA = 10000.0
MLA_SCALE = (QK_NOPE + QK_ROPE) ** -0.5
MEM_TOKENS = 256
MEM_HEADS = 4
MEM_HD = 128
MEM_SCALE = MEM_HD ** -0.5
N_EXPERTS = 32
TOP_K = 4
SWIGLU_LIMIT = 7.0
SWIGLU_ALPHA = 1.702
EPS = 1e-6
NEG_INF = -1e30

LANES = 128
SUBLANES = 8
HEAD_PAD = 128
C_VAL, C_GATE, C_Q, C_CKV, C_KPE, C_KPE_SW, C_END = 0, 512, 1024, 1408, 1664, 1792, 1920
TOKEN_TILE = 256
ATTN_TILE = 256
ATTN_HEADS = 4
CONV_TILE = 256
CONV_HALO = 32
CONV_CHUNK = 32
MEM_ATTN_ROWS = 32
PAGES_PER_CHUNK = 8
SLAB = 8
MOE_ROWS = 256
CAST_ROWS = 32
VMEM_LIMIT = 48 * 1024 * 1024


def _rms(x, g):
    return x * lax.rsqrt(jnp.mean(x * x, axis=-1, keepdims=True) + EPS) * g


def _dot(a, b):
    return jnp.dot(a, b, preferred_element_type=F32)


def _dot_t(a, b):
    return lax.dot_general(a, b, (((1,), (1,)), ((), ())), preferred_element_type=F32)


def _params(*sem, vmem=None):
    return pltpu.CompilerParams(dimension_semantics=sem, vmem_limit_bytes=vmem)


def _const_spec(shape):
    nd = len(shape)
    return pl.BlockSpec(shape, lambda *_: (0,) * nd)


def _inproj_common(x_ref, gmix_ref, win_ref, qg_ref, wq_ref, wqsw_ref, kvg_ref, tab_ref,
                   u_ref, ckv_ref, kpe_ref):
    h = _rms(x_ref[...], gmix_ref[...]).astype(BF16)
    proj = _dot(h, win_ref[...])
    u_ref[...] = proj[:, C_VAL:C_GATE] * jax.nn.sigmoid(proj[:, C_GATE:C_Q])
    qn = _rms(proj[:, C_Q:C_CKV], qg_ref[...]).astype(BF16)
    ckv = _rms(proj[:, C_CKV:C_KPE], kvg_ref[...])
    ckv_ref[...] = ckv
    cq, sq, ck, sk = tab_ref[0], tab_ref[1], tab_ref[2], tab_ref[3]
    kpe_rot = proj[:, C_KPE:C_KPE_SW] * ck + proj[:, C_KPE_SW:C_END] * sk
    kpe_ref[...] = kpe_rot[:, :QK_ROPE]
    q = _dot(qn, wq_ref[...])
    qs = _dot(qn, wqsw_ref[...])
    q_heads = []
    for hd in range(N_HEADS):
        sl = slice(hd * HEAD_PAD, (hd + 1) * HEAD_PAD)
        q_heads.append(q[:, sl] * cq + qs[:, sl] * sq)
    return ckv, kpe_rot, q_heads


def _inproj_prompt_kernel(x_ref, gmix_ref, win_ref, qg_ref, wq_ref, wqsw_ref, kvg_ref, tab_ref,
                          wuk_ref, wuv_ref,
                          u_ref, ckv_ref, kpe_ref, q_ref, k_ref, v_ref):
    ckv, kpe_rot, q_heads = _inproj_common(x_ref, gmix_ref, win_ref, qg_ref, wq_ref, wqsw_ref,
                                           kvg_ref, tab_ref, u_ref, ckv_ref, kpe_ref)
    ckv_b = ckv.astype(BF16)
    k_nope = _dot(ckv_b, wuk_ref[...])
    for hd in range(N_HEADS):
        sl = slice(hd * HEAD_PAD, (hd + 1) * HEAD_PAD)
        q_ref[:, sl] = q_heads[hd].astype(BF16)
        k_ref[:, sl] = (k_nope[:, sl] + kpe_rot).astype(BF16)
    v_ref[...] = _dot(ckv_b, wuv_ref[...]).astype(BF16)


def _inproj_sample_kernel(x_ref, gmix_ref, win_ref, qg_ref, wq_ref, wqsw_ref, kvg_ref, tab_ref,
                          wukt_ref,
                          u_ref, ckv_ref, kpe_ref, qlat_ref, qpe_ref):
    _, _, q_heads = _inproj_common(x_ref, gmix_ref, win_ref, qg_ref, wq_ref, wqsw_ref,
                                   kvg_ref, tab_ref, u_ref, ckv_ref, kpe_ref)
    for hd in range(N_HEADS):
        qlat_ref[hd] = _dot(q_heads[hd].astype(BF16), wukt_ref[hd])
        qpe_ref[hd] = q_heads[hd][:, :QK_ROPE]


def _inproj(x2d, tab, w, sample):
    t = x2d.shape[0]
    tm = TOKEN_TILE
    n_tab = tab.shape[1] // tm
    row = lambda n: pl.BlockSpec((tm, n), lambda i: (i, 0))
    in_specs = [row(D_MODEL), _const_spec((1, D_MODEL)), _const_spec((D_MODEL, C_END)),
                _const_spec((1, Q_RANK)), _const_spec((Q_RANK, N_HEADS * HEAD_PAD)),
                _const_spec((Q_RANK, N_HEADS * HEAD_PAD)), _const_spec((1, KV_RANK)),
                pl.BlockSpec((4, tm, LANES), lambda i: (0, i % n_tab, 0))]
    args = [x2d, w['g_mix'], w['w_in_ext'], w['q_norm_g'], w['wq'], w['wq_sw'], w['kv_norm_g'], tab]
    out_shape = [jax.ShapeDtypeStruct((t, CONV_CH), F32), jax.ShapeDtypeStruct((t, KV_RANK), F32),
                 jax.ShapeDtypeStruct((t, QK_ROPE), F32)]
    out_specs = [row(CONV_CH), row(KV_RANK), row(QK_ROPE)]
    if sample:
        body = _inproj_sample_kernel
        in_specs += [_const_spec((N_HEADS, HEAD_PAD, KV_RANK))]
        args += [w['wukt_pad']]
        out_shape += [jax.ShapeDtypeStruct((N_HEADS, t, KV_RANK), F32),
                      jax.ShapeDtypeStruct((N_HEADS, t, QK_ROPE), F32)]
        out_specs += [pl.BlockSpec((N_HEADS, tm, KV_RANK), lambda i: (0, i, 0)),
                      pl.BlockSpec((N_HEADS, tm, QK_ROPE), lambda i: (0, i, 0))]
    else:
        body = _inproj_prompt_kernel
        in_specs += [_const_spec((KV_RANK, N_HEADS * HEAD_PAD)), _const_spec((KV_RANK, N_HEADS * V_DIM))]
        args += [w['wuk_pad'], w['wuv']]
        out_shape += [jax.ShapeDtypeStruct((t, N_HEADS * HEAD_PAD), BF16),
                      jax.ShapeDtypeStruct((t, N_HEADS * HEAD_PAD), BF16),
                      jax.ShapeDtypeStruct((t, N_HEADS * V_DIM), BF16)]
        out_specs += [row(N_HEADS * HEAD_PAD), row(N_HEADS * HEAD_PAD), row(N_HEADS * V_DIM)]
    return pl.pallas_call(
        body, grid=(t // tm,), in_specs=in_specs, out_specs=out_specs, out_shape=out_shape,
        compiler_params=_params("parallel", vmem=VMEM_LIMIT),
        name="inproj_sample" if sample else "inproj_prompt")(*args)


def _ln_swish(conv, g, b):
    mu = jnp.mean(conv, axis=-1, keepdims=True)
    xc = conv - mu
    var = jnp.mean(xc * xc, axis=-1, keepdims=True)
    y = xc * lax.rsqrt(var + EPS) * g + b
    return y * jax.nn.sigmoid(y)


def _conv_prompt_kernel(main_ref, halo_ref, prev_ref, w_ref, b_ref, g_ref, lb_ref, o_ref, win_ref):
    tt = main_ref.shape[1]
    first = pl.program_id(1) == 0
    win_ref[0, 0:CONV_HALO, :] = jnp.where(first, prev_ref[0], halo_ref[0])
    win_ref[0, CONV_HALO:CONV_HALO + tt, :] = main_ref[0]
    n = tt + CONV_HALO - SUBLANES
    for s in range(1, SUBLANES):
        win_ref[s, 0:n, :] = win_ref[0, s:s + n, :]
    for c in range(tt // CONV_CHUNK):
        acc = None
        for j in range(CONV_WIDTH):
            q, s = divmod(c * CONV_CHUNK + 2 + j, SUBLANES)
            term = win_ref[s, q * SUBLANES:q * SUBLANES + CONV_CHUNK, :] * w_ref[j:j + 1, :]
            acc = term if acc is None else acc + term
        y = _ln_swish(acc + b_ref[...], g_ref[...], lb_ref[...])
        o_ref[0, c * CONV_CHUNK:(c + 1) * CONV_CHUNK, :] = y


def _conv_prompt(u3, prev_pad, w):
    b, s, _ = u3.shape
    tt = CONV_TILE
    ratio = tt // CONV_HALO
    return pl.pallas_call(
        _conv_prompt_kernel, grid=(b, s // tt),
        in_specs=[pl.BlockSpec((1, tt, CONV_CH), lambda bi, i: (bi, i, 0)),
                  pl.BlockSpec((1, CONV_HALO, CONV_CH), lambda bi, i: (bi, jnp.maximum(i * ratio - 1, 0), 0)),
                  pl.BlockSpec((1, CONV_HALO, CONV_CH), lambda bi, i: (bi, 0, 0)),
                  _const_spec((CONV_HALO, CONV_CH)), _const_spec((1, CONV_CH)),
                  _const_spec((1, CONV_CH)), _const_spec((1, CONV_CH))],
        out_specs=pl.BlockSpec((1, tt, CONV_CH), lambda bi, i: (bi, i, 0)),
        out_shape=jax.ShapeDtypeStruct((b, s, CONV_CH), F32),
        scratch_shapes=[pltpu.VMEM((SUBLANES, tt + CONV_HALO, CONV_CH), F32)],
        compiler_params=_params("parallel", "parallel"),
        name="conv_prompt")(u3, u3, prev_pad, w['conv_w'], w['conv_b'], w['conv_ln_g'], w['conv_ln_b'])


def _conv_sample_kernel(win_ref, w_ref, b_ref, g_ref, lb_ref, o_ref):
    t = o_ref.shape[1]
    acc = win_ref[:, 2:2 + t, :] * w_ref[0:1, :]
    for j in range(1, CONV_WIDTH):
        acc = acc + win_ref[:, 2 + j:2 + j + t, :] * w_ref[j:j + 1, :]
    y = _ln_swish(acc + b_ref[...], g_ref[...], lb_ref[...])
    o_ref[...] = y


def _conv_sample(upad, w):
    b, s_pad, _ = upad.shape
    t = s_pad - CONV_HALO
    bb = 8
    return pl.pallas_call(
        _conv_sample_kernel, grid=(b // bb,),
        in_specs=[pl.BlockSpec((bb, s_pad, CONV_CH), lambda i: (i, 0, 0)),
                  _const_spec((CONV_HALO, CONV_CH)), _const_spec((1, CONV_CH)),
                  _const_spec((1, CONV_CH)), _const_spec((1, CONV_CH))],
        out_specs=pl.BlockSpec((bb, t, CONV_CH), lambda i: (i, 0, 0)),
        out_shape=jax.ShapeDtypeStruct((b, t, CONV_CH), F32),
        compiler_params=_params("parallel"),
        name="conv_sample")(upad, w['conv_w'], w['conv_b'], w['conv_ln_g'], w['conv_ln_b'])


def _softmax_step(s, m, l):
    m_new = jnp.maximum(m, jnp.max(s, axis=-1, keepdims=True))
    alpha = jnp.exp(m - m_new)
    p = jnp.exp(s - m_new)
    return m_new, alpha, p, alpha * l + jnp.sum(p, axis=-1, keepdims=True)


def _attn_prompt_kernel(q_ref, k_ref, v_ref, o_ref):
    tq = q_ref.shape[1]
    qi = pl.program_id(2)
    nh = ATTN_HEADS
    head = lambda hd: slice(hd * HEAD_PAD, (hd + 1) * HEAD_PAD)
    pair = lambda hd: slice((hd // 2) * LANES, (hd // 2 + 1) * LANES)
    qs = [q_ref[0, :, head(hd)] for hd in range(nh)]

    def tile(j, carry, masked):
        start = pl.multiple_of(j * tq, tq)
        if masked:
            rows = lax.broadcasted_iota(jnp.int32, (tq, tq), 0)
            cols = lax.broadcasted_iota(jnp.int32, (tq, tq), 1)
            keep = cols <= rows
        new = []
        for hd in range(nh):
            m, l, acc = carry[hd]
            s = _dot_t(qs[hd], k_ref[0, pl.ds(start, tq), head(hd)])
            if masked:
                s = jnp.where(keep, s, NEG_INF)
            m, alpha, p, l = _softmax_step(s, m, l)
            acc = alpha * acc + _dot(p.astype(BF16), v_ref[0, pl.ds(start, tq), pair(hd)])
            new.append((m, l, acc))
        return tuple(new)

    init = tuple((jnp.full((tq, 1), NEG_INF, F32), jnp.zeros((tq, 1), F32), jnp.zeros((tq, LANES), F32))
                 for _ in range(nh))
    carry = lax.fori_loop(0, qi, functools.partial(tile, masked=False), init)
    carry = tile(qi, carry, True)
    outs = [acc / l for (_, l, acc) in carry]
    lane = lax.broadcasted_iota(jnp.int32, outs[0].shape, 1)
    for hp in range(nh // 2):
        o_ref[0, :, hp * LANES:(hp + 1) * LANES] = jnp.where(
            lane < V_DIM, outs[2 * hp], outs[2 * hp + 1]).astype(BF16)


def _attn_prompt(q, k, v):
    b, s, _ = q.shape
    tq = ATTN_TILE
    nh = ATTN_HEADS
    return pl.pallas_call(
        _attn_prompt_kernel, grid=(b, N_HEADS // nh, s // tq),
        in_specs=[pl.BlockSpec((1, tq, nh * HEAD_PAD), lambda bi, hq, qi: (bi, qi, hq)),
                  pl.BlockSpec((1, s, nh * HEAD_PAD), lambda bi, hq, qi: (bi, 0, hq)),
                  pl.BlockSpec((1, s, nh * V_DIM), lambda bi, hq, qi: (bi, 0, hq))],
        out_specs=pl.BlockSpec((1, tq, nh * V_DIM), lambda bi, hq, qi: (bi, qi, hq)),
        out_shape=jax.ShapeDtypeStruct((b, s, N_HEADS * V_DIM), BF16),
        compiler_params=_params("parallel", "parallel", "arbitrary"),
        name="attn_prompt")(q, k, v)


def _attn_sample_kernel(pt_ref, ql_ref, qp_ref, cn_ref, kn_ref, ckv_hbm, kpe_hbm, o_ref,
                        ckv_buf, kpe_buf, kb_ref, s_ref, sem):
    b = pl.program_id(0)
    n_pages = kpe_buf.shape[1]
    n_chunks = n_pages // PAGES_PER_CHUNK
    chunk = PAGES_PER_CHUNK * PAGE_SIZE
    t_new = cn_ref.shape[1]
    rows_q = N_HEADS * t_new

    def fetch(batch, slot):
        def one(p, _):
            page = pt_ref[batch * n_pages + p]
            rows = pl.ds(pl.multiple_of(p * PAGE_SIZE, PAGE_SIZE), PAGE_SIZE)
            pltpu.make_async_copy(ckv_hbm.at[page], ckv_buf.at[slot, rows], sem.at[0, slot]).start()
            pltpu.make_async_copy(kpe_hbm.at[page], kpe_buf.at[slot, p], sem.at[1, slot]).start()
            return 0
        lax.fori_loop(0, n_pages, one, 0)

    slot = b % 2

    @pl.when(b == 0)
    def _():
        fetch(0, 0)

    @pl.when(b + 1 < pl.num_programs(0))
    def _():
        fetch(b + 1, 1 - slot)

    pltpu.make_async_copy(ckv_buf.at[slot], ckv_buf.at[slot], sem.at[0, slot]).wait()
    pltpu.make_async_copy(kpe_buf.at[slot], kpe_buf.at[slot], sem.at[1, slot]).wait()

    ql = ql_ref[:, 0].reshape(rows_q, KV_RANK).astype(BF16)
    qp = qp_ref[:, 0].reshape(rows_q, QK_ROPE).astype(BF16)

    pad = PAGE_SIZE - t_new
    kc_new = jnp.concatenate([cn_ref[0], jnp.zeros((pad, KV_RANK), F32)], axis=0).astype(BF16)
    kp_new = jnp.concatenate([kn_ref[0], jnp.zeros((pad, QK_ROPE), F32)], axis=0).astype(BF16)
    s_new = _dot_t(ql, kc_new) + _dot_t(qp, kp_new)
    t_q = lax.broadcasted_iota(jnp.int32, s_new.shape, 0) % t_new
    cols = lax.broadcasted_iota(jnp.int32, s_new.shape, 1)
    s_new = jnp.where(cols <= t_q, s_new, NEG_INF)
    m = jnp.max(s_new, axis=-1, keepdims=True)

    for c in range(n_chunks):
        kc = ckv_buf[slot, c * chunk:(c + 1) * chunk, :].astype(BF16)
        kb_ref[c * chunk:(c + 1) * chunk, :] = kc
        kpt = jnp.concatenate([kpe_buf[slot, c * PAGES_PER_CHUNK + i] for i in range(PAGES_PER_CHUNK)],
                              axis=1).astype(BF16)
        s = _dot_t(ql, kc) + _dot(qp, kpt)
        s_ref[c] = s
        m = jnp.maximum(m, jnp.max(s, axis=-1, keepdims=True))

    p_new = jnp.exp(s_new - m)
    l = jnp.sum(p_new, axis=-1, keepdims=True)
    acc = _dot(p_new.astype(BF16), kc_new)
    for c in range(n_chunks):
        p = jnp.exp(s_ref[c] - m)
        l = l + jnp.sum(p, axis=-1, keepdims=True)
        acc = acc + _dot(p.astype(BF16), kb_ref[c * chunk:(c + 1) * chunk, :])
    o_ref[:, 0] = (acc / l).reshape(N_HEADS, t_new, KV_RANK)


def _attn_sample(page_table, qlat, qpe, ckv_new, kpe_new, cache_ckv, cache_kpe_t):
    bs, n_pages = page_table.shape
    t_new = ckv_new.shape[1]
    qlat4 = qlat.reshape(N_HEADS, bs, t_new, KV_RANK)
    qpe4 = qpe.reshape(N_HEADS, bs, t_new, QK_ROPE)
    past = n_pages * PAGE_SIZE
    in_specs = [pl.BlockSpec((N_HEADS, 1, t_new, KV_RANK), lambda b, pt: (0, b, 0, 0)),
                pl.BlockSpec((N_HEADS, 1, t_new, QK_ROPE), lambda b, pt: (0, b, 0, 0)),
                pl.BlockSpec((1, t_new, KV_RANK), lambda b, pt: (b, 0, 0)),
                pl.BlockSpec((1, t_new, QK_ROPE), lambda b, pt: (b, 0, 0)),
                pl.BlockSpec(memory_space=pl.ANY), pl.BlockSpec(memory_space=pl.ANY)]
    grid_spec = pltpu.PrefetchScalarGridSpec(
        num_scalar_prefetch=1, grid=(bs,), in_specs=in_specs,
        out_specs=pl.BlockSpec((N_HEADS, 1, t_new, KV_RANK), lambda b, pt: (0, b, 0, 0)),
        scratch_shapes=[pltpu.VMEM((2, past, KV_RANK), F32),
                        pltpu.VMEM((2, n_pages, QK_ROPE, PAGE_SIZE), F32),
                        pltpu.VMEM((past, KV_RANK), BF16),
                        pltpu.VMEM((n_pages // PAGES_PER_CHUNK, N_HEADS * t_new, PAGES_PER_CHUNK * PAGE_SIZE), F32),
                        pltpu.SemaphoreType.DMA((2, 2))])
    o = pl.pallas_call(
        _attn_sample_kernel, grid_spec=grid_spec,
        out_shape=jax.ShapeDtypeStruct((N_HEADS, bs, t_new, KV_RANK), F32),
        compiler_params=_params("arbitrary", vmem=VMEM_LIMIT),
        name="attn_sample")(page_table.reshape(-1), qlat4, qpe4, ckv_new, kpe_new, cache_ckv, cache_kpe_t)
    return o.reshape(N_HEADS, bs * t_new, KV_RANK)


def _post_a_kernel(sample, x_ref, conv_ref, attn_ref, *rest):
    if sample:
        wuv_ref, wout_ref, g_ref, wmq_ref, x1_ref, qm_ref = rest
        attn = None
        for hd in range(N_HEADS):
            d = _dot(attn_ref[hd].astype(BF16), wuv_ref[hd])
            attn = d if attn is None else attn + d
        attn = attn.astype(BF16)
    else:
        wout_ref, g_ref, wmq_ref, x1_ref, qm_ref = rest
        attn = attn_ref[...]
    mix = _dot(conv_ref[...].astype(BF16), wout_ref[0:CONV_CH, :]) + _dot(attn, wout_ref[CONV_CH:, :])
    x1 = x_ref[...] + mix
    x1_ref[...] = x1
    qm_ref[...] = _dot(_rms(x1, g_ref[...]).astype(BF16), wmq_ref[...])


def _post_a(x2d, conv2d, attn, w, sample):
    t = x2d.shape[0]
    tm = TOKEN_TILE
    row = lambda n: pl.BlockSpec((tm, n), lambda i: (i, 0))
    mem_w = MEM_HEADS * MEM_HD
    if sample:
        in_specs = [row(D_MODEL), row(CONV_CH), pl.BlockSpec((N_HEADS, tm, KV_RANK), lambda i: (0, i, 0)),
                    _const_spec((N_HEADS, KV_RANK, N_HEADS * V_DIM))]
        args = [x2d, conv2d, attn, w['wuv_pad']]
    else:
        in_specs = [row(D_MODEL), row(CONV_CH), row(N_HEADS * V_DIM)]
        args = [x2d, conv2d, attn]
    in_specs += [_const_spec((D_MODEL, D_MODEL)), _const_spec((1, D_MODEL)), _const_spec((D_MODEL, mem_w))]
    args += [w['w_out'], w['g_mem_q'], w['w_mq']]
    return pl.pallas_call(
        functools.partial(_post_a_kernel, sample), grid=(t // tm,), in_specs=in_specs,
        out_specs=[row(D_MODEL), row(mem_w)],
        out_shape=[jax.ShapeDtypeStruct((t, D_MODEL), F32), jax.ShapeDtypeStruct((t, mem_w), F32)],
        compiler_params=_params("parallel", vmem=VMEM_LIMIT),
        name="post_a_sample" if sample else "post_a_prompt")(*args)


def _mem_kv_kernel(m_ref, g_ref, wk_ref, wv_ref, k_ref, v_ref, kw_ref, vw_ref):
    tm = m_ref.shape[0]
    m = _rms(m_ref[...], g_ref[...]).astype(BF16)
    k = _dot(m, wk_ref[...])
    v = _dot(m, wv_ref[...])
    kw_ref[...] = k
    vw_ref[...] = v
    for hd in range(MEM_HEADS):
        sl = slice(hd * MEM_HD, (hd + 1) * MEM_HD)
        k_ref[pl.ds(hd, tm, stride=MEM_HEADS), :] = k[:, sl]
        v_ref[pl.ds(hd, tm, stride=MEM_HEADS), :] = v[:, sl]


def _mem_kv(mem2d, w):
    t = mem2d.shape[0]
    tm = TOKEN_TILE
    mem_w = MEM_HEADS * MEM_HD
    rows = pl.BlockSpec((tm * MEM_HEADS, MEM_HD), lambda i: (i, 0))
    wide = pl.BlockSpec((tm, mem_w), lambda i: (i, 0))
    return pl.pallas_call(
        _mem_kv_kernel, grid=(t // tm,),
        in_specs=[pl.BlockSpec((tm, D_MODEL), lambda i: (i, 0)), _const_spec((1, D_MODEL)),
                  _const_spec((D_MODEL, mem_w)), _const_spec((D_MODEL, mem_w))],
        out_specs=[rows, rows, wide, wide],
        out_shape=[jax.ShapeDtypeStruct((t * MEM_HEADS, MEM_HD), F32)] * 2
                  + [jax.ShapeDtypeStruct((t, mem_w), F32)] * 2,
        compiler_params=_params("parallel"),
        name="mem_kv")(mem2d, w['g_mem_kv'], w['w_mk'], w['w_mv'])


def _mem_attn_wide_kernel(q_ref, k_ref, v_ref, o_ref):
    for hd in range(MEM_HEADS):
        sl = slice(hd * MEM_HD, (hd + 1) * MEM_HD)
        s = _dot_t(q_ref[0, :, sl].astype(BF16), k_ref[0, :, sl].astype(BF16)) * MEM_SCALE
        p = jnp.exp(s - jnp.max(s, axis=-1, keepdims=True))
        p = p / jnp.sum(p, axis=-1, keepdims=True)
        o_ref[0, :, sl] = _dot(p.astype(BF16), v_ref[0, :, sl].astype(BF16))


def _mem_attn_wide(qm, mem_k, mem_v):
    b, s, mem_w = qm.shape
    tq = min(TOKEN_TILE, s)
    kv = pl.BlockSpec((1, MEM_TOKENS, mem_w), lambda bi, i: (bi, 0, 0))
    return pl.pallas_call(
        _mem_attn_wide_kernel, grid=(b, s // tq),
        in_specs=[pl.BlockSpec((1, tq, mem_w), lambda bi, i: (bi, i, 0)), kv, kv],
        out_specs=pl.BlockSpec((1, tq, mem_w), lambda bi, i: (bi, i, 0)),
        out_shape=jax.ShapeDtypeStruct((b, s, mem_w), F32),
        compiler_params=_params("parallel", "parallel"),
        name="mem_attn_wide")(qm, mem_k, mem_v)


def _mem_attn_rows_kernel(q_ref, k_ref, v_ref, o_ref):
    bb, tq, _ = q_ref.shape
    rows = MEM_TOKENS * MEM_HEADS
    head = lambda hd: slice(hd * MEM_HD, (hd + 1) * MEM_HD)
    q_head = lax.broadcasted_iota(jnp.int32, (MEM_HEADS * tq, rows), 0) // tq
    k_head = lax.broadcasted_iota(jnp.int32, (MEM_HEADS * tq, rows), 1) % MEM_HEADS
    own = q_head == k_head
    for bi in range(bb):
        q = jnp.concatenate([q_ref[bi, :, head(hd)] for hd in range(MEM_HEADS)], axis=0).astype(BF16)
        k = k_ref[bi * rows:(bi + 1) * rows, :].astype(BF16)
        v = v_ref[bi * rows:(bi + 1) * rows, :].astype(BF16)
        s = jnp.where(own, _dot_t(q, k) * MEM_SCALE, NEG_INF)
        p = jnp.exp(s - jnp.max(s, axis=-1, keepdims=True))
        p = p / jnp.sum(p, axis=-1, keepdims=True)
        o = _dot(p.astype(BF16), v)
        for hd in range(MEM_HEADS):
            o_ref[bi, :, head(hd)] = o[hd * tq:(hd + 1) * tq, :]


def _mem_attn_rows(qm, mem_k, mem_v):
    b, s, mem_w = qm.shape
    bb = max(1, MEM_ATTN_ROWS // s)
    kv = pl.BlockSpec((bb * MEM_TOKENS * MEM_HEADS, MEM_HD), lambda bi: (bi, 0))
    return pl.pallas_call(
        _mem_attn_rows_kernel, grid=(b // bb,),
        in_specs=[pl.BlockSpec((bb, s, mem_w), lambda bi: (bi, 0, 0)), kv, kv],
        out_specs=pl.BlockSpec((bb, s, mem_w), lambda bi: (bi, 0, 0)),
        out_shape=jax.ShapeDtypeStruct((b, s, mem_w), F32),
        compiler_params=_params("parallel"),
        name="mem_attn_rows")(qm, mem_k, mem_v)


def _slab_rows(j, n):
    return pl.ds(j, n, stride=SLAB)


def _post_b_kernel(n_p, x1p_ref, x1s_ref, omp_ref, oms_ref, wmo_ref, g_ref, wr_ref, br_ref,
                   x2_ref, h3_ref, idx_ref, gate_ref, rank_ref, cnt_ref, carry_ref):
    tm = x2_ref.shape[0]
    i = pl.program_id(0)

    @pl.when(i == 0)
    def _():
        carry_ref[...] = jnp.zeros(carry_ref.shape, F32)

    is_p = i < n_p
    x1 = jnp.where(is_p, x1p_ref[...], x1s_ref[...])
    om = jnp.where(is_p, omp_ref[...], oms_ref[...])
    x2 = x1 + _dot(om.astype(BF16), wmo_ref[...])
    x2_ref[...] = x2
    h3 = _rms(x2, g_ref[...])
    for j in range(SLAB):
        h3_ref[_slab_rows(j, tm), :] = h3[:, j * LANES:(j + 1) * LANES]
    logits = _dot(h3.astype(BF16), wr_ref[...]) + br_ref[...]
    lane = lax.broadcasted_iota(jnp.int32, logits.shape, 1)
    lane_f = lane.astype(F32)
    vals, hots = [], []
    idx_out = jnp.zeros(logits.shape, F32)
    for kk in range(TOP_K):
        mx = jnp.max(logits, axis=-1, keepdims=True)
        first = jnp.min(jnp.where(logits == mx, lane_f, float(LANES)), axis=-1, keepdims=True)
        hot = lane_f == first
        logits = jnp.where(hot, -jnp.inf, logits)
        vals.append(mx)
        hots.append(hot)
        idx_out = jnp.where(lane == kk, first, idx_out)
    exps = [jnp.exp(v - vals[0]) for v in vals]
    denom = exps[0] + exps[1] + exps[2] + exps[3]
    chosen = jnp.zeros(logits.shape, F32)
    gate_out = jnp.zeros(logits.shape, F32)
    for kk in range(TOP_K):
        chosen = chosen + hots[kk].astype(F32)
        gate_out = jnp.where(lane == kk, exps[kk] / denom, gate_out)
    r_i = lax.broadcasted_iota(jnp.int32, (tm, tm), 0)
    c_i = lax.broadcasted_iota(jnp.int32, (tm, tm), 1)
    tril = (c_i < r_i).astype(BF16)
    before = _dot(tril, chosen.astype(BF16)) + carry_ref[...]
    rank_out = jnp.zeros(logits.shape, F32)
    for kk in range(TOP_K):
        rk = jnp.sum(jnp.where(hots[kk], before, 0.0), axis=-1, keepdims=True)
        rank_out = jnp.where(lane == kk, rk, rank_out)
    carry = carry_ref[...] + jnp.sum(chosen, axis=0, keepdims=True)
    carry_ref[...] = carry
    idx_ref[...] = idx_out.astype(jnp.int32)
    gate_ref[...] = gate_out
    rank_ref[...] = rank_out.astype(jnp.int32)
    cnt_ref[...] = carry.astype(jnp.int32)


def _post_b(x1_p, om_p, x1_s, om_s, w):
    tm = TOKEN_TILE
    n_p, n_s = x1_p.shape[0] // tm, x1_s.shape[0] // tm
    t = (n_p + n_s) * tm
    mem_w = MEM_HEADS * MEM_HD
    row = lambda n: pl.BlockSpec((tm, n), lambda i: (i, 0))
    row_p = lambda n: pl.BlockSpec((tm, n), lambda i: (jnp.minimum(i, n_p - 1), 0))
    row_s = lambda n: pl.BlockSpec((tm, n), lambda i: (jnp.maximum(i - n_p, 0), 0))
    return pl.pallas_call(
        functools.partial(_post_b_kernel, n_p), grid=(n_p + n_s,),
        in_specs=[row_p(D_MODEL), row_s(D_MODEL), row_p(mem_w), row_s(mem_w),
                  _const_spec((mem_w, D_MODEL)), _const_spec((1, D_MODEL)),
                  _const_spec((D_MODEL, LANES)), _const_spec((1, LANES))],
        out_specs=[row(D_MODEL), pl.BlockSpec((tm * SLAB, LANES), lambda i: (i, 0)),
                   row(LANES), row(LANES), row(LANES), _const_spec((1, LANES))],
        out_shape=[jax.ShapeDtypeStruct((t, D_MODEL), F32), jax.ShapeDtypeStruct((t * SLAB, LANES), F32),
                   jax.ShapeDtypeStruct((t, LANES), jnp.int32), jax.ShapeDtypeStruct((t, LANES), F32),
                   jax.ShapeDtypeStruct((t, LANES), jnp.int32), jax.ShapeDtypeStruct((1, LANES), jnp.int32)],
        scratch_shapes=[pltpu.VMEM((1, LANES), F32)],
        compiler_params=_params("arbitrary", vmem=VMEM_LIMIT),
        name="post_b")(x1_p, x1_s, om_p, om_s, w['w_mo'], w['g_ffn'], w['w_router'], w['b_router'])


def _slab(row):
    return pl.ds(pl.multiple_of(row * SLAB, SLAB), SLAB)


def _dispatch_kernel(start_ref, lo_ref, hi_ref, idx_ref, rank_ref, h_ref, xs_hbm, zero_ref, sem):
    i = pl.program_id(0)
    tm = h_ref.shape[0] // SLAB

    def issue(t, _):
        for k in range(TOP_K):
            r = t * TOP_K + k
            d = start_ref[idx_ref[r]] + rank_ref[r]
            pltpu.make_async_copy(h_ref.at[_slab(t)], xs_hbm.at[_slab(d)], sem).start(priority=k % 2)
        return 0

    lax.fori_loop(0, tm, issue, 0, unroll=2)
    for _ in range(TOP_K):
        pltpu.make_async_copy(h_ref, xs_hbm.at[pl.ds(0, tm * SLAB)], sem).wait()

    @pl.when(i == pl.num_programs(0) - 1)
    def _():
        zero_ref[...] = jnp.zeros(zero_ref.shape, F32)
        for e in range(N_EXPERTS):
            lo, hi = lo_ref[e], hi_ref[e]

            def zissue(r, _):
                pltpu.make_async_copy(zero_ref.at[_slab(0)], xs_hbm.at[_slab(r)], sem).start()
                return 0

            def zdrain(r, _):
                pltpu.make_async_copy(zero_ref.at[_slab(0)], xs_hbm.at[_slab(0)], sem).wait()
                return 0

            lax.fori_loop(lo, hi, zissue, 0)
            lax.fori_loop(lo, hi, zdrain, 0)

        bm = zero_ref.shape[0] // SLAB

        def tail(blk, _):
            rows = pl.ds(pl.multiple_of(blk * bm * SLAB, SLAB), bm * SLAB)
            cp = pltpu.make_async_copy(zero_ref, xs_hbm.at[rows], sem)
            cp.start()
            cp.wait()
            return 0

        lax.fori_loop(hi_ref[N_EXPERTS - 1] // bm, xs_hbm.shape[0] // (bm * SLAB), tail, 0)


def _dispatch(h3s, idx_flat, rank_flat, pad_start, pad_lo, pad_hi, n_rows):
    tm = TOKEN_TILE
    t = h3s.shape[0] // SLAB
    grid_spec = pltpu.PrefetchScalarGridSpec(
        num_scalar_prefetch=3, grid=(t // tm,),
        in_specs=[pl.BlockSpec((tm * TOP_K,), lambda i, *_: (i,), memory_space=pltpu.SMEM),
                  pl.BlockSpec((tm * TOP_K,), lambda i, *_: (i,), memory_space=pltpu.SMEM),
                  pl.BlockSpec((tm * SLAB, LANES), lambda i, *_: (i, 0))],
        out_specs=pl.BlockSpec(memory_space=pl.ANY),
        scratch_shapes=[pltpu.VMEM((MOE_ROWS * SLAB, LANES), F32), pltpu.SemaphoreType.DMA])
    return pl.pallas_call(
        _dispatch_kernel, grid_spec=grid_spec,
        out_shape=jax.ShapeDtypeStruct((n_rows * SLAB, LANES), F32),
        compiler_params=pltpu.CompilerParams(dimension_semantics=("arbitrary",), has_side_effects=True),
        name="moe_dispatch")(pad_start, pad_lo, pad_hi, idx_flat, rank_flat, h3s)


def _expert_kernel(src_ref, exp_ref, first_ref, valid_ref, next_ref, x_ref, wg_hbm, wu_hbm, wd_hbm,
                   bg_ref, bu_ref, bd_ref, y_ref, wf_ref, wb_ref, xb_ref, sem):
    b = pl.program_id(0)
    bm = xb_ref.shape[0]
    w_hbm = (wg_hbm, wu_hbm, wd_hbm)

    def fetch(e):
        for i in range(3):
            pltpu.make_async_copy(w_hbm[i].at[e], wf_ref.at[i], sem).start()

    @pl.when(b == 0)
    def _():
        fetch(exp_ref[0])

    @pl.when(first_ref[b] == 1)
    def _():
        for i in range(3):
            pltpu.make_async_copy(w_hbm[i].at[0], wf_ref.at[i], sem).wait()
        def cast(r, _):
            rows = pl.ds(pl.multiple_of(r * CAST_ROWS, CAST_ROWS), CAST_ROWS)
            for i in range(3):
                wb_ref[i, rows, :] = wf_ref[i, rows, :].astype(BF16)
            return 0

        lax.fori_loop(0, D_MODEL // CAST_ROWS, cast, 0)

        @pl.when(next_ref[b] >= 0)
        def _():
            fetch(next_ref[b])

    @pl.when(valid_ref[b] == 1)
    def _():
        for j in range(SLAB):
            xb_ref[:, j * LANES:(j + 1) * LANES] = x_ref[_slab_rows(j, bm), :].astype(BF16)
        x = xb_ref[...]
        g = _dot(x, wb_ref[0]) + bg_ref[0]
        u = _dot(x, wb_ref[1]) + bu_ref[0]
        g = jnp.minimum(g, SWIGLU_LIMIT)
        u = jnp.clip(u, -SWIGLU_LIMIT, SWIGLU_LIMIT)
        a = (u + 1.0) * (g * jax.nn.sigmoid(SWIGLU_ALPHA * g))
        y = _dot(a.astype(BF16), wb_ref[2]) + bd_ref[0]
        for j in range(SLAB):
            y_ref[_slab_rows(j, bm), :] = y[:, j * LANES:(j + 1) * LANES]

    @pl.when(valid_ref[b] == 0)
    def _():
        y_ref[...] = jnp.zeros(y_ref.shape, F32)


def _experts(xs, blk_src, blk_exp, blk_first, blk_valid, blk_next, w):
    bm = MOE_ROWS
    n_blocks = xs.shape[0] // (bm * SLAB)
    hbm = pl.BlockSpec(memory_space=pl.ANY)
    bspec = pl.BlockSpec((1, 1, D_MODEL), lambda b, src, ex, *_: (ex[b], 0, 0))
    grid_spec = pltpu.PrefetchScalarGridSpec(
        num_scalar_prefetch=5, grid=(n_blocks,),
        in_specs=[pl.BlockSpec((bm * SLAB, LANES), lambda b, src, *_: (src[b], 0)),
                  hbm, hbm, hbm, bspec, bspec, bspec],
        out_specs=pl.BlockSpec((bm * SLAB, LANES), lambda b, *_: (b, 0)),
        scratch_shapes=[pltpu.VMEM((3, D_MODEL, D_MODEL), F32), pltpu.VMEM((3, D_MODEL, D_MODEL), BF16),
                        pltpu.VMEM((bm, D_MODEL), BF16), pltpu.SemaphoreType.DMA])
    return pl.pallas_call(
        _expert_kernel, grid_spec=grid_spec,
        out_shape=jax.ShapeDtypeStruct(xs.shape, F32),
        compiler_params=_params("arbitrary", vmem=VMEM_LIMIT),
        name="moe_experts")(blk_src, blk_exp, blk_first, blk_valid, blk_next, xs,
                            w['w_gate'], w['w_up'], w['w_down'], w['b_gate'], w['b_up'], w['b_down'])


def _combine_kernel(n_p, start_ref, idx_ref, rank_ref, yb_hbm, x2_ref, gate_ref, g_ref, yp_ref, ys_ref,
                    rows_ref, y_ref, sem):
    i = pl.program_id(0)
    tm = x2_ref.shape[0]

    def issue(t, _):
        for k in range(TOP_K):
            r = t * TOP_K + k
            d = start_ref[idx_ref[r]] + rank_ref[r]
            pltpu.make_async_copy(yb_hbm.at[_slab(d)], rows_ref.at[_slab(r)], sem).start(priority=k % 2)
        return 0

    lax.fori_loop(0, tm, issue, 0, unroll=2)
    for _ in range(TOP_K):
        pltpu.make_async_copy(yb_hbm.at[pl.ds(0, tm * SLAB)], rows_ref.at[pl.ds(0, tm * SLAB)], sem).wait()

    gate = gate_ref[...]
    ss = jnp.zeros((tm, 1), F32)
    for j in range(SLAB):
        y = x2_ref[:, j * LANES:(j + 1) * LANES]
        for k in range(TOP_K):
            y = y + rows_ref[pl.ds(k * SLAB + j, tm, stride=TOP_K * SLAB), :] * gate[:, k:k + 1]
        y_ref[:, j * LANES:(j + 1) * LANES] = y
        ss = ss + jnp.sum(y * y, axis=-1, keepdims=True)
    out = y_ref[...] * lax.rsqrt(ss * (1.0 / D_MODEL) + EPS) * g_ref[...]

    @pl.when(i < n_p)
    def _():
        yp_ref[...] = out

    @pl.when(i >= n_p)
    def _():
        ys_ref[...] = out


def _combine(yb, idx_flat, rank_flat, pad_start, x2, gate128, g_final, n_p):
    t = x2.shape[0]
    tm = TOKEN_TILE
    n_s = t // tm - n_p
    grid_spec = pltpu.PrefetchScalarGridSpec(
        num_scalar_prefetch=1, grid=(t // tm,),
        in_specs=[pl.BlockSpec((tm * TOP_K,), lambda i, *_: (i,), memory_space=pltpu.SMEM),
                  pl.BlockSpec((tm * TOP_K,), lambda i, *_: (i,), memory_space=pltpu.SMEM),
                  pl.BlockSpec(memory_space=pl.ANY),
                  pl.BlockSpec((tm, D_MODEL), lambda i, *_: (i, 0)),
                  pl.BlockSpec((tm, LANES), lambda i, *_: (i, 0)),
                  pl.BlockSpec((1, D_MODEL), lambda i, *_: (0, 0))],
        out_specs=[pl.BlockSpec((tm, D_MODEL), lambda i, *_: (jnp.minimum(i, n_p - 1), 0)),
                   pl.BlockSpec((tm, D_MODEL), lambda i, *_: (jnp.maximum(i - n_p, 0), 0))],
        scratch_shapes=[pltpu.VMEM((tm * TOP_K * SLAB, LANES), F32), pltpu.VMEM((tm, D_MODEL), F32),
                        pltpu.SemaphoreType.DMA])
    return pl.pallas_call(
        functools.partial(_combine_kernel, n_p), grid_spec=grid_spec,
        out_shape=[jax.ShapeDtypeStruct((n_p * tm, D_MODEL), F32), jax.ShapeDtypeStruct((n_s * tm, D_MODEL), F32)],
        compiler_params=_params("arbitrary"),
        name="moe_combine")(pad_start, idx_flat, rank_flat, yb, x2, gate128, g_final)


def _moe_and_final(x2, h3s, idx128, gate128, rank128, cnt128, w, n_p):
    t = x2.shape[0]
    bm = MOE_ROWS
    n_assign = t * TOP_K
    n_blocks = (n_assign + N_EXPERTS * (bm - 1) + bm - 1) // bm
    counts = cnt128[0, :N_EXPERTS]
    padded = ((counts + bm - 1) // bm) * bm
    pad_end = jnp.cumsum(padded).astype(jnp.int32)
    pad_start = pad_end - padded
    idx_flat = idx128[:, :TOP_K].reshape(-1)
    rank_flat = rank128[:, :TOP_K].reshape(-1)
    blk = jnp.arange(n_blocks, dtype=jnp.int32)
    blk_valid = (blk * bm < pad_end[-1]).astype(jnp.int32)
    blk_src = jnp.minimum(blk, jnp.maximum(pad_end[-1] // bm - 1, 0))
    blk_exp = jnp.sum((pad_end[None, :] <= (blk_src * bm)[:, None]).astype(jnp.int32), axis=1)
    blk_exp = jnp.minimum(blk_exp, N_EXPERTS - 1)
    blk_first = jnp.concatenate([jnp.ones((1,), jnp.int32),
                                 (blk_exp[1:] != blk_exp[:-1]).astype(jnp.int32)])
    later_start = (blk[None, :] > blk[:, None]) & (blk_first[None, :] == 1)
    next_pos = jnp.min(jnp.where(later_start, blk[None, :], n_blocks), axis=1)
    blk_next = jnp.where(next_pos < n_blocks, blk_exp[jnp.minimum(next_pos, n_blocks - 1)], -1)
    xs = _dispatch(h3s, idx_flat, rank_flat, pad_start, pad_start + counts, pad_end, n_blocks * bm)
    yb = _experts(xs, blk_src, blk_exp, blk_first, blk_valid, blk_next.astype(jnp.int32), w)
    return _combine(yb, idx_flat, rank_flat, pad_start, x2, gate128, w['g_final'], n_p)


def _swap_halves(wcols):
    half = QK_ROPE // 2
    return jnp.concatenate([-wcols[..., half:], wcols[..., :half]], axis=-1)


def _prep_weights(g_mix, w_in, conv_w, conv_b, conv_ln_g, conv_ln_b, q_norm_g, w_q_up, kv_norm_g, w_kv_up,
                  w_out, g_mem_q, g_mem_kv, w_mq, w_mk, w_mv, w_mo, g_ffn, w_router, b_router,
                  w_gate, b_gate, w_up, b_up, w_down, b_down, g_final):
    l = 0
    w_kpe = w_in[l][:, C_KPE:]
    zpad = jnp.zeros((D_MODEL, HEAD_PAD - QK_ROPE), F32)
    w_in_ext = jnp.concatenate([w_in[l][:, :C_KPE], w_kpe, zpad, _swap_halves(w_kpe), zpad], axis=1)
    wq3 = w_q_up[l].reshape(Q_RANK, N_HEADS, QK_NOPE + QK_ROPE)
    q_nope, q_rope = wq3[..., :QK_NOPE], wq3[..., QK_NOPE:]
    z32 = jnp.zeros((Q_RANK, N_HEADS, HEAD_PAD - QK_NOPE - QK_ROPE), F32)
    wq = jnp.concatenate([q_rope, q_nope, z32], axis=-1).reshape(Q_RANK, N_HEADS * HEAD_PAD)
    wq_sw = jnp.concatenate([_swap_halves(q_rope), jnp.zeros_like(q_nope), z32], axis=-1)
    wq_sw = wq_sw.reshape(Q_RANK, N_HEADS * HEAD_PAD)
    w_uk = w_kv_up[l][:, :, :QK_NOPE]
    w_uv = w_kv_up[l][:, :, QK_NOPE:]
    wuk_pad = jnp.concatenate([jnp.zeros((KV_RANK, N_HEADS, QK_ROPE), F32), w_uk,
                               jnp.zeros((KV_RANK, N_HEADS, HEAD_PAD - QK_NOPE - QK_ROPE), F32)], axis=-1)
    wukt_pad = jnp.transpose(wuk_pad, (1, 2, 0))
    eye = jnp.eye(N_HEADS, dtype=F32)
    wuv_pad = (w_uv.transpose(1, 0, 2)[:, :, None, :] * eye[:, None, :, None])
    conv_w_pad = jnp.concatenate([conv_w[l], jnp.zeros((CONV_HALO - CONV_WIDTH, CONV_CH), F32)], axis=0)
    w_router_pad = jnp.concatenate([w_router[l], jnp.zeros((D_MODEL, LANES - N_EXPERTS), F32)], axis=1)
    b_router_pad = jnp.concatenate([b_router[l], jnp.full((LANES - N_EXPERTS,), NEG_INF, F32)])
    return {
        'g_mix': g_mix[l][None], 'w_in_ext': w_in_ext.astype(BF16),
        'q_norm_g': q_norm_g[l][None], 'wq': wq.astype(BF16), 'wq_sw': wq_sw.astype(BF16),
        'kv_norm_g': kv_norm_g[l][None],
        'wuk_pad': wuk_pad.reshape(KV_RANK, N_HEADS * HEAD_PAD).astype(BF16),
        'wuv': w_uv.reshape(KV_RANK, N_HEADS * V_DIM).astype(BF16),
        'wukt_pad': wukt_pad.astype(BF16),
        'wuv_pad': wuv_pad.reshape(N_HEADS, KV_RANK, N_HEADS * V_DIM).astype(BF16),
        'conv_w': conv_w_pad, 'conv_b': conv_b[l][None],
        'conv_ln_g': conv_ln_g[l][None], 'conv_ln_b': conv_ln_b[l][None],
        'w_out': w_out[l].astype(BF16), 'g_mem_q': g_mem_q[l][None], 'w_mq': w_mq[l].astype(BF16),
        'g_mem_kv': g_mem_kv[l][None], 'w_mk': w_mk[l].astype(BF16), 'w_mv': w_mv[l].astype(BF16),
        'w_mo': w_mo[l].astype(BF16), 'g_ffn': g_ffn[l][None],
        'w_router': w_router_pad.astype(BF16), 'b_router': b_router_pad[None],
        'w_gate': w_gate[l], 'w_up': w_up[l], 'w_down': w_down[l],
        'b_gate': b_gate[l][:, None, :], 'b_up': b_up[l][:, None, :], 'b_down': b_down[l][:, None, :],
        'g_final': g_final[None],
    }


def _rope_table(pos):
    half = QK_ROPE // 2
    inv = ROPE_THETA ** (-jnp.arange(half, dtype=F32) / half)
    ang = pos.astype(F32)[:, None] * inv[None, :]
    cos, sin = jnp.cos(ang), jnp.sin(ang)
    n = pos.shape[0]
    ones = jnp.ones((n, QK_NOPE), F32)
    z = lambda k: jnp.zeros((n, k), F32)
    cq = MLA_SCALE * jnp.concatenate([cos, cos, ones, z(HEAD_PAD - QK_NOPE - QK_ROPE)], axis=1)
    sq = MLA_SCALE * jnp.concatenate([sin, sin, z(HEAD_PAD - QK_ROPE)], axis=1)
    ck = jnp.concatenate([cos, cos, z(HEAD_PAD - QK_ROPE)], axis=1)
    sk = jnp.concatenate([sin, sin, z(HEAD_PAD - QK_ROPE)], axis=1)
    return jnp.stack([cq, sq, ck, sk])


def _front(x, conv_prev, mem_k, mem_v, pos, w, paged):
    b, s, _ = x.shape
    t = b * s
    x2d = x.reshape(t, D_MODEL)
    sample = paged is not None
    if sample:
        tab = _rope_table(jnp.tile(pos, TOKEN_TILE // s))
    else:
        tab = _rope_table(pos)
    outs = _inproj(x2d, tab, w, sample)
    u, ckv, kpe = outs[0], outs[1], outs[2]
    u3 = u.reshape(b, s, CONV_CH)
    tail = CONV_WIDTH - 1
    if s >= tail:
        conv_tail = u3[:, s - tail:]
    else:
        conv_tail = jnp.concatenate([conv_prev[:, s:], u3], axis=1)
    prev_pad = jnp.concatenate([jnp.zeros((b, CONV_HALO - tail, CONV_CH), F32), conv_prev], axis=1)
    ckv3 = ckv.reshape(b, s, KV_RANK)
    kpe3 = kpe.reshape(b, s, QK_ROPE)
    if sample:
        page_table, cache_ckv, cache_kpe_t = paged
        conv_out = _conv_sample(jnp.concatenate([prev_pad, u3], axis=1), w)
        attn = _attn_sample(page_table, outs[3], outs[4], ckv3, kpe3, cache_ckv, cache_kpe_t)
    else:
        conv_out = _conv_prompt(u3, prev_pad, w)
        q, k, v = (a.reshape(b, s, -1) for a in outs[3:6])
        attn = _attn_prompt(q, k, v).reshape(t, N_HEADS * V_DIM)
    x1, qm = _post_a(x2d, conv_out.reshape(t, CONV_CH), attn, w, sample)
    mem_attn = _mem_attn_rows if sample else _mem_attn_wide
    om = mem_attn(qm.reshape(b, s, -1), mem_k, mem_v)
    return x1, om.reshape(t, -1), conv_tail, ckv3, kpe3


def kernel(x_prompt, x_sample, mem_prompt, cache_ckv, cache_kpe, page_table, cache_mem_k, cache_mem_v, state_conv, g_mix, w_in, conv_w, conv_b, conv_ln_g, conv_ln_b, q_norm_g, w_q_up, kv_norm_g, w_kv_up, w_out, g_mem_q, g_mem_kv, w_mq, w_mk, w_mv, w_mo, g_ffn, w_router, b_router, w_gate, b_gate, w_up, b_up, w_down, b_down, g_final):
    assert g_mix.shape[0] == 1, "single-layer step"
    w = _prep_weights(g_mix, w_in, conv_w, conv_b, conv_ln_g, conv_ln_b, q_norm_g, w_q_up, kv_norm_g, w_kv_up,
                      w_out, g_mem_q, g_mem_kv, w_mq, w_mk, w_mv, w_mo, g_ffn, w_router, b_router,
                      w_gate, b_gate, w_up, b_up, w_down, b_down, g_final)
    b_p, s_p, _ = x_prompt.shape
    b_s, t_s, _ = x_sample.shape
    past = page_table.shape[1] * PAGE_SIZE

    mk, mv, mk_wide, mv_wide = _mem_kv(mem_prompt.reshape(-1, D_MODEL), w)
    conv0 = jnp.zeros((b_p, CONV_WIDTH - 1, CONV_CH), F32)
    x1_p, om_p, conv_p, ckv_p, kpe_p = _front(
        x_prompt, conv0, mk_wide.reshape(b_p, MEM_TOKENS, -1), mv_wide.reshape(b_p, MEM_TOKENS, -1),
        jnp.arange(s_p, dtype=jnp.int32), w, None)
    x1_s, om_s, conv_s, ckv_s, kpe_s = _front(
        x_sample, state_conv[0], cache_mem_k[0].reshape(-1, MEM_HD), cache_mem_v[0].reshape(-1, MEM_HD),
        past + jnp.arange(t_s, dtype=jnp.int32), w,
        (page_table, cache_ckv[0], jnp.swapaxes(cache_kpe[0], 1, 2)))

    x2, h3s, idx128, gate128, rank128, cnt128 = _post_b(x1_p, om_p, x1_s, om_s, w)
    y_p, y_s = _moe_and_final(x2, h3s, idx128, gate128, rank128, cnt128, w, x1_p.shape[0] // TOKEN_TILE)

    mem_shape = (1, b_p, MEM_TOKENS, MEM_HEADS, MEM_HD)
    return (y_p.reshape(b_p, s_p, D_MODEL), y_s.reshape(b_s, t_s, D_MODEL), ckv_p[None], kpe_p[None],
            mk.reshape(mem_shape), mv.reshape(mem_shape), conv_p[None], ckv_s[None], kpe_s[None], conv_s[None])
```

```python
import functools

import jax
import jax.numpy as jnp
from jax import lax
from jax.experimental import pallas as pl
from jax.experimental.pallas import tpu as pltpu

F32 = jnp.float32
BF16 = jnp.bfloat16

D_MODEL = 1024
PAGE_SIZE = 128
CONV_CH = 512
CONV_WIDTH = 31
N_HEADS = 8
QK_NOPE = 64
QK_ROPE = 32
V_DIM = 64
Q_RANK = 384
KV_RANK = 256
ROPE_THETA = 10000.0
MLA_SCALE = (QK_NOPE + QK_ROPE) ** -0.5
MEM_TOKENS = 256
MEM_HEADS = 4
MEM_HD = 128
MEM_SCALE = MEM_HD ** -0.5
N_EXPERTS = 32
TOP_K = 4
SWIGLU_LIMIT = 7.0
SWIGLU_ALPHA = 1.702
EPS = 1e-6
NEG_INF = -1e30

LANES = 128
SUBLANES = 8
HEAD_PAD = 128
C_VAL, C_GATE, C_Q, C_CKV, C_KPE, C_KPE_SW, C_END = 0, 512, 1024, 1408, 1664, 1792, 1920
TOKEN_TILE = 256
ATTN_TILE = 256
ATTN_HEADS = 4
CONV_TILE = 256
CONV_HALO = 32
CONV_CHUNK = 32
MEM_ATTN_ROWS = 32
PAGES_PER_CHUNK = 8
SLAB = 8
MOE_ROWS = 256
CAST_ROWS = 32
VMEM_LIMIT = 48 * 1024 * 1024


def _rms(x, g):
    return x * lax.rsqrt(jnp.mean(x * x, axis=-1, keepdims=True) + EPS) * g


def _dot(a, b):
    return jnp.dot(a, b, preferred_element_type=F32)


def _dot_t(a, b):
    return lax.dot_general(a, b, (((1,), (1,)), ((), ())), preferred_element_type=F32)


def _params(*sem, vmem=None):
    return pltpu.CompilerParams(dimension_semantics=sem, vmem_limit_bytes=vmem)


def _const_spec(shape):
    nd = len(shape)
    return pl.BlockSpec(shape, lambda *_: (0,) * nd)


def _inproj_common(x_ref, gmix_ref, win_ref, qg_ref, wq_ref, wqsw_ref, kvg_ref, tab_ref,
                   u_ref, ckv_ref, kpe_ref):
    h = _rms(x_ref[...], gmix_ref[...]).astype(BF16)
    proj = _dot(h, win_ref[...])
    u_ref[...] = proj[:, C_VAL:C_GATE] * jax.nn.sigmoid(proj[:, C_GATE:C_Q])
    qn = _rms(proj[:, C_Q:C_CKV], qg_ref[...]).astype(BF16)
    ckv = _rms(proj[:, C_CKV:C_KPE], kvg_ref[...])
    ckv_ref[...] = ckv
    cq, sq, ck, sk = tab_ref[0], tab_ref[1], tab_ref[2], tab_ref[3]
    kpe_rot = proj[:, C_KPE:C_KPE_SW] * ck + proj[:, C_KPE_SW:C_END] * sk
    kpe_ref[...] = kpe_rot[:, :QK_ROPE]
    q = _dot(qn, wq_ref[...])
    qs = _dot(qn, wqsw_ref[...])
    q_heads = []
    for hd in range(N_HEADS):
        sl = slice(hd * HEAD_PAD, (hd + 1) * HEAD_PAD)
        q_heads.append(q[:, sl] * cq + qs[:, sl] * sq)
    return ckv, kpe_rot, q_heads


def _inproj_prompt_kernel(x_ref, gmix_ref, win_ref, qg_ref, wq_ref, wqsw_ref, kvg_ref, tab_ref,
                          wuk_ref, wuv_ref, vone_ref,
                          u_ref, ckv_ref, kpe_ref, q_ref, k_ref, v_ref):
    ckv, kpe_rot, q_heads = _inproj_common(x_ref, gmix_ref, win_ref, qg_ref, wq_ref, wqsw_ref,
                                           kvg_ref, tab_ref, u_ref, ckv_ref, kpe_ref)
    ckv_b = ckv.astype(BF16)
    k_nope = _dot(ckv_b, wuk_ref[...])
    for hd in range(N_HEADS):
        sl = slice(hd * HEAD_PAD, (hd + 1) * HEAD_PAD)
        q_ref[:, sl] = q_heads[hd].astype(BF16)
        k_ref[:, sl] = (k_nope[:, sl] + kpe_rot).astype(BF16)
    v_ref[...] = (_dot(ckv_b, wuv_ref[...]) + vone_ref[...]).astype(BF16)


def _inproj_sample_kernel(x_ref, gmix_ref, win_ref, qg_ref, wq_ref, wqsw_ref, kvg_ref, tab_ref,
                          wukt_ref,
                          u_ref, ckv_ref, kpe_ref, qlat_ref, qpe_ref):
    _, _, q_heads = _inproj_common(x_ref, gmix_ref, win_ref, qg_ref, wq_ref, wqsw_ref,
                                   kvg_ref, tab_ref, u_ref, ckv_ref, kpe_ref)
    for hd in range(N_HEADS):
        qlat_ref[hd] = _dot(q_heads[hd].astype(BF16), wukt_ref[hd])
        qpe_ref[hd] = q_heads[hd][:, :QK_ROPE]


def _inproj(x2d, tab, w, sample):
    t = x2d.shape[0]
    tm = TOKEN_TILE
    n_tab = tab.shape[1] // tm
    row = lambda n: pl.BlockSpec((tm, n), lambda i: (i, 0))
    in_specs = [row(D_MODEL), _const_spec((1, D_MODEL)), _const_spec((D_MODEL, C_END)),
                _const_spec((1, Q_RANK)), _const_spec((Q_RANK, N_HEADS * HEAD_PAD)),
                _const_spec((Q_RANK, N_HEADS * HEAD_PAD)), _const_spec((1, KV_RANK)),
                pl.BlockSpec((4, tm, LANES), lambda i: (0, i % n_tab, 0))]
    args = [x2d, w['g_mix'], w['w_in_ext'], w['q_norm_g'], w['wq'], w['wq_sw'], w['kv_norm_g'], tab]
    out_shape = [jax.ShapeDtypeStruct((t, CONV_CH), F32), jax.ShapeDtypeStruct((t, KV_RANK), F32),
                 jax.ShapeDtypeStruct((t, QK_ROPE), F32)]
    out_specs = [row(CONV_CH), row(KV_RANK), row(QK_ROPE)]
    if sample:
        body = _inproj_sample_kernel
        in_specs += [_const_spec((N_HEADS, HEAD_PAD, KV_RANK))]
        args += [w['wukt_pad']]
        out_shape += [jax.ShapeDtypeStruct((N_HEADS, t, KV_RANK), F32),
                      jax.ShapeDtypeStruct((N_HEADS, t, QK_ROPE), F32)]
        out_specs += [pl.BlockSpec((N_HEADS, tm, KV_RANK), lambda i: (0, i, 0)),
                      pl.BlockSpec((N_HEADS, tm, QK_ROPE), lambda i: (0, i, 0))]
    else:
        body = _inproj_prompt_kernel
        in_specs += [_const_spec((KV_RANK, N_HEADS * HEAD_PAD)), _const_spec((KV_RANK, N_HEADS * HEAD_PAD)),
                     _const_spec((1, N_HEADS * HEAD_PAD))]
        args += [w['wuk_pad'], w['wuv_slot'], w['v_ones']]
        out_shape += [jax.ShapeDtypeStruct((t, N_HEADS * HEAD_PAD), BF16)] * 3
        out_specs += [row(N_HEADS * HEAD_PAD)] * 3
    return pl.pallas_call(
        body, grid=(t // tm,), in_specs=in_specs, out_specs=out_specs, out_shape=out_shape,
        compiler_params=_params("parallel", vmem=VMEM_LIMIT),
        name="inproj_sample" if sample else "inproj_prompt")(*args)


def _ln_swish(conv, g, b):
    mu = jnp.mean(conv, axis=-1, keepdims=True)
    xc = conv - mu
    var = jnp.mean(xc * xc, axis=-1, keepdims=True)
    y = xc * lax.rsqrt(var + EPS) * g + b
    return y * jax.nn.sigmoid(y)


def _conv_prompt_kernel(main_ref, halo_ref, prev_ref, w_ref, b_ref, g_ref, lb_ref, o_ref, win_ref):
    tt = main_ref.shape[1]
    first = pl.program_id(1) == 0
    win_ref[0, 0:CONV_HALO, :] = jnp.where(first, prev_ref[0], halo_ref[0])
    win_ref[0, CONV_HALO:CONV_HALO + tt, :] = main_ref[0]
    n = tt + CONV_HALO - SUBLANES
    for s in range(1, SUBLANES):
        win_ref[s, 0:n, :] = win_ref[0, s:s + n, :]
    for c in range(tt // CONV_CHUNK):
        acc = None
        for j in range(CONV_WIDTH):
            q, s = divmod(c * CONV_CHUNK + 2 + j, SUBLANES)
            term = win_ref[s, q * SUBLANES:q * SUBLANES + CONV_CHUNK, :] * w_ref[j:j + 1, :]
            acc = term if acc is None else acc + term
        y = _ln_swish(acc + b_ref[...], g_ref[...], lb_ref[...])
        o_ref[0, c * CONV_CHUNK:(c + 1) * CONV_CHUNK, :] = y


def _conv_prompt(u3, prev_pad, w):
    b, s, _ = u3.shape
    tt = CONV_TILE
    ratio = tt // CONV_HALO
    return pl.pallas_call(
        _conv_prompt_kernel, grid=(b, s // tt),
        in_specs=[pl.BlockSpec((1, tt, CONV_CH), lambda bi, i: (bi, i, 0)),
                  pl.BlockSpec((1, CONV_HALO, CONV_CH), lambda bi, i: (bi, jnp.maximum(i * ratio - 1, 0), 0)),
                  pl.BlockSpec((1, CONV_HALO, CONV_CH), lambda bi, i: (bi, 0, 0)),
                  _const_spec((CONV_HALO, CONV_CH)), _const_spec((1, CONV_CH)),
                  _const_spec((1, CONV_CH)), _const_spec((1, CONV_CH))],
        out_specs=pl.BlockSpec((1, tt, CONV_CH), lambda bi, i: (bi, i, 0)),
        out_shape=jax.ShapeDtypeStruct((b, s, CONV_CH), F32),
        scratch_shapes=[pltpu.VMEM((SUBLANES, tt + CONV_HALO, CONV_CH), F32)],
        compiler_params=_params("parallel", "parallel"),
        name="conv_prompt")(u3, u3, prev_pad, w['conv_w'], w['conv_b'], w['conv_ln_g'], w['conv_ln_b'])


def _conv_sample_kernel(win_ref, w_ref, b_ref, g_ref, lb_ref, o_ref):
    t = o_ref.shape[1]
    acc = win_ref[:, 2:2 + t, :] * w_ref[0:1, :]
    for j in range(1, CONV_WIDTH):
        acc = acc + win_ref[:, 2 + j:2 + j + t, :] * w_ref[j:j + 1, :]
    y = _ln_swish(acc + b_ref[...], g_ref[...], lb_ref[...])
    o_ref[...] = y


def _conv_sample(upad, w):
    b, s_pad, _ = upad.shape
    t = s_pad - CONV_HALO
    bb = 8
    return pl.pallas_call(
        _conv_sample_kernel, grid=(b // bb,),
        in_specs=[pl.BlockSpec((bb, s_pad, CONV_CH), lambda i: (i, 0, 0)),
                  _const_spec((CONV_HALO, CONV_CH)), _const_spec((1, CONV_CH)),
                  _const_spec((1, CONV_CH)), _const_spec((1, CONV_CH))],
        out_specs=pl.BlockSpec((bb, t, CONV_CH), lambda i: (i, 0, 0)),
        out_shape=jax.ShapeDtypeStruct((b, t, CONV_CH), F32),
        compiler_params=_params("parallel"),
        name="conv_sample")(upad, w['conv_w'], w['conv_b'], w['conv_ln_g'], w['conv_ln_b'])


def _attn_prompt_kernel(q_ref, k_ref, v_ref, o_ref, s_ref):
    tq = q_ref.shape[1]
    qi = pl.program_id(2)
    nh = ATTN_HEADS
    head = lambda hd: slice(hd * HEAD_PAD, (hd + 1) * HEAD_PAD)
    qs = [q_ref[0, :, head(hd)] for hd in range(nh)]
    rows = lax.broadcasted_iota(jnp.int32, (tq, tq), 0)
    cols = lax.broadcasted_iota(jnp.int32, (tq, tq), 1)

    def scores(j, ms, masked):
        start = pl.multiple_of(j * tq, tq)
        out = []
        for hd in range(nh):
            s = _dot_t(qs[hd], k_ref[0, pl.ds(start, tq), head(hd)])
            if masked:
                s = jnp.where(cols <= rows, s, NEG_INF)
            s_ref[hd, j] = s
            out.append(jnp.maximum(ms[hd], jnp.max(s, axis=-1, keepdims=True)))
        return tuple(out)

    ms = tuple(jnp.full((tq, 1), NEG_INF, F32) for _ in range(nh))
    ms = lax.fori_loop(0, qi, functools.partial(scores, masked=False), ms)
    ms = scores(qi, ms, True)

    def values(j, accs):
        start = pl.multiple_of(j * tq, tq)
        out = []
        for hd in range(nh):
            p = jnp.exp(s_ref[hd, j] - ms[hd]).astype(BF16)
            out.append(accs[hd] + _dot(p, v_ref[0, pl.ds(start, tq), head(hd)]))
        return tuple(out)

    accs = tuple(jnp.zeros((tq, HEAD_PAD), F32) for _ in range(nh))
    accs = lax.fori_loop(0, qi + 1, values, accs)
    lane = lax.broadcasted_iota(jnp.int32, (tq, HEAD_PAD), 1)
    for hp in range(nh // 2):
        even, odd = accs[2 * hp], accs[2 * hp + 1]
        o_even = even / even[:, V_DIM:V_DIM + 1]
        o_odd = odd / odd[:, 0:1]
        o_ref[0, :, hp * LANES:(hp + 1) * LANES] = jnp.where(lane < V_DIM, o_even, o_odd).astype(BF16)


def _attn_prompt(q, k, v):
    b, s, _ = q.shape
    tq = ATTN_TILE
    nh = ATTN_HEADS
    return pl.pallas_call(
        _attn_prompt_kernel, grid=(b, N_HEADS // nh, s // tq),
        in_specs=[pl.BlockSpec((1, tq, nh * HEAD_PAD), lambda bi, hq, qi: (bi, qi, hq)),
                  pl.BlockSpec((1, s, nh * HEAD_PAD), lambda bi, hq, qi: (bi, 0, hq)),
                  pl.BlockSpec((1, s, nh * HEAD_PAD), lambda bi, hq, qi: (bi, 0, hq))],
        out_specs=pl.BlockSpec((1, tq, nh * V_DIM), lambda bi, hq, qi: (bi, qi, hq)),
        out_shape=jax.ShapeDtypeStruct((b, s, N_HEADS * V_DIM), BF16),
        scratch_shapes=[pltpu.VMEM((nh, s // tq, tq, tq), F32)],
        compiler_params=_params("parallel", "parallel", "arbitrary", vmem=VMEM_LIMIT),
        name="attn_prompt")(q, k, v)


def _attn_sample_kernel(pt_ref, ql_ref, qp_ref, cn_ref, kn_ref, ckv_hbm, kpe_hbm, o_ref,
                        ckv_buf, kpe_buf, kb_ref, s_ref, sem):
    b = pl.program_id(0)
    n_pages = kpe_buf.shape[1]
    n_chunks = n_pages // PAGES_PER_CHUNK
    chunk = PAGES_PER_CHUNK * PAGE_SIZE
    t_new = cn_ref.shape[1]
    rows_q = N_HEADS * t_new

    def fetch(batch, slot):
        def one(p, _):
            page = pt_ref[batch * n_pages + p]
            rows = pl.ds(pl.multiple_of(p * PAGE_SIZE, PAGE_SIZE), PAGE_SIZE)
            pltpu.make_async_copy(ckv_hbm.at[page], ckv_buf.at[slot, rows], sem.at[0, slot]).start()
            pltpu.make_async_copy(kpe_hbm.at[page], kpe_buf.at[slot, p], sem.at[1, slot]).start()
            return 0
        lax.fori_loop(0, n_pages, one, 0)

    slot = b % 2

    @pl.when(b == 0)
    def _():
        fetch(0, 0)

    @pl.when(b + 1 < pl.num_programs(0))
    def _():
        fetch(b + 1, 1 - slot)

    pltpu.make_async_copy(ckv_buf.at[slot], ckv_buf.at[slot], sem.at[0, slot]).wait()
    pltpu.make_async_copy(kpe_buf.at[slot], kpe_buf.at[slot], sem.at[1, slot]).wait()

    ql = ql_ref[:, 0].reshape(rows_q, KV_RANK).astype(BF16)
    qp = qp_ref[:, 0].reshape(rows_q, QK_ROPE).astype(BF16)

    pad = PAGE_SIZE - t_new
    kc_new = jnp.concatenate([cn_ref[0], jnp.zeros((pad, KV_RANK), F32)], axis=0).astype(BF16)
    kp_new = jnp.concatenate([kn_ref[0], jnp.zeros((pad, QK_ROPE), F32)], axis=0).astype(BF16)
    s_new = _dot_t(ql, kc_new) + _dot_t(qp, kp_new)
    t_q = lax.broadcasted_iota(jnp.int32, s_new.shape, 0) % t_new
    cols = lax.broadcasted_iota(jnp.int32, s_new.shape, 1)
    s_new = jnp.where(cols <= t_q, s_new, NEG_INF)
    m = jnp.max(s_new, axis=-1, keepdims=True)

    for c in range(n_chunks):
        kc = ckv_buf[slot, c * chunk:(c + 1) * chunk, :].astype(BF16)
        kb_ref[c * chunk:(c + 1) * chunk, :] = kc
        kpt = jnp.concatenate([kpe_buf[slot, c * PAGES_PER_CHUNK + i] for i in range(PAGES_PER_CHUNK)],
                              axis=1).astype(BF16)
        s = _dot_t(ql, kc) + _dot(qp, kpt)
        s_ref[c] = s
        m = jnp.maximum(m, jnp.max(s, axis=-1, keepdims=True))

    p_new = jnp.exp(s_new - m)
    l = jnp.sum(p_new, axis=-1, keepdims=True)
    acc = _dot(p_new.astype(BF16), kc_new)
    for c in range(n_chunks):
        p = jnp.exp(s_ref[c] - m)
        l = l + jnp.sum(p, axis=-1, keepdims=True)
        acc = acc + _dot(p.astype(BF16), kb_ref[c * chunk:(c + 1) * chunk, :])
    o_ref[:, 0] = (acc / l).reshape(N_HEADS, t_new, KV_RANK)


def _attn_sample(page_table, qlat, qpe, ckv_new, kpe_new, cache_ckv, cache_kpe_t):
    bs, n_pages = page_table.shape
    t_new = ckv_new.shape[1]
    qlat4 = qlat.reshape(N_HEADS, bs, t_new, KV_RANK)
    qpe4 = qpe.reshape(N_HEADS, bs, t_new, QK_ROPE)
    past = n_pages * PAGE_SIZE
    in_specs = [pl.BlockSpec((N_HEADS, 1, t_new, KV_RANK), lambda b, pt: (0, b, 0, 0)),
                pl.BlockSpec((N_HEADS, 1, t_new, QK_ROPE), lambda b, pt: (0, b, 0, 0)),
                pl.BlockSpec((1, t_new, KV_RANK), lambda b, pt: (b, 0, 0)),
                pl.BlockSpec((1, t_new, QK_ROPE), lambda b, pt: (b, 0, 0)),
                pl.BlockSpec(memory_space=pl.ANY), pl.BlockSpec(memory_space=pl.ANY)]
    grid_spec = pltpu.PrefetchScalarGridSpec(
        num_scalar_prefetch=1, grid=(bs,), in_specs=in_specs,
        out_specs=pl.BlockSpec((N_HEADS, 1, t_new, KV_RANK), lambda b, pt: (0, b, 0, 0)),
        scratch_shapes=[pltpu.VMEM((2, past, KV_RANK), F32),
                        pltpu.VMEM((2, n_pages, QK_ROPE, PAGE_SIZE), F32),
                        pltpu.VMEM((past, KV_RANK), BF16),
                        pltpu.VMEM((n_pages // PAGES_PER_CHUNK, N_HEADS * t_new, PAGES_PER_CHUNK * PAGE_SIZE), F32),
                        pltpu.SemaphoreType.DMA((2, 2))])
    o = pl.pallas_call(
        _attn_sample_kernel, grid_spec=grid_spec,
        out_shape=jax.ShapeDtypeStruct((N_HEADS, bs, t_new, KV_RANK), F32),
        compiler_params=_params("arbitrary", vmem=VMEM_LIMIT),
        name="attn_sample")(page_table.reshape(-1), qlat4, qpe4, ckv_new, kpe_new, cache_ckv, cache_kpe_t)
    return o.reshape(N_HEADS, bs * t_new, KV_RANK)


def _post_a_kernel(sample, x_ref, conv_ref, attn_ref, *rest):
    if sample:
        wuv_ref, wout_ref, g_ref, wmq_ref, x1_ref, qm_ref = rest
        attn = None
        for hd in range(N_HEADS):
            d = _dot(attn_ref[hd].astype(BF16), wuv_ref[hd])
            attn = d if attn is None else attn + d
        attn = attn.astype(BF16)
    else:
        wout_ref, g_ref, wmq_ref, x1_ref, qm_ref = rest
        attn = attn_ref[...]
    mix = _dot(conv_ref[...].astype(BF16), wout_ref[0:CONV_CH, :]) + _dot(attn, wout_ref[CONV_CH:, :])
    x1 = x_ref[...] + mix
    x1_ref[...] = x1
    qm_ref[...] = _dot(_rms(x1, g_ref[...]).astype(BF16), wmq_ref[...])


def _post_a(x2d, conv2d, attn, w, sample):
    t = x2d.shape[0]
    tm = TOKEN_TILE
    row = lambda n: pl.BlockSpec((tm, n), lambda i: (i, 0))
    mem_w = MEM_HEADS * MEM_HD
    if sample:
        in_specs = [row(D_MODEL), row(CONV_CH), pl.BlockSpec((N_HEADS, tm, KV_RANK), lambda i: (0, i, 0)),
                    _const_spec((N_HEADS, KV_RANK, N_HEADS * V_DIM))]
        args = [x2d, conv2d, attn, w['wuv_pad']]
    else:
        in_specs = [row(D_MODEL), row(CONV_CH), row(N_HEADS * V_DIM)]
        args = [x2d, conv2d, attn]
    in_specs += [_const_spec((D_MODEL, D_MODEL)), _const_spec((1, D_MODEL)), _const_spec((D_MODEL, mem_w))]
    args += [w['w_out'], w['g_mem_q'], w['w_mq']]
    return pl.pallas_call(
        functools.partial(_post_a_kernel, sample), grid=(t // tm,), in_specs=in_specs,
        out_specs=[row(D_MODEL), row(mem_w)],
        out_shape=[jax.ShapeDtypeStruct((t, D_MODEL), F32), jax.ShapeDtypeStruct((t, mem_w), F32)],
        compiler_params=_params("parallel", vmem=VMEM_LIMIT),
        name="post_a_sample" if sample else "post_a_prompt")(*args)


def _mem_kv_kernel(m_ref, g_ref, wk_ref, wv_ref, k_ref, v_ref, kw_ref, vw_ref):
    tm = m_ref.shape[0]
    m = _rms(m_ref[...], g_ref[...]).astype(BF16)
    k = _dot(m, wk_ref[...])
    v = _dot(m, wv_ref[...])
    kw_ref[...] = k
    vw_ref[...] = v
    for hd in range(MEM_HEADS):
        sl = slice(hd * MEM_HD, (hd + 1) * MEM_HD)
        k_ref[pl.ds(hd, tm, stride=MEM_HEADS), :] = k[:, sl]
        v_ref[pl.ds(hd, tm, stride=MEM_HEADS), :] = v[:, sl]


def _mem_kv(mem2d, w):
    t = mem2d.shape[0]
    tm = TOKEN_TILE
    mem_w = MEM_HEADS * MEM_HD
    rows = pl.BlockSpec((tm * MEM_HEADS, MEM_HD), lambda i: (i, 0))
    wide = pl.BlockSpec((tm, mem_w), lambda i: (i, 0))
    return pl.pallas_call(
        _mem_kv_kernel, grid=(t // tm,),
        in_specs=[pl.BlockSpec((tm, D_MODEL), lambda i: (i, 0)), _const_spec((1, D_MODEL)),
                  _const_spec((D_MODEL, mem_w)), _const_spec((D_MODEL, mem_w))],
        out_specs=[rows, rows, wide, wide],
        out_shape=[jax.ShapeDtypeStruct((t * MEM_HEADS, MEM_HD), F32)] * 2
                  + [jax.ShapeDtypeStruct((t, mem_w), F32)] * 2,
        compiler_params=_params("parallel"),
        name="mem_kv")(mem2d, w['g_mem_kv'], w['w_mk'], w['w_mv'])


def _mem_attn_wide_kernel(q_ref, k_ref, v_ref, o_ref):
    for hd in range(MEM_HEADS):
        sl = slice(hd * MEM_HD, (hd + 1) * MEM_HD)
        s = _dot_t(q_ref[0, :, sl].astype(BF16), k_ref[0, :, sl].astype(BF16)) * MEM_SCALE
        p = jnp.exp(s - jnp.max(s, axis=-1, keepdims=True))
        p = p / jnp.sum(p, axis=-1, keepdims=True)
        o_ref[0, :, sl] = _dot(p.astype(BF16), v_ref[0, :, sl].astype(BF16))


def _mem_attn_wide(qm, mem_k, mem_v):
    b, s, mem_w = qm.shape
    tq = min(TOKEN_TILE, s)
    kv = pl.BlockSpec((1, MEM_TOKENS, mem_w), lambda bi, i: (bi, 0, 0))
    return pl.pallas_call(
        _mem_attn_wide_kernel, grid=(b, s // tq),
        in_specs=[pl.BlockSpec((1, tq, mem_w), lambda bi, i: (bi, i, 0)), kv, kv],
        out_specs=pl.BlockSpec((1, tq, mem_w), lambda bi, i: (bi, i, 0)),
        out_shape=jax.ShapeDtypeStruct((b, s, mem_w), F32),
        compiler_params=_params("parallel", "parallel"),
        name="mem_attn_wide")(qm, mem_k, mem_v)


def _mem_attn_rows_kernel(q_ref, k_ref, v_ref, o_ref):
    bb, tq, _ = q_ref.shape
    rows = MEM_TOKENS * MEM_HEADS
    head = lambda hd: slice(hd * MEM_HD, (hd + 1) * MEM_HD)
    q_head = lax.broadcasted_iota(jnp.int32, (MEM_HEADS * tq, rows), 0) // tq
    k_head = lax.broadcasted_iota(jnp.int32, (MEM_HEADS * tq, rows), 1) % MEM_HEADS
    own = q_head == k_head
    for bi in range(bb):
        q = jnp.concatenate([q_ref[bi, :, head(hd)] for hd in range(MEM_HEADS)], axis=0).astype(BF16)
        k = k_ref[bi * rows:(bi + 1) * rows, :].astype(BF16)
        v = v_ref[bi * rows:(bi + 1) * rows, :].astype(BF16)
        s = jnp.where(own, _dot_t(q, k) * MEM_SCALE, NEG_INF)
        p = jnp.exp(s - jnp.max(s, axis=-1, keepdims=True))
        p = p / jnp.sum(p, axis=-1, keepdims=True)
        o = _dot(p.astype(BF16), v)
        for hd in range(MEM_HEADS):
            o_ref[bi, :, head(hd)] = o[hd * tq:(hd + 1) * tq, :]


def _mem_attn_rows(qm, mem_k, mem_v):
    b, s, mem_w = qm.shape
    bb = max(1, MEM_ATTN_ROWS // s)
    kv = pl.BlockSpec((bb * MEM_TOKENS * MEM_HEADS, MEM_HD), lambda bi: (bi, 0))
    return pl.pallas_call(
        _mem_attn_rows_kernel, grid=(b // bb,),
        in_specs=[pl.BlockSpec((bb, s, mem_w), lambda bi: (bi, 0, 0)), kv, kv],
        out_specs=pl.BlockSpec((bb, s, mem_w), lambda bi: (bi, 0, 0)),
        out_shape=jax.ShapeDtypeStruct((b, s, mem_w), F32),
        compiler_params=_params("parallel"),
        name="mem_attn_rows")(qm, mem_k, mem_v)


def _slab_rows(j, n):
    return pl.ds(j, n, stride=SLAB)


def _post_b_kernel(n_p, x1p_ref, x1s_ref, omp_ref, oms_ref, wmo_ref, g_ref, wr_ref, br_ref,
                   x2_ref, h3_ref, idx_ref, gate_ref, rank_ref, cnt_ref, carry_ref):
    tm = x2_ref.shape[0]
    i = pl.program_id(0)

    @pl.when(i == 0)
    def _():
        carry_ref[...] = jnp.zeros(carry_ref.shape, F32)

    is_p = i < n_p
    x1 = jnp.where(is_p, x1p_ref[...], x1s_ref[...])
    om = jnp.where(is_p, omp_ref[...], oms_ref[...])
    x2 = x1 + _dot(om.astype(BF16), wmo_ref[...])
    x2_ref[...] = x2
    h3 = _rms(x2, g_ref[...])
    for j in range(SLAB):
        h3_ref[_slab_rows(j, tm), :] = h3[:, j * LANES:(j + 1) * LANES]
    logits = _dot(h3.astype(BF16), wr_ref[...]) + br_ref[...]
    lane = lax.broadcasted_iota(jnp.int32, logits.shape, 1)
    lane_f = lane.astype(F32)
    vals, hots = [], []
    idx_out = jnp.zeros(logits.shape, F32)
    for kk in range(TOP_K):
        mx = jnp.max(logits, axis=-1, keepdims=True)
        first = jnp.min(jnp.where(logits == mx, lane_f, float(LANES)), axis=-1, keepdims=True)
        hot = lane_f == first
        logits = jnp.where(hot, -jnp.inf, logits)
        vals.append(mx)
        hots.append(hot)
        idx_out = jnp.where(lane == kk, first, idx_out)
    exps = [jnp.exp(v - vals[0]) for v in vals]
    denom = exps[0] + exps[1] + exps[2] + exps[3]
    chosen = jnp.zeros(logits.shape, F32)
    gate_out = jnp.zeros(logits.shape, F32)
    for kk in range(TOP_K):
        chosen = chosen + hots[kk].astype(F32)
        gate_out = jnp.where(lane == kk, exps[kk] / denom, gate_out)
    r_i = lax.broadcasted_iota(jnp.int32, (tm, tm), 0)
    c_i = lax.broadcasted_iota(jnp.int32, (tm, tm), 1)
    tril = (c_i < r_i).astype(BF16)
    before = _dot(tril, chosen.astype(BF16)) + carry_ref[...]
    rank_out = jnp.zeros(logits.shape, F32)
    for kk in range(TOP_K):
        rk = jnp.sum(jnp.where(hots[kk], before, 0.0), axis=-1, keepdims=True)
        rank_out = jnp.where(lane == kk, rk, rank_out)
    carry = carry_ref[...] + jnp.sum(chosen, axis=0, keepdims=True)
    carry_ref[...] = carry
    idx_ref[...] = idx_out.astype(jnp.int32)
    gate_ref[...] = gate_out
    rank_ref[...] = rank_out.astype(jnp.int32)
    cnt_ref[...] = carry.astype(jnp.int32)


def _post_b(x1_p, om_p, x1_s, om_s, w):
    tm = TOKEN_TILE
    n_p, n_s = x1_p.shape[0] // tm, x1_s.shape[0] // tm
    t = (n_p + n_s) * tm
    mem_w = MEM_HEADS * MEM_HD
    row = lambda n: pl.BlockSpec((tm, n), lambda i: (i, 0))
    row_p = lambda n: pl.BlockSpec((tm, n), lambda i: (jnp.minimum(i, n_p - 1), 0))
    row_s = lambda n: pl.BlockSpec((tm, n), lambda i: (jnp.maximum(i - n_p, 0), 0))
    return pl.pallas_call(
        functools.partial(_post_b_kernel, n_p), grid=(n_p + n_s,),
        in_specs=[row_p(D_MODEL), row_s(D_MODEL), row_p(mem_w), row_s(mem_w),
                  _const_spec((mem_w, D_MODEL)), _const_spec((1, D_MODEL)),
                  _const_spec((D_MODEL, LANES)), _const_spec((1, LANES))],
        out_specs=[row(D_MODEL), pl.BlockSpec((tm * SLAB, LANES), lambda i: (i, 0)),
                   row(LANES), row(LANES), row(LANES), _const_spec((1, LANES))],
        out_shape=[jax.ShapeDtypeStruct((t, D_MODEL), F32), jax.ShapeDtypeStruct((t * SLAB, LANES), F32),
                   jax.ShapeDtypeStruct((t, LANES), jnp.int32), jax.ShapeDtypeStruct((t, LANES), F32),
                   jax.ShapeDtypeStruct((t, LANES), jnp.int32), jax.ShapeDtypeStruct((1, LANES), jnp.int32)],
        scratch_shapes=[pltpu.VMEM((1, LANES), F32)],
        compiler_params=_params("arbitrary", vmem=VMEM_LIMIT),
        name="post_b")(x1_p, x1_s, om_p, om_s, w['w_mo'], w['g_ffn'], w['w_router'], w['b_router'])


def _slab(row):
    return pl.ds(pl.multiple_of(row * SLAB, SLAB), SLAB)


def _dispatch_kernel(start_ref, lo_ref, hi_ref, idx_ref, rank_ref, h_ref, xs_hbm, zero_ref, sem):
    i = pl.program_id(0)
    tm = h_ref.shape[0] // SLAB

    def issue(t, _):
        for k in range(TOP_K):
            r = t * TOP_K + k
            d = start_ref[idx_ref[r]] + rank_ref[r]
            pltpu.make_async_copy(h_ref.at[_slab(t)], xs_hbm.at[_slab(d)], sem).start(priority=k % 2)
        return 0

    lax.fori_loop(0, tm, issue, 0, unroll=2)
    for _ in range(TOP_K):
        pltpu.make_async_copy(h_ref, xs_hbm.at[pl.ds(0, tm * SLAB)], sem).wait()

    @pl.when(i == pl.num_programs(0) - 1)
    def _():
        zero_ref[...] = jnp.zeros(zero_ref.shape, F32)
        for e in range(N_EXPERTS):
            lo, hi = lo_ref[e], hi_ref[e]

            def zissue(r, _):
                pltpu.make_async_copy(zero_ref.at[_slab(0)], xs_hbm.at[_slab(r)], sem).start()
                return 0

            def zdrain(r, _):
                pltpu.make_async_copy(zero_ref.at[_slab(0)], xs_hbm.at[_slab(0)], sem).wait()
                return 0

            lax.fori_loop(lo, hi, zissue, 0)
            lax.fori_loop(lo, hi, zdrain, 0)

        bm = zero_ref.shape[0] // SLAB

        def tail(blk, _):
            rows = pl.ds(pl.multiple_of(blk * bm * SLAB, SLAB), bm * SLAB)
            cp = pltpu.make_async_copy(zero_ref, xs_hbm.at[rows], sem)
            cp.start()
            cp.wait()
            return 0

        lax.fori_loop(hi_ref[N_EXPERTS - 1] // bm, xs_hbm.shape[0] // (bm * SLAB), tail, 0)


def _dispatch(h3s, idx_flat, rank_flat, pad_start, pad_lo, pad_hi, n_rows):
    tm = TOKEN_TILE
    t = h3s.shape[0] // SLAB
    grid_spec = pltpu.PrefetchScalarGridSpec(
        num_scalar_prefetch=3, grid=(t // tm,),
        in_specs=[pl.BlockSpec((tm * TOP_K,), lambda i, *_: (i,), memory_space=pltpu.SMEM),
                  pl.BlockSpec((tm * TOP_K,), lambda i, *_: (i,), memory_space=pltpu.SMEM),
                  pl.BlockSpec((tm * SLAB, LANES), lambda i, *_: (i, 0))],
        out_specs=pl.BlockSpec(memory_space=pl.ANY),
        scratch_shapes=[pltpu.VMEM((MOE_ROWS * SLAB, LANES), F32), pltpu.SemaphoreType.DMA])
    return pl.pallas_call(
        _dispatch_kernel, grid_spec=grid_spec,
        out_shape=jax.ShapeDtypeStruct((n_rows * SLAB, LANES), F32),
        compiler_params=pltpu.CompilerParams(dimension_semantics=("arbitrary",), has_side_effects=True),
        name="moe_dispatch")(pad_start, pad_lo, pad_hi, idx_flat, rank_flat, h3s)


def _expert_kernel(src_ref, exp_ref, first_ref, valid_ref, next_ref, x_ref, wg_hbm, wu_hbm, wd_hbm,
                   bg_ref, bu_ref, bd_ref, y_ref, wf_ref, wb_ref, xb_ref, sem):
    b = pl.program_id(0)
    bm = xb_ref.shape[0]
    w_hbm = (wg_hbm, wu_hbm, wd_hbm)

    def fetch(e):
        for i in range(3):
            pltpu.make_async_copy(w_hbm[i].at[e], wf_ref.at[i], sem).start()

    @pl.when(b == 0)
    def _():
        fetch(exp_ref[0])

    @pl.when(first_ref[b] == 1)
    def _():
        for i in range(3):
            pltpu.make_async_copy(w_hbm[i].at[0], wf_ref.at[i], sem).wait()
        def cast(r, _):
            rows = pl.ds(pl.multiple_of(r * CAST_ROWS, CAST_ROWS), CAST_ROWS)
            for i in range(3):
                wb_ref[i, rows, :] = wf_ref[i, rows, :].astype(BF16)
            return 0

        lax.fori_loop(0, D_MODEL // CAST_ROWS, cast, 0)

        @pl.when(next_ref[b] >= 0)
        def _():
            fetch(next_ref[b])

    @pl.when(valid_ref[b] == 1)
    def _():
        for j in range(SLAB):
            xb_ref[:, j * LANES:(j + 1) * LANES] = x_ref[_slab_rows(j, bm), :].astype(BF16)
        x = xb_ref[...]
        g = _dot(x, wb_ref[0]) + bg_ref[0]
        u = _dot(x, wb_ref[1]) + bu_ref[0]
        g = jnp.minimum(g, SWIGLU_LIMIT)
        u = jnp.clip(u, -SWIGLU_LIMIT, SWIGLU_LIMIT)
        a = (u + 1.0) * (g * jax.nn.sigmoid(SWIGLU_ALPHA * g))
        y = _dot(a.astype(BF16), wb_ref[2]) + bd_ref[0]
        for j in range(SLAB):
            y_ref[_slab_rows(j, bm), :] = y[:, j * LANES:(j + 1) * LANES]

    @pl.when(valid_ref[b] == 0)
    def _():
        y_ref[...] = jnp.zeros(y_ref.shape, F32)


def _experts(xs, blk_src, blk_exp, blk_first, blk_valid, blk_next, w):
    bm = MOE_ROWS
    n_blocks = xs.shape[0] // (bm * SLAB)
    hbm = pl.BlockSpec(memory_space=pl.ANY)
    bspec = pl.BlockSpec((1, 1, D_MODEL), lambda b, src, ex, *_: (ex[b], 0, 0))
    grid_spec = pltpu.PrefetchScalarGridSpec(
        num_scalar_prefetch=5, grid=(n_blocks,),
        in_specs=[pl.BlockSpec((bm * SLAB, LANES), lambda b, src, *_: (src[b], 0)),
                  hbm, hbm, hbm, bspec, bspec, bspec],
        out_specs=pl.BlockSpec((bm * SLAB, LANES), lambda b, *_: (b, 0)),
        scratch_shapes=[pltpu.VMEM((3, D_MODEL, D_MODEL), F32), pltpu.VMEM((3, D_MODEL, D_MODEL), BF16),
                        pltpu.VMEM((bm, D_MODEL), BF16), pltpu.SemaphoreType.DMA])
    return pl.pallas_call(
        _expert_kernel, grid_spec=grid_spec,
        out_shape=jax.ShapeDtypeStruct(xs.shape, F32),
        compiler_params=_params("arbitrary", vmem=VMEM_LIMIT),
        name="moe_experts")(blk_src, blk_exp, blk_first, blk_valid, blk_next, xs,
                            w['w_gate'], w['w_up'], w['w_down'], w['b_gate'], w['b_up'], w['b_down'])


def _combine_kernel(n_p, start_ref, idx_ref, rank_ref, yb_hbm, x2_ref, gate_ref, g_ref, yp_ref, ys_ref,
                    rows_ref, y_ref, sem):
    i = pl.program_id(0)
    tm = x2_ref.shape[0]

    def issue(t, _):
        for k in range(TOP_K):
            r = t * TOP_K + k
            d = start_ref[idx_ref[r]] + rank_ref[r]
            pltpu.make_async_copy(yb_hbm.at[_slab(d)], rows_ref.at[_slab(r)], sem).start(priority=k % 2)
        return 0

    lax.fori_loop(0, tm, issue, 0, unroll=2)
    for _ in range(TOP_K):
        pltpu.make_async_copy(yb_hbm.at[pl.ds(0, tm * SLAB)], rows_ref.at[pl.ds(0, tm * SLAB)], sem).wait()

    gate = gate_ref[...]
    ss = jnp.zeros((tm, 1), F32)
    for j in range(SLAB):
        y = x2_ref[:, j * LANES:(j + 1) * LANES]
        for k in range(TOP_K):
            y = y + rows_ref[pl.ds(k * SLAB + j, tm, stride=TOP_K * SLAB), :] * gate[:, k:k + 1]
        y_ref[:, j * LANES:(j + 1) * LANES] = y
        ss = ss + jnp.sum(y * y, axis=-1, keepdims=True)
    out = y_ref[...] * lax.rsqrt(ss * (1.0 / D_MODEL) + EPS) * g_ref[...]

    @pl.when(i < n_p)
    def _():
        yp_ref[...] = out

    @pl.when(i >= n_p)
    def _():
        ys_ref[...] = out


def _combine(yb, idx_flat, rank_flat, pad_start, x2, gate128, g_final, n_p):
    t = x2.shape[0]
    tm = TOKEN_TILE
    n_s = t // tm - n_p
    grid_spec = pltpu.PrefetchScalarGridSpec(
        num_scalar_prefetch=1, grid=(t // tm,),
        in_specs=[pl.BlockSpec((tm * TOP_K,), lambda i, *_: (i,), memory_space=pltpu.SMEM),
                  pl.BlockSpec((tm * TOP_K,), lambda i, *_: (i,), memory_space=pltpu.SMEM),
                  pl.BlockSpec(memory_space=pl.ANY),
                  pl.BlockSpec((tm, D_MODEL), lambda i, *_: (i, 0)),
                  pl.BlockSpec((tm, LANES), lambda i, *_: (i, 0)),
                  pl.BlockSpec((1, D_MODEL), lambda i, *_: (0, 0))],
        out_specs=[pl.BlockSpec((tm, D_MODEL), lambda i, *_: (jnp.minimum(i, n_p - 1), 0)),
                   pl.BlockSpec((tm, D_MODEL), lambda i, *_: (jnp.maximum(i - n_p, 0), 0))],
        scratch_shapes=[pltpu.VMEM((tm * TOP_K * SLAB, LANES), F32), pltpu.VMEM((tm, D_MODEL), F32),
                        pltpu.SemaphoreType.DMA])
    return pl.pallas_call(
        functools.partial(_combine_kernel, n_p), grid_spec=grid_spec,
        out_shape=[jax.ShapeDtypeStruct((n_p * tm, D_MODEL), F32), jax.ShapeDtypeStruct((n_s * tm, D_MODEL), F32)],
        compiler_params=_params("arbitrary"),
        name="moe_combine")(pad_start, idx_flat, rank_flat, yb, x2, gate128, g_final)


def _moe_and_final(x2, h3s, idx128, gate128, rank128, cnt128, w, n_p):
    t = x2.shape[0]
    bm = MOE_ROWS
    n_assign = t * TOP_K
    n_blocks = (n_assign + N_EXPERTS * (bm - 1) + bm - 1) // bm
    counts = cnt128[0, :N_EXPERTS]
    padded = ((counts + bm - 1) // bm) * bm
    pad_end = jnp.cumsum(padded).astype(jnp.int32)
    pad_start = pad_end - padded
    idx_flat = idx128[:, :TOP_K].reshape(-1)
    rank_flat = rank128[:, :TOP_K].reshape(-1)
    blk = jnp.arange(n_blocks, dtype=jnp.int32)
    blk_valid = (blk * bm < pad_end[-1]).astype(jnp.int32)
    blk_src = jnp.minimum(blk, jnp.maximum(pad_end[-1] // bm - 1, 0))
    blk_exp = jnp.sum((pad_end[None, :] <= (blk_src * bm)[:, None]).astype(jnp.int32), axis=1)
    blk_exp = jnp.minimum(blk_exp, N_EXPERTS - 1)
    blk_first = jnp.concatenate([jnp.ones((1,), jnp.int32),
                                 (blk_exp[1:] != blk_exp[:-1]).astype(jnp.int32)])
    later_start = (blk[None, :] > blk[:, None]) & (blk_first[None, :] == 1)
    next_pos = jnp.min(jnp.where(later_start, blk[None, :], n_blocks), axis=1)
    blk_next = jnp.where(next_pos < n_blocks, blk_exp[jnp.minimum(next_pos, n_blocks - 1)], -1)
    xs = _dispatch(h3s, idx_flat, rank_flat, pad_start, pad_start + counts, pad_end, n_blocks * bm)
    yb = _experts(xs, blk_src, blk_exp, blk_first, blk_valid, blk_next.astype(jnp.int32), w)
    return _combine(yb, idx_flat, rank_flat, pad_start, x2, gate128, w['g_final'], n_p)


def _swap_halves(wcols):
    half = QK_ROPE // 2
    return jnp.concatenate([-wcols[..., half:], wcols[..., :half]], axis=-1)


def _prep_weights(g_mix, w_in, conv_w, conv_b, conv_ln_g, conv_ln_b, q_norm_g, w_q_up, kv_norm_g, w_kv_up,
                  w_out, g_mem_q, g_mem_kv, w_mq, w_mk, w_mv, w_mo, g_ffn, w_router, b_router,
                  w_gate, b_gate, w_up, b_up, w_down, b_down, g_final):
    l = 0
    w_kpe = w_in[l][:, C_KPE:]
    zpad = jnp.zeros((D_MODEL, HEAD_PAD - QK_ROPE), F32)
    w_in_ext = jnp.concatenate([w_in[l][:, :C_KPE], w_kpe, zpad, _swap_halves(w_kpe), zpad], axis=1)
    wq3 = w_q_up[l].reshape(Q_RANK, N_HEADS, QK_NOPE + QK_ROPE)
    q_nope, q_rope = wq3[..., :QK_NOPE], wq3[..., QK_NOPE:]
    z32 = jnp.zeros((Q_RANK, N_HEADS, HEAD_PAD - QK_NOPE - QK_ROPE), F32)
    wq = jnp.concatenate([q_rope, q_nope, z32], axis=-1).reshape(Q_RANK, N_HEADS * HEAD_PAD)
    wq_sw = jnp.concatenate([_swap_halves(q_rope), jnp.zeros_like(q_nope), z32], axis=-1)
    wq_sw = wq_sw.reshape(Q_RANK, N_HEADS * HEAD_PAD)
    w_uk = w_kv_up[l][:, :, :QK_NOPE]
    w_uv = w_kv_up[l][:, :, QK_NOPE:]
    wuk_pad = jnp.concatenate([jnp.zeros((KV_RANK, N_HEADS, QK_ROPE), F32), w_uk,
                               jnp.zeros((KV_RANK, N_HEADS, HEAD_PAD - QK_NOPE - QK_ROPE), F32)], axis=-1)
    wukt_pad = jnp.transpose(wuk_pad, (1, 2, 0))
    odd_head = (jnp.arange(N_HEADS) % 2 == 1)[None, :, None]
    zv = jnp.zeros_like(w_uv)
    wuv_slot = jnp.where(odd_head, jnp.concatenate([zv, w_uv], axis=-1), jnp.concatenate([w_uv, zv], axis=-1))
    lane_id = jnp.arange(HEAD_PAD)[None, :]
    v_ones = jnp.where(odd_head[0], lane_id == 0, lane_id == V_DIM).astype(F32)
    eye = jnp.eye(N_HEADS, dtype=F32)
    wuv_pad = (w_uv.transpose(1, 0, 2)[:, :, None, :] * eye[:, None, :, None])
    conv_w_pad = jnp.concatenate([conv_w[l], jnp.zeros((CONV_HALO - CONV_WIDTH, CONV_CH), F32)], axis=0)
    w_router_pad = jnp.concatenate([w_router[l], jnp.zeros((D_MODEL, LANES - N_EXPERTS), F32)], axis=1)
    b_router_pad = jnp.concatenate([b_router[l], jnp.full((LANES - N_EXPERTS,), NEG_INF, F32)])
    return {
        'g_mix': g_mix[l][None], 'w_in_ext': w_in_ext.astype(BF16),
        'q_norm_g': q_norm_g[l][None], 'wq': wq.astype(BF16), 'wq_sw': wq_sw.astype(BF16),
        'kv_norm_g': kv_norm_g[l][None],
        'wuk_pad': wuk_pad.reshape(KV_RANK, N_HEADS * HEAD_PAD).astype(BF16),
        'wuv_slot': wuv_slot.reshape(KV_RANK, N_HEADS * HEAD_PAD).astype(BF16),
        'v_ones': v_ones.reshape(1, N_HEADS * HEAD_PAD),
        'wukt_pad': wukt_pad.astype(BF16),
        'wuv_pad': wuv_pad.reshape(N_HEADS, KV_RANK, N_HEADS * V_DIM).astype(BF16),
        'conv_w': conv_w_pad, 'conv_b': conv_b[l][None],
        'conv_ln_g': conv_ln_g[l][None], 'conv_ln_b': conv_ln_b[l][None],
        'w_out': w_out[l].astype(BF16), 'g_mem_q': g_mem_q[l][None], 'w_mq': w_mq[l].astype(BF16),
        'g_mem_kv': g_mem_kv[l][None], 'w_mk': w_mk[l].astype(BF16), 'w_mv': w_mv[l].astype(BF16),
        'w_mo': w_mo[l].astype(BF16), 'g_ffn': g_ffn[l][None],
        'w_router': w_router_pad.astype(BF16), 'b_router': b_router_pad[None],
        'w_gate': w_gate[l], 'w_up': w_up[l], 'w_down': w_down[l],
        'b_gate': b_gate[l][:, None, :], 'b_up': b_up[l][:, None, :], 'b_down': b_down[l][:, None, :],
        'g_final': g_final[None],
    }


def _rope_table(pos):
    half = QK_ROPE // 2
    inv = ROPE_THETA ** (-jnp.arange(half, dtype=F32) / half)
    ang = pos.astype(F32)[:, None] * inv[None, :]
    cos, sin = jnp.cos(ang), jnp.sin(ang)
    n = pos.shape[0]
    ones = jnp.ones((n, QK_NOPE), F32)
    z = lambda k: jnp.zeros((n, k), F32)
    cq = MLA_SCALE * jnp.concatenate([cos, cos, ones, z(HEAD_PAD - QK_NOPE - QK_ROPE)], axis=1)
    sq = MLA_SCALE * jnp.concatenate([sin, sin, z(HEAD_PAD - QK_ROPE)], axis=1)
    ck = jnp.concatenate([cos, cos, z(HEAD_PAD - QK_ROPE)], axis=1)
    sk = jnp.concatenate([sin, sin, z(HEAD_PAD - QK_ROPE)], axis=1)
    return jnp.stack([cq, sq, ck, sk])


def _front(x, conv_prev, mem_k, mem_v, pos, w, paged):
    b, s, _ = x.shape
    t = b * s
    x2d = x.reshape(t, D_MODEL)
    sample = paged is not None
    if sample:
        tab = _rope_table(jnp.tile(pos, TOKEN_TILE // s))
    else:
        tab = _rope_table(pos)
    outs = _inproj(x2d, tab, w, sample)
    u, ckv, kpe = outs[0], outs[1], outs[2]
    u3 = u.reshape(b, s, CONV_CH)
    tail = CONV_WIDTH - 1
    if s >= tail:
        conv_tail = u3[:, s - tail:]
    else:
        conv_tail = jnp.concatenate([conv_prev[:, s:], u3], axis=1)
    prev_pad = jnp.concatenate([jnp.zeros((b, CONV_HALO - tail, CONV_CH), F32), conv_prev], axis=1)
    ckv3 = ckv.reshape(b, s, KV_RANK)
    kpe3 = kpe.reshape(b, s, QK_ROPE)
    if sample:
        page_table, cache_ckv, cache_kpe_t = paged
        conv_out = _conv_sample(jnp.concatenate([prev_pad, u3], axis=1), w)
        attn = _attn_sample(page_table, outs[3], outs[4], ckv3, kpe3, cache_ckv, cache_kpe_t)
    else:
        conv_out = _conv_prompt(u3, prev_pad, w)
        q, k, v = (a.reshape(b, s, -1) for a in outs[3:6])
        attn = _attn_prompt(q, k, v).reshape(t, N_HEADS * V_DIM)
    x1, qm = _post_a(x2d, conv_out.reshape(t, CONV_CH), attn, w, sample)
    mem_attn = _mem_attn_rows if sample else _mem_attn_wide
    om = mem_attn(qm.reshape(b, s, -1), mem_k, mem_v)
    return x1, om.reshape(t, -1), conv_tail, ckv3, kpe3


def kernel(x_prompt, x_sample, mem_prompt, cache_ckv, cache_kpe, page_table, cache_mem_k, cache_mem_v, state_conv, g_mix, w_in, conv_w, conv_b, conv_ln_g, conv_ln_b, q_norm_g, w_q_up, kv_norm_g, w_kv_up, w_out, g_mem_q, g_mem_kv, w_mq, w_mk, w_mv, w_mo, g_ffn, w_router, b_router, w_gate, b_gate, w_up, b_up, w_down, b_down, g_final):
    assert g_mix.shape[0] == 1, "single-layer step"
    w = _prep_weights(g_mix, w_in, conv_w, conv_b, conv_ln_g, conv_ln_b, q_norm_g, w_q_up, kv_norm_g, w_kv_up,
                      w_out, g_mem_q, g_mem_kv, w_mq, w_mk, w_mv, w_mo, g_ffn, w_router, b_router,
                      w_gate, b_gate, w_up, b_up, w_down, b_down, g_final)
    b_p, s_p, _ = x_prompt.shape
    b_s, t_s, _ = x_sample.shape
    past = page_table.shape[1] * PAGE_SIZE

    mk, mv, mk_wide, mv_wide = _mem_kv(mem_prompt.reshape(-1, D_MODEL), w)
    conv0 = jnp.zeros((b_p, CONV_WIDTH - 1, CONV_CH), F32)
    x1_p, om_p, conv_p, ckv_p, kpe_p = _front(
        x_prompt, conv0, mk_wide.reshape(b_p, MEM_TOKENS, -1), mv_wide.reshape(b_p, MEM_TOKENS, -1),
        jnp.arange(s_p, dtype=jnp.int32), w, None)
    x1_s, om_s, conv_s, ckv_s, kpe_s = _front(
        x_sample, state_conv[0], cache_mem_k[0].reshape(-1, MEM_HD), cache_mem_v[0].reshape(-1, MEM_HD),
        past + jnp.arange(t_s, dtype=jnp.int32), w,
        (page_table, cache_ckv[0], jnp.swapaxes(cache_kpe[0], 1, 2)))

    x2, h3s, idx128, gate128, rank128, cnt128 = _post_b(x1_p, om_p, x1_s, om_s, w)
    y_p, y_s = _moe_and_final(x2, h3s, idx128, gate128, rank128, cnt128, w, x1_p.shape[0] // TOKEN_TILE)

    mem_shape = (1, b_p, MEM_TOKENS, MEM_HEADS, MEM_HD)
    return (y_p.reshape(b_p, s_p, D_MODEL), y_s.reshape(b_s, t_s, D_MODEL), ckv_p[None], kpe_p[None],
            mk.reshape(mem_shape), mv.reshape(mem_shape), conv_p[None], ckv_s[None], kpe_s[None], conv_s[None])
```

```python
import functools

import jax
import jax.numpy as jnp
from jax import lax
from jax.experimental import pallas as pl
from jax.experimental.pallas import tpu as pltpu

F32 = jnp.float32
BF16 = jnp.bfloat16

D_MODEL = 1024
PAGE_SIZE = 128
CONV_CH = 512
CONV_WIDTH = 31
N_HEADS = 8
QK_NOPE = 64
QK_ROPE = 32
V_DIM = 64
Q_RANK = 384
KV_RANK = 256
ROPE_THETA = 10000.0
MLA_SCALE = (QK_NOPE + QK_ROPE) ** -0.5
MEM_TOKENS = 256
MEM_HEADS = 4
MEM_HD = 128
MEM_SCALE = MEM_HD ** -0.5
N_EXPERTS = 32
TOP_K = 4
SWIGLU_LIMIT = 7.0
SWIGLU_ALPHA = 1.702
EPS = 1e-6
NEG_INF = -1e30

LANES = 128
SUBLANES = 8
HEAD_PAD = 128
C_VAL, C_GATE, C_Q, C_CKV, C_KPE, C_KPE_SW, C_END = 0, 512, 1024, 1408, 1664, 1792, 1920
TOKEN_TILE = 256
ATTN_TILE = 256
ATTN_HEADS = 8
CONV_TILE = 256
CONV_HALO = 32
CONV_CHUNK = 32
MEM_ATTN_ROWS = 32
PAGES_PER_CHUNK = 32
SLAB = 8
MOE_ROWS = 256
CAST_ROWS = 32
VMEM_LIMIT = 48 * 1024 * 1024


def _rms(x, g):
    return x * lax.rsqrt(jnp.mean(x * x, axis=-1, keepdims=True) + EPS) * g


def _dot(a, b):
    return jnp.dot(a, b, preferred_element_type=F32)


def _dot_t(a, b):
    return lax.dot_general(a, b, (((1,), (1,)), ((), ())), preferred_element_type=F32)


def _params(*sem, vmem=None):
    return pltpu.CompilerParams(dimension_semantics=sem, vmem_limit_bytes=vmem)


def _const_spec(shape):
    nd = len(shape)
    return pl.BlockSpec(shape, lambda *_: (0,) * nd)


def _inproj_common(x_ref, gmix_ref, win_ref, qg_ref, wq_ref, wqsw_ref, kvg_ref, tab_ref,
                   u_ref, ckv_ref, kpe_ref):
    h = _rms(x_ref[...], gmix_ref[...]).astype(BF16)
    proj = _dot(h, win_ref[...])
    u_ref[...] = proj[:, C_VAL:C_GATE] * jax.nn.sigmoid(proj[:, C_GATE:C_Q])
    qn = _rms(proj[:, C_Q:C_CKV], qg_ref[...]).astype(BF16)
    ckv = _rms(proj[:, C_CKV:C_KPE], kvg_ref[...])
    ckv_ref[...] = ckv
    cq, sq, ck, sk = tab_ref[0], tab_ref[1], tab_ref[2], tab_ref[3]
    kpe_rot = proj[:, C_KPE:C_KPE_SW] * ck + proj[:, C_KPE_SW:C_END] * sk
    kpe_ref[...] = kpe_rot[:, :QK_ROPE]
    q = _dot(qn, wq_ref[...])
    qs = _dot(qn, wqsw_ref[...])
    q_heads = []
    for hd in range(N_HEADS):
        sl = slice(hd * HEAD_PAD, (hd + 1) * HEAD_PAD)
        q_heads.append(q[:, sl] * cq + qs[:, sl] * sq)
    return ckv, kpe_rot, q_heads


def _inproj_prompt_kernel(x_ref, gmix_ref, win_ref, qg_ref, wq_ref, wqsw_ref, kvg_ref, tab_ref,
                          wuk_ref, wuv_ref, vone_ref,
                          u_ref, ckv_ref, kpe_ref, q_ref, k_ref, v_ref):
    ckv, kpe_rot, q_heads = _inproj_common(x_ref, gmix_ref, win_ref, qg_ref, wq_ref, wqsw_ref,
                                           kvg_ref, tab_ref, u_ref, ckv_ref, kpe_ref)
    ckv_b = ckv.astype(BF16)
    k_nope = _dot(ckv_b, wuk_ref[...])
    for hd in range(N_HEADS):
        sl = slice(hd * HEAD_PAD, (hd + 1) * HEAD_PAD)
        q_ref[:, sl] = q_heads[hd].astype(BF16)
        k_ref[:, sl] = (k_nope[:, sl] + kpe_rot).astype(BF16)
    v_ref[...] = (_dot(ckv_b, wuv_ref[...]) + vone_ref[...]).astype(BF16)


def _inproj_sample_kernel(x_ref, gmix_ref, win_ref, qg_ref, wq_ref, wqsw_ref, kvg_ref, tab_ref,
                          wukt_ref,
                          u_ref, ckv_ref, kpe_ref, qlat_ref, qpe_ref):
    _, _, q_heads = _inproj_common(x_ref, gmix_ref, win_ref, qg_ref, wq_ref, wqsw_ref,
                                   kvg_ref, tab_ref, u_ref, ckv_ref, kpe_ref)
    for hd in range(N_HEADS):
        qlat_ref[hd] = _dot(q_heads[hd].astype(BF16), wukt_ref[hd])
        qpe_ref[hd] = q_heads[hd][:, :QK_ROPE]


def _inproj(x2d, tab, w, sample):
    t = x2d.shape[0]
    tm = TOKEN_TILE
    n_tab = tab.shape[1] // tm
    row = lambda n: pl.BlockSpec((tm, n), lambda i: (i, 0))
    in_specs = [row(D_MODEL), _const_spec((1, D_MODEL)), _const_spec((D_MODEL, C_END)),
                _const_spec((1, Q_RANK)), _const_spec((Q_RANK, N_HEADS * HEAD_PAD)),
                _const_spec((Q_RANK, N_HEADS * HEAD_PAD)), _const_spec((1, KV_RANK)),
                pl.BlockSpec((4, tm, LANES), lambda i: (0, i % n_tab, 0))]
    args = [x2d, w['g_mix'], w['w_in_ext'], w['q_norm_g'], w['wq'], w['wq_sw'], w['kv_norm_g'], tab]
    out_shape = [jax.ShapeDtypeStruct((t, CONV_CH), F32), jax.ShapeDtypeStruct((t, KV_RANK), F32),
                 jax.ShapeDtypeStruct((t, QK_ROPE), F32)]
    out_specs = [row(CONV_CH), row(KV_RANK), row(QK_ROPE)]
    if sample:
        body = _inproj_sample_kernel
        in_specs += [_const_spec((N_HEADS, HEAD_PAD, KV_RANK))]
        args += [w['wukt_pad']]
        out_shape += [jax.ShapeDtypeStruct((N_HEADS, t, KV_RANK), F32),
                      jax.ShapeDtypeStruct((N_HEADS, t, QK_ROPE), F32)]
        out_specs += [pl.BlockSpec((N_HEADS, tm, KV_RANK), lambda i: (0, i, 0)),
                      pl.BlockSpec((N_HEADS, tm, QK_ROPE), lambda i: (0, i, 0))]
    else:
        body = _inproj_prompt_kernel
        in_specs += [_const_spec((KV_RANK, N_HEADS * HEAD_PAD)), _const_spec((KV_RANK, N_HEADS * HEAD_PAD)),
                     _const_spec((1, N_HEADS * HEAD_PAD))]
        args += [w['wuk_pad'], w['wuv_slot'], w['v_ones']]
        out_shape += [jax.ShapeDtypeStruct((t, N_HEADS * HEAD_PAD), BF16)] * 3
        out_specs += [row(N_HEADS * HEAD_PAD)] * 3
    return pl.pallas_call(
        body, grid=(t // tm,), in_specs=in_specs, out_specs=out_specs, out_shape=out_shape,
        compiler_params=_params("parallel", vmem=VMEM_LIMIT),
        name="inproj_sample" if sample else "inproj_prompt")(*args)


def _ln_swish(conv, g, b):
    mu = jnp.mean(conv, axis=-1, keepdims=True)
    xc = conv - mu
    var = jnp.mean(xc * xc, axis=-1, keepdims=True)
    y = xc * lax.rsqrt(var + EPS) * g + b
    return y * jax.nn.sigmoid(y)


def _conv_prompt_kernel(main_ref, halo_ref, prev_ref, w_ref, b_ref, g_ref, lb_ref, o_ref, win_ref):
    tt = main_ref.shape[1]
    first = pl.program_id(1) == 0
    win_ref[0, 0:CONV_HALO, :] = jnp.where(first, prev_ref[0], halo_ref[0])
    win_ref[0, CONV_HALO:CONV_HALO + tt, :] = main_ref[0]
    n = tt + CONV_HALO - SUBLANES
    for s in range(1, SUBLANES):
        win_ref[s, 0:n, :] = win_ref[0, s:s + n, :]
    for c in range(tt // CONV_CHUNK):
        acc = None
        for j in range(CONV_WIDTH):
            q, s = divmod(c * CONV_CHUNK + 2 + j, SUBLANES)
            term = win_ref[s, q * SUBLANES:q * SUBLANES + CONV_CHUNK, :] * w_ref[j:j + 1, :]
            acc = term if acc is None else acc + term
        y = _ln_swish(acc + b_ref[...], g_ref[...], lb_ref[...])
        o_ref[0, c * CONV_CHUNK:(c + 1) * CONV_CHUNK, :] = y


def _conv_prompt(u3, prev_pad, w):
    b, s, _ = u3.shape
    tt = CONV_TILE
    ratio = tt // CONV_HALO
    return pl.pallas_call(
        _conv_prompt_kernel, grid=(b, s // tt),
        in_specs=[pl.BlockSpec((1, tt, CONV_CH), lambda bi, i: (bi, i, 0)),
                  pl.BlockSpec((1, CONV_HALO, CONV_CH), lambda bi, i: (bi, jnp.maximum(i * ratio - 1, 0), 0)),
                  pl.BlockSpec((1, CONV_HALO, CONV_CH), lambda bi, i: (bi, 0, 0)),
                  _const_spec((CONV_HALO, CONV_CH)), _const_spec((1, CONV_CH)),
                  _const_spec((1, CONV_CH)), _const_spec((1, CONV_CH))],
        out_specs=pl.BlockSpec((1, tt, CONV_CH), lambda bi, i: (bi, i, 0)),
        out_shape=jax.ShapeDtypeStruct((b, s, CONV_CH), F32),
        scratch_shapes=[pltpu.VMEM((SUBLANES, tt + CONV_HALO, CONV_CH), F32)],
        compiler_params=_params("parallel", "parallel"),
        name="conv_prompt")(u3, u3, prev_pad, w['conv_w'], w['conv_b'], w['conv_ln_g'], w['conv_ln_b'])


def _conv_sample_kernel(win_ref, w_ref, b_ref, g_ref, lb_ref, o_ref):
    t = o_ref.shape[1]
    acc = win_ref[:, 2:2 + t, :] * w_ref[0:1, :]
    for j in range(1, CONV_WIDTH):
        acc = acc + win_ref[:, 2 + j:2 + j + t, :] * w_ref[j:j + 1, :]
    y = _ln_swish(acc + b_ref[...], g_ref[...], lb_ref[...])
    o_ref[...] = y


def _conv_sample(upad, w):
    b, s_pad, _ = upad.shape
    t = s_pad - CONV_HALO
    bb = 8
    return pl.pallas_call(
        _conv_sample_kernel, grid=(b // bb,),
        in_specs=[pl.BlockSpec((bb, s_pad, CONV_CH), lambda i: (i, 0, 0)),
                  _const_spec((CONV_HALO, CONV_CH)), _const_spec((1, CONV_CH)),
                  _const_spec((1, CONV_CH)), _const_spec((1, CONV_CH))],
        out_specs=pl.BlockSpec((bb, t, CONV_CH), lambda i: (i, 0, 0)),
        out_shape=jax.ShapeDtypeStruct((b, t, CONV_CH), F32),
        compiler_params=_params("parallel"),
        name="conv_sample")(upad, w['conv_w'], w['conv_b'], w['conv_ln_g'], w['conv_ln_b'])


def _attn_prompt_kernel(q_ref, k_ref, v_ref, o_ref, s_ref):
    tq = q_ref.shape[1]
    qi = pl.program_id(2)
    nh = ATTN_HEADS
    head = lambda hd: slice(hd * HEAD_PAD, (hd + 1) * HEAD_PAD)
    qs = [q_ref[0, :, head(hd)] for hd in range(nh)]
    rows = lax.broadcasted_iota(jnp.int32, (tq, tq), 0)
    cols = lax.broadcasted_iota(jnp.int32, (tq, tq), 1)

    def scores(j, ms, masked):
        start = pl.multiple_of(j * tq, tq)
        out = []
        for hd in range(nh):
            s = _dot_t(qs[hd], k_ref[0, pl.ds(start, tq), head(hd)])
            if masked:
                s = jnp.where(cols <= rows, s, NEG_INF)
            s_ref[hd, j] = s
            out.append(jnp.maximum(ms[hd], jnp.max(s, axis=-1, keepdims=True)))
        return tuple(out)

    ms = tuple(jnp.full((tq, 1), NEG_INF, F32) for _ in range(nh))
    ms = lax.fori_loop(0, qi, functools.partial(scores, masked=False), ms)
    ms = scores(qi, ms, True)

    def values(j, accs):
        start = pl.multiple_of(j * tq, tq)
        out = []
        for hd in range(nh):
            p = jnp.exp(s_ref[hd, j] - ms[hd]).astype(BF16)
            out.append(accs[hd] + _dot(p, v_ref[0, pl.ds(start, tq), head(hd)]))
        return tuple(out)

    accs = tuple(jnp.zeros((tq, HEAD_PAD), F32) for _ in range(nh))
    accs = lax.fori_loop(0, qi + 1, values, accs)
    lane = lax.broadcasted_iota(jnp.int32, (tq, HEAD_PAD), 1)
    for hp in range(nh // 2):
        even, odd = accs[2 * hp], accs[2 * hp + 1]
        o_even = even / even[:, V_DIM:V_DIM + 1]
        o_odd = odd / odd[:, 0:1]
        o_ref[0, :, hp * LANES:(hp + 1) * LANES] = jnp.where(lane < V_DIM, o_even, o_odd).astype(BF16)


def _attn_prompt(q, k, v):
    b, s, _ = q.shape
    tq = ATTN_TILE
    nh = ATTN_HEADS
    return pl.pallas_call(
        _attn_prompt_kernel, grid=(b, N_HEADS // nh, s // tq),
        in_specs=[pl.BlockSpec((1, tq, nh * HEAD_PAD), lambda bi, hq, qi: (bi, qi, hq)),
                  pl.BlockSpec((1, s, nh * HEAD_PAD), lambda bi, hq, qi: (bi, 0, hq)),
                  pl.BlockSpec((1, s, nh * HEAD_PAD), lambda bi, hq, qi: (bi, 0, hq))],
        out_specs=pl.BlockSpec((1, tq, nh * V_DIM), lambda bi, hq, qi: (bi, qi, hq)),
        out_shape=jax.ShapeDtypeStruct((b, s, N_HEADS * V_DIM), BF16),
        scratch_shapes=[pltpu.VMEM((nh, s // tq, tq, tq), F32)],
        compiler_params=_params("parallel", "parallel", "arbitrary", vmem=VMEM_LIMIT),
        name="attn_prompt")(q, k, v)


def _attn_sample_kernel(pt_ref, ql_ref, qp_ref, cn_ref, kn_ref, ckv_hbm, kpe_hbm, o_ref,
                        ckv_buf, kpe_buf, sem):
    b = pl.program_id(0)
    n_pages = kpe_buf.shape[1]
    n_chunks = n_pages // PAGES_PER_CHUNK
    chunk = PAGES_PER_CHUNK * PAGE_SIZE
    t_new = cn_ref.shape[1]
    rows_q = N_HEADS * t_new

    def fetch(batch, slot):
        def one(p, _):
            page = pt_ref[batch * n_pages + p]
            rows = pl.ds(pl.multiple_of(p * PAGE_SIZE, PAGE_SIZE), PAGE_SIZE)
            pltpu.make_async_copy(ckv_hbm.at[page], ckv_buf.at[slot, rows], sem.at[0, slot]).start()
            pltpu.make_async_copy(kpe_hbm.at[page], kpe_buf.at[slot, p], sem.at[1, slot]).start()
            return 0
        lax.fori_loop(0, n_pages, one, 0, unroll=4)

    slot = b % 2

    @pl.when(b == 0)
    def _():
        fetch(0, 0)

    @pl.when(b + 1 < pl.num_programs(0))
    def _():
        fetch(b + 1, 1 - slot)

    pltpu.make_async_copy(ckv_buf.at[slot], ckv_buf.at[slot], sem.at[0, slot]).wait()
    pltpu.make_async_copy(kpe_buf.at[slot], kpe_buf.at[slot], sem.at[1, slot]).wait()

    ql = ql_ref[:, 0].reshape(rows_q, KV_RANK).astype(BF16)
    qp = qp_ref[:, 0].reshape(rows_q, QK_ROPE).astype(BF16)

    pad = PAGE_SIZE - t_new
    kc_new = jnp.concatenate([cn_ref[0], jnp.zeros((pad, KV_RANK), F32)], axis=0).astype(BF16)
    kp_new = jnp.concatenate([kn_ref[0], jnp.zeros((pad, QK_ROPE), F32)], axis=0).astype(BF16)
    s_new = _dot_t(ql, kc_new) + _dot_t(qp, kp_new)
    t_q = lax.broadcasted_iota(jnp.int32, s_new.shape, 0) % t_new
    cols = lax.broadcasted_iota(jnp.int32, s_new.shape, 1)
    s_new = jnp.where(cols <= t_q, s_new, NEG_INF)
    def part(s, values):
        m = jnp.max(s, axis=-1, keepdims=True)
        p = jnp.exp(s - m)
        return m, jnp.sum(p, axis=-1, keepdims=True), _dot(p.astype(BF16), values)

    def scores(c):
        kc = ckv_buf[slot, c * chunk:(c + 1) * chunk, :].astype(BF16)
        kpt = jnp.concatenate([kpe_buf[slot, c * PAGES_PER_CHUNK + i] for i in range(PAGES_PER_CHUNK)],
                              axis=1).astype(BF16)
        return _dot_t(ql, kc) + _dot(qp, kpt), kc

    parts = [part(s_new, kc_new)]
    nxt = scores(0)
    for c in range(n_chunks):
        cur = nxt
        if c + 1 < n_chunks:
            nxt = scores(c + 1)
        parts.append(part(*cur))
    m = parts[0][0]
    for mp, _, _ in parts[1:]:
        m = jnp.maximum(m, mp)
    l = jnp.zeros_like(m)
    acc = jnp.zeros((rows_q, KV_RANK), F32)
    for mp, lp, ap in parts:
        w = jnp.exp(mp - m)
        l = l + w * lp
        acc = acc + w * ap
    o_ref[:, 0] = (acc / l).reshape(N_HEADS, t_new, KV_RANK)


def _attn_sample(page_table, qlat, qpe, ckv_new, kpe_new, cache_ckv, cache_kpe_t):
    bs, n_pages = page_table.shape
    t_new = ckv_new.shape[1]
    qlat4 = qlat.reshape(N_HEADS, bs, t_new, KV_RANK)
    qpe4 = qpe.reshape(N_HEADS, bs, t_new, QK_ROPE)
    past = n_pages * PAGE_SIZE
    in_specs = [pl.BlockSpec((N_HEADS, 1, t_new, KV_RANK), lambda b, pt: (0, b, 0, 0)),
                pl.BlockSpec((N_HEADS, 1, t_new, QK_ROPE), lambda b, pt: (0, b, 0, 0)),
                pl.BlockSpec((1, t_new, KV_RANK), lambda b, pt: (b, 0, 0)),
                pl.BlockSpec((1, t_new, QK_ROPE), lambda b, pt: (b, 0, 0)),
                pl.BlockSpec(memory_space=pl.ANY), pl.BlockSpec(memory_space=pl.ANY)]
    grid_spec = pltpu.PrefetchScalarGridSpec(
        num_scalar_prefetch=1, grid=(bs,), in_specs=in_specs,
        out_specs=pl.BlockSpec((N_HEADS, 1, t_new, KV_RANK), lambda b, pt: (0, b, 0, 0)),
        scratch_shapes=[pltpu.VMEM((2, past, KV_RANK), F32),
                        pltpu.VMEM((2, n_pages, QK_ROPE, PAGE_SIZE), F32),
                        pltpu.SemaphoreType.DMA((2, 2))])
    o = pl.pallas_call(
        _attn_sample_kernel, grid_spec=grid_spec,
        out_shape=jax.ShapeDtypeStruct((N_HEADS, bs, t_new, KV_RANK), F32),
        compiler_params=_params("arbitrary", vmem=VMEM_LIMIT),
        name="attn_sample")(page_table.reshape(-1), qlat4, qpe4, ckv_new, kpe_new, cache_ckv, cache_kpe_t)
    return o.reshape(N_HEADS, bs * t_new, KV_RANK)


def _post_a_kernel(sample, x_ref, conv_ref, attn_ref, *rest):
    if sample:
        wuv_ref, wout_ref, g_ref, wmq_ref, x1_ref, qm_ref = rest
        attn = None
        for hd in range(N_HEADS):
            d = _dot(attn_ref[hd].astype(BF16), wuv_ref[hd])
            attn = d if attn is None else attn + d
        attn = attn.astype(BF16)
    else:
        wout_ref, g_ref, wmq_ref, x1_ref, qm_ref = rest
        attn = attn_ref[...]
    mix = _dot(conv_ref[...].astype(BF16), wout_ref[0:CONV_CH, :]) + _dot(attn, wout_ref[CONV_CH:, :])
    x1 = x_ref[...] + mix
    x1_ref[...] = x1
    qm_ref[...] = _dot(_rms(x1, g_ref[...]).astype(BF16), wmq_ref[...])


def _post_a(x2d, conv2d, attn, w, sample):
    t = x2d.shape[0]
    tm = TOKEN_TILE
    row = lambda n: pl.BlockSpec((tm, n), lambda i: (i, 0))
    mem_w = MEM_HEADS * MEM_HD
    if sample:
        in_specs = [row(D_MODEL), row(CONV_CH), pl.BlockSpec((N_HEADS, tm, KV_RANK), lambda i: (0, i, 0)),
                    _const_spec((N_HEADS, KV_RANK, N_HEADS * V_DIM))]
        args = [x2d, conv2d, attn, w['wuv_pad']]
    else:
        in_specs = [row(D_MODEL), row(CONV_CH), row(N_HEADS * V_DIM)]
        args = [x2d, conv2d, attn]
    in_specs += [_const_spec((D_MODEL, D_MODEL)), _const_spec((1, D_MODEL)), _const_spec((D_MODEL, mem_w))]
    args += [w['w_out'], w['g_mem_q'], w['w_mq']]
    return pl.pallas_call(
        functools.partial(_post_a_kernel, sample), grid=(t // tm,), in_specs=in_specs,
        out_specs=[row(D_MODEL), row(mem_w)],
        out_shape=[jax.ShapeDtypeStruct((t, D_MODEL), F32), jax.ShapeDtypeStruct((t, mem_w), F32)],
        compiler_params=_params("parallel", vmem=VMEM_LIMIT),
        name="post_a_sample" if sample else "post_a_prompt")(*args)


def _mem_kv_kernel(m_ref, g_ref, wk_ref, wv_ref, k_ref, v_ref, kw_ref, vw_ref):
    tm = m_ref.shape[0]
    m = _rms(m_ref[...], g_ref[...]).astype(BF16)
    k = _dot(m, wk_ref[...])
    v = _dot(m, wv_ref[...])
    kw_ref[...] = k
    vw_ref[...] = v
    for hd in range(MEM_HEADS):
        sl = slice(hd * MEM_HD, (hd + 1) * MEM_HD)
        k_ref[pl.ds(hd, tm, stride=MEM_HEADS), :] = k[:, sl]
        v_ref[pl.ds(hd, tm, stride=MEM_HEADS), :] = v[:, sl]


def _mem_kv(mem2d, w):
    t = mem2d.shape[0]
    tm = TOKEN_TILE
    mem_w = MEM_HEADS * MEM_HD
    rows = pl.BlockSpec((tm * MEM_HEADS, MEM_HD), lambda i: (i, 0))
    wide = pl.BlockSpec((tm, mem_w), lambda i: (i, 0))
    return pl.pallas_call(
        _mem_kv_kernel, grid=(t // tm,),
        in_specs=[pl.BlockSpec((tm, D_MODEL), lambda i: (i, 0)), _const_spec((1, D_MODEL)),
                  _const_spec((D_MODEL, mem_w)), _const_spec((D_MODEL, mem_w))],
        out_specs=[rows, rows, wide, wide],
        out_shape=[jax.ShapeDtypeStruct((t * MEM_HEADS, MEM_HD), F32)] * 2
                  + [jax.ShapeDtypeStruct((t, mem_w), F32)] * 2,
        compiler_params=_params("parallel"),
        name="mem_kv")(mem2d, w['g_mem_kv'], w['w_mk'], w['w_mv'])


def _mem_attn_wide_kernel(q_ref, k_ref, v_ref, o_ref):
    for hd in range(MEM_HEADS):
        sl = slice(hd * MEM_HD, (hd + 1) * MEM_HD)
        s = _dot_t(q_ref[0, :, sl].astype(BF16), k_ref[0, :, sl].astype(BF16)) * MEM_SCALE
        p = jnp.exp(s - jnp.max(s, axis=-1, keepdims=True))
        p = p / jnp.sum(p, axis=-1, keepdims=True)
        o_ref[0, :, sl] = _dot(p.astype(BF16), v_ref[0, :, sl].astype(BF16))


def _mem_attn_wide(qm, mem_k, mem_v):
    b, s, mem_w = qm.shape
    tq = min(TOKEN_TILE, s)
    kv = pl.BlockSpec((1, MEM_TOKENS, mem_w), lambda bi, i: (bi, 0, 0))
    return pl.pallas_call(
        _mem_attn_wide_kernel, grid=(b, s // tq),
        in_specs=[pl.BlockSpec((1, tq, mem_w), lambda bi, i: (bi, i, 0)), kv, kv],
        out_specs=pl.BlockSpec((1, tq, mem_w), lambda bi, i: (bi, i, 0)),
        out_shape=jax.ShapeDtypeStruct((b, s, mem_w), F32),
        compiler_params=_params("parallel", "parallel"),
        name="mem_attn_wide")(qm, mem_k, mem_v)


def _mem_attn_rows_kernel(q_ref, k_ref, v_ref, o_ref):
    bb, tq, _ = q_ref.shape
    rows = MEM_TOKENS * MEM_HEADS
    head = lambda hd: slice(hd * MEM_HD, (hd + 1) * MEM_HD)
    q_head = lax.broadcasted_iota(jnp.int32, (MEM_HEADS * tq, rows), 0) // tq
    k_head = lax.broadcasted_iota(jnp.int32, (MEM_HEADS * tq, rows), 1) % MEM_HEADS
    own = q_head == k_head
    for bi in range(bb):
        q = jnp.concatenate([q_ref[bi, :, head(hd)] for hd in range(MEM_HEADS)], axis=0).astype(BF16)
        k = k_ref[bi * rows:(bi + 1) * rows, :].astype(BF16)
        v = v_ref[bi * rows:(bi + 1) * rows, :].astype(BF16)
        s = jnp.where(own, _dot_t(q, k) * MEM_SCALE, NEG_INF)
        p = jnp.exp(s - jnp.max(s, axis=-1, keepdims=True))
        p = p / jnp.sum(p, axis=-1, keepdims=True)
        o = _dot(p.astype(BF16), v)
        for hd in range(MEM_HEADS):
            o_ref[bi, :, head(hd)] = o[hd * tq:(hd + 1) * tq, :]


def _mem_attn_rows(qm, mem_k, mem_v):
    b, s, mem_w = qm.shape
    bb = max(1, MEM_ATTN_ROWS // s)
    kv = pl.BlockSpec((bb * MEM_TOKENS * MEM_HEADS, MEM_HD), lambda bi: (bi, 0))
    return pl.pallas_call(
        _mem_attn_rows_kernel, grid=(b // bb,),
        in_specs=[pl.BlockSpec((bb, s, mem_w), lambda bi: (bi, 0, 0)), kv, kv],
        out_specs=pl.BlockSpec((bb, s, mem_w), lambda bi: (bi, 0, 0)),
        out_shape=jax.ShapeDtypeStruct((b, s, mem_w), F32),
        compiler_params=_params("parallel"),
        name="mem_attn_rows")(qm, mem_k, mem_v)


def _slab_rows(j, n):
    return pl.ds(j, n, stride=SLAB)


def _post_b_kernel(n_p, x1p_ref, x1s_ref, omp_ref, oms_ref, wmo_ref, g_ref, wr_ref, br_ref,
                   x2_ref, h3_ref, idx_ref, gate_ref, rank_ref, cnt_ref, carry_ref):
    tm = x2_ref.shape[0]
    i = pl.program_id(0)

    @pl.when(i == 0)
    def _():
        carry_ref[...] = jnp.zeros(carry_ref.shape, F32)

    is_p = i < n_p
    x1 = jnp.where(is_p, x1p_ref[...], x1s_ref[...])
    om = jnp.where(is_p, omp_ref[...], oms_ref[...])
    x2 = x1 + _dot(om.astype(BF16), wmo_ref[...])
    x2_ref[...] = x2
    h3 = _rms(x2, g_ref[...])
    for j in range(SLAB):
        h3_ref[_slab_rows(j, tm), :] = h3[:, j * LANES:(j + 1) * LANES]
    logits = _dot(h3.astype(BF16), wr_ref[...]) + br_ref[...]
    lane = lax.broadcasted_iota(jnp.int32, logits.shape, 1)
    lane_f = lane.astype(F32)
    vals, hots = [], []
    idx_out = jnp.zeros(logits.shape, F32)
    for kk in range(TOP_K):
        mx = jnp.max(logits, axis=-1, keepdims=True)
        first = jnp.min(jnp.where(logits == mx, lane_f, float(LANES)), axis=-1, keepdims=True)
        hot = lane_f == first
        logits = jnp.where(hot, -jnp.inf, logits)
        vals.append(mx)
        hots.append(hot)
        idx_out = jnp.where(lane == kk, first, idx_out)
    exps = [jnp.exp(v - vals[0]) for v in vals]
    denom = exps[0] + exps[1] + exps[2] + exps[3]
    chosen = jnp.zeros(logits.shape, F32)
    gate_out = jnp.zeros(logits.shape, F32)
    for kk in range(TOP_K):
        chosen = chosen + hots[kk].astype(F32)
        gate_out = jnp.where(lane == kk, exps[kk] / denom, gate_out)
    r_i = lax.broadcasted_iota(jnp.int32, (tm, tm), 0)
    c_i = lax.broadcasted_iota(jnp.int32, (tm, tm), 1)
    tril = (c_i < r_i).astype(BF16)
    before = _dot(tril, chosen.astype(BF16)) + carry_ref[...]
    rank_out = jnp.zeros(logits.shape, F32)
    for kk in range(TOP_K):
        rk = jnp.sum(jnp.where(hots[kk], before, 0.0), axis=-1, keepdims=True)
        rank_out = jnp.where(lane == kk, rk, rank_out)
    carry = carry_ref[...] + jnp.sum(chosen, axis=0, keepdims=True)
    carry_ref[...] = carry
    idx_ref[...] = idx_out.astype(jnp.int32)
    gate_ref[...] = gate_out
    rank_ref[...] = rank_out.astype(jnp.int32)
    cnt_ref[...] = carry.astype(jnp.int32)


def _post_b(x1_p, om_p, x1_s, om_s, w):
    tm = TOKEN_TILE
    n_p, n_s = x1_p.shape[0] // tm, x1_s.shape[0] // tm
    t = (n_p + n_s) * tm
    mem_w = MEM_HEADS * MEM_HD
    row = lambda n: pl.BlockSpec((tm, n), lambda i: (i, 0))
    row_p = lambda n: pl.BlockSpec((tm, n), lambda i: (jnp.minimum(i, n_p - 1), 0))
    row_s = lambda n: pl.BlockSpec((tm, n), lambda i: (jnp.maximum(i - n_p, 0), 0))
    return pl.pallas_call(
        functools.partial(_post_b_kernel, n_p), grid=(n_p + n_s,),
        in_specs=[row_p(D_MODEL), row_s(D_MODEL), row_p(mem_w), row_s(mem_w),
                  _const_spec((mem_w, D_MODEL)), _const_spec((1, D_MODEL)),
                  _const_spec((D_MODEL, LANES)), _const_spec((1, LANES))],
        out_specs=[row(D_MODEL), pl.BlockSpec((tm * SLAB, LANES), lambda i: (i, 0)),
                   row(LANES), row(LANES), row(LANES), _const_spec((1, LANES))],
        out_shape=[jax.ShapeDtypeStruct((t, D_MODEL), F32), jax.ShapeDtypeStruct((t * SLAB, LANES), F32),
                   jax.ShapeDtypeStruct((t, LANES), jnp.int32), jax.ShapeDtypeStruct((t, LANES), F32),
                   jax.ShapeDtypeStruct((t, LANES), jnp.int32), jax.ShapeDtypeStruct((1, LANES), jnp.int32)],
        scratch_shapes=[pltpu.VMEM((1, LANES), F32)],
        compiler_params=_params("arbitrary", vmem=VMEM_LIMIT),
        name="post_b")(x1_p, x1_s, om_p, om_s, w['w_mo'], w['g_ffn'], w['w_router'], w['b_router'])


def _slab(row):
    return pl.ds(pl.multiple_of(row * SLAB, SLAB), SLAB)


def _dispatch_kernel(start_ref, lo_ref, hi_ref, idx_ref, rank_ref, h_ref, xs_hbm, zero_ref, sem):
    i = pl.program_id(0)
    tm = h_ref.shape[0] // SLAB

    def issue(t, _):
        for k in range(TOP_K):
            r = t * TOP_K + k
            d = start_ref[idx_ref[r]] + rank_ref[r]
            pltpu.make_async_copy(h_ref.at[_slab(t)], xs_hbm.at[_slab(d)], sem).start(priority=k % 2)
        return 0

    lax.fori_loop(0, tm, issue, 0, unroll=2)
    for _ in range(TOP_K):
        pltpu.make_async_copy(h_ref, xs_hbm.at[pl.ds(0, tm * SLAB)], sem).wait()

    @pl.when(i == pl.num_programs(0) - 1)
    def _():
        zero_ref[...] = jnp.zeros(zero_ref.shape, F32)
        for e in range(N_EXPERTS):
            lo, hi = lo_ref[e], hi_ref[e]

            def zissue(r, _):
                pltpu.make_async_copy(zero_ref.at[_slab(0)], xs_hbm.at[_slab(r)], sem).start()
                return 0

            def zdrain(r, _):
                pltpu.make_async_copy(zero_ref.at[_slab(0)], xs_hbm.at[_slab(0)], sem).wait()
                return 0

            lax.fori_loop(lo, hi, zissue, 0)
            lax.fori_loop(lo, hi, zdrain, 0)

        bm = zero_ref.shape[0] // SLAB

        def tail(blk, _):
            rows = pl.ds(pl.multiple_of(blk * bm * SLAB, SLAB), bm * SLAB)
            cp = pltpu.make_async_copy(zero_ref, xs_hbm.at[rows], sem)
            cp.start()
            cp.wait()
            return 0

        lax.fori_loop(hi_ref[N_EXPERTS - 1] // bm, xs_hbm.shape[0] // (bm * SLAB), tail, 0)


def _dispatch(h3s, idx_flat, rank_flat, pad_start, pad_lo, pad_hi, n_rows):
    tm = TOKEN_TILE
    t = h3s.shape[0] // SLAB
    grid_spec = pltpu.PrefetchScalarGridSpec(
        num_scalar_prefetch=3, grid=(t // tm,),
        in_specs=[pl.BlockSpec((tm * TOP_K,), lambda i, *_: (i,), memory_space=pltpu.SMEM),
                  pl.BlockSpec((tm * TOP_K,), lambda i, *_: (i,), memory_space=pltpu.SMEM),
                  pl.BlockSpec((tm * SLAB, LANES), lambda i, *_: (i, 0))],
        out_specs=pl.BlockSpec(memory_space=pl.ANY),
        scratch_shapes=[pltpu.VMEM((MOE_ROWS * SLAB, LANES), F32), pltpu.SemaphoreType.DMA])
    return pl.pallas_call(
        _dispatch_kernel, grid_spec=grid_spec,
        out_shape=jax.ShapeDtypeStruct((n_rows * SLAB, LANES), F32),
        compiler_params=pltpu.CompilerParams(dimension_semantics=("arbitrary",), has_side_effects=True),
        name="moe_dispatch")(pad_start, pad_lo, pad_hi, idx_flat, rank_flat, h3s)


def _expert_kernel(src_ref, exp_ref, first_ref, valid_ref, next_ref, x_ref, wg_hbm, wu_hbm, wd_hbm,
                   bg_ref, bu_ref, bd_ref, y_ref, wf_ref, wb_ref, xb_ref, sem):
    b = pl.program_id(0)
    bm = xb_ref.shape[0]
    w_hbm = (wg_hbm, wu_hbm, wd_hbm)

    def fetch(e):
        for i in range(3):
            pltpu.make_async_copy(w_hbm[i].at[e], wf_ref.at[i], sem).start()

    @pl.when(b == 0)
    def _():
        fetch(exp_ref[0])

    @pl.when(first_ref[b] == 1)
    def _():
        for i in range(3):
            pltpu.make_async_copy(w_hbm[i].at[0], wf_ref.at[i], sem).wait()
        def cast(r, _):
            rows = pl.ds(pl.multiple_of(r * CAST_ROWS, CAST_ROWS), CAST_ROWS)
            for i in range(3):
                wb_ref[i, rows, :] = wf_ref[i, rows, :].astype(BF16)
            return 0

        lax.fori_loop(0, D_MODEL // CAST_ROWS, cast, 0)

        @pl.when(next_ref[b] >= 0)
        def _():
            fetch(next_ref[b])

    @pl.when(valid_ref[b] == 1)
    def _():
        for j in range(SLAB):
            xb_ref[:, j * LANES:(j + 1) * LANES] = x_ref[_slab_rows(j, bm), :].astype(BF16)
        x = xb_ref[...]
        g = _dot(x, wb_ref[0]) + bg_ref[0]
        u = _dot(x, wb_ref[1]) + bu_ref[0]
        g = jnp.minimum(g, SWIGLU_LIMIT)
        u = jnp.clip(u, -SWIGLU_LIMIT, SWIGLU_LIMIT)
        a = (u + 1.0) * (g * jax.nn.sigmoid(SWIGLU_ALPHA * g))
        y = _dot(a.astype(BF16), wb_ref[2]) + bd_ref[0]
        for j in range(SLAB):
            y_ref[_slab_rows(j, bm), :] = y[:, j * LANES:(j + 1) * LANES]

    @pl.when(valid_ref[b] == 0)
    def _():
        y_ref[...] = jnp.zeros(y_ref.shape, F32)


def _experts(xs, blk_src, blk_exp, blk_first, blk_valid, blk_next, w):
    bm = MOE_ROWS
    n_blocks = xs.shape[0] // (bm * SLAB)
    hbm = pl.BlockSpec(memory_space=pl.ANY)
    bspec = pl.BlockSpec((1, 1, D_MODEL), lambda b, src, ex, *_: (ex[b], 0, 0))
    grid_spec = pltpu.PrefetchScalarGridSpec(
        num_scalar_prefetch=5, grid=(n_blocks,),
        in_specs=[pl.BlockSpec((bm * SLAB, LANES), lambda b, src, *_: (src[b], 0)),
                  hbm, hbm, hbm, bspec, bspec, bspec],
        out_specs=pl.BlockSpec((bm * SLAB, LANES), lambda b, *_: (b, 0)),
        scratch_shapes=[pltpu.VMEM((3, D_MODEL, D_MODEL), F32), pltpu.VMEM((3, D_MODEL, D_MODEL), BF16),
                        pltpu.VMEM((bm, D_MODEL), BF16), pltpu.SemaphoreType.DMA])
    return pl.pallas_call(
        _expert_kernel, grid_spec=grid_spec,
        out_shape=jax.ShapeDtypeStruct(xs.shape, F32),
        compiler_params=_params("arbitrary", vmem=VMEM_LIMIT),
        name="moe_experts")(blk_src, blk_exp, blk_first, blk_valid, blk_next, xs,
                            w['w_gate'], w['w_up'], w['w_down'], w['b_gate'], w['b_up'], w['b_down'])


def _combine_kernel(n_p, start_ref, idx_ref, rank_ref, idx_nx_ref, rank_nx_ref, gate_ref, yb_hbm, x2_ref, g_ref,
                    yp_ref, ys_ref, rows_ref, ysum_ref, y_ref, sem):
    i = pl.program_id(0)
    tm = x2_ref.shape[0]

    def gather(ix_ref, rk_ref, slot):
        def issue(t, _):
            for k in range(TOP_K):
                r = t * TOP_K + k
                d = start_ref[ix_ref[r]] + rk_ref[r]
                pltpu.make_async_copy(yb_hbm.at[_slab(d)], rows_ref.at[slot, _slab(r)],
                                      sem.at[slot]).start(priority=k % 2)
            return 0
        lax.fori_loop(0, tm, issue, 0, unroll=2)

    slot = i % 2

    @pl.when(i == 0)
    def _():
        gather(idx_ref, rank_ref, 0)

    @pl.when(i + 1 < pl.num_programs(0))
    def _():
        gather(idx_nx_ref, rank_nx_ref, 1 - slot)

    for _ in range(TOP_K):
        pltpu.make_async_copy(yb_hbm.at[pl.ds(0, tm * SLAB)], rows_ref.at[slot, pl.ds(0, tm * SLAB)],
                              sem.at[slot]).wait()

    def token(t, _):
        acc = rows_ref[slot, _slab(t * TOP_K)] * gate_ref[t * TOP_K]
        for k in range(1, TOP_K):
            acc = acc + rows_ref[slot, _slab(t * TOP_K + k)] * gate_ref[t * TOP_K + k]
        ysum_ref[_slab(t)] = acc
        return 0

    lax.fori_loop(0, tm, token, 0, unroll=4)

    ss = jnp.zeros((tm, 1), F32)
    for j in range(SLAB):
        y = x2_ref[:, j * LANES:(j + 1) * LANES] + ysum_ref[_slab_rows(j, tm), :]
        y_ref[:, j * LANES:(j + 1) * LANES] = y
        ss = ss + jnp.sum(y * y, axis=-1, keepdims=True)
    out = y_ref[...] * lax.rsqrt(ss * (1.0 / D_MODEL) + EPS) * g_ref[...]

    @pl.when(i < n_p)
    def _():
        yp_ref[...] = out

    @pl.when(i >= n_p)
    def _():
        ys_ref[...] = out


def _combine(yb, idx_flat, rank_flat, gate_flat, pad_start, x2, g_final, n_p):
    t = x2.shape[0]
    tm = TOKEN_TILE
    n = t // tm
    n_s = n - n_p
    cur = lambda i, *_: (i,)
    nxt = lambda i, *_: (jnp.minimum(i + 1, n - 1),)
    smem = lambda index_map: pl.BlockSpec((tm * TOP_K,), index_map, memory_space=pltpu.SMEM)
    grid_spec = pltpu.PrefetchScalarGridSpec(
        num_scalar_prefetch=1, grid=(n,),
        in_specs=[smem(cur), smem(cur), smem(nxt), smem(nxt), smem(cur),
                  pl.BlockSpec(memory_space=pl.ANY),
                  pl.BlockSpec((tm, D_MODEL), lambda i, *_: (i, 0)),
                  pl.BlockSpec((1, D_MODEL), lambda i, *_: (0, 0))],
        out_specs=[pl.BlockSpec((tm, D_MODEL), lambda i, *_: (jnp.minimum(i, n_p - 1), 0)),
                   pl.BlockSpec((tm, D_MODEL), lambda i, *_: (jnp.maximum(i - n_p, 0), 0))],
        scratch_shapes=[pltpu.VMEM((2, tm * TOP_K * SLAB, LANES), F32), pltpu.VMEM((tm * SLAB, LANES), F32),
                        pltpu.VMEM((tm, D_MODEL), F32), pltpu.SemaphoreType.DMA((2,))])
    return pl.pallas_call(
        functools.partial(_combine_kernel, n_p), grid_spec=grid_spec,
        out_shape=[jax.ShapeDtypeStruct((n_p * tm, D_MODEL), F32), jax.ShapeDtypeStruct((n_s * tm, D_MODEL), F32)],
        compiler_params=_params("arbitrary", vmem=VMEM_LIMIT),
        name="moe_combine")(pad_start, idx_flat, rank_flat, idx_flat, rank_flat, gate_flat, yb, x2, g_final)


def _moe_and_final(x2, h3s, idx128, gate128, rank128, cnt128, w, n_p):
    t = x2.shape[0]
    bm = MOE_ROWS
    n_assign = t * TOP_K
    n_blocks = (n_assign + N_EXPERTS * (bm - 1) + bm - 1) // bm
    counts = cnt128[0, :N_EXPERTS]
    padded = ((counts + bm - 1) // bm) * bm
    pad_end = jnp.cumsum(padded).astype(jnp.int32)
    pad_start = pad_end - padded
    idx_flat = idx128[:, :TOP_K].reshape(-1)
    rank_flat = rank128[:, :TOP_K].reshape(-1)
    blk = jnp.arange(n_blocks, dtype=jnp.int32)
    blk_valid = (blk * bm < pad_end[-1]).astype(jnp.int32)
    blk_src = jnp.minimum(blk, jnp.maximum(pad_end[-1] // bm - 1, 0))
    blk_exp = jnp.sum((pad_end[None, :] <= (blk_src * bm)[:, None]).astype(jnp.int32), axis=1)
    blk_exp = jnp.minimum(blk_exp, N_EXPERTS - 1)
    blk_first = jnp.concatenate([jnp.ones((1,), jnp.int32),
                                 (blk_exp[1:] != blk_exp[:-1]).astype(jnp.int32)])
    later_start = (blk[None, :] > blk[:, None]) & (blk_first[None, :] == 1)
    next_pos = jnp.min(jnp.where(later_start, blk[None, :], n_blocks), axis=1)
    blk_next = jnp.where(next_pos < n_blocks, blk_exp[jnp.minimum(next_pos, n_blocks - 1)], -1)
    xs = _dispatch(h3s, idx_flat, rank_flat, pad_start, pad_start + counts, pad_end, n_blocks * bm)
    yb = _experts(xs, blk_src, blk_exp, blk_first, blk_valid, blk_next.astype(jnp.int32), w)
    gate_flat = gate128[:, :TOP_K].reshape(-1)
    return _combine(yb, idx_flat, rank_flat, gate_flat, pad_start, x2, w['g_final'], n_p)


def _swap_halves(wcols):
    half = QK_ROPE // 2
    return jnp.concatenate([-wcols[..., half:], wcols[..., :half]], axis=-1)


def _prep_weights(g_mix, w_in, conv_w, conv_b, conv_ln_g, conv_ln_b, q_norm_g, w_q_up, kv_norm_g, w_kv_up,
                  w_out, g_mem_q, g_mem_kv, w_mq, w_mk, w_mv, w_mo, g_ffn, w_router, b_router,
                  w_gate, b_gate, w_up, b_up, w_down, b_down, g_final):
    l = 0
    w_kpe = w_in[l][:, C_KPE:]
    zpad = jnp.zeros((D_MODEL, HEAD_PAD - QK_ROPE), F32)
    w_in_ext = jnp.concatenate([w_in[l][:, :C_KPE], w_kpe, zpad, _swap_halves(w_kpe), zpad], axis=1)
    wq3 = w_q_up[l].reshape(Q_RANK, N_HEADS, QK_NOPE + QK_ROPE)
    q_nope, q_rope = wq3[..., :QK_NOPE], wq3[..., QK_NOPE:]
    z32 = jnp.zeros((Q_RANK, N_HEADS, HEAD_PAD - QK_NOPE - QK_ROPE), F32)
    wq = jnp.concatenate([q_rope, q_nope, z32], axis=-1).reshape(Q_RANK, N_HEADS * HEAD_PAD)
    wq_sw = jnp.concatenate([_swap_halves(q_rope), jnp.zeros_like(q_nope), z32], axis=-1)
    wq_sw = wq_sw.reshape(Q_RANK, N_HEADS * HEAD_PAD)
    w_uk = w_kv_up[l][:, :, :QK_NOPE]
    w_uv = w_kv_up[l][:, :, QK_NOPE:]
    wuk_pad = jnp.concatenate([jnp.zeros((KV_RANK, N_HEADS, QK_ROPE), F32), w_uk,
                               jnp.zeros((KV_RANK, N_HEADS, HEAD_PAD - QK_NOPE - QK_ROPE), F32)], axis=-1)
    wukt_pad = jnp.transpose(wuk_pad, (1, 2, 0))
    odd_head = (jnp.arange(N_HEADS) % 2 == 1)[None, :, None]
    zv = jnp.zeros_like(w_uv)
    wuv_slot = jnp.where(odd_head, jnp.concatenate([zv, w_uv], axis=-1), jnp.concatenate([w_uv, zv], axis=-1))
    lane_id = jnp.arange(HEAD_PAD)[None, :]
    v_ones = jnp.where(odd_head[0], lane_id == 0, lane_id == V_DIM).astype(F32)
    eye = jnp.eye(N_HEADS, dtype=F32)
    wuv_pad = (w_uv.transpose(1, 0, 2)[:, :, None, :] * eye[:, None, :, None])
    conv_w_pad = jnp.concatenate([conv_w[l], jnp.zeros((CONV_HALO - CONV_WIDTH, CONV_CH), F32)], axis=0)
    w_router_pad = jnp.concatenate([w_router[l], jnp.zeros((D_MODEL, LANES - N_EXPERTS), F32)], axis=1)
    b_router_pad = jnp.concatenate([b_router[l], jnp.full((LANES - N_EXPERTS,), NEG_INF, F32)])
    return {
        'g_mix': g_mix[l][None], 'w_in_ext': w_in_ext.astype(BF16),
        'q_norm_g': q_norm_g[l][None], 'wq': wq.astype(BF16), 'wq_sw': wq_sw.astype(BF16),
        'kv_norm_g': kv_norm_g[l][None],
        'wuk_pad': wuk_pad.reshape(KV_RANK, N_HEADS * HEAD_PAD).astype(BF16),
        'wuv_slot': wuv_slot.reshape(KV_RANK, N_HEADS * HEAD_PAD).astype(BF16),
        'v_ones': v_ones.reshape(1, N_HEADS * HEAD_PAD),
        'wukt_pad': wukt_pad.astype(BF16),
        'wuv_pad': wuv_pad.reshape(N_HEADS, KV_RANK, N_HEADS * V_DIM).astype(BF16),
        'conv_w': conv_w_pad, 'conv_b': conv_b[l][None],
        'conv_ln_g': conv_ln_g[l][None], 'conv_ln_b': conv_ln_b[l][None],
        'w_out': w_out[l].astype(BF16), 'g_mem_q': g_mem_q[l][None], 'w_mq': w_mq[l].astype(BF16),
        'g_mem_kv': g_mem_kv[l][None], 'w_mk': w_mk[l].astype(BF16), 'w_mv': w_mv[l].astype(BF16),
        'w_mo': w_mo[l].astype(BF16), 'g_ffn': g_ffn[l][None],
        'w_router': w_router_pad.astype(BF16), 'b_router': b_router_pad[None],
        'w_gate': w_gate[l], 'w_up': w_up[l], 'w_down': w_down[l],
        'b_gate': b_gate[l][:, None, :], 'b_up': b_up[l][:, None, :], 'b_down': b_down[l][:, None, :],
        'g_final': g_final[None],
    }


def _rope_table(pos):
    half = QK_ROPE // 2
    inv = ROPE_THETA ** (-jnp.arange(half, dtype=F32) / half)
    ang = pos.astype(F32)[:, None] * inv[None, :]
    cos, sin = jnp.cos(ang), jnp.sin(ang)
    n = pos.shape[0]
    ones = jnp.ones((n, QK_NOPE), F32)
    z = lambda k: jnp.zeros((n, k), F32)
    cq = MLA_SCALE * jnp.concatenate([cos, cos, ones, z(HEAD_PAD - QK_NOPE - QK_ROPE)], axis=1)
    sq = MLA_SCALE * jnp.concatenate([sin, sin, z(HEAD_PAD - QK_ROPE)], axis=1)
    ck = jnp.concatenate([cos, cos, z(HEAD_PAD - QK_ROPE)], axis=1)
    sk = jnp.concatenate([sin, sin, z(HEAD_PAD - QK_ROPE)], axis=1)
    return jnp.stack([cq, sq, ck, sk])


def _front(x, conv_prev, mem_k, mem_v, pos, w, paged):
    b, s, _ = x.shape
    t = b * s
    x2d = x.reshape(t, D_MODEL)
    sample = paged is not None
    if sample:
        tab = _rope_table(jnp.tile(pos, TOKEN_TILE // s))
    else:
        tab = _rope_table(pos)
    outs = _inproj(x2d, tab, w, sample)
    u, ckv, kpe = outs[0], outs[1], outs[2]
    u3 = u.reshape(b, s, CONV_CH)
    tail = CONV_WIDTH - 1
    if s >= tail:
        conv_tail = u3[:, s - tail:]
    else:
        conv_tail = jnp.concatenate([conv_prev[:, s:], u3], axis=1)
    prev_pad = jnp.concatenate([jnp.zeros((b, CONV_HALO - tail, CONV_CH), F32), conv_prev], axis=1)
    ckv3 = ckv.reshape(b, s, KV_RANK)
    kpe3 = kpe.reshape(b, s, QK_ROPE)
    if sample:
        page_table, cache_ckv, cache_kpe_t = paged
        conv_out = _conv_sample(jnp.concatenate([prev_pad, u3], axis=1), w)
        attn = _attn_sample(page_table, outs[3], outs[4], ckv3, kpe3, cache_ckv, cache_kpe_t)
    else:
        conv_out = _conv_prompt(u3, prev_pad, w)
        q, k, v = (a.reshape(b, s, -1) for a in outs[3:6])
        attn = _attn_prompt(q, k, v).reshape(t, N_HEADS * V_DIM)
    x1, qm = _post_a(x2d, conv_out.reshape(t, CONV_CH), attn, w, sample)
    mem_attn = _mem_attn_rows if sample else _mem_attn_wide
    om = mem_attn(qm.reshape(b, s, -1), mem_k, mem_v)
    return x1, om.reshape(t, -1), conv_tail, ckv3, kpe3


def kernel(x_prompt, x_sample, mem_prompt, cache_ckv, cache_kpe, page_table, cache_mem_k, cache_mem_v, state_conv, g_mix, w_in, conv_w, conv_b, conv_ln_g, conv_ln_b, q_norm_g, w_q_up, kv_norm_g, w_kv_up, w_out, g_mem_q, g_mem_kv, w_mq, w_mk, w_mv, w_mo, g_ffn, w_router, b_router, w_gate, b_gate, w_up, b_up, w_down, b_down, g_final):
    assert g_mix.shape[0] == 1, "single-layer step"
    w = _prep_weights(g_mix, w_in, conv_w, conv_b, conv_ln_g, conv_ln_b, q_norm_g, w_q_up, kv_norm_g, w_kv_up,
                      w_out, g_mem_q, g_mem_kv, w_mq, w_mk, w_mv, w_mo, g_ffn, w_router, b_router,
                      w_gate, b_gate, w_up, b_up, w_down, b_down, g_final)
    b_p, s_p, _ = x_prompt.shape
    b_s, t_s, _ = x_sample.shape
    past = page_table.shape[1] * PAGE_SIZE

    mk, mv, mk_wide, mv_wide = _mem_kv(mem_prompt.reshape(-1, D_MODEL), w)
    conv0 = jnp.zeros((b_p, CONV_WIDTH - 1, CONV_CH), F32)
    x1_p, om_p, conv_p, ckv_p, kpe_p = _front(
        x_prompt, conv0, mk_wide.reshape(b_p, MEM_TOKENS, -1), mv_wide.reshape(b_p, MEM_TOKENS, -1),
        jnp.arange(s_p, dtype=jnp.int32), w, None)
    x1_s, om_s, conv_s, ckv_s, kpe_s = _front(
        x_sample, state_conv[0], cache_mem_k[0].reshape(-1, MEM_HD), cache_mem_v[0].reshape(-1, MEM_HD),
        past + jnp.arange(t_s, dtype=jnp.int32), w,
        (page_table, cache_ckv[0], jnp.swapaxes(cache_kpe[0], 1, 2)))

    x2, h3s, idx128, gate128, rank128, cnt128 = _post_b(x1_p, om_p, x1_s, om_s, w)
    y_p, y_s = _moe_and_final(x2, h3s, idx128, gate128, rank128, cnt128, w, x1_p.shape[0] // TOKEN_TILE)

    mem_shape = (1, b_p, MEM_TOKENS, MEM_HEADS, MEM_HD)
    return (y_p.reshape(b_p, s_p, D_MODEL), y_s.reshape(b_s, t_s, D_MODEL), ckv_p[None], kpe_p[None],
            mk.reshape(mem_shape), mv.reshape(mem_shape), conv_p[None], ckv_s[None], kpe_s[None], conv_s[None])
```

```python
import functools

import jax
import jax.numpy as jnp
from jax import lax
from jax.experimental import pallas as pl
from jax.experimental.pallas import tpu as pltpu

F32 = jnp.float32
BF16 = jnp.bfloat16

D_MODEL = 1024
PAGE_SIZE = 128
CONV_CH = 512
CONV_WIDTH = 31
N_HEADS = 8
QK_NOPE = 64
QK_ROPE = 32
V_DIM = 64
Q_RANK = 384
KV_RANK = 256
ROPE_THETA = 10000.0
MLA_SCALE = (QK_NOPE + QK_ROPE) ** -0.5
MEM_TOKENS = 256
MEM_HEADS = 4
MEM_HD = 128
MEM_SCALE = MEM_HD ** -0.5
N_EXPERTS = 32
TOP_K = 4
SWIGLU_LIMIT = 7.0
SWIGLU_ALPHA = 1.702
EPS = 1e-6
NEG_INF = -1e30

LANES = 128
SUBLANES = 8
HEAD_PAD = 128
C_VAL, C_GATE, C_Q, C_CKV, C_KPE, C_KPE_SW, C_END = 0, 512, 1024, 1408, 1664, 1792, 1920
TOKEN_TILE = 256
ATTN_TILE = 256
ATTN_HEADS = 8
CONV_TILE = 256
CONV_HALO = 32
CONV_CHUNK = 32
MEM_ATTN_ROWS = 32
PAGES_PER_CHUNK = 32
SLAB = 8
MOE_ROWS = 256
CAST_ROWS = 32
VMEM_LIMIT = 48 * 1024 * 1024


def _rms(x, g):
    return x * lax.rsqrt(jnp.mean(x * x, axis=-1, keepdims=True) + EPS) * g


def _dot(a, b):
    return jnp.dot(a, b, preferred_element_type=F32)


def _dot_t(a, b):
    return lax.dot_general(a, b, (((1,), (1,)), ((), ())), preferred_element_type=F32)


def _params(*sem, vmem=None):
    return pltpu.CompilerParams(dimension_semantics=sem, vmem_limit_bytes=vmem)


def _const_spec(shape):
    nd = len(shape)
    return pl.BlockSpec(shape, lambda *_: (0,) * nd)


def _inproj_common(x_ref, gmix_ref, win_ref, qg_ref, wq_ref, wqsw_ref, kvg_ref, tab_ref,
                   u_ref, ckv_ref, kpe_ref):
    h = _rms(x_ref[...], gmix_ref[...]).astype(BF16)
    proj = _dot(h, win_ref[...])
    u_ref[...] = proj[:, C_VAL:C_GATE] * jax.nn.sigmoid(proj[:, C_GATE:C_Q])
    qn = _rms(proj[:, C_Q:C_CKV], qg_ref[...]).astype(BF16)
    ckv = _rms(proj[:, C_CKV:C_KPE], kvg_ref[...])
    ckv_ref[...] = ckv
    cq, sq, ck, sk = tab_ref[0], tab_ref[1], tab_ref[2], tab_ref[3]
    kpe_rot = proj[:, C_KPE:C_KPE_SW] * ck + proj[:, C_KPE_SW:C_END] * sk
    kpe_ref[...] = kpe_rot[:, :QK_ROPE]
    q = _dot(qn, wq_ref[...])
    qs = _dot(qn, wqsw_ref[...])
    q_heads = []
    for hd in range(N_HEADS):
        sl = slice(hd * HEAD_PAD, (hd + 1) * HEAD_PAD)
        q_heads.append(q[:, sl] * cq + qs[:, sl] * sq)
    return ckv, kpe_rot, q_heads


def _inproj_prompt_kernel(x_ref, gmix_ref, win_ref, qg_ref, wq_ref, wqsw_ref, kvg_ref, tab_ref,
                          wuk_ref, wuv_ref, vone_ref,
                          u_ref, ckv_ref, kpe_ref, q_ref, k_ref, v_ref):
    ckv, kpe_rot, q_heads = _inproj_common(x_ref, gmix_ref, win_ref, qg_ref, wq_ref, wqsw_ref,
                                           kvg_ref, tab_ref, u_ref, ckv_ref, kpe_ref)
    ckv_b = ckv.astype(BF16)
    k_nope = _dot(ckv_b, wuk_ref[...])
    for hd in range(N_HEADS):
        sl = slice(hd * HEAD_PAD, (hd + 1) * HEAD_PAD)
        q_ref[:, sl] = q_heads[hd].astype(BF16)
        k_ref[:, sl] = (k_nope[:, sl] + kpe_rot).astype(BF16)
    v_ref[...] = (_dot(ckv_b, wuv_ref[...]) + vone_ref[...]).astype(BF16)


def _inproj_sample_kernel(x_ref, gmix_ref, win_ref, qg_ref, wq_ref, wqsw_ref, kvg_ref, tab_ref,
                          wukt_ref,
                          u_ref, ckv_ref, kpe_ref, qlat_ref, qpe_ref):
    _, _, q_heads = _inproj_common(x_ref, gmix_ref, win_ref, qg_ref, wq_ref, wqsw_ref,
                                   kvg_ref, tab_ref, u_ref, ckv_ref, kpe_ref)
    for hd in range(N_HEADS):
        qlat_ref[hd] = _dot(q_heads[hd].astype(BF16), wukt_ref[hd])
        qpe_ref[hd] = q_heads[hd][:, :QK_ROPE]


def _inproj(x2d, tab, w, sample):
    t = x2d.shape[0]
    tm = TOKEN_TILE
    n_tab = tab.shape[1] // tm
    row = lambda n: pl.BlockSpec((tm, n), lambda i: (i, 0))
    in_specs = [row(D_MODEL), _const_spec((1, D_MODEL)), _const_spec((D_MODEL, C_END)),
                _const_spec((1, Q_RANK)), _const_spec((Q_RANK, N_HEADS * HEAD_PAD)),
                _const_spec((Q_RANK, N_HEADS * HEAD_PAD)), _const_spec((1, KV_RANK)),
                pl.BlockSpec((4, tm, LANES), lambda i: (0, i % n_tab, 0))]
    args = [x2d, w['g_mix'], w['w_in_ext'], w['q_norm_g'], w['wq'], w['wq_sw'], w['kv_norm_g'], tab]
    out_shape = [jax.ShapeDtypeStruct((t, CONV_CH), F32), jax.ShapeDtypeStruct((t, KV_RANK), F32),
                 jax.ShapeDtypeStruct((t, QK_ROPE), F32)]
    out_specs = [row(CONV_CH), row(KV_RANK), row(QK_ROPE)]
    if sample:
        body = _inproj_sample_kernel
        in_specs += [_const_spec((N_HEADS, HEAD_PAD, KV_RANK))]
        args += [w['wukt_pad']]
        out_shape += [jax.ShapeDtypeStruct((N_HEADS, t, KV_RANK), F32),
                      jax.ShapeDtypeStruct((N_HEADS, t, QK_ROPE), F32)]
        out_specs += [pl.BlockSpec((N_HEADS, tm, KV_RANK), lambda i: (0, i, 0)),
                      pl.BlockSpec((N_HEADS, tm, QK_ROPE), lambda i: (0, i, 0))]
    else:
        body = _inproj_prompt_kernel
        in_specs += [_const_spec((KV_RANK, N_HEADS * HEAD_PAD)), _const_spec((KV_RANK, N_HEADS * HEAD_PAD)),
                     _const_spec((1, N_HEADS * HEAD_PAD))]
        args += [w['wuk_pad'], w['wuv_slot'], w['v_ones']]
        out_shape += [jax.ShapeDtypeStruct((t, N_HEADS * HEAD_PAD), BF16)] * 3
        out_specs += [row(N_HEADS * HEAD_PAD)] * 3
    return pl.pallas_call(
        body, grid=(t // tm,), in_specs=in_specs, out_specs=out_specs, out_shape=out_shape,
        compiler_params=_params("parallel", vmem=VMEM_LIMIT),
        name="inproj_sample" if sample else "inproj_prompt")(*args)


def _ln_swish(conv, g, b):
    mu = jnp.mean(conv, axis=-1, keepdims=True)
    xc = conv - mu
    var = jnp.mean(xc * xc, axis=-1, keepdims=True)
    y = xc * lax.rsqrt(var + EPS) * g + b
    return y * jax.nn.sigmoid(y)


def _conv_prompt_kernel(main_ref, halo_ref, prev_ref, w_ref, b_ref, g_ref, lb_ref, o_ref, win_ref):
    tt = main_ref.shape[1]
    first = pl.program_id(1) == 0
    win_ref[0, 0:CONV_HALO, :] = jnp.where(first, prev_ref[0], halo_ref[0])
    win_ref[0, CONV_HALO:CONV_HALO + tt, :] = main_ref[0]
    n = tt + CONV_HALO - SUBLANES
    for s in range(1, SUBLANES):
        win_ref[s, 0:n, :] = win_ref[0, s:s + n, :]
    for c in range(tt // CONV_CHUNK):
        acc = None
        for j in range(CONV_WIDTH):
            q, s = divmod(c * CONV_CHUNK + 2 + j, SUBLANES)
            term = win_ref[s, q * SUBLANES:q * SUBLANES + CONV_CHUNK, :] * w_ref[j:j + 1, :]
            acc = term if acc is None else acc + term
        y = _ln_swish(acc + b_ref[...], g_ref[...], lb_ref[...])
        o_ref[0, c * CONV_CHUNK:(c + 1) * CONV_CHUNK, :] = y


def _conv_prompt(u3, prev_pad, w):
    b, s, _ = u3.shape
    tt = CONV_TILE
    ratio = tt // CONV_HALO
    return pl.pallas_call(
        _conv_prompt_kernel, grid=(b, s // tt),
        in_specs=[pl.BlockSpec((1, tt, CONV_CH), lambda bi, i: (bi, i, 0)),
                  pl.BlockSpec((1, CONV_HALO, CONV_CH), lambda bi, i: (bi, jnp.maximum(i * ratio - 1, 0), 0)),
                  pl.BlockSpec((1, CONV_HALO, CONV_CH), lambda bi, i: (bi, 0, 0)),
                  _const_spec((CONV_HALO, CONV_CH)), _const_spec((1, CONV_CH)),
                  _const_spec((1, CONV_CH)), _const_spec((1, CONV_CH))],
        out_specs=pl.BlockSpec((1, tt, CONV_CH), lambda bi, i: (bi, i, 0)),
        out_shape=jax.ShapeDtypeStruct((b, s, CONV_CH), F32),
        scratch_shapes=[pltpu.VMEM((SUBLANES, tt + CONV_HALO, CONV_CH), F32)],
        compiler_params=_params("parallel", "parallel"),
        name="conv_prompt")(u3, u3, prev_pad, w['conv_w'], w['conv_b'], w['conv_ln_g'], w['conv_ln_b'])


def _conv_sample_kernel(win_ref, w_ref, b_ref, g_ref, lb_ref, o_ref):
    t = o_ref.shape[1]
    acc = win_ref[:, 2:2 + t, :] * w_ref[0:1, :]
    for j in range(1, CONV_WIDTH):
        acc = acc + win_ref[:, 2 + j:2 + j + t, :] * w_ref[j:j + 1, :]
    y = _ln_swish(acc + b_ref[...], g_ref[...], lb_ref[...])
    o_ref[...] = y


def _conv_sample(upad, w):
    b, s_pad, _ = upad.shape
    t = s_pad - CONV_HALO
    bb = 8
    return pl.pallas_call(
        _conv_sample_kernel, grid=(b // bb,),
        in_specs=[pl.BlockSpec((bb, s_pad, CONV_CH), lambda i: (i, 0, 0)),
                  _const_spec((CONV_HALO, CONV_CH)), _const_spec((1, CONV_CH)),
                  _const_spec((1, CONV_CH)), _const_spec((1, CONV_CH))],
        out_specs=pl.BlockSpec((bb, t, CONV_CH), lambda i: (i, 0, 0)),
        out_shape=jax.ShapeDtypeStruct((b, t, CONV_CH), F32),
        compiler_params=_params("parallel"),
        name="conv_sample")(upad, w['conv_w'], w['conv_b'], w['conv_ln_g'], w['conv_ln_b'])


def _attn_prompt_kernel(q_ref, k_ref, v_ref, o_ref, s_ref, m_ref):
    tq = q_ref.shape[1]
    qi = pl.program_id(2)
    nh = ATTN_HEADS
    head = lambda hd: slice(hd * HEAD_PAD, (hd + 1) * HEAD_PAD)
    rows = lax.broadcasted_iota(jnp.int32, (tq, tq), 0)
    cols = lax.broadcasted_iota(jnp.int32, (tq, tq), 1)
    lane = lax.broadcasted_iota(jnp.int32, (tq, LANES), 1)

    def scores(j, m_all, masked):
        start = pl.multiple_of(j * tq, tq)
        for hd in range(nh):
            s = _dot_t(q_ref[0, :, head(hd)], k_ref[0, pl.ds(start, tq), head(hd)])
            if masked:
                s = jnp.where(cols <= rows, s, NEG_INF)
            s_ref[hd, j] = s
            m_all = jnp.where(lane == hd, jnp.maximum(m_all, jnp.max(s, axis=-1, keepdims=True)), m_all)
        return m_all

    m_all = jnp.full((tq, LANES), NEG_INF, F32)
    m_all = lax.fori_loop(0, qi, functools.partial(scores, masked=False), m_all)
    m_all = scores(qi, m_all, True)
    for hd in range(nh):
        m_ref[hd] = jnp.broadcast_to(m_all[:, hd:hd + 1], (tq, LANES))

    def values(j, accs):
        start = pl.multiple_of(j * tq, tq)
        out = []
        for hd in range(nh):
            m = m_ref[hd]
            p = jnp.exp(s_ref[hd, j] - jnp.concatenate([m] * (tq // LANES), axis=1)).astype(BF16)
            out.append(accs[hd] + _dot(p, v_ref[0, pl.ds(start, tq), head(hd)]))
        return tuple(out)

    accs = tuple(jnp.zeros((tq, HEAD_PAD), F32) for _ in range(nh))
    accs = lax.fori_loop(0, qi + 1, values, accs)
    for hp in range(nh // 2):
        even, odd = accs[2 * hp], accs[2 * hp + 1]
        o_even = even / even[:, V_DIM:V_DIM + 1]
        o_odd = odd / odd[:, 0:1]
        o_ref[0, :, hp * LANES:(hp + 1) * LANES] = jnp.where(lane < V_DIM, o_even, o_odd).astype(BF16)


def _attn_prompt(q, k, v):
    b, s, _ = q.shape
    tq = ATTN_TILE
    nh = ATTN_HEADS
    return pl.pallas_call(
        _attn_prompt_kernel, grid=(b, N_HEADS // nh, s // tq),
        in_specs=[pl.BlockSpec((1, tq, nh * HEAD_PAD), lambda bi, hq, qi: (bi, qi, hq)),
                  pl.BlockSpec((1, s, nh * HEAD_PAD), lambda bi, hq, qi: (bi, 0, hq)),
                  pl.BlockSpec((1, s, nh * HEAD_PAD), lambda bi, hq, qi: (bi, 0, hq))],
        out_specs=pl.BlockSpec((1, tq, nh * V_DIM), lambda bi, hq, qi: (bi, qi, hq)),
        out_shape=jax.ShapeDtypeStruct((b, s, N_HEADS * V_DIM), BF16),
        scratch_shapes=[pltpu.VMEM((nh, s // tq, tq, tq), F32), pltpu.VMEM((nh, tq, LANES), F32)],
        compiler_params=_params("parallel", "parallel", "arbitrary", vmem=VMEM_LIMIT),
        name="attn_prompt")(q, k, v)


def _attn_sample_kernel(pt_ref, ql_ref, qp_ref, cn_ref, kn_ref, ckv_hbm, kpe_hbm, o_ref,
                        ckv_buf, kpe_buf, sem):
    b = pl.program_id(0)
    n_pages = kpe_buf.shape[1]
    n_chunks = n_pages // PAGES_PER_CHUNK
    chunk = PAGES_PER_CHUNK * PAGE_SIZE
    t_new = cn_ref.shape[1]
    rows_q = N_HEADS * t_new

    def fetch(batch, slot):
        def one(p, _):
            page = pt_ref[batch * n_pages + p]
            rows = pl.ds(pl.multiple_of(p * PAGE_SIZE, PAGE_SIZE), PAGE_SIZE)
            pltpu.make_async_copy(ckv_hbm.at[page], ckv_buf.at[slot, rows], sem.at[0, slot]).start()
            pltpu.make_async_copy(kpe_hbm.at[page], kpe_buf.at[slot, p], sem.at[1, slot]).start()
            return 0
        lax.fori_loop(0, n_pages, one, 0, unroll=4)

    slot = b % 2

    @pl.when(b == 0)
    def _():
        fetch(0, 0)

    @pl.when(b + 1 < pl.num_programs(0))
    def _():
        fetch(b + 1, 1 - slot)

    pltpu.make_async_copy(ckv_buf.at[slot], ckv_buf.at[slot], sem.at[0, slot]).wait()
    pltpu.make_async_copy(kpe_buf.at[slot], kpe_buf.at[slot], sem.at[1, slot]).wait()

    ql = ql_ref[:, 0].reshape(rows_q, KV_RANK).astype(BF16)
    qp = qp_ref[:, 0].reshape(rows_q, QK_ROPE).astype(BF16)

    pad = PAGE_SIZE - t_new
    kc_new = jnp.concatenate([cn_ref[0], jnp.zeros((pad, KV_RANK), F32)], axis=0).astype(BF16)
    kp_new = jnp.concatenate([kn_ref[0], jnp.zeros((pad, QK_ROPE), F32)], axis=0).astype(BF16)
    s_new = _dot_t(ql, kc_new) + _dot_t(qp, kp_new)
    t_q = lax.broadcasted_iota(jnp.int32, s_new.shape, 0) % t_new
    cols = lax.broadcasted_iota(jnp.int32, s_new.shape, 1)
    s_new = jnp.where(cols <= t_q, s_new, NEG_INF)
    def part(s, values):
        m = jnp.max(s, axis=-1, keepdims=True)
        p = jnp.exp(s - m)
        return m, jnp.sum(p, axis=-1, keepdims=True), _dot(p.astype(BF16), values)

    def scores(c):
        kc = ckv_buf[slot, c * chunk:(c + 1) * chunk, :].astype(BF16)
        kpt = jnp.concatenate([kpe_buf[slot, c * PAGES_PER_CHUNK + i] for i in range(PAGES_PER_CHUNK)],
                              axis=1).astype(BF16)
        return _dot_t(ql, kc) + _dot(qp, kpt), kc

    parts = [part(s_new, kc_new)]
    nxt = scores(0)
    for c in range(n_chunks):
        cur = nxt
        if c + 1 < n_chunks:
            nxt = scores(c + 1)
        parts.append(part(*cur))
    m = parts[0][0]
    for mp, _, _ in parts[1:]:
        m = jnp.maximum(m, mp)
    l = jnp.zeros_like(m)
    acc = jnp.zeros((rows_q, KV_RANK), F32)
    for mp, lp, ap in parts:
        w = jnp.exp(mp - m)
        l = l + w * lp
        acc = acc + w * ap
    o_ref[:, 0] = (acc / l).reshape(N_HEADS, t_new, KV_RANK)


def _attn_sample(page_table, qlat, qpe, ckv_new, kpe_new, cache_ckv, cache_kpe_t):
    bs, n_pages = page_table.shape
    t_new = ckv_new.shape[1]
    qlat4 = qlat.reshape(N_HEADS, bs, t_new, KV_RANK)
    qpe4 = qpe.reshape(N_HEADS, bs, t_new, QK_ROPE)
    past = n_pages * PAGE_SIZE
    in_specs = [pl.BlockSpec((N_HEADS, 1, t_new, KV_RANK), lambda b, pt: (0, b, 0, 0)),
                pl.BlockSpec((N_HEADS, 1, t_new, QK_ROPE), lambda b, pt: (0, b, 0, 0)),
                pl.BlockSpec((1, t_new, KV_RANK), lambda b, pt: (b, 0, 0)),
                pl.BlockSpec((1, t_new, QK_ROPE), lambda b, pt: (b, 0, 0)),
                pl.BlockSpec(memory_space=pl.ANY), pl.BlockSpec(memory_space=pl.ANY)]
    grid_spec = pltpu.PrefetchScalarGridSpec(
        num_scalar_prefetch=1, grid=(bs,), in_specs=in_specs,
        out_specs=pl.BlockSpec((N_HEADS, 1, t_new, KV_RANK), lambda b, pt: (0, b, 0, 0)),
        scratch_shapes=[pltpu.VMEM((2, past, KV_RANK), F32),
                        pltpu.VMEM((2, n_pages, QK_ROPE, PAGE_SIZE), F32),
                        pltpu.SemaphoreType.DMA((2, 2))])
    o = pl.pallas_call(
        _attn_sample_kernel, grid_spec=grid_spec,
        out_shape=jax.ShapeDtypeStruct((N_HEADS, bs, t_new, KV_RANK), F32),
        compiler_params=_params("arbitrary", vmem=VMEM_LIMIT),
        name="attn_sample")(page_table.reshape(-1), qlat4, qpe4, ckv_new, kpe_new, cache_ckv, cache_kpe_t)
    return o.reshape(N_HEADS, bs * t_new, KV_RANK)


def _post_a_kernel(x_ref, conv_ref, olat_ref, wuv_ref, wout_ref, g_ref, wmq_ref, x1_ref, qm_ref):
    attn = None
    for hd in range(N_HEADS):
        d = _dot(olat_ref[hd].astype(BF16), wuv_ref[hd])
        attn = d if attn is None else attn + d
    mix = (_dot(conv_ref[...].astype(BF16), wout_ref[0:CONV_CH, :])
           + _dot(attn.astype(BF16), wout_ref[CONV_CH:, :]))
    x1 = x_ref[...] + mix
    x1_ref[...] = x1
    qm_ref[...] = _dot(_rms(x1, g_ref[...]).astype(BF16), wmq_ref[...])


def _post_a(x2d, conv2d, o_lat, w):
    t = x2d.shape[0]
    tm = TOKEN_TILE
    row = lambda n: pl.BlockSpec((tm, n), lambda i: (i, 0))
    mem_w = MEM_HEADS * MEM_HD
    return pl.pallas_call(
        _post_a_kernel, grid=(t // tm,),
        in_specs=[row(D_MODEL), row(CONV_CH), pl.BlockSpec((N_HEADS, tm, KV_RANK), lambda i: (0, i, 0)),
                  _const_spec((N_HEADS, KV_RANK, N_HEADS * V_DIM)),
                  _const_spec((D_MODEL, D_MODEL)), _const_spec((1, D_MODEL)), _const_spec((D_MODEL, mem_w))],
        out_specs=[row(D_MODEL), row(mem_w)],
        out_shape=[jax.ShapeDtypeStruct((t, D_MODEL), F32), jax.ShapeDtypeStruct((t, mem_w), F32)],
        compiler_params=_params("parallel", vmem=VMEM_LIMIT),
        name="post_a_sample")(x2d, conv2d, o_lat, w['wuv_pad'], w['w_out'], w['g_mem_q'], w['w_mq'])


def _mem_kv_kernel(m_ref, g_ref, wk_ref, wv_ref, k_ref, v_ref, kw_ref, vw_ref):
    tm = m_ref.shape[0]
    m = _rms(m_ref[...], g_ref[...]).astype(BF16)
    k = _dot(m, wk_ref[...])
    v = _dot(m, wv_ref[...])
    kw_ref[...] = k
    vw_ref[...] = v
    for hd in range(MEM_HEADS):
        sl = slice(hd * MEM_HD, (hd + 1) * MEM_HD)
        k_ref[pl.ds(hd, tm, stride=MEM_HEADS), :] = k[:, sl]
        v_ref[pl.ds(hd, tm, stride=MEM_HEADS), :] = v[:, sl]


def _mem_kv(mem2d, w):
    t = mem2d.shape[0]
    tm = TOKEN_TILE
    mem_w = MEM_HEADS * MEM_HD
    rows = pl.BlockSpec((tm * MEM_HEADS, MEM_HD), lambda i: (i, 0))
    wide = pl.BlockSpec((tm, mem_w), lambda i: (i, 0))
    return pl.pallas_call(
        _mem_kv_kernel, grid=(t // tm,),
        in_specs=[pl.BlockSpec((tm, D_MODEL), lambda i: (i, 0)), _const_spec((1, D_MODEL)),
                  _const_spec((D_MODEL, mem_w)), _const_spec((D_MODEL, mem_w))],
        out_specs=[rows, rows, wide, wide],
        out_shape=[jax.ShapeDtypeStruct((t * MEM_HEADS, MEM_HD), F32)] * 2
                  + [jax.ShapeDtypeStruct((t, mem_w), F32)] * 2,
        compiler_params=_params("parallel"),
        name="mem_kv")(mem2d, w['g_mem_kv'], w['w_mk'], w['w_mv'])


def _mem_attn_rows_kernel(q_ref, k_ref, v_ref, o_ref):
    bb, tq, _ = q_ref.shape
    rows = MEM_TOKENS * MEM_HEADS
    head = lambda hd: slice(hd * MEM_HD, (hd + 1) * MEM_HD)
    q_head = lax.broadcasted_iota(jnp.int32, (MEM_HEADS * tq, rows), 0) // tq
    k_head = lax.broadcasted_iota(jnp.int32, (MEM_HEADS * tq, rows), 1) % MEM_HEADS
    own = q_head == k_head
    for bi in range(bb):
        q = jnp.concatenate([q_ref[bi, :, head(hd)] for hd in range(MEM_HEADS)], axis=0).astype(BF16)
        k = k_ref[bi * rows:(bi + 1) * rows, :].astype(BF16)
        v = v_ref[bi * rows:(bi + 1) * rows, :].astype(BF16)
        s = jnp.where(own, _dot_t(q, k) * MEM_SCALE, NEG_INF)
        p = jnp.exp(s - jnp.max(s, axis=-1, keepdims=True))
        p = p / jnp.sum(p, axis=-1, keepdims=True)
        o = _dot(p.astype(BF16), v)
        for hd in range(MEM_HEADS):
            o_ref[bi, :, head(hd)] = o[hd * tq:(hd + 1) * tq, :]


def _mem_attn_rows(qm, mem_k, mem_v):
    b, s, mem_w = qm.shape
    bb = max(1, MEM_ATTN_ROWS // s)
    kv = pl.BlockSpec((bb * MEM_TOKENS * MEM_HEADS, MEM_HD), lambda bi: (bi, 0))
    return pl.pallas_call(
        _mem_attn_rows_kernel, grid=(b // bb,),
        in_specs=[pl.BlockSpec((bb, s, mem_w), lambda bi: (bi, 0, 0)), kv, kv],
        out_specs=pl.BlockSpec((bb, s, mem_w), lambda bi: (bi, 0, 0)),
        out_shape=jax.ShapeDtypeStruct((b, s, mem_w), F32),
        compiler_params=_params("parallel"),
        name="mem_attn_rows")(qm, mem_k, mem_v)


def _slab_rows(j, n):
    return pl.ds(j, n, stride=SLAB)


def _post_b_kernel(n_p, x_ref, conv_ref, attn_ref, mk_ref, mv_ref, wout_ref, gq_ref, wmq_ref,
                   x1s_ref, oms_ref, wmo_ref, g_ref, wr_ref, br_ref,
                   x2_ref, h3_ref, idx_ref, gate_ref, rank_ref, cnt_ref, carry_ref, x1_scr, om_scr):
    tm = x2_ref.shape[0]
    i = pl.program_id(0)

    @pl.when(i == 0)
    def _():
        carry_ref[...] = jnp.zeros(carry_ref.shape, F32)

    @pl.when(i < n_p)
    def _():
        mix = (_dot(conv_ref[...].astype(BF16), wout_ref[0:CONV_CH, :])
               + _dot(attn_ref[...], wout_ref[CONV_CH:, :]))
        x1p = x_ref[...] + mix
        x1_scr[...] = x1p
        qm = _dot(_rms(x1p, gq_ref[...]).astype(BF16), wmq_ref[...])
        for hd in range(MEM_HEADS):
            sl = slice(hd * MEM_HD, (hd + 1) * MEM_HD)
            s = _dot_t(qm[:, sl].astype(BF16), mk_ref[0, :, sl].astype(BF16)) * MEM_SCALE
            p = jnp.exp(s - jnp.max(s, axis=-1, keepdims=True))
            p = p / jnp.sum(p, axis=-1, keepdims=True)
            om_scr[:, sl] = _dot(p.astype(BF16), mv_ref[0, :, sl].astype(BF16))

    @pl.when(i >= n_p)
    def _():
        x1_scr[...] = x1s_ref[...]
        om_scr[...] = oms_ref[...]

    x1 = x1_scr[...]
    om = om_scr[...]
    x2 = x1 + _dot(om.astype(BF16), wmo_ref[...])
    x2_ref[...] = x2
    h3 = _rms(x2, g_ref[...])
    for j in range(SLAB):
        h3_ref[_slab_rows(j, tm), :] = h3[:, j * LANES:(j + 1) * LANES]
    logits = _dot(h3.astype(BF16), wr_ref[...]) + br_ref[...]
    lane = lax.broadcasted_iota(jnp.int32, logits.shape, 1)
    lane_f = lane.astype(F32)
    vals, hots = [], []
    idx_out = jnp.zeros(logits.shape, F32)
    for kk in range(TOP_K):
        mx = jnp.max(logits, axis=-1, keepdims=True)
        first = jnp.min(jnp.where(logits == mx, lane_f, float(LANES)), axis=-1, keepdims=True)
        hot = lane_f == first
        logits = jnp.where(hot, -jnp.inf, logits)
        vals.append(mx)
        hots.append(hot)
        idx_out = jnp.where(lane == kk, first, idx_out)
    exps = [jnp.exp(v - vals[0]) for v in vals]
    denom = exps[0] + exps[1] + exps[2] + exps[3]
    chosen = jnp.zeros(logits.shape, F32)
    gate_out = jnp.zeros(logits.shape, F32)
    for kk in range(TOP_K):
        chosen = chosen + hots[kk].astype(F32)
        gate_out = jnp.where(lane == kk, exps[kk] / denom, gate_out)
    r_i = lax.broadcasted_iota(jnp.int32, (tm, tm), 0)
    c_i = lax.broadcasted_iota(jnp.int32, (tm, tm), 1)
    tril = (c_i < r_i).astype(BF16)
    before = _dot(tril, chosen.astype(BF16)) + carry_ref[...]
    rank_out = jnp.zeros(logits.shape, F32)
    for kk in range(TOP_K):
        rk = jnp.sum(jnp.where(hots[kk], before, 0.0), axis=-1, keepdims=True)
        rank_out = jnp.where(lane == kk, rk, rank_out)
    carry = carry_ref[...] + jnp.sum(chosen, axis=0, keepdims=True)
    carry_ref[...] = carry
    idx_ref[...] = idx_out.astype(jnp.int32)
    gate_ref[...] = gate_out
    rank_ref[...] = rank_out.astype(jnp.int32)
    cnt_ref[...] = carry.astype(jnp.int32)


def _post_b(x_p, conv_p, attn_p, mk_wide, mv_wide, x1_s, om_s, w):
    tm = TOKEN_TILE
    n_p, n_s = x_p.shape[0] // tm, x1_s.shape[0] // tm
    tiles_per_seq = n_p // mk_wide.shape[0]
    t = (n_p + n_s) * tm
    mem_w = MEM_HEADS * MEM_HD
    row = lambda n: pl.BlockSpec((tm, n), lambda i: (i, 0))
    row_p = lambda n: pl.BlockSpec((tm, n), lambda i: (jnp.minimum(i, n_p - 1), 0))
    row_s = lambda n: pl.BlockSpec((tm, n), lambda i: (jnp.maximum(i - n_p, 0), 0))
    mem = pl.BlockSpec((1, MEM_TOKENS, mem_w), lambda i: (jnp.minimum(i, n_p - 1) // tiles_per_seq, 0, 0))
    return pl.pallas_call(
        functools.partial(_post_b_kernel, n_p), grid=(n_p + n_s,),
        in_specs=[row_p(D_MODEL), row_p(CONV_CH), row_p(N_HEADS * V_DIM), mem, mem,
                  _const_spec((D_MODEL, D_MODEL)), _const_spec((1, D_MODEL)), _const_spec((D_MODEL, mem_w)),
                  row_s(D_MODEL), row_s(mem_w),
                  _const_spec((mem_w, D_MODEL)), _const_spec((1, D_MODEL)),
                  _const_spec((D_MODEL, LANES)), _const_spec((1, LANES))],
        out_specs=[row(D_MODEL), pl.BlockSpec((tm * SLAB, LANES), lambda i: (i, 0)),
                   row(LANES), row(LANES), row(LANES), _const_spec((1, LANES))],
        out_shape=[jax.ShapeDtypeStruct((t, D_MODEL), F32), jax.ShapeDtypeStruct((t * SLAB, LANES), F32),
                   jax.ShapeDtypeStruct((t, LANES), jnp.int32), jax.ShapeDtypeStruct((t, LANES), F32),
                   jax.ShapeDtypeStruct((t, LANES), jnp.int32), jax.ShapeDtypeStruct((1, LANES), jnp.int32)],
        scratch_shapes=[pltpu.VMEM((1, LANES), F32), pltpu.VMEM((tm, D_MODEL), F32), pltpu.VMEM((tm, mem_w), F32)],
        compiler_params=_params("arbitrary", vmem=VMEM_LIMIT),
        name="post_b")(x_p, conv_p, attn_p, mk_wide, mv_wide, w['w_out'], w['g_mem_q'], w['w_mq'],
                       x1_s, om_s, w['w_mo'], w['g_ffn'], w['w_router'], w['b_router'])


def _slab(row):
    return pl.ds(pl.multiple_of(row * SLAB, SLAB), SLAB)


def _dispatch_kernel(start_ref, lo_ref, hi_ref, idx_ref, rank_ref, h_ref, xs_hbm, zero_ref, sem):
    i = pl.program_id(0)
    tm = h_ref.shape[0] // SLAB

    def issue(t, _):
        for k in range(TOP_K):
            r = t * TOP_K + k
            d = start_ref[idx_ref[r]] + rank_ref[r]
            pltpu.make_async_copy(h_ref.at[_slab(t)], xs_hbm.at[_slab(d)], sem).start(priority=k % 2)
        return 0

    lax.fori_loop(0, tm, issue, 0, unroll=2)
    for _ in range(TOP_K):
        pltpu.make_async_copy(h_ref, xs_hbm.at[pl.ds(0, tm * SLAB)], sem).wait()

    @pl.when(i == pl.num_programs(0) - 1)
    def _():
        zero_ref[...] = jnp.zeros(zero_ref.shape, F32)
        for e in range(N_EXPERTS):
            lo, hi = lo_ref[e], hi_ref[e]

            def zissue(r, _):
                pltpu.make_async_copy(zero_ref.at[_slab(0)], xs_hbm.at[_slab(r)], sem).start()
                return 0

            def zdrain(r, _):
                pltpu.make_async_copy(zero_ref.at[_slab(0)], xs_hbm.at[_slab(0)], sem).wait()
                return 0

            lax.fori_loop(lo, hi, zissue, 0)
            lax.fori_loop(lo, hi, zdrain, 0)

        bm = zero_ref.shape[0] // SLAB

        def tail(blk, _):
            rows = pl.ds(pl.multiple_of(blk * bm * SLAB, SLAB), bm * SLAB)
            cp = pltpu.make_async_copy(zero_ref, xs_hbm.at[rows], sem)
            cp.start()
            cp.wait()
            return 0

        lax.fori_loop(hi_ref[N_EXPERTS - 1] // bm, xs_hbm.shape[0] // (bm * SLAB), tail, 0)


def _dispatch(h3s, idx_flat, rank_flat, pad_start, pad_lo, pad_hi, n_rows):
    tm = TOKEN_TILE
    t = h3s.shape[0] // SLAB
    grid_spec = pltpu.PrefetchScalarGridSpec(
        num_scalar_prefetch=3, grid=(t // tm,),
        in_specs=[pl.BlockSpec((tm * TOP_K,), lambda i, *_: (i,), memory_space=pltpu.SMEM),
                  pl.BlockSpec((tm * TOP_K,), lambda i, *_: (i,), memory_space=pltpu.SMEM),
                  pl.BlockSpec((tm * SLAB, LANES), lambda i, *_: (i, 0))],
        out_specs=pl.BlockSpec(memory_space=pl.ANY),
        scratch_shapes=[pltpu.VMEM((MOE_ROWS * SLAB, LANES), F32), pltpu.SemaphoreType.DMA])
    return pl.pallas_call(
        _dispatch_kernel, grid_spec=grid_spec,
        out_shape=jax.ShapeDtypeStruct((n_rows * SLAB, LANES), F32),
        compiler_params=pltpu.CompilerParams(dimension_semantics=("arbitrary",), has_side_effects=True),
        name="moe_dispatch")(pad_start, pad_lo, pad_hi, idx_flat, rank_flat, h3s)


def _expert_kernel(src_ref, exp_ref, first_ref, valid_ref, next_ref, x_ref, wg_hbm, wu_hbm, wd_hbm,
                   bg_ref, bu_ref, bd_ref, y_ref, wf_ref, wb_ref, xb_ref, sem):
    b = pl.program_id(0)
    bm = xb_ref.shape[0]
    w_hbm = (wg_hbm, wu_hbm, wd_hbm)

    def fetch(e):
        for i in range(3):
            pltpu.make_async_copy(w_hbm[i].at[e], wf_ref.at[i], sem).start()

    @pl.when(b == 0)
    def _():
        fetch(exp_ref[0])

    @pl.when(first_ref[b] == 1)
    def _():
        for i in range(3):
            pltpu.make_async_copy(w_hbm[i].at[0], wf_ref.at[i], sem).wait()
        def cast(r, _):
            rows = pl.ds(pl.multiple_of(r * CAST_ROWS, CAST_ROWS), CAST_ROWS)
            for i in range(3):
                wb_ref[i, rows, :] = wf_ref[i, rows, :].astype(BF16)
            return 0

        lax.fori_loop(0, D_MODEL // CAST_ROWS, cast, 0)

        @pl.when(next_ref[b] >= 0)
        def _():
            fetch(next_ref[b])

    @pl.when(valid_ref[b] == 1)
    def _():
        for j in range(SLAB):
            xb_ref[:, j * LANES:(j + 1) * LANES] = x_ref[_slab_rows(j, bm), :].astype(BF16)
        x = xb_ref[...]
        g = _dot(x, wb_ref[0]) + bg_ref[0]
        u = _dot(x, wb_ref[1]) + bu_ref[0]
        g = jnp.minimum(g, SWIGLU_LIMIT)
        u = jnp.clip(u, -SWIGLU_LIMIT, SWIGLU_LIMIT)
        a = (u + 1.0) * (g * jax.nn.sigmoid(SWIGLU_ALPHA * g))
        y = _dot(a.astype(BF16), wb_ref[2]) + bd_ref[0]
        for j in range(SLAB):
            y_ref[_slab_rows(j, bm), :] = y[:, j * LANES:(j + 1) * LANES]

    @pl.when(valid_ref[b] == 0)
    def _():
        y_ref[...] = jnp.zeros(y_ref.shape, F32)


def _experts(xs, blk_src, blk_exp, blk_first, blk_valid, blk_next, w):
    bm = MOE_ROWS
    n_blocks = xs.shape[0] // (bm * SLAB)
    hbm = pl.BlockSpec(memory_space=pl.ANY)
    bspec = pl.BlockSpec((1, 1, D_MODEL), lambda b, src, ex, *_: (ex[b], 0, 0))
    grid_spec = pltpu.PrefetchScalarGridSpec(
        num_scalar_prefetch=5, grid=(n_blocks,),
        in_specs=[pl.BlockSpec((bm * SLAB, LANES), lambda b, src, *_: (src[b], 0)),
                  hbm, hbm, hbm, bspec, bspec, bspec],
        out_specs=pl.BlockSpec((bm * SLAB, LANES), lambda b, *_: (b, 0)),
        scratch_shapes=[pltpu.VMEM((3, D_MODEL, D_MODEL), F32), pltpu.VMEM((3, D_MODEL, D_MODEL), BF16),
                        pltpu.VMEM((bm, D_MODEL), BF16), pltpu.SemaphoreType.DMA])
    return pl.pallas_call(
        _expert_kernel, grid_spec=grid_spec,
        out_shape=jax.ShapeDtypeStruct(xs.shape, F32),
        compiler_params=_params("arbitrary", vmem=VMEM_LIMIT),
        name="moe_experts")(blk_src, blk_exp, blk_first, blk_valid, blk_next, xs,
                            w['w_gate'], w['w_up'], w['w_down'], w['b_gate'], w['b_up'], w['b_down'])


def _combine_kernel(n_p, start_ref, idx_ref, rank_ref, idx_nx_ref, rank_nx_ref, gate_ref, yb_hbm, x2_ref, g_ref,
                    yp_ref, ys_ref, rows_ref, ysum_ref, y_ref, sem):
    i = pl.program_id(0)
    tm = x2_ref.shape[0]

    def gather(ix_ref, rk_ref, slot):
        def issue(t, _):
            for k in range(TOP_K):
                r = t * TOP_K + k
                d = start_ref[ix_ref[r]] + rk_ref[r]
                pltpu.make_async_copy(yb_hbm.at[_slab(d)], rows_ref.at[slot, _slab(r)],
                                      sem.at[slot]).start(priority=k % 2)
            return 0
        lax.fori_loop(0, tm, issue, 0, unroll=2)

    slot = i % 2

    @pl.when(i == 0)
    def _():
        gather(idx_ref, rank_ref, 0)

    @pl.when(i + 1 < pl.num_programs(0))
    def _():
        gather(idx_nx_ref, rank_nx_ref, 1 - slot)

    for _ in range(TOP_K):
        pltpu.make_async_copy(yb_hbm.at[pl.ds(0, tm * SLAB)], rows_ref.at[slot, pl.ds(0, tm * SLAB)],
                              sem.at[slot]).wait()

    def token(t, _):
        acc = rows_ref[slot, _slab(t * TOP_K)] * gate_ref[t * TOP_K]
        for k in range(1, TOP_K):
            acc = acc + rows_ref[slot, _slab(t * TOP_K + k)] * gate_ref[t * TOP_K + k]
        ysum_ref[_slab(t)] = acc
        return 0

    lax.fori_loop(0, tm, token, 0, unroll=4)

    ss = jnp.zeros((tm, 1), F32)
    for j in range(SLAB):
        y = x2_ref[:, j * LANES:(j + 1) * LANES] + ysum_ref[_slab_rows(j, tm), :]
        y_ref[:, j * LANES:(j + 1) * LANES] = y
        ss = ss + jnp.sum(y * y, axis=-1, keepdims=True)
    out = y_ref[...] * lax.rsqrt(ss * (1.0 / D_MODEL) + EPS) * g_ref[...]

    @pl.when(i < n_p)
    def _():
        yp_ref[...] = out

    @pl.when(i >= n_p)
    def _():
        ys_ref[...] = out


def _combine(yb, idx_flat, rank_flat, gate_flat, pad_start, x2, g_final, n_p):
    t = x2.shape[0]
    tm = TOKEN_TILE
    n = t // tm
    n_s = n - n_p
    cur = lambda i, *_: (i,)
    nxt = lambda i, *_: (jnp.minimum(i + 1, n - 1),)
    smem = lambda index_map: pl.BlockSpec((tm * TOP_K,), index_map, memory_space=pltpu.SMEM)
    grid_spec = pltpu.PrefetchScalarGridSpec(
        num_scalar_prefetch=1, grid=(n,),
        in_specs=[smem(cur), smem(cur), smem(nxt), smem(nxt), smem(cur),
                  pl.BlockSpec(memory_space=pl.ANY),
                  pl.BlockSpec((tm, D_MODEL), lambda i, *_: (i, 0)),
                  pl.BlockSpec((1, D_MODEL), lambda i, *_: (0, 0))],
        out_specs=[pl.BlockSpec((tm, D_MODEL), lambda i, *_: (jnp.minimum(i, n_p - 1), 0)),
                   pl.BlockSpec((tm, D_MODEL), lambda i, *_: (jnp.maximum(i - n_p, 0), 0))],
        scratch_shapes=[pltpu.VMEM((2, tm * TOP_K * SLAB, LANES), F32), pltpu.VMEM((tm * SLAB, LANES), F32),
                        pltpu.VMEM((tm, D_MODEL), F32), pltpu.SemaphoreType.DMA((2,))])
    return pl.pallas_call(
        functools.partial(_combine_kernel, n_p), grid_spec=grid_spec,
        out_shape=[jax.ShapeDtypeStruct((n_p * tm, D_MODEL), F32), jax.ShapeDtypeStruct((n_s * tm, D_MODEL), F32)],
        compiler_params=_params("arbitrary", vmem=VMEM_LIMIT),
        name="moe_combine")(pad_start, idx_flat, rank_flat, idx_flat, rank_flat, gate_flat, yb, x2, g_final)


def _moe_and_final(x2, h3s, idx128, gate128, rank128, cnt128, w, n_p):
    t = x2.shape[0]
    bm = MOE_ROWS
    n_assign = t * TOP_K
    n_blocks = (n_assign + N_EXPERTS * (bm - 1) + bm - 1) // bm
    counts = cnt128[0, :N_EXPERTS]
    padded = ((counts + bm - 1) // bm) * bm
    pad_end = jnp.cumsum(padded).astype(jnp.int32)
    pad_start = pad_end - padded
    idx_flat = idx128[:, :TOP_K].reshape(-1)
    rank_flat = rank128[:, :TOP_K].reshape(-1)
    blk = jnp.arange(n_blocks, dtype=jnp.int32)
    blk_valid = (blk * bm < pad_end[-1]).astype(jnp.int32)
    blk_src = jnp.minimum(blk, jnp.maximum(pad_end[-1] // bm - 1, 0))
    blk_exp = jnp.sum((pad_end[None, :] <= (blk_src * bm)[:, None]).astype(jnp.int32), axis=1)
    blk_exp = jnp.minimum(blk_exp, N_EXPERTS - 1)
    blk_first = jnp.concatenate([jnp.ones((1,), jnp.int32),
                                 (blk_exp[1:] != blk_exp[:-1]).astype(jnp.int32)])
    later_start = (blk[None, :] > blk[:, None]) & (blk_first[None, :] == 1)
    next_pos = jnp.min(jnp.where(later_start, blk[None, :], n_blocks), axis=1)
    blk_next = jnp.where(next_pos < n_blocks, blk_exp[jnp.minimum(next_pos, n_blocks - 1)], -1)
    xs = _dispatch(h3s, idx_flat, rank_flat, pad_start, pad_start + counts, pad_end, n_blocks * bm)
    yb = _experts(xs, blk_src, blk_exp, blk_first, blk_valid, blk_next.astype(jnp.int32), w)
    gate_flat = gate128[:, :TOP_K].reshape(-1)
    return _combine(yb, idx_flat, rank_flat, gate_flat, pad_start, x2, w['g_final'], n_p)


def _swap_halves(wcols):
    half = QK_ROPE // 2
    return jnp.concatenate([-wcols[..., half:], wcols[..., :half]], axis=-1)


def _prep_weights(g_mix, w_in, conv_w, conv_b, conv_ln_g, conv_ln_b, q_norm_g, w_q_up, kv_norm_g, w_kv_up,
                  w_out, g_mem_q, g_mem_kv, w_mq, w_mk, w_mv, w_mo, g_ffn, w_router, b_router,
                  w_gate, b_gate, w_up, b_up, w_down, b_down, g_final):
    l = 0
    w_kpe = w_in[l][:, C_KPE:]
    zpad = jnp.zeros((D_MODEL, HEAD_PAD - QK_ROPE), F32)
    w_in_ext = jnp.concatenate([w_in[l][:, :C_KPE], w_kpe, zpad, _swap_halves(w_kpe), zpad], axis=1)
    wq3 = w_q_up[l].reshape(Q_RANK, N_HEADS, QK_NOPE + QK_ROPE)
    q_nope, q_rope = wq3[..., :QK_NOPE], wq3[..., QK_NOPE:]
    z32 = jnp.zeros((Q_RANK, N_HEADS, HEAD_PAD - QK_NOPE - QK_ROPE), F32)
    wq = jnp.concatenate([q_rope, q_nope, z32], axis=-1).reshape(Q_RANK, N_HEADS * HEAD_PAD)
    wq_sw = jnp.concatenate([_swap_halves(q_rope), jnp.zeros_like(q_nope), z32], axis=-1)
    wq_sw = wq_sw.reshape(Q_RANK, N_HEADS * HEAD_PAD)
    w_uk = w_kv_up[l][:, :, :QK_NOPE]
    w_uv = w_kv_up[l][:, :, QK_NOPE:]
    wuk_pad = jnp.concatenate([jnp.zeros((KV_RANK, N_HEADS, QK_ROPE), F32), w_uk,
                               jnp.zeros((KV_RANK, N_HEADS, HEAD_PAD - QK_NOPE - QK_ROPE), F32)], axis=-1)
    wukt_pad = jnp.transpose(wuk_pad, (1, 2, 0))
    odd_head = (jnp.arange(N_HEADS) % 2 == 1)[None, :, None]
    zv = jnp.zeros_like(w_uv)
    wuv_slot = jnp.where(odd_head, jnp.concatenate([zv, w_uv], axis=-1), jnp.concatenate([w_uv, zv], axis=-1))
    lane_id = jnp.arange(HEAD_PAD)[None, :]
    v_ones = jnp.where(odd_head[0], lane_id == 0, lane_id == V_DIM).astype(F32)
    eye = jnp.eye(N_HEADS, dtype=F32)
    wuv_pad = (w_uv.transpose(1, 0, 2)[:, :, None, :] * eye[:, None, :, None])
    conv_w_pad = jnp.concatenate([conv_w[l], jnp.zeros((CONV_HALO - CONV_WIDTH, CONV_CH), F32)], axis=0)
    w_router_pad = jnp.concatenate([w_router[l], jnp.zeros((D_MODEL, LANES - N_EXPERTS), F32)], axis=1)
    b_router_pad = jnp.concatenate([b_router[l], jnp.full((LANES - N_EXPERTS,), NEG_INF, F32)])
    return {
        'g_mix': g_mix[l][None], 'w_in_ext': w_in_ext.astype(BF16),
        'q_norm_g': q_norm_g[l][None], 'wq': wq.astype(BF16), 'wq_sw': wq_sw.astype(BF16),
        'kv_norm_g': kv_norm_g[l][None],
        'wuk_pad': wuk_pad.reshape(KV_RANK, N_HEADS * HEAD_PAD).astype(BF16),
        'wuv_slot': wuv_slot.reshape(KV_RANK, N_HEADS * HEAD_PAD).astype(BF16),
        'v_ones': v_ones.reshape(1, N_HEADS * HEAD_PAD),
        'wukt_pad': wukt_pad.astype(BF16),
        'wuv_pad': wuv_pad.reshape(N_HEADS, KV_RANK, N_HEADS * V_DIM).astype(BF16),
        'conv_w': conv_w_pad, 'conv_b': conv_b[l][None],
        'conv_ln_g': conv_ln_g[l][None], 'conv_ln_b': conv_ln_b[l][None],
        'w_out': w_out[l].astype(BF16), 'g_mem_q': g_mem_q[l][None], 'w_mq': w_mq[l].astype(BF16),
        'g_mem_kv': g_mem_kv[l][None], 'w_mk': w_mk[l].astype(BF16), 'w_mv': w_mv[l].astype(BF16),
        'w_mo': w_mo[l].astype(BF16), 'g_ffn': g_ffn[l][None],
        'w_router': w_router_pad.astype(BF16), 'b_router': b_router_pad[None],
        'w_gate': w_gate[l], 'w_up': w_up[l], 'w_down': w_down[l],
        'b_gate': b_gate[l][:, None, :], 'b_up': b_up[l][:, None, :], 'b_down': b_down[l][:, None, :],
        'g_final': g_final[None],
    }


def _rope_table(pos):
    half = QK_ROPE // 2
    inv = ROPE_THETA ** (-jnp.arange(half, dtype=F32) / half)
    ang = pos.astype(F32)[:, None] * inv[None, :]
    cos, sin = jnp.cos(ang), jnp.sin(ang)
    n = pos.shape[0]
    ones = jnp.ones((n, QK_NOPE), F32)
    z = lambda k: jnp.zeros((n, k), F32)
    cq = MLA_SCALE * jnp.concatenate([cos, cos, ones, z(HEAD_PAD - QK_NOPE - QK_ROPE)], axis=1)
    sq = MLA_SCALE * jnp.concatenate([sin, sin, z(HEAD_PAD - QK_ROPE)], axis=1)
    ck = jnp.concatenate([cos, cos, z(HEAD_PAD - QK_ROPE)], axis=1)
    sk = jnp.concatenate([sin, sin, z(HEAD_PAD - QK_ROPE)], axis=1)
    return jnp.stack([cq, sq, ck, sk])


def _front(x, conv_prev, mem_k, mem_v, pos, w, paged):
    b, s, _ = x.shape
    t = b * s
    x2d = x.reshape(t, D_MODEL)
    sample = paged is not None
    if sample:
        tab = _rope_table(jnp.tile(pos, TOKEN_TILE // s))
    else:
        tab = _rope_table(pos)
    outs = _inproj(x2d, tab, w, sample)
    u, ckv, kpe = outs[0], outs[1], outs[2]
    u3 = u.reshape(b, s, CONV_CH)
    tail = CONV_WIDTH - 1
    if s >= tail:
        conv_tail = u3[:, s - tail:]
    else:
        conv_tail = jnp.concatenate([conv_prev[:, s:], u3], axis=1)
    prev_pad = jnp.concatenate([jnp.zeros((b, CONV_HALO - tail, CONV_CH), F32), conv_prev], axis=1)
    ckv3 = ckv.reshape(b, s, KV_RANK)
    kpe3 = kpe.reshape(b, s, QK_ROPE)
    if sample:
        page_table, cache_ckv, cache_kpe_t = paged
        conv_out = _conv_sample(jnp.concatenate([prev_pad, u3], axis=1), w)
        attn = _attn_sample(page_table, outs[3], outs[4], ckv3, kpe3, cache_ckv, cache_kpe_t)
    else:
        conv_out = _conv_prompt(u3, prev_pad, w)
        q, k, v = (a.reshape(b, s, -1) for a in outs[3:6])
        attn = _attn_prompt(q, k, v).reshape(t, N_HEADS * V_DIM)
    conv2d = conv_out.reshape(t, CONV_CH)
    if not sample:
        return (x2d, conv2d, attn), conv_tail, ckv3, kpe3
    x1, qm = _post_a(x2d, conv2d, attn, w)
    om = _mem_attn_rows(qm.reshape(b, s, -1), mem_k, mem_v)
    return (x1, om.reshape(t, -1)), conv_tail, ckv3, kpe3


def kernel(x_prompt, x_sample, mem_prompt, cache_ckv, cache_kpe, page_table, cache_mem_k, cache_mem_v, state_conv, g_mix, w_in, conv_w, conv_b, conv_ln_g, conv_ln_b, q_norm_g, w_q_up, kv_norm_g, w_kv_up, w_out, g_mem_q, g_mem_kv, w_mq, w_mk, w_mv, w_mo, g_ffn, w_router, b_router, w_gate, b_gate, w_up, b_up, w_down, b_down, g_final):
    assert g_mix.shape[0] == 1, "single-layer step"
    w = _prep_weights(g_mix, w_in, conv_w, conv_b, conv_ln_g, conv_ln_b, q_norm_g, w_q_up, kv_norm_g, w_kv_up,
                      w_out, g_mem_q, g_mem_kv, w_mq, w_mk, w_mv, w_mo, g_ffn, w_router, b_router,
                      w_gate, b_gate, w_up, b_up, w_down, b_down, g_final)
    b_p, s_p, _ = x_prompt.shape
    b_s, t_s, _ = x_sample.shape
    past = page_table.shape[1] * PAGE_SIZE

    mk, mv, mk_wide, mv_wide = _mem_kv(mem_prompt.reshape(-1, D_MODEL), w)
    conv0 = jnp.zeros((b_p, CONV_WIDTH - 1, CONV_CH), F32)
    (x_p, conv_out_p, attn_p), conv_p, ckv_p, kpe_p = _front(
        x_prompt, conv0, None, None, jnp.arange(s_p, dtype=jnp.int32), w, None)
    (x1_s, om_s), conv_s, ckv_s, kpe_s = _front(
        x_sample, state_conv[0], cache_mem_k[0].reshape(-1, MEM_HD), cache_mem_v[0].reshape(-1, MEM_HD),
        past + jnp.arange(t_s, dtype=jnp.int32), w,
        (page_table, cache_ckv[0], jnp.swapaxes(cache_kpe[0], 1, 2)))

    x2, h3s, idx128, gate128, rank128, cnt128 = _post_b(
        x_p, conv_out_p, attn_p, mk_wide.reshape(b_p, MEM_TOKENS, -1), mv_wide.reshape(b_p, MEM_TOKENS, -1),
        x1_s, om_s, w)
    y_p, y_s = _moe_and_final(x2, h3s, idx128, gate128, rank128, cnt128, w, x_p.shape[0] // TOKEN_TILE)

    mem_shape = (1, b_p, MEM_TOKENS, MEM_HEADS, MEM_HD)
    return (y_p.reshape(b_p, s_p, D_MODEL), y_s.reshape(b_s, t_s, D_MODEL), ckv_p[None], kpe_p[None],
            mk.reshape(mem_shape), mv.reshape(mem_shape), conv_p[None], ckv_s[None], kpe_s[None], conv_s[None])
```

```python
import functools

import jax
import jax.numpy as jnp
from jax import lax
from jax.experimental import pallas as pl
from jax.experimental.pallas import tpu as pltpu

F32 = jnp.float32
BF16 = jnp.bfloat16

D_MODEL = 1024
PAGE_SIZE = 128
CONV_CH = 512
CONV_WIDTH = 31
N_HEADS = 8
QK_NOPE = 64
QK_ROPE = 32
V_DIM = 64
Q_RANK = 384
KV_RANK = 256
ROPE_THETA = 10000.0
MLA_SCALE = (QK_NOPE + QK_ROPE) ** -0.5
MEM_TOKENS = 256
MEM_HEADS = 4
MEM_HD = 128
MEM_SCALE = MEM_HD ** -0.5
N_EXPERTS = 32
TOP_K = 4
SWIGLU_LIMIT = 7.0
SWIGLU_ALPHA = 1.702
EPS = 1e-6
NEG_INF = -1e30

LANES = 128
SUBLANES = 8
HEAD_PAD = 128
C_VAL, C_GATE, C_Q, C_CKV, C_KPE, C_KPE_SW, C_END = 0, 512, 1024, 1408, 1664, 1792, 1920
TOKEN_TILE = 256
ATTN_TILE = 256
ATTN_HEADS = 8
CONV_HALO = 32
CONV_CHUNK = 32
MEM_ATTN_ROWS = 32
PAGES_PER_CHUNK = 32
SLAB = 8
MOE_ROWS = 256
CAST_ROWS = 32
VMEM_LIMIT = 48 * 1024 * 1024


def _rms(x, g):
    return x * lax.rsqrt(jnp.mean(x * x, axis=-1, keepdims=True) + EPS) * g


def _dot(a, b):
    return jnp.dot(a, b, preferred_element_type=F32)


def _dot_t(a, b):
    return lax.dot_general(a, b, (((1,), (1,)), ((), ())), preferred_element_type=F32)


def _params(*sem, vmem=None):
    return pltpu.CompilerParams(dimension_semantics=sem, vmem_limit_bytes=vmem)


def _const_spec(shape):
    nd = len(shape)
    return pl.BlockSpec(shape, lambda *_: (0,) * nd)


def _inproj_common(x_ref, gmix_ref, win_ref, qg_ref, wq_ref, wqsw_ref, kvg_ref, tab_ref,
                   u_ref, ckv_ref, kpe_ref, after_u=None, after_q=None):
    h = _rms(x_ref[...], gmix_ref[...]).astype(BF16)
    glu = _dot(h, win_ref[:, C_VAL:C_Q])
    u = glu[:, :CONV_CH] * jax.nn.sigmoid(glu[:, CONV_CH:])
    u_ref[...] = u
    proj = _dot(h, win_ref[:, C_Q:C_END])
    if after_u is not None:
        after_u(u)
    qn = _rms(proj[:, 0:C_CKV - C_Q], qg_ref[...]).astype(BF16)
    ckv = _rms(proj[:, C_CKV - C_Q:C_KPE - C_Q], kvg_ref[...])
    ckv_ref[...] = ckv
    cq, sq, ck, sk = tab_ref[0], tab_ref[1], tab_ref[2], tab_ref[3]
    kpe_rot = proj[:, C_KPE - C_Q:C_KPE_SW - C_Q] * ck + proj[:, C_KPE_SW - C_Q:C_END - C_Q] * sk
    kpe_ref[...] = kpe_rot[:, :QK_ROPE]
    q = _dot(qn, wq_ref[...])
    qs = _dot(qn, wqsw_ref[...])
    if after_q is not None:
        after_q()
    q_heads = []
    for hd in range(N_HEADS):
        sl = slice(hd * HEAD_PAD, (hd + 1) * HEAD_PAD)
        q_heads.append(q[:, sl] * cq + qs[:, sl] * sq)
    return ckv, kpe_rot, q_heads


def _conv_fill_window(win_ref, halo, tile):
    tt = tile.shape[0]
    win_ref[0, 0:CONV_HALO, :] = halo
    win_ref[0, CONV_HALO:CONV_HALO + tt, :] = tile
    n = tt + CONV_HALO - SUBLANES
    for s in range(1, SUBLANES):
        win_ref[s, 0:n, :] = win_ref[0, s:s + n, :]


def _conv_chunk(win_ref, c, w_ref, b_ref, g_ref, lb_ref):
    acc = None
    for j in range(CONV_WIDTH):
        q, s = divmod(c * CONV_CHUNK + 2 + j, SUBLANES)
        term = win_ref[s, q * SUBLANES:q * SUBLANES + CONV_CHUNK, :] * w_ref[j:j + 1, :]
        acc = term if acc is None else acc + term
    return _ln_swish(acc + b_ref[...], g_ref[...], lb_ref[...])


def _inproj_prompt_kernel(tiles_per_seq, x_ref, gmix_ref, win_ref, qg_ref, wq_ref, wqsw_ref, kvg_ref, tab_ref,
                          wuk_ref, wuv_ref, vone_ref, prev_ref, cw_ref, cb_ref, cg_ref, clb_ref,
                          u_ref, ckv_ref, kpe_ref, q_ref, k_ref, v_ref, conv_ref, win_scr, carry_scr):
    tm = x_ref.shape[0]
    n_chunks = tm // CONV_CHUNK
    first = pl.program_id(0) % tiles_per_seq == 0

    def conv_chunks(lo, hi):
        for c in range(lo, hi):
            y = _conv_chunk(win_scr, c, cw_ref, cb_ref, cg_ref, clb_ref)
            conv_ref[c * CONV_CHUNK:(c + 1) * CONV_CHUNK, :] = y.astype(BF16)

    def after_u(u):
        _conv_fill_window(win_scr, jnp.where(first, prev_ref[0], carry_scr[...]), u)
        carry_scr[...] = u[tm - CONV_HALO:, :]
        conv_chunks(0, n_chunks * 3 // 4)

    ckv, kpe_rot, q_heads = _inproj_common(x_ref, gmix_ref, win_ref, qg_ref, wq_ref, wqsw_ref,
                                           kvg_ref, tab_ref, u_ref, ckv_ref, kpe_ref,
                                           after_u=after_u, after_q=lambda: conv_chunks(n_chunks * 3 // 4, n_chunks))
    ckv_b = ckv.astype(BF16)
    k_nope = _dot(ckv_b, wuk_ref[...])
    for hd in range(N_HEADS):
        sl = slice(hd * HEAD_PAD, (hd + 1) * HEAD_PAD)
        q_ref[:, sl] = q_heads[hd].astype(BF16)
        k_ref[:, sl] = (k_nope[:, sl] + kpe_rot).astype(BF16)
    v_ref[...] = (_dot(ckv_b, wuv_ref[...]) + vone_ref[...]).astype(BF16)


def _inproj_sample_kernel(x_ref, gmix_ref, win_ref, qg_ref, wq_ref, wqsw_ref, kvg_ref, tab_ref,
                          wukt_ref,
                          u_ref, ckv_ref, kpe_ref, qlat_ref, qpe_ref):
    _, _, q_heads = _inproj_common(x_ref, gmix_ref, win_ref, qg_ref, wq_ref, wqsw_ref,
                                   kvg_ref, tab_ref, u_ref, ckv_ref, kpe_ref)
    for hd in range(N_HEADS):
        qlat_ref[hd] = _dot(q_heads[hd].astype(BF16), wukt_ref[hd])
        qpe_ref[hd] = q_heads[hd][:, :QK_ROPE]


def _inproj(x2d, tab, w, sample, prev_pad=None):
    t = x2d.shape[0]
    tm = TOKEN_TILE
    n_tab = tab.shape[1] // tm
    row = lambda n: pl.BlockSpec((tm, n), lambda i: (i, 0))
    in_specs = [row(D_MODEL), _const_spec((1, D_MODEL)), _const_spec((D_MODEL, C_END)),
                _const_spec((1, Q_RANK)), _const_spec((Q_RANK, N_HEADS * HEAD_PAD)),
                _const_spec((Q_RANK, N_HEADS * HEAD_PAD)), _const_spec((1, KV_RANK)),
                pl.BlockSpec((4, tm, LANES), lambda i: (0, i % n_tab, 0))]
    args = [x2d, w['g_mix'], w['w_in_ext'], w['q_norm_g'], w['wq'], w['wq_sw'], w['kv_norm_g'], tab]
    out_shape = [jax.ShapeDtypeStruct((t, CONV_CH), F32), jax.ShapeDtypeStruct((t, KV_RANK), F32),
                 jax.ShapeDtypeStruct((t, QK_ROPE), F32)]
    out_specs = [row(CONV_CH), row(KV_RANK), row(QK_ROPE)]
    scratch = []
    if sample:
        body = _inproj_sample_kernel
        in_specs += [_const_spec((N_HEADS, HEAD_PAD, KV_RANK))]
        args += [w['wukt_pad']]
        out_shape += [jax.ShapeDtypeStruct((N_HEADS, t, KV_RANK), F32),
                      jax.ShapeDtypeStruct((N_HEADS, t, QK_ROPE), F32)]
        out_specs += [pl.BlockSpec((N_HEADS, tm, KV_RANK), lambda i: (0, i, 0)),
                      pl.BlockSpec((N_HEADS, tm, QK_ROPE), lambda i: (0, i, 0))]
    else:
        tiles_per_seq = t // prev_pad.shape[0] // tm
        body = functools.partial(_inproj_prompt_kernel, tiles_per_seq)
        in_specs += [_const_spec((KV_RANK, N_HEADS * HEAD_PAD)), _const_spec((KV_RANK, N_HEADS * HEAD_PAD)),
                     _const_spec((1, N_HEADS * HEAD_PAD)),
                     pl.BlockSpec((1, CONV_HALO, CONV_CH), lambda i: (i // tiles_per_seq, 0, 0)),
                     _const_spec((CONV_HALO, CONV_CH)), _const_spec((1, CONV_CH)),
                     _const_spec((1, CONV_CH)), _const_spec((1, CONV_CH))]
        args += [w['wuk_pad'], w['wuv_slot'], w['v_ones'], prev_pad,
                 w['conv_w'], w['conv_b'], w['conv_ln_g'], w['conv_ln_b']]
        out_shape += [jax.ShapeDtypeStruct((t, N_HEADS * HEAD_PAD), BF16)] * 3
        out_shape += [jax.ShapeDtypeStruct((t, CONV_CH), BF16)]
        out_specs += [row(N_HEADS * HEAD_PAD)] * 3 + [row(CONV_CH)]
        scratch = [pltpu.VMEM((SUBLANES, tm + CONV_HALO, CONV_CH), F32), pltpu.VMEM((CONV_HALO, CONV_CH), F32)]
    return pl.pallas_call(
        body, grid=(t // tm,), in_specs=in_specs, out_specs=out_specs, out_shape=out_shape,
        scratch_shapes=scratch,
        compiler_params=_params("parallel" if sample else "arbitrary", vmem=VMEM_LIMIT),
        name="inproj_sample" if sample else "inproj_prompt")(*args)


def _ln_swish(conv, g, b):
    mu = jnp.mean(conv, axis=-1, keepdims=True)
    xc = conv - mu
    var = jnp.mean(xc * xc, axis=-1, keepdims=True)
    y = xc * lax.rsqrt(var + EPS) * g + b
    return y * jax.nn.sigmoid(y)


def _conv_sample_kernel(win_ref, w_ref, b_ref, g_ref, lb_ref, o_ref):
    t = o_ref.shape[1]
    acc = win_ref[:, 2:2 + t, :] * w_ref[0:1, :]
    for j in range(1, CONV_WIDTH):
        acc = acc + win_ref[:, 2 + j:2 + j + t, :] * w_ref[j:j + 1, :]
    y = _ln_swish(acc + b_ref[...], g_ref[...], lb_ref[...])
    o_ref[...] = y


def _conv_sample(upad, w):
    b, s_pad, _ = upad.shape
    t = s_pad - CONV_HALO
    bb = 8
    return pl.pallas_call(
        _conv_sample_kernel, grid=(b // bb,),
        in_specs=[pl.BlockSpec((bb, s_pad, CONV_CH), lambda i: (i, 0, 0)),
                  _const_spec((CONV_HALO, CONV_CH)), _const_spec((1, CONV_CH)),
                  _const_spec((1, CONV_CH)), _const_spec((1, CONV_CH))],
        out_specs=pl.BlockSpec((bb, t, CONV_CH), lambda i: (i, 0, 0)),
        out_shape=jax.ShapeDtypeStruct((b, t, CONV_CH), F32),
        compiler_params=_params("parallel"),
        name="conv_sample")(upad, w['conv_w'], w['conv_b'], w['conv_ln_g'], w['conv_ln_b'])


def _attn_prompt_kernel(q_ref, k_ref, v_ref, o_ref, s_ref, m_ref):
    tq = q_ref.shape[1]
    qi = pl.program_id(2)
    nh = ATTN_HEADS
    head = lambda hd: slice(hd * HEAD_PAD, (hd + 1) * HEAD_PAD)
    rows = lax.broadcasted_iota(jnp.int32, (tq, tq), 0)
    cols = lax.broadcasted_iota(jnp.int32, (tq, tq), 1)
    lane = lax.broadcasted_iota(jnp.int32, (tq, LANES), 1)

    def scores(j, m_all, masked):
        start = pl.multiple_of(j * tq, tq)
        for hd in range(nh):
            s = _dot_t(q_ref[0, :, head(hd)], k_ref[0, pl.ds(start, tq), head(hd)])
            if masked:
                s = jnp.where(cols <= rows, s, NEG_INF)
            s_ref[hd, j] = s
            m_all = jnp.where(lane == hd, jnp.maximum(m_all, jnp.max(s, axis=-1, keepdims=True)), m_all)
        return m_all

    m_all = jnp.full((tq, LANES), NEG_INF, F32)
    m_all = lax.fori_loop(0, qi, functools.partial(scores, masked=False), m_all)
    m_all = scores(qi, m_all, True)
    for hd in range(nh):
        m_ref[hd] = jnp.broadcast_to(m_all[:, hd:hd + 1], (tq, LANES))

    def values(j, accs):
        start = pl.multiple_of(j * tq, tq)
        out = []
        for hd in range(nh):
            m = m_ref[hd]
            p = jnp.exp(s_ref[hd, j] - jnp.concatenate([m] * (tq // LANES), axis=1)).astype(BF16)
            out.append(accs[hd] + _dot(p, v_ref[0, pl.ds(start, tq), head(hd)]))
        return tuple(out)

    accs = tuple(jnp.zeros((tq, HEAD_PAD), F32) for _ in range(nh))
    accs = lax.fori_loop(0, qi + 1, values, accs)
    for hp in range(nh // 2):
        even, odd = accs[2 * hp], accs[2 * hp + 1]
        o_even = even / even[:, V_DIM:V_DIM + 1]
        o_odd = odd / odd[:, 0:1]
        o_ref[0, :, hp * LANES:(hp + 1) * LANES] = jnp.where(lane < V_DIM, o_even, o_odd).astype(BF16)


def _attn_prompt(q, k, v):
    b, s, _ = q.shape
    tq = ATTN_TILE
    nh = ATTN_HEADS
    return pl.pallas_call(
        _attn_prompt_kernel, grid=(b, N_HEADS // nh, s // tq),
        in_specs=[pl.BlockSpec((1, tq, nh * HEAD_PAD), lambda bi, hq, qi: (bi, qi, hq)),
                  pl.BlockSpec((1, s, nh * HEAD_PAD), lambda bi, hq, qi: (bi, 0, hq)),
                  pl.BlockSpec((1, s, nh * HEAD_PAD), lambda bi, hq, qi: (bi, 0, hq))],
        out_specs=pl.BlockSpec((1, tq, nh * V_DIM), lambda bi, hq, qi: (bi, qi, hq)),
        out_shape=jax.ShapeDtypeStruct((b, s, N_HEADS * V_DIM), BF16),
        scratch_shapes=[pltpu.VMEM((nh, s // tq, tq, tq), F32), pltpu.VMEM((nh, tq, LANES), F32)],
        compiler_params=_params("parallel", "parallel", "arbitrary", vmem=VMEM_LIMIT),
        name="attn_prompt")(q, k, v)


def _attn_sample_kernel(pt_ref, ql_ref, qp_ref, cn_ref, kn_ref, ckv_hbm, kpe_hbm, o_ref,
                        ckv_buf, kpe_buf, sem):
    b = pl.program_id(0)
    n_pages = kpe_buf.shape[1]
    n_chunks = n_pages // PAGES_PER_CHUNK
    chunk = PAGES_PER_CHUNK * PAGE_SIZE
    t_new = cn_ref.shape[1]
    rows_q = N_HEADS * t_new

    def fetch(batch, slot):
        def one(p, _):
            page = pt_ref[batch * n_pages + p]
            rows = pl.ds(pl.multiple_of(p * PAGE_SIZE, PAGE_SIZE), PAGE_SIZE)
            pltpu.make_async_copy(ckv_hbm.at[page], ckv_buf.at[slot, rows], sem.at[0, slot]).start()
            pltpu.make_async_copy(kpe_hbm.at[page], kpe_buf.at[slot, p], sem.at[1, slot]).start()
            return 0
        lax.fori_loop(0, n_pages, one, 0, unroll=4)

    slot = b % 2

    @pl.when(b == 0)
    def _():
        fetch(0, 0)

    @pl.when(b + 1 < pl.num_programs(0))
    def _():
        fetch(b + 1, 1 - slot)

    pltpu.make_async_copy(ckv_buf.at[slot], ckv_buf.at[slot], sem.at[0, slot]).wait()
    pltpu.make_async_copy(kpe_buf.at[slot], kpe_buf.at[slot], sem.at[1, slot]).wait()

    ql = ql_ref[:, 0].reshape(rows_q, KV_RANK).astype(BF16)
    qp = qp_ref[:, 0].reshape(rows_q, QK_ROPE).astype(BF16)

    pad = PAGE_SIZE - t_new
    kc_new = jnp.concatenate([cn_ref[0], jnp.zeros((pad, KV_RANK), F32)], axis=0).astype(BF16)
    kp_new = jnp.concatenate([kn_ref[0], jnp.zeros((pad, QK_ROPE), F32)], axis=0).astype(BF16)
    s_new = _dot_t(ql, kc_new) + _dot_t(qp, kp_new)
    t_q = lax.broadcasted_iota(jnp.int32, s_new.shape, 0) % t_new
    cols = lax.broadcasted_iota(jnp.int32, s_new.shape, 1)
    s_new = jnp.where(cols <= t_q, s_new, NEG_INF)
    def part(s, values):
        m = jnp.max(s, axis=-1, keepdims=True)
        p = jnp.exp(s - m)
        return m, jnp.sum(p, axis=-1, keepdims=True), _dot(p.astype(BF16), values)

    def scores(c):
        kc = ckv_buf[slot, c * chunk:(c + 1) * chunk, :].astype(BF16)
        kpt = jnp.concatenate([kpe_buf[slot, c * PAGES_PER_CHUNK + i] for i in range(PAGES_PER_CHUNK)],
                              axis=1).astype(BF16)
        return _dot_t(ql, kc) + _dot(qp, kpt), kc

    parts = [part(s_new, kc_new)]
    nxt = scores(0)
    for c in range(n_chunks):
        cur = nxt
        if c + 1 < n_chunks:
            nxt = scores(c + 1)
        parts.append(part(*cur))
    m = parts[0][0]
    for mp, _, _ in parts[1:]:
        m = jnp.maximum(m, mp)
    l = jnp.zeros_like(m)
    acc = jnp.zeros((rows_q, KV_RANK), F32)
    for mp, lp, ap in parts:
        w = jnp.exp(mp - m)
        l = l + w * lp
        acc = acc + w * ap
    o_ref[:, 0] = (acc / l).reshape(N_HEADS, t_new, KV_RANK)


def _attn_sample(page_table, qlat, qpe, ckv_new, kpe_new, cache_ckv, cache_kpe_t):
    bs, n_pages = page_table.shape
    t_new = ckv_new.shape[1]
    qlat4 = qlat.reshape(N_HEADS, bs, t_new, KV_RANK)
    qpe4 = qpe.reshape(N_HEADS, bs, t_new, QK_ROPE)
    past = n_pages * PAGE_SIZE
    in_specs = [pl.BlockSpec((N_HEADS, 1, t_new, KV_RANK), lambda b, pt: (0, b, 0, 0)),
                pl.BlockSpec((N_HEADS, 1, t_new, QK_ROPE), lambda b, pt: (0, b, 0, 0)),
                pl.BlockSpec((1, t_new, KV_RANK), lambda b, pt: (b, 0, 0)),
                pl.BlockSpec((1, t_new, QK_ROPE), lambda b, pt: (b, 0, 0)),
                pl.BlockSpec(memory_space=pl.ANY), pl.BlockSpec(memory_space=pl.ANY)]
    grid_spec = pltpu.PrefetchScalarGridSpec(
        num_scalar_prefetch=1, grid=(bs,), in_specs=in_specs,
        out_specs=pl.BlockSpec((N_HEADS, 1, t_new, KV_RANK), lambda b, pt: (0, b, 0, 0)),
        scratch_shapes=[pltpu.VMEM((2, past, KV_RANK), F32),
                        pltpu.VMEM((2, n_pages, QK_ROPE, PAGE_SIZE), F32),
                        pltpu.SemaphoreType.DMA((2, 2))])
    o = pl.pallas_call(
        _attn_sample_kernel, grid_spec=grid_spec,
        out_shape=jax.ShapeDtypeStruct((N_HEADS, bs, t_new, KV_RANK), F32),
        compiler_params=_params("arbitrary", vmem=VMEM_LIMIT),
        name="attn_sample")(page_table.reshape(-1), qlat4, qpe4, ckv_new, kpe_new, cache_ckv, cache_kpe_t)
    return o.reshape(N_HEADS, bs * t_new, KV_RANK)


def _post_a_kernel(x_ref, conv_ref, olat_ref, wuv_ref, wout_ref, g_ref, wmq_ref, x1_ref, qm_ref):
    attn = None
    for hd in range(N_HEADS):
        d = _dot(olat_ref[hd].astype(BF16), wuv_ref[hd])
        attn = d if attn is None else attn + d
    mix = (_dot(conv_ref[...].astype(BF16), wout_ref[0:CONV_CH, :])
           + _dot(attn.astype(BF16), wout_ref[CONV_CH:, :]))
    x1 = x_ref[...] + mix
    x1_ref[...] = x1
    qm_ref[...] = _dot(_rms(x1, g_ref[...]).astype(BF16), wmq_ref[...])


def _post_a(x2d, conv2d, o_lat, w):
    t = x2d.shape[0]
    tm = TOKEN_TILE
    row = lambda n: pl.BlockSpec((tm, n), lambda i: (i, 0))
    mem_w = MEM_HEADS * MEM_HD
    return pl.pallas_call(
        _post_a_kernel, grid=(t // tm,),
        in_specs=[row(D_MODEL), row(CONV_CH), pl.BlockSpec((N_HEADS, tm, KV_RANK), lambda i: (0, i, 0)),
                  _const_spec((N_HEADS, KV_RANK, N_HEADS * V_DIM)),
                  _const_spec((D_MODEL, D_MODEL)), _const_spec((1, D_MODEL)), _const_spec((D_MODEL, mem_w))],
        out_specs=[row(D_MODEL), row(mem_w)],
        out_shape=[jax.ShapeDtypeStruct((t, D_MODEL), F32), jax.ShapeDtypeStruct((t, mem_w), F32)],
        compiler_params=_params("parallel", vmem=VMEM_LIMIT),
        name="post_a_sample")(x2d, conv2d, o_lat, w['wuv_pad'], w['w_out'], w['g_mem_q'], w['w_mq'])


def _mem_kv_kernel(m_ref, g_ref, wk_ref, wv_ref, k_ref, v_ref, kw_ref, vw_ref):
    tm = m_ref.shape[0]
    m = _rms(m_ref[...], g_ref[...]).astype(BF16)
    k = _dot(m, wk_ref[...])
    v = _dot(m, wv_ref[...])
    kw_ref[...] = k
    vw_ref[...] = v
    for hd in range(MEM_HEADS):
        sl = slice(hd * MEM_HD, (hd + 1) * MEM_HD)
        k_ref[pl.ds(hd, tm, stride=MEM_HEADS), :] = k[:, sl]
        v_ref[pl.ds(hd, tm, stride=MEM_HEADS), :] = v[:, sl]


def _mem_kv(mem2d, w):
    t = mem2d.shape[0]
    tm = TOKEN_TILE
    mem_w = MEM_HEADS * MEM_HD
    rows = pl.BlockSpec((tm * MEM_HEADS, MEM_HD), lambda i: (i, 0))
    wide = pl.BlockSpec((tm, mem_w), lambda i: (i, 0))
    return pl.pallas_call(
        _mem_kv_kernel, grid=(t // tm,),
        in_specs=[pl.BlockSpec((tm, D_MODEL), lambda i: (i, 0)), _const_spec((1, D_MODEL)),
                  _const_spec((D_MODEL, mem_w)), _const_spec((D_MODEL, mem_w))],
        out_specs=[rows, rows, wide, wide],
        out_shape=[jax.ShapeDtypeStruct((t * MEM_HEADS, MEM_HD), F32)] * 2
                  + [jax.ShapeDtypeStruct((t, mem_w), F32)] * 2,
        compiler_params=_params("parallel"),
        name="mem_kv")(mem2d, w['g_mem_kv'], w['w_mk'], w['w_mv'])


def _mem_attn_rows_kernel(q_ref, k_ref, v_ref, o_ref):
    bb, tq, _ = q_ref.shape
    rows = MEM_TOKENS * MEM_HEADS
    head = lambda hd: slice(hd * MEM_HD, (hd + 1) * MEM_HD)
    q_head = lax.broadcasted_iota(jnp.int32, (MEM_HEADS * tq, rows), 0) // tq
    k_head = lax.broadcasted_iota(jnp.int32, (MEM_HEADS * tq, rows), 1) % MEM_HEADS
    own = q_head == k_head
    for bi in range(bb):
        q = jnp.concatenate([q_ref[bi, :, head(hd)] for hd in range(MEM_HEADS)], axis=0).astype(BF16)
        k = k_ref[bi * rows:(bi + 1) * rows, :].astype(BF16)
        v = v_ref[bi * rows:(bi + 1) * rows, :].astype(BF16)
        s = jnp.where(own, _dot_t(q, k) * MEM_SCALE, NEG_INF)
        p = jnp.exp(s - jnp.max(s, axis=-1, keepdims=True))
        p = p / jnp.sum(p, axis=-1, keepdims=True)
        o = _dot(p.astype(BF16), v)
        for hd in range(MEM_HEADS):
            o_ref[bi, :, head(hd)] = o[hd * tq:(hd + 1) * tq, :]


def _mem_attn_rows(qm, mem_k, mem_v):
    b, s, mem_w = qm.shape
    bb = max(1, MEM_ATTN_ROWS // s)
    kv = pl.BlockSpec((bb * MEM_TOKENS * MEM_HEADS, MEM_HD), lambda bi: (bi, 0))
    return pl.pallas_call(
        _mem_attn_rows_kernel, grid=(b // bb,),
        in_specs=[pl.BlockSpec((bb, s, mem_w), lambda bi: (bi, 0, 0)), kv, kv],
        out_specs=pl.BlockSpec((bb, s, mem_w), lambda bi: (bi, 0, 0)),
        out_shape=jax.ShapeDtypeStruct((b, s, mem_w), F32),
        compiler_params=_params("parallel"),
        name="mem_attn_rows")(qm, mem_k, mem_v)


def _slab_rows(j, n):
    return pl.ds(j, n, stride=SLAB)


def _post_b_kernel(n_p, x_ref, conv_ref, attn_ref, mk_ref, mv_ref, wout_ref, gq_ref, wmq_ref,
                   x1s_ref, oms_ref, wmo_ref, g_ref, wr_ref, br_ref,
                   x2_ref, h3_ref, idx_ref, gate_ref, rank_ref, cnt_ref, carry_ref, x1_scr, om_scr):
    tm = x2_ref.shape[0]
    i = pl.program_id(0)

    @pl.when(i == 0)
    def _():
        carry_ref[...] = jnp.zeros(carry_ref.shape, F32)

    @pl.when(i < n_p)
    def _():
        mix = _dot(conv_ref[...], wout_ref[0:CONV_CH, :]) + _dot(attn_ref[...], wout_ref[CONV_CH:, :])
        x1p = x_ref[...] + mix
        x1_scr[...] = x1p
        qm = _dot(_rms(x1p, gq_ref[...]).astype(BF16), wmq_ref[...])
        for hd in range(MEM_HEADS):
            sl = slice(hd * MEM_HD, (hd + 1) * MEM_HD)
            s = _dot_t(qm[:, sl].astype(BF16), mk_ref[0, :, sl].astype(BF16)) * MEM_SCALE
            p = jnp.exp(s - jnp.max(s, axis=-1, keepdims=True))
            p = p / jnp.sum(p, axis=-1, keepdims=True)
            om_scr[:, sl] = _dot(p.astype(BF16), mv_ref[0, :, sl].astype(BF16))

    @pl.when(i >= n_p)
    def _():
        x1_scr[...] = x1s_ref[...]
        om_scr[...] = oms_ref[...]

    x1 = x1_scr[...]
    om = om_scr[...]
    x2 = x1 + _dot(om.astype(BF16), wmo_ref[...])
    x2_ref[...] = x2
    h3 = _rms(x2, g_ref[...])
    for j in range(SLAB):
        h3_ref[_slab_rows(j, tm), :] = h3[:, j * LANES:(j + 1) * LANES]
    logits = _dot(h3.astype(BF16), wr_ref[...]) + br_ref[...]
    lane = lax.broadcasted_iota(jnp.int32, logits.shape, 1)
    lane_f = lane.astype(F32)
    vals, hots = [], []
    idx_out = jnp.zeros(logits.shape, F32)
    for kk in range(TOP_K):
        mx = jnp.max(logits, axis=-1, keepdims=True)
        first = jnp.min(jnp.where(logits == mx, lane_f, float(LANES)), axis=-1, keepdims=True)
        hot = lane_f == first
        logits = jnp.where(hot, -jnp.inf, logits)
        vals.append(mx)
        hots.append(hot)
        idx_out = jnp.where(lane == kk, first, idx_out)
    exps = [jnp.exp(v - vals[0]) for v in vals]
    denom = exps[0] + exps[1] + exps[2] + exps[3]
    chosen = jnp.zeros(logits.shape, F32)
    gate_out = jnp.zeros(logits.shape, F32)
    for kk in range(TOP_K):
        chosen = chosen + hots[kk].astype(F32)
        gate_out = jnp.where(lane == kk, exps[kk] / denom, gate_out)
    r_i = lax.broadcasted_iota(jnp.int32, (tm, tm), 0)
    c_i = lax.broadcasted_iota(jnp.int32, (tm, tm), 1)
    tril = (c_i < r_i).astype(BF16)
    before = _dot(tril, chosen.astype(BF16)) + carry_ref[...]
    rank_out = jnp.zeros(logits.shape, F32)
    for kk in range(TOP_K):
        rk = jnp.sum(jnp.where(hots[kk], before, 0.0), axis=-1, keepdims=True)
        rank_out = jnp.where(lane == kk, rk, rank_out)
    carry = carry_ref[...] + jnp.sum(chosen, axis=0, keepdims=True)
    carry_ref[...] = carry
    idx_ref[...] = idx_out.astype(jnp.int32)
    gate_ref[...] = gate_out
    rank_ref[...] = rank_out.astype(jnp.int32)
    cnt_ref[...] = carry.astype(jnp.int32)


def _post_b(x_p, conv_p, attn_p, mk_wide, mv_wide, x1_s, om_s, w):
    tm = TOKEN_TILE
    n_p, n_s = x_p.shape[0] // tm, x1_s.shape[0] // tm
    tiles_per_seq = n_p // mk_wide.shape[0]
    t = (n_p + n_s) * tm
    mem_w = MEM_HEADS * MEM_HD
    row = lambda n: pl.BlockSpec((tm, n), lambda i: (i, 0))
    row_p = lambda n: pl.BlockSpec((tm, n), lambda i: (jnp.minimum(i, n_p - 1), 0))
    row_s = lambda n: pl.BlockSpec((tm, n), lambda i: (jnp.maximum(i - n_p, 0), 0))
    mem = pl.BlockSpec((1, MEM_TOKENS, mem_w), lambda i: (jnp.minimum(i, n_p - 1) // tiles_per_seq, 0, 0))
    return pl.pallas_call(
        functools.partial(_post_b_kernel, n_p), grid=(n_p + n_s,),
        in_specs=[row_p(D_MODEL), row_p(CONV_CH), row_p(N_HEADS * V_DIM), mem, mem,
                  _const_spec((D_MODEL, D_MODEL)), _const_spec((1, D_MODEL)), _const_spec((D_MODEL, mem_w)),
                  row_s(D_MODEL), row_s(mem_w),
                  _const_spec((mem_w, D_MODEL)), _const_spec((1, D_MODEL)),
                  _const_spec((D_MODEL, LANES)), _const_spec((1, LANES))],
        out_specs=[row(D_MODEL), pl.BlockSpec((tm * SLAB, LANES), lambda i: (i, 0)),
                   row(LANES), row(LANES), row(LANES), _const_spec((1, LANES))],
        out_shape=[jax.ShapeDtypeStruct((t, D_MODEL), F32), jax.ShapeDtypeStruct((t * SLAB, LANES), F32),
                   jax.ShapeDtypeStruct((t, LANES), jnp.int32), jax.ShapeDtypeStruct((t, LANES), F32),
                   jax.ShapeDtypeStruct((t, LANES), jnp.int32), jax.ShapeDtypeStruct((1, LANES), jnp.int32)],
        scratch_shapes=[pltpu.VMEM((1, LANES), F32), pltpu.VMEM((tm, D_MODEL), F32), pltpu.VMEM((tm, mem_w), F32)],
        compiler_params=_params("arbitrary", vmem=VMEM_LIMIT),
        name="post_b")(x_p, conv_p, attn_p, mk_wide, mv_wide, w['w_out'], w['g_mem_q'], w['w_mq'],
                       x1_s, om_s, w['w_mo'], w['g_ffn'], w['w_router'], w['b_router'])


def _slab(row):
    return pl.ds(pl.multiple_of(row * SLAB, SLAB), SLAB)


def _dispatch_kernel(start_ref, lo_ref, hi_ref, idx_ref, rank_ref, h_ref, xs_hbm, dest_ref, zero_ref, sem):
    i = pl.program_id(0)
    tm = h_ref.shape[0] // SLAB

    def issue(t, _):
        for k in range(TOP_K):
            r = t * TOP_K + k
            d = start_ref[idx_ref[r]] + rank_ref[r]
            dest_ref[r] = d
            pltpu.make_async_copy(h_ref.at[_slab(t)], xs_hbm.at[_slab(d)], sem).start(priority=k % 2)
        return 0

    lax.fori_loop(0, tm, issue, 0, unroll=2)
    for _ in range(TOP_K):
        pltpu.make_async_copy(h_ref, xs_hbm.at[pl.ds(0, tm * SLAB)], sem).wait()

    @pl.when(i == pl.num_programs(0) - 1)
    def _():
        zero_ref[...] = jnp.zeros(zero_ref.shape, F32)
        for e in range(N_EXPERTS):
            lo, hi = lo_ref[e], hi_ref[e]

            def zissue(r, _):
                pltpu.make_async_copy(zero_ref.at[_slab(0)], xs_hbm.at[_slab(r)], sem).start()
                return 0

            def zdrain(r, _):
                pltpu.make_async_copy(zero_ref.at[_slab(0)], xs_hbm.at[_slab(0)], sem).wait()
                return 0

            lax.fori_loop(lo, hi, zissue, 0)
            lax.fori_loop(lo, hi, zdrain, 0)

        bm = zero_ref.shape[0] // SLAB

        def tail(blk, _):
            rows = pl.ds(pl.multiple_of(blk * bm * SLAB, SLAB), bm * SLAB)
            cp = pltpu.make_async_copy(zero_ref, xs_hbm.at[rows], sem)
            cp.start()
            cp.wait()
            return 0

        lax.fori_loop(hi_ref[N_EXPERTS - 1] // bm, xs_hbm.shape[0] // (bm * SLAB), tail, 0)


def _dispatch(h3s, idx_flat, rank_flat, pad_start, pad_lo, pad_hi, n_rows):
    tm = TOKEN_TILE
    t = h3s.shape[0] // SLAB
    grid_spec = pltpu.PrefetchScalarGridSpec(
        num_scalar_prefetch=3, grid=(t // tm,),
        in_specs=[pl.BlockSpec((tm * TOP_K,), lambda i, *_: (i,), memory_space=pltpu.SMEM),
                  pl.BlockSpec((tm * TOP_K,), lambda i, *_: (i,), memory_space=pltpu.SMEM),
                  pl.BlockSpec((tm * SLAB, LANES), lambda i, *_: (i, 0))],
        out_specs=[pl.BlockSpec(memory_space=pl.ANY),
                   pl.BlockSpec((tm * TOP_K,), lambda i, *_: (i,), memory_space=pltpu.SMEM)],
        scratch_shapes=[pltpu.VMEM((MOE_ROWS * SLAB, LANES), F32), pltpu.SemaphoreType.DMA])
    return pl.pallas_call(
        _dispatch_kernel, grid_spec=grid_spec,
        out_shape=[jax.ShapeDtypeStruct((n_rows * SLAB, LANES), F32),
                   jax.ShapeDtypeStruct((t * TOP_K,), jnp.int32)],
        compiler_params=pltpu.CompilerParams(dimension_semantics=("arbitrary",), has_side_effects=True),
        name="moe_dispatch")(pad_start, pad_lo, pad_hi, idx_flat, rank_flat, h3s)


def _expert_kernel(src_ref, exp_ref, first_ref, valid_ref, next_ref, x_ref, wg_hbm, wu_hbm, wd_hbm,
                   bg_ref, bu_ref, bd_ref, y_ref, wf_ref, wb_ref, xb_ref, sem):
    b = pl.program_id(0)
    bm = xb_ref.shape[0]
    w_hbm = (wg_hbm, wu_hbm, wd_hbm)

    def fetch(e):
        for i in range(3):
            pltpu.make_async_copy(w_hbm[i].at[e], wf_ref.at[i], sem).start()

    @pl.when(b == 0)
    def _():
        fetch(exp_ref[0])

    @pl.when(first_ref[b] == 1)
    def _():
        for i in range(3):
            pltpu.make_async_copy(w_hbm[i].at[0], wf_ref.at[i], sem).wait()
        def cast(r, _):
            rows = pl.ds(pl.multiple_of(r * CAST_ROWS, CAST_ROWS), CAST_ROWS)
            for i in range(3):
                wb_ref[i, rows, :] = wf_ref[i, rows, :].astype(BF16)
            return 0

        lax.fori_loop(0, D_MODEL // CAST_ROWS, cast, 0)

        @pl.when(next_ref[b] >= 0)
        def _():
            fetch(next_ref[b])

    @pl.when(valid_ref[b] == 1)
    def _():
        for j in range(SLAB):
            xb_ref[:, j * LANES:(j + 1) * LANES] = x_ref[_slab_rows(j, bm), :].astype(BF16)
        x = xb_ref[...]
        g = _dot(x, wb_ref[0]) + bg_ref[0]
        u = _dot(x, wb_ref[1]) + bu_ref[0]
        g = jnp.minimum(g, SWIGLU_LIMIT)
        u = jnp.clip(u, -SWIGLU_LIMIT, SWIGLU_LIMIT)
        a = (u + 1.0) * (g * jax.nn.sigmoid(SWIGLU_ALPHA * g))
        y = _dot(a.astype(BF16), wb_ref[2]) + bd_ref[0]
        for j in range(SLAB):
            y_ref[_slab_rows(j, bm), :] = y[:, j * LANES:(j + 1) * LANES]

    @pl.when(valid_ref[b] == 0)
    def _():
        y_ref[...] = jnp.zeros(y_ref.shape, F32)


def _experts(xs, blk_src, blk_exp, blk_first, blk_valid, blk_next, w):
    bm = MOE_ROWS
    n_blocks = xs.shape[0] // (bm * SLAB)
    hbm = pl.BlockSpec(memory_space=pl.ANY)
    bspec = pl.BlockSpec((1, 1, D_MODEL), lambda b, src, ex, *_: (ex[b], 0, 0))
    grid_spec = pltpu.PrefetchScalarGridSpec(
        num_scalar_prefetch=5, grid=(n_blocks,),
        in_specs=[pl.BlockSpec((bm * SLAB, LANES), lambda b, src, *_: (src[b], 0)),
                  hbm, hbm, hbm, bspec, bspec, bspec],
        out_specs=pl.BlockSpec((bm * SLAB, LANES), lambda b, *_: (b, 0)),
        scratch_shapes=[pltpu.VMEM((3, D_MODEL, D_MODEL), F32), pltpu.VMEM((3, D_MODEL, D_MODEL), BF16),
                        pltpu.VMEM((bm, D_MODEL), BF16), pltpu.SemaphoreType.DMA])
    return pl.pallas_call(
        _expert_kernel, grid_spec=grid_spec,
        out_shape=jax.ShapeDtypeStruct(xs.shape, F32),
        compiler_params=_params("arbitrary", vmem=VMEM_LIMIT),
        name="moe_experts")(blk_src, blk_exp, blk_first, blk_valid, blk_next, xs,
                            w['w_gate'], w['w_up'], w['w_down'], w['b_gate'], w['b_up'], w['b_down'])


def _combine_kernel(n_p, dest_ref, dest_nx_ref, gate_ref, yb_hbm, x2_ref, g_ref,
                    yp_ref, ys_ref, rows_ref, ysum_ref, y_ref, sem):
    i = pl.program_id(0)
    tm = x2_ref.shape[0]

    def gather(d_ref, slot):
        def issue(t, _):
            for k in range(TOP_K):
                r = t * TOP_K + k
                pltpu.make_async_copy(yb_hbm.at[_slab(d_ref[r])], rows_ref.at[slot, _slab(r)],
                                      sem.at[slot]).start(priority=k % 2)
            return 0
        lax.fori_loop(0, tm, issue, 0, unroll=2)

    slot = i % 2

    @pl.when(i == 0)
    def _():
        gather(dest_ref, 0)

    @pl.when(i + 1 < pl.num_programs(0))
    def _():
        gather(dest_nx_ref, 1 - slot)

    for _ in range(TOP_K):
        pltpu.make_async_copy(yb_hbm.at[pl.ds(0, tm * SLAB)], rows_ref.at[slot, pl.ds(0, tm * SLAB)],
                              sem.at[slot]).wait()

    def token(t, _):
        acc = rows_ref[slot, _slab(t * TOP_K)] * gate_ref[t * TOP_K]
        for k in range(1, TOP_K):
            acc = acc + rows_ref[slot, _slab(t * TOP_K + k)] * gate_ref[t * TOP_K + k]
        ysum_ref[_slab(t)] = acc
        return 0

    lax.fori_loop(0, tm, token, 0, unroll=4)

    ss = jnp.zeros((tm, 1), F32)
    for j in range(SLAB):
        y = x2_ref[:, j * LANES:(j + 1) * LANES] + ysum_ref[_slab_rows(j, tm), :]
        y_ref[:, j * LANES:(j + 1) * LANES] = y
        ss = ss + jnp.sum(y * y, axis=-1, keepdims=True)
    out = y_ref[...] * lax.rsqrt(ss * (1.0 / D_MODEL) + EPS) * g_ref[...]

    @pl.when(i < n_p)
    def _():
        yp_ref[...] = out

    @pl.when(i >= n_p)
    def _():
        ys_ref[...] = out


def _combine(yb, dest_flat, gate_flat, x2, g_final, n_p):
    t = x2.shape[0]
    tm = TOKEN_TILE
    n = t // tm
    n_s = n - n_p
    cur = lambda i: (i,)
    nxt = lambda i: (jnp.minimum(i + 1, n - 1),)
    smem = lambda index_map: pl.BlockSpec((tm * TOP_K,), index_map, memory_space=pltpu.SMEM)
    return pl.pallas_call(
        functools.partial(_combine_kernel, n_p), grid=(n,),
        in_specs=[smem(cur), smem(nxt), smem(cur),
                  pl.BlockSpec(memory_space=pl.ANY),
                  pl.BlockSpec((tm, D_MODEL), lambda i: (i, 0)),
                  pl.BlockSpec((1, D_MODEL), lambda i: (0, 0))],
        out_specs=[pl.BlockSpec((tm, D_MODEL), lambda i: (jnp.minimum(i, n_p - 1), 0)),
                   pl.BlockSpec((tm, D_MODEL), lambda i: (jnp.maximum(i - n_p, 0), 0))],
        out_shape=[jax.ShapeDtypeStruct((n_p * tm, D_MODEL), F32), jax.ShapeDtypeStruct((n_s * tm, D_MODEL), F32)],
        scratch_shapes=[pltpu.VMEM((2, tm * TOP_K * SLAB, LANES), F32), pltpu.VMEM((tm * SLAB, LANES), F32),
                        pltpu.VMEM((tm, D_MODEL), F32), pltpu.SemaphoreType.DMA((2,))],
        compiler_params=_params("arbitrary", vmem=VMEM_LIMIT),
        name="moe_combine")(dest_flat, dest_flat, gate_flat, yb, x2, g_final)


def _moe_and_final(x2, h3s, idx128, gate128, rank128, cnt128, w, n_p):
    t = x2.shape[0]
    bm = MOE_ROWS
    n_assign = t * TOP_K
    n_blocks = (n_assign + N_EXPERTS * (bm - 1) + bm - 1) // bm
    counts = cnt128[0, :N_EXPERTS]
    padded = ((counts + bm - 1) // bm) * bm
    pad_end = jnp.cumsum(padded).astype(jnp.int32)
    pad_start = pad_end - padded
    idx_flat = idx128[:, :TOP_K].reshape(-1)
    rank_flat = rank128[:, :TOP_K].reshape(-1)
    blk = jnp.arange(n_blocks, dtype=jnp.int32)
    blk_valid = (blk * bm < pad_end[-1]).astype(jnp.int32)
    blk_src = jnp.minimum(blk, jnp.maximum(pad_end[-1] // bm - 1, 0))
    blk_exp = jnp.sum((pad_end[None, :] <= (blk_src * bm)[:, None]).astype(jnp.int32), axis=1)
    blk_exp = jnp.minimum(blk_exp, N_EXPERTS - 1)
    blk_first = jnp.concatenate([jnp.ones((1,), jnp.int32),
                                 (blk_exp[1:] != blk_exp[:-1]).astype(jnp.int32)])
    later_start = (blk[None, :] > blk[:, None]) & (blk_first[None, :] == 1)
    next_pos = jnp.min(jnp.where(later_start, blk[None, :], n_blocks), axis=1)
    blk_next = jnp.where(next_pos < n_blocks, blk_exp[jnp.minimum(next_pos, n_blocks - 1)], -1)
    xs, dest_flat = _dispatch(h3s, idx_flat, rank_flat, pad_start, pad_start + counts, pad_end, n_blocks * bm)
    yb = _experts(xs, blk_src, blk_exp, blk_first, blk_valid, blk_next.astype(jnp.int32), w)
    gate_flat = gate128[:, :TOP_K].reshape(-1)
    return _combine(yb, dest_flat, gate_flat, x2, w['g_final'], n_p)


def _swap_halves(wcols):
    half = QK_ROPE // 2
    return jnp.concatenate([-wcols[..., half:], wcols[..., :half]], axis=-1)


def _prep_weights(g_mix, w_in, conv_w, conv_b, conv_ln_g, conv_ln_b, q_norm_g, w_q_up, kv_norm_g, w_kv_up,
                  w_out, g_mem_q, g_mem_kv, w_mq, w_mk, w_mv, w_mo, g_ffn, w_router, b_router,
                  w_gate, b_gate, w_up, b_up, w_down, b_down, g_final):
    l = 0
    w_kpe = w_in[l][:, C_KPE:]
    zpad = jnp.zeros((D_MODEL, HEAD_PAD - QK_ROPE), F32)
    w_in_ext = jnp.concatenate([w_in[l][:, :C_KPE], w_kpe, zpad, _swap_halves(w_kpe), zpad], axis=1)
    wq3 = w_q_up[l].reshape(Q_RANK, N_HEADS, QK_NOPE + QK_ROPE)
    q_nope, q_rope = wq3[..., :QK_NOPE], wq3[..., QK_NOPE:]
    z32 = jnp.zeros((Q_RANK, N_HEADS, HEAD_PAD - QK_NOPE - QK_ROPE), F32)
    wq = jnp.concatenate([q_rope, q_nope, z32], axis=-1).reshape(Q_RANK, N_HEADS * HEAD_PAD)
    wq_sw = jnp.concatenate([_swap_halves(q_rope), jnp.zeros_like(q_nope), z32], axis=-1)
    wq_sw = wq_sw.reshape(Q_RANK, N_HEADS * HEAD_PAD)
    w_uk = w_kv_up[l][:, :, :QK_NOPE]
    w_uv = w_kv_up[l][:, :, QK_NOPE:]
    wuk_pad = jnp.concatenate([jnp.zeros((KV_RANK, N_HEADS, QK_ROPE), F32), w_uk,
                               jnp.zeros((KV_RANK, N_HEADS, HEAD_PAD - QK_NOPE - QK_ROPE), F32)], axis=-1)
    wukt_pad = jnp.transpose(wuk_pad, (1, 2, 0))
    odd_head = (jnp.arange(N_HEADS) % 2 == 1)[None, :, None]
    zv = jnp.zeros_like(w_uv)
    wuv_slot = jnp.where(odd_head, jnp.concatenate([zv, w_uv], axis=-1), jnp.concatenate([w_uv, zv], axis=-1))
    lane_id = jnp.arange(HEAD_PAD)[None, :]
    v_ones = jnp.where(odd_head[0], lane_id == 0, lane_id == V_DIM).astype(F32)
    eye = jnp.eye(N_HEADS, dtype=F32)
    wuv_pad = (w_uv.transpose(1, 0, 2)[:, :, None, :] * eye[:, None, :, None])
    conv_w_pad = jnp.concatenate([conv_w[l], jnp.zeros((CONV_HALO - CONV_WIDTH, CONV_CH), F32)], axis=0)
    w_router_pad = jnp.concatenate([w_router[l], jnp.zeros((D_MODEL, LANES - N_EXPERTS), F32)], axis=1)
    b_router_pad = jnp.concatenate([b_router[l], jnp.full((LANES - N_EXPERTS,), NEG_INF, F32)])
    return {
        'g_mix': g_mix[l][None], 'w_in_ext': w_in_ext.astype(BF16),
        'q_norm_g': q_norm_g[l][None], 'wq': wq.astype(BF16), 'wq_sw': wq_sw.astype(BF16),
        'kv_norm_g': kv_norm_g[l][None],
        'wuk_pad': wuk_pad.reshape(KV_RANK, N_HEADS * HEAD_PAD).astype(BF16),
        'wuv_slot': wuv_slot.reshape(KV_RANK, N_HEADS * HEAD_PAD).astype(BF16),
        'v_ones': v_ones.reshape(1, N_HEADS * HEAD_PAD),
        'wukt_pad': wukt_pad.astype(BF16),
        'wuv_pad': wuv_pad.reshape(N_HEADS, KV_RANK, N_HEADS * V_DIM).astype(BF16),
        'conv_w': conv_w_pad, 'conv_b': conv_b[l][None],
        'conv_ln_g': conv_ln_g[l][None], 'conv_ln_b': conv_ln_b[l][None],
        'w_out': w_out[l].astype(BF16), 'g_mem_q': g_mem_q[l][None], 'w_mq': w_mq[l].astype(BF16),
        'g_mem_kv': g_mem_kv[l][None], 'w_mk': w_mk[l].astype(BF16), 'w_mv': w_mv[l].astype(BF16),
        'w_mo': w_mo[l].astype(BF16), 'g_ffn': g_ffn[l][None],
        'w_router': w_router_pad.astype(BF16), 'b_router': b_router_pad[None],
        'w_gate': w_gate[l], 'w_up': w_up[l], 'w_down': w_down[l],
        'b_gate': b_gate[l][:, None, :], 'b_up': b_up[l][:, None, :], 'b_down': b_down[l][:, None, :],
        'g_final': g_final[None],
    }


def _rope_table(pos):
    half = QK_ROPE // 2
    inv = ROPE_THETA ** (-jnp.arange(half, dtype=F32) / half)
    ang = pos.astype(F32)[:, None] * inv[None, :]
    cos, sin = jnp.cos(ang), jnp.sin(ang)
    n = pos.shape[0]
    ones = jnp.ones((n, QK_NOPE), F32)
    z = lambda k: jnp.zeros((n, k), F32)
    cq = MLA_SCALE * jnp.concatenate([cos, cos, ones, z(HEAD_PAD - QK_NOPE - QK_ROPE)], axis=1)
    sq = MLA_SCALE * jnp.concatenate([sin, sin, z(HEAD_PAD - QK_ROPE)], axis=1)
    ck = jnp.concatenate([cos, cos, z(HEAD_PAD - QK_ROPE)], axis=1)
    sk = jnp.concatenate([sin, sin, z(HEAD_PAD - QK_ROPE)], axis=1)
    return jnp.stack([cq, sq, ck, sk])


def _front(x, conv_prev, mem_k, mem_v, pos, w, paged):
    b, s, _ = x.shape
    t = b * s
    x2d = x.reshape(t, D_MODEL)
    sample = paged is not None
    if sample:
        tab = _rope_table(jnp.tile(pos, TOKEN_TILE // s))
    else:
        tab = _rope_table(pos)
    tail = CONV_WIDTH - 1
    prev_pad = jnp.concatenate([jnp.zeros((b, CONV_HALO - tail, CONV_CH), F32), conv_prev], axis=1)
    outs = _inproj(x2d, tab, w, sample, prev_pad)
    u, ckv, kpe = outs[0], outs[1], outs[2]
    u3 = u.reshape(b, s, CONV_CH)
    if s >= tail:
        conv_tail = u3[:, s - tail:]
    else:
        conv_tail = jnp.concatenate([conv_prev[:, s:], u3], axis=1)
    ckv3 = ckv.reshape(b, s, KV_RANK)
    kpe3 = kpe.reshape(b, s, QK_ROPE)
    if sample:
        page_table, cache_ckv, cache_kpe_t = paged
        conv_out = _conv_sample(jnp.concatenate([prev_pad, u3], axis=1), w)
        attn = _attn_sample(page_table, outs[3], outs[4], ckv3, kpe3, cache_ckv, cache_kpe_t)
    else:
        conv_out = outs[6]
        q, k, v = (a.reshape(b, s, -1) for a in outs[3:6])
        attn = _attn_prompt(q, k, v).reshape(t, N_HEADS * V_DIM)
    conv2d = conv_out.reshape(t, CONV_CH)
    if not sample:
        return (x2d, conv2d, attn), conv_tail, ckv3, kpe3
    x1, qm = _post_a(x2d, conv2d, attn, w)
    om = _mem_attn_rows(qm.reshape(b, s, -1), mem_k, mem_v)
    return (x1, om.reshape(t, -1)), conv_tail, ckv3, kpe3


def kernel(x_prompt, x_sample, mem_prompt, cache_ckv, cache_kpe, page_table, cache_mem_k, cache_mem_v, state_conv, g_mix, w_in, conv_w, conv_b, conv_ln_g, conv_ln_b, q_norm_g, w_q_up, kv_norm_g, w_kv_up, w_out, g_mem_q, g_mem_kv, w_mq, w_mk, w_mv, w_mo, g_ffn, w_router, b_router, w_gate, b_gate, w_up, b_up, w_down, b_down, g_final):
    assert g_mix.shape[0] == 1, "single-layer step"
    w = _prep_weights(g_mix, w_in, conv_w, conv_b, conv_ln_g, conv_ln_b, q_norm_g, w_q_up, kv_norm_g, w_kv_up,
                      w_out, g_mem_q, g_mem_kv, w_mq, w_mk, w_mv, w_mo, g_ffn, w_router, b_router,
                      w_gate, b_gate, w_up, b_up, w_down, b_down, g_final)
    b_p, s_p, _ = x_prompt.shape
    b_s, t_s, _ = x_sample.shape
    past = page_table.shape[1] * PAGE_SIZE

    mk, mv, mk_wide, mv_wide = _mem_kv(mem_prompt.reshape(-1, D_MODEL), w)
    conv0 = jnp.zeros((b_p, CONV_WIDTH - 1, CONV_CH), F32)
    (x_p, conv_out_p, attn_p), conv_p, ckv_p, kpe_p = _front(
        x_prompt, conv0, None, None, jnp.arange(s_p, dtype=jnp.int32), w, None)
    (x1_s, om_s), conv_s, ckv_s, kpe_s = _front(
        x_sample, state_conv[0], cache_mem_k[0].reshape(-1, MEM_HD), cache_mem_v[0].reshape(-1, MEM_HD),
        past + jnp.arange(t_s, dtype=jnp.int32), w,
        (page_table, cache_ckv[0], jnp.swapaxes(cache_kpe[0], 1, 2)))

    x2, h3s, idx128, gate128, rank128, cnt128 = _post_b(
        x_p, conv_out_p, attn_p, mk_wide.reshape(b_p, MEM_TOKENS, -1), mv_wide.reshape(b_p, MEM_TOKENS, -1),
        x1_s, om_s, w)
    y_p, y_s = _moe_and_final(x2, h3s, idx128, gate128, rank128, cnt128, w, x_p.shape[0] // TOKEN_TILE)

    mem_shape = (1, b_p, MEM_TOKENS, MEM_HEADS, MEM_HD)
    return (y_p.reshape(b_p, s_p, D_MODEL), y_s.reshape(b_s, t_s, D_MODEL), ckv_p[None], kpe_p[None],
            mk.reshape(mem_shape), mv.reshape(mem_shape), conv_p[None], ckv_s[None], kpe_s[None], conv_s[None])
```

```python
import functools

import jax
import jax.numpy as jnp
from jax import lax
from jax.experimental import pallas as pl
from jax.experimental.pallas import tpu as pltpu

F32 = jnp.float32
BF16 = jnp.bfloat16

D_MODEL = 1024
PAGE_SIZE = 128
CONV_CH = 512
CONV_WIDTH = 31
N_HEADS = 8
QK_NOPE = 64
QK_ROPE = 32
V_DIM = 64
Q_RANK = 384
KV_RANK = 256
ROPE_THETA = 10000.0
MLA_SCALE = (QK_NOPE + QK_ROPE) ** -0.5
MEM_TOKENS = 256
MEM_HEADS = 4
MEM_HD = 128
MEM_SCALE = MEM_HD ** -0.5
N_EXPERTS = 32
TOP_K = 4
SWIGLU_LIMIT = 7.0
SWIGLU_ALPHA = 1.702
EPS = 1e-6
NEG_INF = -1e30

LANES = 128
SUBLANES = 8
HEAD_PAD = 128
C_VAL, C_GATE, C_Q, C_CKV, C_KPE, C_KPE_SW, C_END = 0, 512, 1024, 1408, 1664, 1792, 1920
TOKEN_TILE = 256
ATTN_TILE = 256
ATTN_HEADS = 8
CONV_HALO = 32
CONV_CHUNK = 32
MEM_ATTN_ROWS = 32
PAGES_PER_CHUNK = 32
SLAB = 8
MOE_ROWS = 256
CAST_ROWS = 32
VMEM_LIMIT = 48 * 1024 * 1024


def _rms(x, g):
    return x * lax.rsqrt(jnp.mean(x * x, axis=-1, keepdims=True) + EPS) * g


def _dot(a, b):
    return jnp.dot(a, b, preferred_element_type=F32)


def _dot_t(a, b):
    return lax.dot_general(a, b, (((1,), (1,)), ((), ())), preferred_element_type=F32)


def _params(*sem, vmem=None):
    return pltpu.CompilerParams(dimension_semantics=sem, vmem_limit_bytes=vmem)


def _const_spec(shape):
    nd = len(shape)
    return pl.BlockSpec(shape, lambda *_: (0,) * nd)


def _inproj_common(x_ref, gmix_ref, win_ref, qg_ref, wq_ref, wqsw_ref, kvg_ref, tab_ref,
                   u_ref, ckv_ref, kpe_ref, after_u=None, after_q=None):
    h = _rms(x_ref[...], gmix_ref[...]).astype(BF16)
    glu = _dot(h, win_ref[:, C_VAL:C_Q])
    u = glu[:, :CONV_CH] * jax.nn.sigmoid(glu[:, CONV_CH:])
    u_ref[...] = u
    proj = _dot(h, win_ref[:, C_Q:C_END])
    if after_u is not None:
        after_u(u)
    qn = _rms(proj[:, 0:C_CKV - C_Q], qg_ref[...]).astype(BF16)
    ckv = _rms(proj[:, C_CKV - C_Q:C_KPE - C_Q], kvg_ref[...])
    ckv_ref[...] = ckv
    cq, sq, ck, sk = tab_ref[0], tab_ref[1], tab_ref[2], tab_ref[3]
    kpe_rot = proj[:, C_KPE - C_Q:C_KPE_SW - C_Q] * ck + proj[:, C_KPE_SW - C_Q:C_END - C_Q] * sk
    kpe_ref[...] = kpe_rot[:, :QK_ROPE]
    q = _dot(qn, wq_ref[...])
    qs = _dot(qn, wqsw_ref[...])
    if after_q is not None:
        after_q()
    q_heads = []
    for hd in range(N_HEADS):
        sl = slice(hd * HEAD_PAD, (hd + 1) * HEAD_PAD)
        q_heads.append(q[:, sl] * cq + qs[:, sl] * sq)
    return ckv, kpe_rot, q_heads


def _conv_fill_window(win_ref, halo, tile):
    tt = tile.shape[0]
    win_ref[0, 0:CONV_HALO, :] = halo
    win_ref[0, CONV_HALO:CONV_HALO + tt, :] = tile
    n = tt + CONV_HALO - SUBLANES
    for s in range(1, SUBLANES):
        win_ref[s, 0:n, :] = win_ref[0, s:s + n, :]


def _conv_chunk(win_ref, c, w_ref, b_ref, g_ref, lb_ref):
    acc = None
    for j in range(CONV_WIDTH):
        q, s = divmod(c * CONV_CHUNK + 2 + j, SUBLANES)
        term = win_ref[s, q * SUBLANES:q * SUBLANES + CONV_CHUNK, :] * w_ref[j:j + 1, :]
        acc = term if acc is None else acc + term
    return _ln_swish(acc + b_ref[...], g_ref[...], lb_ref[...])


def _inproj_prompt_kernel(tiles_per_seq, x_ref, gmix_ref, win_ref, qg_ref, wq_ref, wqsw_ref, kvg_ref, tab_ref,
                          wuk_ref, wuv_ref, vone_ref, prev_ref, cw_ref, cb_ref, cg_ref, clb_ref,
                          u_ref, ckv_ref, kpe_ref, q_ref, k_ref, v_ref, conv_ref, win_scr, carry_scr):
    tm = x_ref.shape[0]
    n_chunks = tm // CONV_CHUNK
    first = pl.program_id(0) % tiles_per_seq == 0

    def conv_chunks(lo, hi):
        for c in range(lo, hi):
            y = _conv_chunk(win_scr, c, cw_ref, cb_ref, cg_ref, clb_ref)
            conv_ref[c * CONV_CHUNK:(c + 1) * CONV_CHUNK, :] = y.astype(BF16)

    def after_u(u):
        _conv_fill_window(win_scr, jnp.where(first, prev_ref[0], carry_scr[...]), u)
        carry_scr[...] = u[tm - CONV_HALO:, :]
        conv_chunks(0, n_chunks * 3 // 4)

    ckv, kpe_rot, q_heads = _inproj_common(x_ref, gmix_ref, win_ref, qg_ref, wq_ref, wqsw_ref,
                                           kvg_ref, tab_ref, u_ref, ckv_ref, kpe_ref,
                                           after_u=after_u, after_q=lambda: conv_chunks(n_chunks * 3 // 4, n_chunks))
    ckv_b = ckv.astype(BF16)
    k_nope = _dot(ckv_b, wuk_ref[...])
    for hd in range(N_HEADS):
        sl = slice(hd * HEAD_PAD, (hd + 1) * HEAD_PAD)
        q_ref[:, sl] = q_heads[hd].astype(BF16)
        k_ref[:, sl] = (k_nope[:, sl] + kpe_rot).astype(BF16)
    v_ref[...] = (_dot(ckv_b, wuv_ref[...]) + vone_ref[...]).astype(BF16)


def _inproj_sample_kernel(x_ref, gmix_ref, win_ref, qg_ref, wq_ref, wqsw_ref, kvg_ref, tab_ref,
                          wukt_ref,
                          u_ref, ckv_ref, kpe_ref, qlat_ref, qpe_ref):
    _, _, q_heads = _inproj_common(x_ref, gmix_ref, win_ref, qg_ref, wq_ref, wqsw_ref,
                                   kvg_ref, tab_ref, u_ref, ckv_ref, kpe_ref)
    for hd in range(N_HEADS):
        qlat_ref[hd] = _dot(q_heads[hd].astype(BF16), wukt_ref[hd])
        qpe_ref[hd] = q_heads[hd][:, :QK_ROPE]


def _inproj(x2d, tab, w, sample, prev_pad=None):
    t = x2d.shape[0]
    tm = TOKEN_TILE
    n_tab = tab.shape[1] // tm
    row = lambda n: pl.BlockSpec((tm, n), lambda i: (i, 0))
    in_specs = [row(D_MODEL), _const_spec((1, D_MODEL)), _const_spec((D_MODEL, C_END)),
                _const_spec((1, Q_RANK)), _const_spec((Q_RANK, N_HEADS * HEAD_PAD)),
                _const_spec((Q_RANK, N_HEADS * HEAD_PAD)), _const_spec((1, KV_RANK)),
                pl.BlockSpec((4, tm, LANES), lambda i: (0, i % n_tab, 0))]
    args = [x2d, w['g_mix'], w['w_in_ext'], w['q_norm_g'], w['wq'], w['wq_sw'], w['kv_norm_g'], tab]
    out_shape = [jax.ShapeDtypeStruct((t, CONV_CH), F32), jax.ShapeDtypeStruct((t, KV_RANK), F32),
                 jax.ShapeDtypeStruct((t, QK_ROPE), F32)]
    out_specs = [row(CONV_CH), row(KV_RANK), row(QK_ROPE)]
    scratch = []
    if sample:
        body = _inproj_sample_kernel
        in_specs += [_const_spec((N_HEADS, HEAD_PAD, KV_RANK))]
        args += [w['wukt_pad']]
        out_shape += [jax.ShapeDtypeStruct((N_HEADS, t, KV_RANK), F32),
                      jax.ShapeDtypeStruct((N_HEADS, t, QK_ROPE), F32)]
        out_specs += [pl.BlockSpec((N_HEADS, tm, KV_RANK), lambda i: (0, i, 0)),
                      pl.BlockSpec((N_HEADS, tm, QK_ROPE), lambda i: (0, i, 0))]
    else:
        tiles_per_seq = t // prev_pad.shape[0] // tm
        body = functools.partial(_inproj_prompt_kernel, tiles_per_seq)
        in_specs += [_const_spec((KV_RANK, N_HEADS * HEAD_PAD)), _const_spec((KV_RANK, N_HEADS * HEAD_PAD)),
                     _const_spec((1, N_HEADS * HEAD_PAD)),
                     pl.BlockSpec((1, CONV_HALO, CONV_CH), lambda i: (i // tiles_per_seq, 0, 0)),
                     _const_spec((CONV_HALO, CONV_CH)), _const_spec((1, CONV_CH)),
                     _const_spec((1, CONV_CH)), _const_spec((1, CONV_CH))]
        args += [w['wuk_pad'], w['wuv_slot'], w['v_ones'], prev_pad,
                 w['conv_w'], w['conv_b'], w['conv_ln_g'], w['conv_ln_b']]
        out_shape += [jax.ShapeDtypeStruct((t, N_HEADS * HEAD_PAD), BF16)] * 3
        out_shape += [jax.ShapeDtypeStruct((t, CONV_CH), BF16)]
        out_specs += [row(N_HEADS * HEAD_PAD)] * 3 + [row(CONV_CH)]
        scratch = [pltpu.VMEM((SUBLANES, tm + CONV_HALO, CONV_CH), F32), pltpu.VMEM((CONV_HALO, CONV_CH), F32)]
    return pl.pallas_call(
        body, grid=(t // tm,), in_specs=in_specs, out_specs=out_specs, out_shape=out_shape,
        scratch_shapes=scratch,
        compiler_params=_params("parallel" if sample else "arbitrary", vmem=VMEM_LIMIT),
        name="inproj_sample" if sample else "inproj_prompt")(*args)


def _ln_swish(conv, g, b):
    mu = jnp.mean(conv, axis=-1, keepdims=True)
    xc = conv - mu
    var = jnp.mean(xc * xc, axis=-1, keepdims=True)
    y = xc * lax.rsqrt(var + EPS) * g + b
    return y * jax.nn.sigmoid(y)


def _conv_sample_kernel(win_ref, w_ref, b_ref, g_ref, lb_ref, o_ref):
    t = o_ref.shape[1]
    acc = win_ref[:, 2:2 + t, :] * w_ref[0:1, :]
    for j in range(1, CONV_WIDTH):
        acc = acc + win_ref[:, 2 + j:2 + j + t, :] * w_ref[j:j + 1, :]
    y = _ln_swish(acc + b_ref[...], g_ref[...], lb_ref[...])
    o_ref[...] = y


def _conv_sample(upad, w):
    b, s_pad, _ = upad.shape
    t = s_pad - CONV_HALO
    bb = 8
    return pl.pallas_call(
        _conv_sample_kernel, grid=(b // bb,),
        in_specs=[pl.BlockSpec((bb, s_pad, CONV_CH), lambda i: (i, 0, 0)),
                  _const_spec((CONV_HALO, CONV_CH)), _const_spec((1, CONV_CH)),
                  _const_spec((1, CONV_CH)), _const_spec((1, CONV_CH))],
        out_specs=pl.BlockSpec((bb, t, CONV_CH), lambda i: (i, 0, 0)),
        out_shape=jax.ShapeDtypeStruct((b, t, CONV_CH), F32),
        compiler_params=_params("parallel"),
        name="conv_sample")(upad, w['conv_w'], w['conv_b'], w['conv_ln_g'], w['conv_ln_b'])


def _attn_prompt_kernel(q_ref, k_ref, v_ref, o_ref, s_ref, m_ref):
    tq = q_ref.shape[1]
    qi = pl.program_id(2)
    nh = ATTN_HEADS
    head = lambda hd: slice(hd * HEAD_PAD, (hd + 1) * HEAD_PAD)
    rows = lax.broadcasted_iota(jnp.int32, (tq, tq), 0)
    cols = lax.broadcasted_iota(jnp.int32, (tq, tq), 1)
    lane = lax.broadcasted_iota(jnp.int32, (tq, LANES), 1)

    def scores(j, m_all, masked):
        start = pl.multiple_of(j * tq, tq)
        for hd in range(nh):
            s = _dot_t(q_ref[0, :, head(hd)], k_ref[0, pl.ds(start, tq), head(hd)])
            if masked:
                s = jnp.where(cols <= rows, s, NEG_INF)
            s_ref[hd, j] = s
            m_all = jnp.where(lane == hd, jnp.maximum(m_all, jnp.max(s, axis=-1, keepdims=True)), m_all)
        return m_all

    m_all = jnp.full((tq, LANES), NEG_INF, F32)
    m_all = lax.fori_loop(0, qi, functools.partial(scores, masked=False), m_all)
    m_all = scores(qi, m_all, True)
    for hd in range(nh):
        m_ref[hd] = jnp.broadcast_to(m_all[:, hd:hd + 1], (tq, LANES))

    def values(j, accs):
        start = pl.multiple_of(j * tq, tq)
        out = []
        for hd in range(nh):
            m = m_ref[hd]
            p = jnp.exp(s_ref[hd, j] - jnp.concatenate([m] * (tq // LANES), axis=1)).astype(BF16)
            out.append(accs[hd] + _dot(p, v_ref[0, pl.ds(start, tq), head(hd)]))
        return tuple(out)

    accs = tuple(jnp.zeros((tq, HEAD_PAD), F32) for _ in range(nh))
    accs = lax.fori_loop(0, qi + 1, values, accs)
    for hp in range(nh // 2):
        even, odd = accs[2 * hp], accs[2 * hp + 1]
        o_even = even / even[:, V_DIM:V_DIM + 1]
        o_odd = odd / odd[:, 0:1]
        o_ref[0, :, hp * LANES:(hp + 1) * LANES] = jnp.where(lane < V_DIM, o_even, o_odd).astype(BF16)


def _attn_prompt(q, k, v):
    b, s, _ = q.shape
    tq = ATTN_TILE
    nh = ATTN_HEADS
    return pl.pallas_call(
        _attn_prompt_kernel, grid=(b, N_HEADS // nh, s // tq),
        in_specs=[pl.BlockSpec((1, tq, nh * HEAD_PAD), lambda bi, hq, qi: (bi, qi, hq)),
                  pl.BlockSpec((1, s, nh * HEAD_PAD), lambda bi, hq, qi: (bi, 0, hq)),
                  pl.BlockSpec((1, s, nh * HEAD_PAD), lambda bi, hq, qi: (bi, 0, hq))],
        out_specs=pl.BlockSpec((1, tq, nh * V_DIM), lambda bi, hq, qi: (bi, qi, hq)),
        out_shape=jax.ShapeDtypeStruct((b, s, N_HEADS * V_DIM), BF16),
        scratch_shapes=[pltpu.VMEM((nh, s // tq, tq, tq), F32), pltpu.VMEM((nh, tq, LANES), F32)],
        compiler_params=_params("parallel", "parallel", "arbitrary", vmem=VMEM_LIMIT),
        name="attn_prompt")(q, k, v)


def _attn_sample_kernel(pt_ref, ql_ref, qp_ref, cn_ref, kn_ref, ckv_hbm, kpe_hbm, o_ref,
                        ckv_buf, kpe_buf, sem):
    b = pl.program_id(0)
    n_pages = kpe_buf.shape[1]
    n_chunks = n_pages // PAGES_PER_CHUNK
    chunk = PAGES_PER_CHUNK * PAGE_SIZE
    t_new = cn_ref.shape[1]
    rows_q = N_HEADS * t_new

    def fetch(batch, slot):
        def one(p, _):
            page = pt_ref[batch * n_pages + p]
            rows = pl.ds(pl.multiple_of(p * PAGE_SIZE, PAGE_SIZE), PAGE_SIZE)
            pltpu.make_async_copy(ckv_hbm.at[page], ckv_buf.at[slot, rows], sem.at[0, slot]).start()
            pltpu.make_async_copy(kpe_hbm.at[page], kpe_buf.at[slot, p], sem.at[1, slot]).start()
            return 0
        lax.fori_loop(0, n_pages, one, 0, unroll=4)

    slot = b % 2

    @pl.when(b == 0)
    def _():
        fetch(0, 0)

    @pl.when(b + 1 < pl.num_programs(0))
    def _():
        fetch(b + 1, 1 - slot)

    pltpu.make_async_copy(ckv_buf.at[slot], ckv_buf.at[slot], sem.at[0, slot]).wait()
    pltpu.make_async_copy(kpe_buf.at[slot], kpe_buf.at[slot], sem.at[1, slot]).wait()

    ql = ql_ref[:, 0].reshape(rows_q, KV_RANK).astype(BF16)
    qp = qp_ref[:, 0].reshape(rows_q, QK_ROPE).astype(BF16)

    pad = PAGE_SIZE - t_new
    kc_new = jnp.concatenate([cn_ref[0], jnp.zeros((pad, KV_RANK), F32)], axis=0).astype(BF16)
    kp_new = jnp.concatenate([kn_ref[0], jnp.zeros((pad, QK_ROPE), F32)], axis=0).astype(BF16)
    s_new = _dot_t(ql, kc_new) + _dot_t(qp, kp_new)
    t_q = lax.broadcasted_iota(jnp.int32, s_new.shape, 0) % t_new
    cols = lax.broadcasted_iota(jnp.int32, s_new.shape, 1)
    s_new = jnp.where(cols <= t_q, s_new, NEG_INF)
    def part(s, values):
        m = jnp.max(s, axis=-1, keepdims=True)
        p = jnp.exp(s - m)
        return m, jnp.sum(p, axis=-1, keepdims=True), _dot(p.astype(BF16), values)

    def scores(c):
        kc = ckv_buf[slot, c * chunk:(c + 1) * chunk, :].astype(BF16)
        kpt = jnp.concatenate([kpe_buf[slot, c * PAGES_PER_CHUNK + i] for i in range(PAGES_PER_CHUNK)],
                              axis=1).astype(BF16)
        return _dot_t(ql, kc) + _dot(qp, kpt), kc

    parts = [part(s_new, kc_new)]
    nxt = scores(0)
    for c in range(n_chunks):
        cur = nxt
        if c + 1 < n_chunks:
            nxt = scores(c + 1)
        parts.append(part(*cur))
    m = parts[0][0]
    for mp, _, _ in parts[1:]:
        m = jnp.maximum(m, mp)
    l = jnp.zeros_like(m)
    acc = jnp.zeros((rows_q, KV_RANK), F32)
    for mp, lp, ap in parts:
        w = jnp.exp(mp - m)
        l = l + w * lp
        acc = acc + w * ap
    o_ref[:, 0] = (acc / l).reshape(N_HEADS, t_new, KV_RANK)


def _attn_sample(page_table, qlat, qpe, ckv_new, kpe_new, cache_ckv, cache_kpe_t):
    bs, n_pages = page_table.shape
    t_new = ckv_new.shape[1]
    qlat4 = qlat.reshape(N_HEADS, bs, t_new, KV_RANK)
    qpe4 = qpe.reshape(N_HEADS, bs, t_new, QK_ROPE)
    past = n_pages * PAGE_SIZE
    in_specs = [pl.BlockSpec((N_HEADS, 1, t_new, KV_RANK), lambda b, pt: (0, b, 0, 0)),
                pl.BlockSpec((N_HEADS, 1, t_new, QK_ROPE), lambda b, pt: (0, b, 0, 0)),
                pl.BlockSpec((1, t_new, KV_RANK), lambda b, pt: (b, 0, 0)),
                pl.BlockSpec((1, t_new, QK_ROPE), lambda b, pt: (b, 0, 0)),
                pl.BlockSpec(memory_space=pl.ANY), pl.BlockSpec(memory_space=pl.ANY)]
    grid_spec = pltpu.PrefetchScalarGridSpec(
        num_scalar_prefetch=1, grid=(bs,), in_specs=in_specs,
        out_specs=pl.BlockSpec((N_HEADS, 1, t_new, KV_RANK), lambda b, pt: (0, b, 0, 0)),
        scratch_shapes=[pltpu.VMEM((2, past, KV_RANK), F32),
                        pltpu.VMEM((2, n_pages, QK_ROPE, PAGE_SIZE), F32),
                        pltpu.SemaphoreType.DMA((2, 2))])
    o = pl.pallas_call(
        _attn_sample_kernel, grid_spec=grid_spec,
        out_shape=jax.ShapeDtypeStruct((N_HEADS, bs, t_new, KV_RANK), F32),
        compiler_params=_params("arbitrary", vmem=VMEM_LIMIT),
        name="attn_sample")(page_table.reshape(-1), qlat4, qpe4, ckv_new, kpe_new, cache_ckv, cache_kpe_t)
    return o.reshape(N_HEADS, bs * t_new, KV_RANK)


def _post_a_kernel(x_ref, conv_ref, olat_ref, wuv_ref, wout_ref, g_ref, wmq_ref, x1_ref, qm_ref):
    attn = None
    for hd in range(N_HEADS):
        d = _dot(olat_ref[hd].astype(BF16), wuv_ref[hd])
        attn = d if attn is None else attn + d
    mix = (_dot(conv_ref[...].astype(BF16), wout_ref[0:CONV_CH, :])
           + _dot(attn.astype(BF16), wout_ref[CONV_CH:, :]))
    x1 = x_ref[...] + mix
    x1_ref[...] = x1
    qm_ref[...] = _dot(_rms(x1, g_ref[...]).astype(BF16), wmq_ref[...])


def _post_a(x2d, conv2d, o_lat, w):
    t = x2d.shape[0]
    tm = TOKEN_TILE
    row = lambda n: pl.BlockSpec((tm, n), lambda i: (i, 0))
    mem_w = MEM_HEADS * MEM_HD
    return pl.pallas_call(
        _post_a_kernel, grid=(t // tm,),
        in_specs=[row(D_MODEL), row(CONV_CH), pl.BlockSpec((N_HEADS, tm, KV_RANK), lambda i: (0, i, 0)),
                  _const_spec((N_HEADS, KV_RANK, N_HEADS * V_DIM)),
                  _const_spec((D_MODEL, D_MODEL)), _const_spec((1, D_MODEL)), _const_spec((D_MODEL, mem_w))],
        out_specs=[row(D_MODEL), row(mem_w)],
        out_shape=[jax.ShapeDtypeStruct((t, D_MODEL), F32), jax.ShapeDtypeStruct((t, mem_w), F32)],
        compiler_params=_params("parallel", vmem=VMEM_LIMIT),
        name="post_a_sample")(x2d, conv2d, o_lat, w['wuv_pad'], w['w_out'], w['g_mem_q'], w['w_mq'])


def _mem_kv_kernel(m_ref, g_ref, wk_ref, wv_ref, k_ref, v_ref, kw_ref, vw_ref):
    tm = m_ref.shape[0]
    m = _rms(m_ref[...], g_ref[...]).astype(BF16)
    k = _dot(m, wk_ref[...])
    v = _dot(m, wv_ref[...])
    kw_ref[...] = k
    vw_ref[...] = v
    for hd in range(MEM_HEADS):
        sl = slice(hd * MEM_HD, (hd + 1) * MEM_HD)
        k_ref[pl.ds(hd, tm, stride=MEM_HEADS), :] = k[:, sl]
        v_ref[pl.ds(hd, tm, stride=MEM_HEADS), :] = v[:, sl]


def _mem_kv(mem2d, w):
    t = mem2d.shape[0]
    tm = TOKEN_TILE
    mem_w = MEM_HEADS * MEM_HD
    rows = pl.BlockSpec((tm * MEM_HEADS, MEM_HD), lambda i: (i, 0))
    wide = pl.BlockSpec((tm, mem_w), lambda i: (i, 0))
    return pl.pallas_call(
        _mem_kv_kernel, grid=(t // tm,),
        in_specs=[pl.BlockSpec((tm, D_MODEL), lambda i: (i, 0)), _const_spec((1, D_MODEL)),
                  _const_spec((D_MODEL, mem_w)), _const_spec((D_MODEL, mem_w))],
        out_specs=[rows, rows, wide, wide],
        out_shape=[jax.ShapeDtypeStruct((t * MEM_HEADS, MEM_HD), F32)] * 2
                  + [jax.ShapeDtypeStruct((t, mem_w), F32)] * 2,
        compiler_params=_params("parallel"),
        name="mem_kv")(mem2d, w['g_mem_kv'], w['w_mk'], w['w_mv'])


def _mem_attn_rows_kernel(q_ref, k_ref, v_ref, o_ref):
    bb, tq, _ = q_ref.shape
    rows = MEM_TOKENS * MEM_HEADS
    head = lambda hd: slice(hd * MEM_HD, (hd + 1) * MEM_HD)
    q_head = lax.broadcasted_iota(jnp.int32, (MEM_HEADS * tq, rows), 0) // tq
    k_head = lax.broadcasted_iota(jnp.int32, (MEM_HEADS * tq, rows), 1) % MEM_HEADS
    own = q_head == k_head
    for bi in range(bb):
        q = jnp.concatenate([q_ref[bi, :, head(hd)] for hd in range(MEM_HEADS)], axis=0).astype(BF16)
        k = k_ref[bi * rows:(bi + 1) * rows, :].astype(BF16)
        v = v_ref[bi * rows:(bi + 1) * rows, :].astype(BF16)
        s = jnp.where(own, _dot_t(q, k) * MEM_SCALE, NEG_INF)
        p = jnp.exp(s - jnp.max(s, axis=-1, keepdims=True))
        p = p / jnp.sum(p, axis=-1, keepdims=True)
        o = _dot(p.astype(BF16), v)
        for hd in range(MEM_HEADS):
            o_ref[bi, :, head(hd)] = o[hd * tq:(hd + 1) * tq, :]


def _mem_attn_rows(qm, mem_k, mem_v):
    b, s, mem_w = qm.shape
    bb = max(1, MEM_ATTN_ROWS // s)
    kv = pl.BlockSpec((bb * MEM_TOKENS * MEM_HEADS, MEM_HD), lambda bi: (bi, 0))
    return pl.pallas_call(
        _mem_attn_rows_kernel, grid=(b // bb,),
        in_specs=[pl.BlockSpec((bb, s, mem_w), lambda bi: (bi, 0, 0)), kv, kv],
        out_specs=pl.BlockSpec((bb, s, mem_w), lambda bi: (bi, 0, 0)),
        out_shape=jax.ShapeDtypeStruct((b, s, mem_w), F32),
        compiler_params=_params("parallel"),
        name="mem_attn_rows")(qm, mem_k, mem_v)


def _slab_rows(j, n):
    return pl.ds(j, n, stride=SLAB)


def _post_b_kernel(n_p, x_ref, conv_ref, attn_ref, mk_ref, mv_ref, wout_ref, gq_ref, wmq_ref,
                   x1s_ref, oms_ref, wmo_ref, g_ref, wr_ref, br_ref,
                   x2_ref, h3_ref, idx_ref, gate_ref, rank_ref, cnt_ref, carry_ref, x1_scr, om_scr):
    tm = x2_ref.shape[0]
    i = pl.program_id(0)

    @pl.when(i == 0)
    def _():
        carry_ref[...] = jnp.zeros(carry_ref.shape, F32)

    @pl.when(i < n_p)
    def _():
        mix = _dot(conv_ref[...], wout_ref[0:CONV_CH, :]) + _dot(attn_ref[...], wout_ref[CONV_CH:, :])
        x1p = x_ref[...] + mix
        x1_scr[...] = x1p
        qm = _dot(_rms(x1p, gq_ref[...]).astype(BF16), wmq_ref[...])
        for hd in range(MEM_HEADS):
            sl = slice(hd * MEM_HD, (hd + 1) * MEM_HD)
            s = _dot_t(qm[:, sl].astype(BF16), mk_ref[0, :, sl].astype(BF16)) * MEM_SCALE
            p = jnp.exp(s - jnp.max(s, axis=-1, keepdims=True))
            p = p / jnp.sum(p, axis=-1, keepdims=True)
            om_scr[:, sl] = _dot(p.astype(BF16), mv_ref[0, :, sl].astype(BF16))

    @pl.when(i >= n_p)
    def _():
        x1_scr[...] = x1s_ref[...]
        om_scr[...] = oms_ref[...]

    x1 = x1_scr[...]
    om = om_scr[...]
    x2 = x1 + _dot(om.astype(BF16), wmo_ref[...])
    x2_ref[...] = x2
    h3 = _rms(x2, g_ref[...])
    for j in range(SLAB):
        h3_ref[_slab_rows(j, tm), :] = h3[:, j * LANES:(j + 1) * LANES]
    logits = _dot(h3.astype(BF16), wr_ref[...]) + br_ref[...]
    lane = lax.broadcasted_iota(jnp.int32, logits.shape, 1)
    lane_f = lane.astype(F32)
    vals, hots = [], []
    idx_out = jnp.zeros(logits.shape, F32)
    for kk in range(TOP_K):
        mx = jnp.max(logits, axis=-1, keepdims=True)
        first = jnp.min(jnp.where(logits == mx, lane_f, float(LANES)), axis=-1, keepdims=True)
        hot = lane_f == first
        logits = jnp.where(hot, -jnp.inf, logits)
        vals.append(mx)
        hots.append(hot)
        idx_out = jnp.where(lane == kk, first, idx_out)
    exps = [jnp.exp(v - vals[0]) for v in vals]
    denom = exps[0] + exps[1] + exps[2] + exps[3]
    chosen = jnp.zeros(logits.shape, F32)
    gate_out = jnp.zeros(logits.shape, F32)
    for kk in range(TOP_K):
        chosen = chosen + hots[kk].astype(F32)
        gate_out = jnp.where(lane == kk, exps[kk] / denom, gate_out)
    r_i = lax.broadcasted_iota(jnp.int32, (tm, tm), 0)
    c_i = lax.broadcasted_iota(jnp.int32, (tm, tm), 1)
    tril = (c_i < r_i).astype(BF16)
    before = _dot(tril, chosen.astype(BF16)) + carry_ref[...]
    rank_out = jnp.zeros(logits.shape, F32)
    for kk in range(TOP_K):
        rk = jnp.sum(jnp.where(hots[kk], before, 0.0), axis=-1, keepdims=True)
        rank_out = jnp.where(lane == kk, rk, rank_out)
    carry = carry_ref[...] + jnp.sum(chosen, axis=0, keepdims=True)
    carry_ref[...] = carry
    idx_ref[...] = idx_out.astype(jnp.int32)
    gate_ref[...] = gate_out
    rank_ref[...] = rank_out.astype(jnp.int32)
    cnt_ref[...] = carry.astype(jnp.int32)


def _post_b(x_p, conv_p, attn_p, mk_wide, mv_wide, x1_s, om_s, w):
    tm = TOKEN_TILE
    n_p, n_s = x_p.shape[0] // tm, x1_s.shape[0] // tm
    tiles_per_seq = n_p // mk_wide.shape[0]
    t = (n_p + n_s) * tm
    mem_w = MEM_HEADS * MEM_HD
    row = lambda n: pl.BlockSpec((tm, n), lambda i: (i, 0))
    row_p = lambda n: pl.BlockSpec((tm, n), lambda i: (jnp.minimum(i, n_p - 1), 0))
    row_s = lambda n: pl.BlockSpec((tm, n), lambda i: (jnp.maximum(i - n_p, 0), 0))
    mem = pl.BlockSpec((1, MEM_TOKENS, mem_w), lambda i: (jnp.minimum(i, n_p - 1) // tiles_per_seq, 0, 0))
    return pl.pallas_call(
        functools.partial(_post_b_kernel, n_p), grid=(n_p + n_s,),
        in_specs=[row_p(D_MODEL), row_p(CONV_CH), row_p(N_HEADS * V_DIM), mem, mem,
                  _const_spec((D_MODEL, D_MODEL)), _const_spec((1, D_MODEL)), _const_spec((D_MODEL, mem_w)),
                  row_s(D_MODEL), row_s(mem_w),
                  _const_spec((mem_w, D_MODEL)), _const_spec((1, D_MODEL)),
                  _const_spec((D_MODEL, LANES)), _const_spec((1, LANES))],
        out_specs=[row(D_MODEL), pl.BlockSpec((tm * SLAB, LANES), lambda i: (i, 0)),
                   row(LANES), row(LANES), row(LANES), _const_spec((1, LANES))],
        out_shape=[jax.ShapeDtypeStruct((t, D_MODEL), F32), jax.ShapeDtypeStruct((t * SLAB, LANES), F32),
                   jax.ShapeDtypeStruct((t, LANES), jnp.int32), jax.ShapeDtypeStruct((t, LANES), F32),
                   jax.ShapeDtypeStruct((t, LANES), jnp.int32), jax.ShapeDtypeStruct((1, LANES), jnp.int32)],
        scratch_shapes=[pltpu.VMEM((1, LANES), F32), pltpu.VMEM((tm, D_MODEL), F32), pltpu.VMEM((tm, mem_w), F32)],
        compiler_params=_params("arbitrary", vmem=VMEM_LIMIT),
        name="post_b")(x_p, conv_p, attn_p, mk_wide, mv_wide, w['w_out'], w['g_mem_q'], w['w_mq'],
                       x1_s, om_s, w['w_mo'], w['g_ffn'], w['w_router'], w['b_router'])


def _slab(row):
    return pl.ds(pl.multiple_of(row * SLAB, SLAB), SLAB)


def _dispatch_kernel(start_ref, lo_ref, hi_ref, idx_ref, rank_ref, h_hbm, xs_hbm, dest_ref,
                     hbuf, zero_ref, sem_in, sem_out, sem_z):
    i = pl.program_id(0)
    n = pl.num_programs(0)
    rows_per_tile = hbuf.shape[1]
    tm = rows_per_tile // SLAB

    def load(tile, slot):
        rows = pl.ds(pl.multiple_of(tile * rows_per_tile, SLAB), rows_per_tile)
        return pltpu.make_async_copy(h_hbm.at[rows], hbuf.at[slot], sem_in.at[slot])

    def wait_scatter(slot):
        for _ in range(TOP_K):
            pltpu.make_async_copy(hbuf.at[0], xs_hbm.at[pl.ds(0, rows_per_tile)], sem_out.at[slot]).wait()

    @pl.when(i == 0)
    def _():
        load(0, 0).start()

    @pl.when(i + 1 < n)
    def _():
        load(i + 1, (i + 1) % 3).start()

    cur = i % 3
    load(i, cur).wait()

    def issue(t, _):
        for k in range(TOP_K):
            r = t * TOP_K + k
            d = start_ref[idx_ref[r]] + rank_ref[r]
            dest_ref[r] = d
            pltpu.make_async_copy(hbuf.at[cur, _slab(t)], xs_hbm.at[_slab(d)],
                                  sem_out.at[i % 2]).start(priority=k % 2)
        return 0

    lax.fori_loop(0, tm, issue, 0, unroll=2)

    @pl.when(i > 0)
    def _():
        wait_scatter((i - 1) % 2)

    @pl.when(i == n - 1)
    def _():
        wait_scatter(i % 2)
        zero_ref[...] = jnp.zeros(zero_ref.shape, F32)
        bm = zero_ref.shape[0] // SLAB
        for e in range(N_EXPERTS):
            lo = lo_ref[e]
            pad = hi_ref[e] - lo
            for bit in range(bm.bit_length() - 1):
                size = 1 << bit

                @pl.when((pad >> bit) & 1 == 1)
                def _():
                    first = lo + (pad & (size - 1))
                    rows = pl.ds(pl.multiple_of(first * SLAB, SLAB), size * SLAB)
                    cp = pltpu.make_async_copy(zero_ref.at[pl.ds(0, size * SLAB)], xs_hbm.at[rows], sem_z)
                    cp.start()
                    cp.wait()

        def tail(blk, _):
            rows = pl.ds(pl.multiple_of(blk * bm * SLAB, SLAB), bm * SLAB)
            cp = pltpu.make_async_copy(zero_ref, xs_hbm.at[rows], sem_z)
            cp.start()
            cp.wait()
            return 0

        lax.fori_loop(hi_ref[N_EXPERTS - 1] // bm, xs_hbm.shape[0] // (bm * SLAB), tail, 0)


def _dispatch(h3s, idx_flat, rank_flat, pad_start, pad_lo, pad_hi, n_rows):
    tm = TOKEN_TILE
    t = h3s.shape[0] // SLAB
    grid_spec = pltpu.PrefetchScalarGridSpec(
        num_scalar_prefetch=3, grid=(t // tm,),
        in_specs=[pl.BlockSpec((tm * TOP_K,), lambda i, *_: (i,), memory_space=pltpu.SMEM),
                  pl.BlockSpec((tm * TOP_K,), lambda i, *_: (i,), memory_space=pltpu.SMEM),
                  pl.BlockSpec(memory_space=pl.ANY)],
        out_specs=[pl.BlockSpec(memory_space=pl.ANY),
                   pl.BlockSpec((tm * TOP_K,), lambda i, *_: (i,), memory_space=pltpu.SMEM)],
        scratch_shapes=[pltpu.VMEM((3, tm * SLAB, LANES), F32), pltpu.VMEM((MOE_ROWS * SLAB, LANES), F32),
                        pltpu.SemaphoreType.DMA((3,)), pltpu.SemaphoreType.DMA((2,)), pltpu.SemaphoreType.DMA])
    return pl.pallas_call(
        _dispatch_kernel, grid_spec=grid_spec,
        out_shape=[jax.ShapeDtypeStruct((n_rows * SLAB, LANES), F32),
                   jax.ShapeDtypeStruct((t * TOP_K,), jnp.int32)],
        compiler_params=pltpu.CompilerParams(dimension_semantics=("arbitrary",), has_side_effects=True),
        name="moe_dispatch")(pad_start, pad_lo, pad_hi, idx_flat, rank_flat, h3s)


def _expert_kernel(src_ref, exp_ref, first_ref, valid_ref, next_ref, x_ref, wg_hbm, wu_hbm, wd_hbm,
                   bg_ref, bu_ref, bd_ref, y_ref, wf_ref, wb_ref, xb_ref, sem):
    b = pl.program_id(0)
    bm = xb_ref.shape[0]
    w_hbm = (wg_hbm, wu_hbm, wd_hbm)

    def fetch(e):
        for i in range(3):
            pltpu.make_async_copy(w_hbm[i].at[e], wf_ref.at[i], sem).start()

    @pl.when(b == 0)
    def _():
        fetch(exp_ref[0])

    @pl.when(first_ref[b] == 1)
    def _():
        for i in range(3):
            pltpu.make_async_copy(w_hbm[i].at[0], wf_ref.at[i], sem).wait()
        def cast(r, _):
            rows = pl.ds(pl.multiple_of(r * CAST_ROWS, CAST_ROWS), CAST_ROWS)
            for i in range(3):
                wb_ref[i, rows, :] = wf_ref[i, rows, :].astype(BF16)
            return 0

        lax.fori_loop(0, D_MODEL // CAST_ROWS, cast, 0)

        @pl.when(next_ref[b] >= 0)
        def _():
            fetch(next_ref[b])

    @pl.when(valid_ref[b] == 1)
    def _():
        for j in range(SLAB):
            xb_ref[:, j * LANES:(j + 1) * LANES] = x_ref[_slab_rows(j, bm), :].astype(BF16)
        x = xb_ref[...]
        g = _dot(x, wb_ref[0]) + bg_ref[0]
        u = _dot(x, wb_ref[1]) + bu_ref[0]
        g = jnp.minimum(g, SWIGLU_LIMIT)
        u = jnp.clip(u, -SWIGLU_LIMIT, SWIGLU_LIMIT)
        a = (u + 1.0) * (g * jax.nn.sigmoid(SWIGLU_ALPHA * g))
        y = _dot(a.astype(BF16), wb_ref[2]) + bd_ref[0]
        for j in range(SLAB):
            y_ref[_slab_rows(j, bm), :] = y[:, j * LANES:(j + 1) * LANES]

    @pl.when(valid_ref[b] == 0)
    def _():
        y_ref[...] = jnp.zeros(y_ref.shape, F32)


def _experts(xs, blk_src, blk_exp, blk_first, blk_valid, blk_next, w):
    bm = MOE_ROWS
    n_blocks = xs.shape[0] // (bm * SLAB)
    hbm = pl.BlockSpec(memory_space=pl.ANY)
    bspec = pl.BlockSpec((1, 1, D_MODEL), lambda b, src, ex, *_: (ex[b], 0, 0))
    grid_spec = pltpu.PrefetchScalarGridSpec(
        num_scalar_prefetch=5, grid=(n_blocks,),
        in_specs=[pl.BlockSpec((bm * SLAB, LANES), lambda b, src, *_: (src[b], 0)),
                  hbm, hbm, hbm, bspec, bspec, bspec],
        out_specs=pl.BlockSpec((bm * SLAB, LANES), lambda b, *_: (b, 0)),
        scratch_shapes=[pltpu.VMEM((3, D_MODEL, D_MODEL), F32), pltpu.VMEM((3, D_MODEL, D_MODEL), BF16),
                        pltpu.VMEM((bm, D_MODEL), BF16), pltpu.SemaphoreType.DMA])
    return pl.pallas_call(
        _expert_kernel, grid_spec=grid_spec,
        out_shape=jax.ShapeDtypeStruct(xs.shape, F32),
        compiler_params=_params("arbitrary", vmem=VMEM_LIMIT),
        name="moe_experts")(blk_src, blk_exp, blk_first, blk_valid, blk_next, xs,
                            w['w_gate'], w['w_up'], w['w_down'], w['b_gate'], w['b_up'], w['b_down'])


def _combine_kernel(n_p, dest_ref, dest_nx_ref, gate_ref, yb_hbm, x2_ref, g_ref,
                    yp_ref, ys_ref, rows_ref, ysum_ref, y_ref, sem):
    i = pl.program_id(0)
    tm = x2_ref.shape[0]

    def gather(d_ref, slot):
        def issue(t, _):
            for k in range(TOP_K):
                r = t * TOP_K + k
                pltpu.make_async_copy(yb_hbm.at[_slab(d_ref[r])], rows_ref.at[slot, _slab(r)],
                                      sem.at[slot]).start(priority=k % 2)
            return 0
        lax.fori_loop(0, tm, issue, 0, unroll=2)

    slot = i % 2

    @pl.when(i == 0)
    def _():
        gather(dest_ref, 0)

    @pl.when(i + 1 < pl.num_programs(0))
    def _():
        gather(dest_nx_ref, 1 - slot)

    for _ in range(TOP_K):
        pltpu.make_async_copy(yb_hbm.at[pl.ds(0, tm * SLAB)], rows_ref.at[slot, pl.ds(0, tm * SLAB)],
                              sem.at[slot]).wait()

    def token(t, _):
        acc = rows_ref[slot, _slab(t * TOP_K)] * gate_ref[t * TOP_K]
        for k in range(1, TOP_K):
            acc = acc + rows_ref[slot, _slab(t * TOP_K + k)] * gate_ref[t * TOP_K + k]
        ysum_ref[_slab(t)] = acc
        return 0

    lax.fori_loop(0, tm, token, 0, unroll=4)

    ss = jnp.zeros((tm, 1), F32)
    for j in range(SLAB):
        y = x2_ref[:, j * LANES:(j + 1) * LANES] + ysum_ref[_slab_rows(j, tm), :]
        y_ref[:, j * LANES:(j + 1) * LANES] = y
        ss = ss + jnp.sum(y * y, axis=-1, keepdims=True)
    out = y_ref[...] * lax.rsqrt(ss * (1.0 / D_MODEL) + EPS) * g_ref[...]

    @pl.when(i < n_p)
    def _():
        yp_ref[...] = out

    @pl.when(i >= n_p)
    def _():
        ys_ref[...] = out


def _combine(yb, dest_flat, gate_flat, x2, g_final, n_p):
    t = x2.shape[0]
    tm = TOKEN_TILE
    n = t // tm
    n_s = n - n_p
    cur = lambda i: (i,)
    nxt = lambda i: (jnp.minimum(i + 1, n - 1),)
    smem = lambda index_map: pl.BlockSpec((tm * TOP_K,), index_map, memory_space=pltpu.SMEM)
    return pl.pallas_call(
        functools.partial(_combine_kernel, n_p), grid=(n,),
        in_specs=[smem(cur), smem(nxt), smem(cur),
                  pl.BlockSpec(memory_space=pl.ANY),
                  pl.BlockSpec((tm, D_MODEL), lambda i: (i, 0)),
                  pl.BlockSpec((1, D_MODEL), lambda i: (0, 0))],
        out_specs=[pl.BlockSpec((tm, D_MODEL), lambda i: (jnp.minimum(i, n_p - 1), 0)),
                   pl.BlockSpec((tm, D_MODEL), lambda i: (jnp.maximum(i - n_p, 0), 0))],
        out_shape=[jax.ShapeDtypeStruct((n_p * tm, D_MODEL), F32), jax.ShapeDtypeStruct((n_s * tm, D_MODEL), F32)],
        scratch_shapes=[pltpu.VMEM((2, tm * TOP_K * SLAB, LANES), F32), pltpu.VMEM((tm * SLAB, LANES), F32),
                        pltpu.VMEM((tm, D_MODEL), F32), pltpu.SemaphoreType.DMA((2,))],
        compiler_params=_params("arbitrary", vmem=VMEM_LIMIT),
        name="moe_combine")(dest_flat, dest_flat, gate_flat, yb, x2, g_final)


def _moe_and_final(x2, h3s, idx128, gate128, rank128, cnt128, w, n_p):
    t = x2.shape[0]
    bm = MOE_ROWS
    n_assign = t * TOP_K
    n_blocks = (n_assign + N_EXPERTS * (bm - 1) + bm - 1) // bm
    counts = cnt128[0, :N_EXPERTS]
    padded = ((counts + bm - 1) // bm) * bm
    pad_end = jnp.cumsum(padded).astype(jnp.int32)
    pad_start = pad_end - padded
    idx_flat = idx128[:, :TOP_K].reshape(-1)
    rank_flat = rank128[:, :TOP_K].reshape(-1)
    blk = jnp.arange(n_blocks, dtype=jnp.int32)
    blk_valid = (blk * bm < pad_end[-1]).astype(jnp.int32)
    blk_src = jnp.minimum(blk, jnp.maximum(pad_end[-1] // bm - 1, 0))
    blk_exp = jnp.sum((pad_end[None, :] <= (blk_src * bm)[:, None]).astype(jnp.int32), axis=1)
    blk_exp = jnp.minimum(blk_exp, N_EXPERTS - 1)
    blk_first = jnp.concatenate([jnp.ones((1,), jnp.int32),
                                 (blk_exp[1:] != blk_exp[:-1]).astype(jnp.int32)])
    later_start = (blk[None, :] > blk[:, None]) & (blk_first[None, :] == 1)
    next_pos = jnp.min(jnp.where(later_start, blk[None, :], n_blocks), axis=1)
    blk_next = jnp.where(next_pos < n_blocks, blk_exp[jnp.minimum(next_pos, n_blocks - 1)], -1)
    xs, dest_flat = _dispatch(h3s, idx_flat, rank_flat, pad_start, pad_start + counts, pad_end, n_blocks * bm)
    yb = _experts(xs, blk_src, blk_exp, blk_first, blk_valid, blk_next.astype(jnp.int32), w)
    gate_flat = gate128[:, :TOP_K].reshape(-1)
    return _combine(yb, dest_flat, gate_flat, x2, w['g_final'], n_p)


def _swap_halves(wcols):
    half = QK_ROPE // 2
    return jnp.concatenate([-wcols[..., half:], wcols[..., :half]], axis=-1)


def _prep_weights(g_mix, w_in, conv_w, conv_b, conv_ln_g, conv_ln_b, q_norm_g, w_q_up, kv_norm_g, w_kv_up,
                  w_out, g_mem_q, g_mem_kv, w_mq, w_mk, w_mv, w_mo, g_ffn, w_router, b_router,
                  w_gate, b_gate, w_up, b_up, w_down, b_down, g_final):
    l = 0
    w_kpe = w_in[l][:, C_KPE:]
    zpad = jnp.zeros((D_MODEL, HEAD_PAD - QK_ROPE), F32)
    w_in_ext = jnp.concatenate([w_in[l][:, :C_KPE], w_kpe, zpad, _swap_halves(w_kpe), zpad], axis=1)
    wq3 = w_q_up[l].reshape(Q_RANK, N_HEADS, QK_NOPE + QK_ROPE)
    q_nope, q_rope = wq3[..., :QK_NOPE], wq3[..., QK_NOPE:]
    z32 = jnp.zeros((Q_RANK, N_HEADS, HEAD_PAD - QK_NOPE - QK_ROPE), F32)
    wq = jnp.concatenate([q_rope, q_nope, z32], axis=-1).reshape(Q_RANK, N_HEADS * HEAD_PAD)
    wq_sw = jnp.concatenate([_swap_halves(q_rope), jnp.zeros_like(q_nope), z32], axis=-1)
    wq_sw = wq_sw.reshape(Q_RANK, N_HEADS * HEAD_PAD)
    w_uk = w_kv_up[l][:, :, :QK_NOPE]
    w_uv = w_kv_up[l][:, :, QK_NOPE:]
    wuk_pad = jnp.concatenate([jnp.zeros((KV_RANK, N_HEADS, QK_ROPE), F32), w_uk,
                               jnp.zeros((KV_RANK, N_HEADS, HEAD_PAD - QK_NOPE - QK_ROPE), F32)], axis=-1)
    wukt_pad = jnp.transpose(wuk_pad, (1, 2, 0))
    odd_head = (jnp.arange(N_HEADS) % 2 == 1)[None, :, None]
    zv = jnp.zeros_like(w_uv)
    wuv_slot = jnp.where(odd_head, jnp.concatenate([zv, w_uv], axis=-1), jnp.concatenate([w_uv, zv], axis=-1))
    lane_id = jnp.arange(HEAD_PAD)[None, :]
    v_ones = jnp.where(odd_head[0], lane_id == 0, lane_id == V_DIM).astype(F32)
    eye = jnp.eye(N_HEADS, dtype=F32)
    wuv_pad = (w_uv.transpose(1, 0, 2)[:, :, None, :] * eye[:, None, :, None])
    conv_w_pad = jnp.concatenate([conv_w[l], jnp.zeros((CONV_HALO - CONV_WIDTH, CONV_CH), F32)], axis=0)
    w_router_pad = jnp.concatenate([w_router[l], jnp.zeros((D_MODEL, LANES - N_EXPERTS), F32)], axis=1)
    b_router_pad = jnp.concatenate([b_router[l], jnp.full((LANES - N_EXPERTS,), NEG_INF, F32)])
    return {
        'g_mix': g_mix[l][None], 'w_in_ext': w_in_ext.astype(BF16),
        'q_norm_g': q_norm_g[l][None], 'wq': wq.astype(BF16), 'wq_sw': wq_sw.astype(BF16),
        'kv_norm_g': kv_norm_g[l][None],
        'wuk_pad': wuk_pad.reshape(KV_RANK, N_HEADS * HEAD_PAD).astype(BF16),
        'wuv_slot': wuv_slot.reshape(KV_RANK, N_HEADS * HEAD_PAD).astype(BF16),
        'v_ones': v_ones.reshape(1, N_HEADS * HEAD_PAD),
        'wukt_pad': wukt_pad.astype(BF16),
        'wuv_pad': wuv_pad.reshape(N_HEADS, KV_RANK, N_HEADS * V_DIM).astype(BF16),
        'conv_w': conv_w_pad, 'conv_b': conv_b[l][None],
        'conv_ln_g': conv_ln_g[l][None], 'conv_ln_b': conv_ln_b[l][None],
        'w_out': w_out[l].astype(BF16), 'g_mem_q': g_mem_q[l][None], 'w_mq': w_mq[l].astype(BF16),
        'g_mem_kv': g_mem_kv[l][None], 'w_mk': w_mk[l].astype(BF16), 'w_mv': w_mv[l].astype(BF16),
        'w_mo': w_mo[l].astype(BF16), 'g_ffn': g_ffn[l][None],
        'w_router': w_router_pad.astype(BF16), 'b_router': b_router_pad[None],
        'w_gate': w_gate[l], 'w_up': w_up[l], 'w_down': w_down[l],
        'b_gate': b_gate[l][:, None, :], 'b_up': b_up[l][:, None, :], 'b_down': b_down[l][:, None, :],
        'g_final': g_final[None],
    }


def _rope_table(pos):
    half = QK_ROPE // 2
    inv = ROPE_THETA ** (-jnp.arange(half, dtype=F32) / half)
    ang = pos.astype(F32)[:, None] * inv[None, :]
    cos, sin = jnp.cos(ang), jnp.sin(ang)
    n = pos.shape[0]
    ones = jnp.ones((n, QK_NOPE), F32)
    z = lambda k: jnp.zeros((n, k), F32)
    cq = MLA_SCALE * jnp.concatenate([cos, cos, ones, z(HEAD_PAD - QK_NOPE - QK_ROPE)], axis=1)
    sq = MLA_SCALE * jnp.concatenate([sin, sin, z(HEAD_PAD - QK_ROPE)], axis=1)
    ck = jnp.concatenate([cos, cos, z(HEAD_PAD - QK_ROPE)], axis=1)
    sk = jnp.concatenate([sin, sin, z(HEAD_PAD - QK_ROPE)], axis=1)
    return jnp.stack([cq, sq, ck, sk])


def _front(x, conv_prev, mem_k, mem_v, pos, w, paged):
    b, s, _ = x.shape
    t = b * s
    x2d = x.reshape(t, D_MODEL)
    sample = paged is not None
    if sample:
        tab = _rope_table(jnp.tile(pos, TOKEN_TILE // s))
    else:
        tab = _rope_table(pos)
    tail = CONV_WIDTH - 1
    prev_pad = jnp.concatenate([jnp.zeros((b, CONV_HALO - tail, CONV_CH), F32), conv_prev], axis=1)
    outs = _inproj(x2d, tab, w, sample, prev_pad)
    u, ckv, kpe = outs[0], outs[1], outs[2]
    u3 = u.reshape(b, s, CONV_CH)
    if s >= tail:
        conv_tail = u3[:, s - tail:]
    else:
        conv_tail = jnp.concatenate([conv_prev[:, s:], u3], axis=1)
    ckv3 = ckv.reshape(b, s, KV_RANK)
    kpe3 = kpe.reshape(b, s, QK_ROPE)
    if sample:
        page_table, cache_ckv, cache_kpe_t = paged
        conv_out = _conv_sample(jnp.concatenate([prev_pad, u3], axis=1), w)
        attn = _attn_sample(page_table, outs[3], outs[4], ckv3, kpe3, cache_ckv, cache_kpe_t)
    else:
        conv_out = outs[6]
        q, k, v = (a.reshape(b, s, -1) for a in outs[3:6])
        attn = _attn_prompt(q, k, v).reshape(t, N_HEADS * V_DIM)
    conv2d = conv_out.reshape(t, CONV_CH)
    if not sample:
        return (x2d, conv2d, attn), conv_tail, ckv3, kpe3
    x1, qm = _post_a(x2d, conv2d, attn, w)
    om = _mem_attn_rows(qm.reshape(b, s, -1), mem_k, mem_v)
    return (x1, om.reshape(t, -1)), conv_tail, ckv3, kpe3


def kernel(x_prompt, x_sample, mem_prompt, cache_ckv, cache_kpe, page_table, cache_mem_k, cache_mem_v, state_conv, g_mix, w_in, conv_w, conv_b, conv_ln_g, conv_ln_b, q_norm_g, w_q_up, kv_norm_g, w_kv_up, w_out, g_mem_q, g_mem_kv, w_mq, w_mk, w_mv, w_mo, g_ffn, w_router, b_router, w_gate, b_gate, w_up, b_up, w_down, b_down, g_final):
    assert g_mix.shape[0] == 1, "single-layer step"
    w = _prep_weights(g_mix, w_in, conv_w, conv_b, conv_ln_g, conv_ln_b, q_norm_g, w_q_up, kv_norm_g, w_kv_up,
                      w_out, g_mem_q, g_mem_kv, w_mq, w_mk, w_mv, w_mo, g_ffn, w_router, b_router,
                      w_gate, b_gate, w_up, b_up, w_down, b_down, g_final)
    b_p, s_p, _ = x_prompt.shape
    b_s, t_s, _ = x_sample.shape
    past = page_table.shape[1] * PAGE_SIZE

    mk, mv, mk_wide, mv_wide = _mem_kv(mem_prompt.reshape(-1, D_MODEL), w)
    conv0 = jnp.zeros((b_p, CONV_WIDTH - 1, CONV_CH), F32)
    (x_p, conv_out_p, attn_p), conv_p, ckv_p, kpe_p = _front(
        x_prompt, conv0, None, None, jnp.arange(s_p, dtype=jnp.int32), w, None)
    (x1_s, om_s), conv_s, ckv_s, kpe_s = _front(
        x_sample, state_conv[0], cache_mem_k[0].reshape(-1, MEM_HD), cache_mem_v[0].reshape(-1, MEM_HD),
        past + jnp.arange(t_s, dtype=jnp.int32), w,
        (page_table, cache_ckv[0], jnp.swapaxes(cache_kpe[0], 1, 2)))

    x2, h3s, idx128, gate128, rank128, cnt128 = _post_b(
        x_p, conv_out_p, attn_p, mk_wide.reshape(b_p, MEM_TOKENS, -1), mv_wide.reshape(b_p, MEM_TOKENS, -1),
        x1_s, om_s, w)
    y_p, y_s = _moe_and_final(x2, h3s, idx128, gate128, rank128, cnt128, w, x_p.shape[0] // TOKEN_TILE)

    mem_shape = (1, b_p, MEM_TOKENS, MEM_HEADS, MEM_HD)
    return (y_p.reshape(b_p, s_p, D_MODEL), y_s.reshape(b_s, t_s, D_MODEL), ckv_p[None], kpe_p[None],
            mk.reshape(mem_shape), mv.reshape(mem_shape), conv_p[None], ckv_s[None], kpe_s[None], conv_s[None])
```

```python
import functools

import jax
import jax.numpy as jnp
from jax import lax
from jax.experimental import pallas as pl
from jax.experimental.pallas import tpu as pltpu

F32 = jnp.float32
BF16 = jnp.bfloat16

D_MODEL = 1024
PAGE_SIZE = 128
CONV_CH = 512
CONV_WIDTH = 31
N_HEADS = 8
QK_NOPE = 64
QK_ROPE = 32
V_DIM = 64
Q_RANK = 384
KV_RANK = 256
ROPE_THETA = 10000.0
MLA_SCALE = (QK_NOPE + QK_ROPE) ** -0.5
LOG2_E = 1.4426950408889634
MEM_TOKENS = 256
MEM_HEADS = 4
MEM_HD = 128
MEM_SCALE = MEM_HD ** -0.5
N_EXPERTS = 32
TOP_K = 4
SWIGLU_LIMIT = 7.0
SWIGLU_ALPHA = 1.702
EPS = 1e-6
NEG_INF = -1e30

LANES = 128
SUBLANES = 8
HEAD_PAD = 128
C_VAL, C_GATE, C_Q, C_CKV, C_KPE, C_KPE_SW, C_END = 0, 512, 1024, 1408, 1664, 1792, 1920
TOKEN_TILE = 256
ATTN_TILE = 256
ATTN_HEADS = 8
CONV_HALO = 32
CONV_CHUNK = 32
MEM_ATTN_ROWS = 32
PAGES_PER_CHUNK = 32
SLAB = 8
MOE_ROWS = 512
CAST_ROWS = 32
VMEM_LIMIT = 48 * 1024 * 1024


def _rms(x, g):
    return x * lax.rsqrt(jnp.mean(x * x, axis=-1, keepdims=True) + EPS) * g


def _dot(a, b):
    return jnp.dot(a, b, preferred_element_type=F32)


def _dot_t(a, b):
    return lax.dot_general(a, b, (((1,), (1,)), ((), ())), preferred_element_type=F32)


def _params(*sem, vmem=None):
    return pltpu.CompilerParams(dimension_semantics=sem, vmem_limit_bytes=vmem)


def _const_spec(shape):
    nd = len(shape)
    return pl.BlockSpec(shape, lambda *_: (0,) * nd)


def _inproj_common(x_ref, gmix_ref, win_ref, qg_ref, wq_ref, wqsw_ref, kvg_ref, tab_ref,
                   u_ref, ckv_ref, kpe_ref, after_u=None, after_q=None):
    h = _rms(x_ref[...], gmix_ref[...]).astype(BF16)
    glu = _dot(h, win_ref[:, C_VAL:C_Q])
    u = glu[:, :CONV_CH] * jax.nn.sigmoid(glu[:, CONV_CH:])
    u_ref[...] = u
    proj = _dot(h, win_ref[:, C_Q:C_END])
    if after_u is not None:
        after_u(u)
    qn = _rms(proj[:, 0:C_CKV - C_Q], qg_ref[...]).astype(BF16)
    ckv = _rms(proj[:, C_CKV - C_Q:C_KPE - C_Q], kvg_ref[...])
    ckv_ref[...] = ckv
    cq, sq, ck, sk = tab_ref[0], tab_ref[1], tab_ref[2], tab_ref[3]
    kpe_rot = proj[:, C_KPE - C_Q:C_KPE_SW - C_Q] * ck + proj[:, C_KPE_SW - C_Q:C_END - C_Q] * sk
    kpe_ref[...] = kpe_rot[:, :QK_ROPE]
    q = _dot(qn, wq_ref[...])
    qs = _dot(qn, wqsw_ref[...])
    if after_q is not None:
        after_q()
    q_heads = []
    for hd in range(N_HEADS):
        sl = slice(hd * HEAD_PAD, (hd + 1) * HEAD_PAD)
        q_heads.append(q[:, sl] * cq + qs[:, sl] * sq)
    return ckv, kpe_rot, q_heads


def _conv_fill_window(win_ref, halo, tile):
    tt = tile.shape[0]
    win_ref[0, 0:CONV_HALO, :] = halo
    win_ref[0, CONV_HALO:CONV_HALO + tt, :] = tile
    n = tt + CONV_HALO - SUBLANES
    for s in range(1, SUBLANES):
        win_ref[s, 0:n, :] = win_ref[0, s:s + n, :]


def _conv_chunk(win_ref, c, w_ref, b_ref, g_ref, lb_ref):
    acc = None
    for j in range(CONV_WIDTH):
        q, s = divmod(c * CONV_CHUNK + 2 + j, SUBLANES)
        term = win_ref[s, q * SUBLANES:q * SUBLANES + CONV_CHUNK, :] * w_ref[j:j + 1, :]
        acc = term if acc is None else acc + term
    return _ln_swish(acc + b_ref[...], g_ref[...], lb_ref[...])


def _inproj_prompt_kernel(tiles_per_seq, x_ref, gmix_ref, win_ref, qg_ref, wq_ref, wqsw_ref, kvg_ref, tab_ref,
                          wuk_ref, wuv_ref, vone_ref, prev_ref, cw_ref, cb_ref, cg_ref, clb_ref,
                          u_ref, ckv_ref, kpe_ref, q_ref, k_ref, v_ref, conv_ref, win_scr, carry_scr):
    tm = x_ref.shape[0]
    n_chunks = tm // CONV_CHUNK
    first = pl.program_id(0) % tiles_per_seq == 0

    def conv_chunks(lo, hi):
        for c in range(lo, hi):
            y = _conv_chunk(win_scr, c, cw_ref, cb_ref, cg_ref, clb_ref)
            conv_ref[c * CONV_CHUNK:(c + 1) * CONV_CHUNK, :] = y.astype(BF16)

    def after_u(u):
        _conv_fill_window(win_scr, jnp.where(first, prev_ref[0], carry_scr[...]), u)
        carry_scr[...] = u[tm - CONV_HALO:, :]
        conv_chunks(0, n_chunks * 3 // 4)

    ckv, kpe_rot, q_heads = _inproj_common(x_ref, gmix_ref, win_ref, qg_ref, wq_ref, wqsw_ref,
                                           kvg_ref, tab_ref, u_ref, ckv_ref, kpe_ref,
                                           after_u=after_u, after_q=lambda: conv_chunks(n_chunks * 3 // 4, n_chunks))
    ckv_b = ckv.astype(BF16)
    k_nope = _dot(ckv_b, wuk_ref[...])
    for hd in range(N_HEADS):
        sl = slice(hd * HEAD_PAD, (hd + 1) * HEAD_PAD)
        q_ref[:, sl] = q_heads[hd].astype(BF16)
        k_ref[:, sl] = (k_nope[:, sl] + kpe_rot).astype(BF16)
    v_ref[...] = (_dot(ckv_b, wuv_ref[...]) + vone_ref[...]).astype(BF16)


def _inproj_sample_kernel(x_ref, gmix_ref, win_ref, qg_ref, wq_ref, wqsw_ref, kvg_ref, tab_ref,
                          wukt_ref,
                          u_ref, ckv_ref, kpe_ref, qlat_ref, qpe_ref):
    _, _, q_heads = _inproj_common(x_ref, gmix_ref, win_ref, qg_ref, wq_ref, wqsw_ref,
                                   kvg_ref, tab_ref, u_ref, ckv_ref, kpe_ref)
    for hd in range(N_HEADS):
        qlat_ref[hd] = _dot(q_heads[hd].astype(BF16), wukt_ref[hd])
        qpe_ref[hd] = q_heads[hd][:, :QK_ROPE]


def _inproj(x2d, tab, w, sample, prev_pad=None):
    t = x2d.shape[0]
    tm = TOKEN_TILE
    n_tab = tab.shape[1] // tm
    row = lambda n: pl.BlockSpec((tm, n), lambda i: (i, 0))
    in_specs = [row(D_MODEL), _const_spec((1, D_MODEL)), _const_spec((D_MODEL, C_END)),
                _const_spec((1, Q_RANK)), _const_spec((Q_RANK, N_HEADS * HEAD_PAD)),
                _const_spec((Q_RANK, N_HEADS * HEAD_PAD)), _const_spec((1, KV_RANK)),
                pl.BlockSpec((4, tm, LANES), lambda i: (0, i % n_tab, 0))]
    args = [x2d, w['g_mix'], w['w_in_ext'], w['q_norm_g'], w['wq'], w['wq_sw'], w['kv_norm_g'], tab]
    out_shape = [jax.ShapeDtypeStruct((t, CONV_CH), F32), jax.ShapeDtypeStruct((t, KV_RANK), F32),
                 jax.ShapeDtypeStruct((t, QK_ROPE), F32)]
    out_specs = [row(CONV_CH), row(KV_RANK), row(QK_ROPE)]
    scratch = []
    if sample:
        body = _inproj_sample_kernel
        in_specs += [_const_spec((N_HEADS, HEAD_PAD, KV_RANK))]
        args += [w['wukt_pad']]
        out_shape += [jax.ShapeDtypeStruct((N_HEADS, t, KV_RANK), F32),
                      jax.ShapeDtypeStruct((N_HEADS, t, QK_ROPE), F32)]
        out_specs += [pl.BlockSpec((N_HEADS, tm, KV_RANK), lambda i: (0, i, 0)),
                      pl.BlockSpec((N_HEADS, tm, QK_ROPE), lambda i: (0, i, 0))]
    else:
        tiles_per_seq = t // prev_pad.shape[0] // tm
        body = functools.partial(_inproj_prompt_kernel, tiles_per_seq)
        in_specs += [_const_spec((KV_RANK, N_HEADS * HEAD_PAD)), _const_spec((KV_RANK, N_HEADS * HEAD_PAD)),
                     _const_spec((1, N_HEADS * HEAD_PAD)),
                     pl.BlockSpec((1, CONV_HALO, CONV_CH), lambda i: (i // tiles_per_seq, 0, 0)),
                     _const_spec((CONV_HALO, CONV_CH)), _const_spec((1, CONV_CH)),
                     _const_spec((1, CONV_CH)), _const_spec((1, CONV_CH))]
        args += [w['wuk_pad'], w['wuv_slot'], w['v_ones'], prev_pad,
                 w['conv_w'], w['conv_b'], w['conv_ln_g'], w['conv_ln_b']]
        out_shape += [jax.ShapeDtypeStruct((t, N_HEADS * HEAD_PAD), BF16)] * 3
        out_shape += [jax.ShapeDtypeStruct((t, CONV_CH), BF16)]
        out_specs += [row(N_HEADS * HEAD_PAD)] * 3 + [row(CONV_CH)]
        scratch = [pltpu.VMEM((SUBLANES, tm + CONV_HALO, CONV_CH), F32), pltpu.VMEM((CONV_HALO, CONV_CH), F32)]
    return pl.pallas_call(
        body, grid=(t // tm,), in_specs=in_specs, out_specs=out_specs, out_shape=out_shape,
        scratch_shapes=scratch,
        compiler_params=_params("parallel" if sample else "arbitrary", vmem=VMEM_LIMIT),
        name="inproj_sample" if sample else "inproj_prompt")(*args)


def _ln_swish(conv, g, b):
    mu = jnp.mean(conv, axis=-1, keepdims=True)
    xc = conv - mu
    var = jnp.mean(xc * xc, axis=-1, keepdims=True)
    y = xc * lax.rsqrt(var + EPS) * g + b
    return y * jax.nn.sigmoid(y)


def _conv_sample_kernel(win_ref, w_ref, b_ref, g_ref, lb_ref, o_ref):
    t = o_ref.shape[1]
    acc = win_ref[:, 2:2 + t, :] * w_ref[0:1, :]
    for j in range(1, CONV_WIDTH):
        acc = acc + win_ref[:, 2 + j:2 + j + t, :] * w_ref[j:j + 1, :]
    y = _ln_swish(acc + b_ref[...], g_ref[...], lb_ref[...])
    o_ref[...] = y


def _conv_sample(upad, w):
    b, s_pad, _ = upad.shape
    t = s_pad - CONV_HALO
    bb = 8
    return pl.pallas_call(
        _conv_sample_kernel, grid=(b // bb,),
        in_specs=[pl.BlockSpec((bb, s_pad, CONV_CH), lambda i: (i, 0, 0)),
                  _const_spec((CONV_HALO, CONV_CH)), _const_spec((1, CONV_CH)),
                  _const_spec((1, CONV_CH)), _const_spec((1, CONV_CH))],
        out_specs=pl.BlockSpec((bb, t, CONV_CH), lambda i: (i, 0, 0)),
        out_shape=jax.ShapeDtypeStruct((b, t, CONV_CH), F32),
        compiler_params=_params("parallel"),
        name="conv_sample")(upad, w['conv_w'], w['conv_b'], w['conv_ln_g'], w['conv_ln_b'])


def _attn_prompt_kernel(q_ref, k_ref, v_ref, o_ref, s_ref, m_ref):
    tq = q_ref.shape[1]
    qi = pl.program_id(2)
    nh = ATTN_HEADS
    head = lambda hd: slice(hd * HEAD_PAD, (hd + 1) * HEAD_PAD)
    rows = lax.broadcasted_iota(jnp.int32, (tq, tq), 0)
    cols = lax.broadcasted_iota(jnp.int32, (tq, tq), 1)
    lane = lax.broadcasted_iota(jnp.int32, (tq, LANES), 1)

    def scores(j, m_all, masked):
        start = pl.multiple_of(j * tq, tq)
        for hd in range(nh):
            s = _dot_t(q_ref[0, :, head(hd)], k_ref[0, pl.ds(start, tq), head(hd)])
            if masked:
                s = jnp.where(cols <= rows, s, NEG_INF)
            s_ref[hd, j] = s
            m_all = jnp.where(lane == hd, jnp.maximum(m_all, jnp.max(s, axis=-1, keepdims=True)), m_all)
        return m_all

    m_all = jnp.full((tq, LANES), NEG_INF, F32)
    m_all = lax.fori_loop(0, qi, functools.partial(scores, masked=False), m_all)
    m_all = scores(qi, m_all, True)
    for hd in range(nh):
        m_ref[hd] = jnp.broadcast_to(m_all[:, hd:hd + 1], (tq, LANES))

    def values(j, accs):
        start = pl.multiple_of(j * tq, tq)
        out = []
        for hd in range(nh):
            m = m_ref[hd]
            p = jnp.exp2(s_ref[hd, j] - jnp.concatenate([m] * (tq // LANES), axis=1)).astype(BF16)
            out.append(accs[hd] + _dot(p, v_ref[0, pl.ds(start, tq), head(hd)]))
        return tuple(out)

    accs = tuple(jnp.zeros((tq, HEAD_PAD), F32) for _ in range(nh))
    accs = lax.fori_loop(0, qi + 1, values, accs)
    for hp in range(nh // 2):
        even, odd = accs[2 * hp], accs[2 * hp + 1]
        o_even = even / even[:, V_DIM:V_DIM + 1]
        o_odd = odd / odd[:, 0:1]
        o_ref[0, :, hp * LANES:(hp + 1) * LANES] = jnp.where(lane < V_DIM, o_even, o_odd).astype(BF16)


def _attn_prompt(q, k, v):
    b, s, _ = q.shape
    tq = ATTN_TILE
    nh = ATTN_HEADS
    return pl.pallas_call(
        _attn_prompt_kernel, grid=(b, N_HEADS // nh, s // tq),
        in_specs=[pl.BlockSpec((1, tq, nh * HEAD_PAD), lambda bi, hq, qi: (bi, qi, hq)),
                  pl.BlockSpec((1, s, nh * HEAD_PAD), lambda bi, hq, qi: (bi, 0, hq)),
                  pl.BlockSpec((1, s, nh * HEAD_PAD), lambda bi, hq, qi: (bi, 0, hq))],
        out_specs=pl.BlockSpec((1, tq, nh * V_DIM), lambda bi, hq, qi: (bi, qi, hq)),
        out_shape=jax.ShapeDtypeStruct((b, s, N_HEADS * V_DIM), BF16),
        scratch_shapes=[pltpu.VMEM((nh, s // tq, tq, tq), F32), pltpu.VMEM((nh, tq, LANES), F32)],
        compiler_params=_params("parallel", "parallel", "arbitrary", vmem=VMEM_LIMIT),
        name="attn_prompt")(q, k, v)


def _attn_sample_kernel(pt_ref, ql_ref, qp_ref, cn_ref, kn_ref, ckv_hbm, kpe_hbm, o_ref,
                        ckv_buf, kpe_buf, sem):
    b = pl.program_id(0)
    n_pages = kpe_buf.shape[1]
    n_chunks = n_pages // PAGES_PER_CHUNK
    chunk = PAGES_PER_CHUNK * PAGE_SIZE
    t_new = cn_ref.shape[1]
    rows_q = N_HEADS * t_new

    def fetch(batch, slot):
        def one(p, _):
            page = pt_ref[batch * n_pages + p]
            rows = pl.ds(pl.multiple_of(p * PAGE_SIZE, PAGE_SIZE), PAGE_SIZE)
            pltpu.make_async_copy(ckv_hbm.at[page], ckv_buf.at[slot, rows], sem.at[0, slot]).start()
            pltpu.make_async_copy(kpe_hbm.at[page], kpe_buf.at[slot, p], sem.at[1, slot]).start()
            return 0
        lax.fori_loop(0, n_pages, one, 0, unroll=4)

    slot = b % 2

    @pl.when(b == 0)
    def _():
        fetch(0, 0)

    @pl.when(b + 1 < pl.num_programs(0))
    def _():
        fetch(b + 1, 1 - slot)

    pltpu.make_async_copy(ckv_buf.at[slot], ckv_buf.at[slot], sem.at[0, slot]).wait()
    pltpu.make_async_copy(kpe_buf.at[slot], kpe_buf.at[slot], sem.at[1, slot]).wait()

    ql = ql_ref[:, 0].reshape(rows_q, KV_RANK).astype(BF16)
    qp = qp_ref[:, 0].reshape(rows_q, QK_ROPE).astype(BF16)

    pad = PAGE_SIZE - t_new
    kc_new = jnp.concatenate([cn_ref[0], jnp.zeros((pad, KV_RANK), F32)], axis=0).astype(BF16)
    kp_new = jnp.concatenate([kn_ref[0], jnp.zeros((pad, QK_ROPE), F32)], axis=0).astype(BF16)
    s_new = _dot_t(ql, kc_new) + _dot_t(qp, kp_new)
    t_q = lax.broadcasted_iota(jnp.int32, s_new.shape, 0) % t_new
    cols = lax.broadcasted_iota(jnp.int32, s_new.shape, 1)
    s_new = jnp.where(cols <= t_q, s_new, NEG_INF)
    def part(s, values):
        m = jnp.max(s, axis=-1, keepdims=True)
        p = jnp.exp(s - m)
        return m, jnp.sum(p, axis=-1, keepdims=True), _dot(p.astype(BF16), values)

    def scores(c):
        kc = ckv_buf[slot, c * chunk:(c + 1) * chunk, :].astype(BF16)
        kpt = jnp.concatenate([kpe_buf[slot, c * PAGES_PER_CHUNK + i] for i in range(PAGES_PER_CHUNK)],
                              axis=1).astype(BF16)
        return _dot_t(ql, kc) + _dot(qp, kpt), kc

    parts = [part(s_new, kc_new)]
    nxt = scores(0)
    for c in range(n_chunks):
        cur = nxt
        if c + 1 < n_chunks:
            nxt = scores(c + 1)
        parts.append(part(*cur))
    m = parts[0][0]
    for mp, _, _ in parts[1:]:
        m = jnp.maximum(m, mp)
    l = jnp.zeros_like(m)
    acc = jnp.zeros((rows_q, KV_RANK), F32)
    for mp, lp, ap in parts:
        w = jnp.exp(mp - m)
        l = l + w * lp
        acc = acc + w * ap
    o_ref[:, 0] = (acc / l).reshape(N_HEADS, t_new, KV_RANK)


def _attn_sample(page_table, qlat, qpe, ckv_new, kpe_new, cache_ckv, cache_kpe_t):
    bs, n_pages = page_table.shape
    t_new = ckv_new.shape[1]
    qlat4 = qlat.reshape(N_HEADS, bs, t_new, KV_RANK)
    qpe4 = qpe.reshape(N_HEADS, bs, t_new, QK_ROPE)
    past = n_pages * PAGE_SIZE
    in_specs = [pl.BlockSpec((N_HEADS, 1, t_new, KV_RANK), lambda b, pt: (0, b, 0, 0)),
                pl.BlockSpec((N_HEADS, 1, t_new, QK_ROPE), lambda b, pt: (0, b, 0, 0)),
                pl.BlockSpec((1, t_new, KV_RANK), lambda b, pt: (b, 0, 0)),
                pl.BlockSpec((1, t_new, QK_ROPE), lambda b, pt: (b, 0, 0)),
                pl.BlockSpec(memory_space=pl.ANY), pl.BlockSpec(memory_space=pl.ANY)]
    grid_spec = pltpu.PrefetchScalarGridSpec(
        num_scalar_prefetch=1, grid=(bs,), in_specs=in_specs,
        out_specs=pl.BlockSpec((N_HEADS, 1, t_new, KV_RANK), lambda b, pt: (0, b, 0, 0)),
        scratch_shapes=[pltpu.VMEM((2, past, KV_RANK), F32),
                        pltpu.VMEM((2, n_pages, QK_ROPE, PAGE_SIZE), F32),
                        pltpu.SemaphoreType.DMA((2, 2))])
    o = pl.pallas_call(
        _attn_sample_kernel, grid_spec=grid_spec,
        out_shape=jax.ShapeDtypeStruct((N_HEADS, bs, t_new, KV_RANK), F32),
        compiler_params=_params("arbitrary", vmem=VMEM_LIMIT),
        name="attn_sample")(page_table.reshape(-1), qlat4, qpe4, ckv_new, kpe_new, cache_ckv, cache_kpe_t)
    return o.reshape(N_HEADS, bs * t_new, KV_RANK)


def _post_a_kernel(x_ref, conv_ref, olat_ref, wuv_ref, wout_ref, g_ref, wmq_ref, x1_ref, qm_ref):
    attn = None
    for hd in range(N_HEADS):
        d = _dot(olat_ref[hd].astype(BF16), wuv_ref[hd])
        attn = d if attn is None else attn + d
    mix = (_dot(conv_ref[...].astype(BF16), wout_ref[0:CONV_CH, :])
           + _dot(attn.astype(BF16), wout_ref[CONV_CH:, :]))
    x1 = x_ref[...] + mix
    x1_ref[...] = x1
    qm_ref[...] = _dot(_rms(x1, g_ref[...]).astype(BF16), wmq_ref[...])


def _post_a(x2d, conv2d, o_lat, w):
    t = x2d.shape[0]
    tm = TOKEN_TILE
    row = lambda n: pl.BlockSpec((tm, n), lambda i: (i, 0))
    mem_w = MEM_HEADS * MEM_HD
    return pl.pallas_call(
        _post_a_kernel, grid=(t // tm,),
        in_specs=[row(D_MODEL), row(CONV_CH), pl.BlockSpec((N_HEADS, tm, KV_RANK), lambda i: (0, i, 0)),
                  _const_spec((N_HEADS, KV_RANK, N_HEADS * V_DIM)),
                  _const_spec((D_MODEL, D_MODEL)), _const_spec((1, D_MODEL)), _const_spec((D_MODEL, mem_w))],
        out_specs=[row(D_MODEL), row(mem_w)],
        out_shape=[jax.ShapeDtypeStruct((t, D_MODEL), F32), jax.ShapeDtypeStruct((t, mem_w), F32)],
        compiler_params=_params("parallel", vmem=VMEM_LIMIT),
        name="post_a_sample")(x2d, conv2d, o_lat, w['wuv_pad'], w['w_out'], w['g_mem_q'], w['w_mq'])


def _mem_kv_kernel(m_ref, g_ref, wk_ref, wv_ref, k_ref, v_ref, kw_ref, vw_ref):
    tm = m_ref.shape[0]
    m = _rms(m_ref[...], g_ref[...]).astype(BF16)
    k = _dot(m, wk_ref[...])
    v = _dot(m, wv_ref[...])
    kw_ref[...] = k
    vw_ref[...] = v
    for hd in range(MEM_HEADS):
        sl = slice(hd * MEM_HD, (hd + 1) * MEM_HD)
        k_ref[pl.ds(hd, tm, stride=MEM_HEADS), :] = k[:, sl]
        v_ref[pl.ds(hd, tm, stride=MEM_HEADS), :] = v[:, sl]


def _mem_kv(mem2d, w):
    t = mem2d.shape[0]
    tm = TOKEN_TILE
    mem_w = MEM_HEADS * MEM_HD
    rows = pl.BlockSpec((tm * MEM_HEADS, MEM_HD), lambda i: (i, 0))
    wide = pl.BlockSpec((tm, mem_w), lambda i: (i, 0))
    return pl.pallas_call(
        _mem_kv_kernel, grid=(t // tm,),
        in_specs=[pl.BlockSpec((tm, D_MODEL), lambda i: (i, 0)), _const_spec((1, D_MODEL)),
                  _const_spec((D_MODEL, mem_w)), _const_spec((D_MODEL, mem_w))],
        out_specs=[rows, rows, wide, wide],
        out_shape=[jax.ShapeDtypeStruct((t * MEM_HEADS, MEM_HD), F32)] * 2
                  + [jax.ShapeDtypeStruct((t, mem_w), F32)] * 2,
        compiler_params=_params("parallel"),
        name="mem_kv")(mem2d, w['g_mem_kv'], w['w_mk'], w['w_mv'])


def _mem_attn_rows_kernel(q_ref, k_ref, v_ref, o_ref):
    bb, tq, _ = q_ref.shape
    rows = MEM_TOKENS * MEM_HEADS
    head = lambda hd: slice(hd * MEM_HD, (hd + 1) * MEM_HD)
    q_head = lax.broadcasted_iota(jnp.int32, (MEM_HEADS * tq, rows), 0) // tq
    k_head = lax.broadcasted_iota(jnp.int32, (MEM_HEADS * tq, rows), 1) % MEM_HEADS
    own = q_head == k_head
    for bi in range(bb):
        q = jnp.concatenate([q_ref[bi, :, head(hd)] for hd in range(MEM_HEADS)], axis=0).astype(BF16)
        k = k_ref[bi * rows:(bi + 1) * rows, :].astype(BF16)
        v = v_ref[bi * rows:(bi + 1) * rows, :].astype(BF16)
        s = jnp.where(own, _dot_t(q, k) * MEM_SCALE, NEG_INF)
        p = jnp.exp(s - jnp.max(s, axis=-1, keepdims=True))
        p = p / jnp.sum(p, axis=-1, keepdims=True)
        o = _dot(p.astype(BF16), v)
        for hd in range(MEM_HEADS):
            o_ref[bi, :, head(hd)] = o[hd * tq:(hd + 1) * tq, :]


def _mem_attn_rows(qm, mem_k, mem_v):
    b, s, mem_w = qm.shape
    bb = max(1, MEM_ATTN_ROWS // s)
    kv = pl.BlockSpec((bb * MEM_TOKENS * MEM_HEADS, MEM_HD), lambda bi: (bi, 0))
    return pl.pallas_call(
        _mem_attn_rows_kernel, grid=(b // bb,),
        in_specs=[pl.BlockSpec((bb, s, mem_w), lambda bi: (bi, 0, 0)), kv, kv],
        out_specs=pl.BlockSpec((bb, s, mem_w), lambda bi: (bi, 0, 0)),
        out_shape=jax.ShapeDtypeStruct((b, s, mem_w), F32),
        compiler_params=_params("parallel"),
        name="mem_attn_rows")(qm, mem_k, mem_v)


def _slab_rows(j, n):
    return pl.ds(j, n, stride=SLAB)


def _post_b_kernel(n_p, x_ref, conv_ref, attn_ref, mk_ref, mv_ref, wout_ref, gq_ref, wmq_ref,
                   x1s_ref, oms_ref, wmo_ref, g_ref, wr_ref, br_ref,
                   x2_ref, h3_ref, idx_ref, gate_ref, rank_ref, cnt_ref, carry_ref, x1_scr, om_scr):
    tm = x2_ref.shape[0]
    i = pl.program_id(0)

    @pl.when(i == 0)
    def _():
        carry_ref[...] = jnp.zeros(carry_ref.shape, F32)

    @pl.when(i < n_p)
    def _():
        mix = _dot(conv_ref[...], wout_ref[0:CONV_CH, :]) + _dot(attn_ref[...], wout_ref[CONV_CH:, :])
        x1p = x_ref[...] + mix
        x1_scr[...] = x1p
        qm = _dot(_rms(x1p, gq_ref[...]).astype(BF16), wmq_ref[...])
        for hd in range(MEM_HEADS):
            sl = slice(hd * MEM_HD, (hd + 1) * MEM_HD)
            s = _dot_t(qm[:, sl].astype(BF16), mk_ref[0, :, sl].astype(BF16)) * MEM_SCALE
            p = jnp.exp(s - jnp.max(s, axis=-1, keepdims=True))
            p = p / jnp.sum(p, axis=-1, keepdims=True)
            om_scr[:, sl] = _dot(p.astype(BF16), mv_ref[0, :, sl].astype(BF16))

    @pl.when(i >= n_p)
    def _():
        x1_scr[...] = x1s_ref[...]
        om_scr[...] = oms_ref[...]

    x1 = x1_scr[...]
    om = om_scr[...]
    x2 = x1 + _dot(om.astype(BF16), wmo_ref[...])
    x2_ref[...] = x2
    h3 = _rms(x2, g_ref[...])
    for j in range(SLAB):
        h3_ref[_slab_rows(j, tm), :] = h3[:, j * LANES:(j + 1) * LANES]
    logits = _dot(h3.astype(BF16), wr_ref[...]) + br_ref[...]
    lane = lax.broadcasted_iota(jnp.int32, logits.shape, 1)
    lane_f = lane.astype(F32)
    vals, hots = [], []
    idx_out = jnp.zeros(logits.shape, F32)
    for kk in range(TOP_K):
        mx = jnp.max(logits, axis=-1, keepdims=True)
        first = jnp.min(jnp.where(logits == mx, lane_f, float(LANES)), axis=-1, keepdims=True)
        hot = lane_f == first
        logits = jnp.where(hot, -jnp.inf, logits)
        vals.append(mx)
        hots.append(hot)
        idx_out = jnp.where(lane == kk, first, idx_out)
    exps = [jnp.exp(v - vals[0]) for v in vals]
    denom = exps[0] + exps[1] + exps[2] + exps[3]
    chosen = jnp.zeros(logits.shape, F32)
    gate_out = jnp.zeros(logits.shape, F32)
    for kk in range(TOP_K):
        chosen = chosen + hots[kk].astype(F32)
        gate_out = jnp.where(lane == kk, exps[kk] / denom, gate_out)
    r_i = lax.broadcasted_iota(jnp.int32, (tm, tm), 0)
    c_i = lax.broadcasted_iota(jnp.int32, (tm, tm), 1)
    tril = (c_i < r_i).astype(BF16)
    before = _dot(tril, chosen.astype(BF16)) + carry_ref[...]
    rank_out = jnp.zeros(logits.shape, F32)
    for kk in range(TOP_K):
        rk = jnp.sum(jnp.where(hots[kk], before, 0.0), axis=-1, keepdims=True)
        rank_out = jnp.where(lane == kk, rk, rank_out)
    carry = carry_ref[...] + jnp.sum(chosen, axis=0, keepdims=True)
    carry_ref[...] = carry
    idx_ref[...] = idx_out.astype(jnp.int32)
    gate_ref[...] = gate_out
    rank_ref[...] = rank_out.astype(jnp.int32)
    cnt_ref[...] = carry.astype(jnp.int32)


def _post_b(x_p, conv_p, attn_p, mk_wide, mv_wide, x1_s, om_s, w):
    tm = TOKEN_TILE
    n_p, n_s = x_p.shape[0] // tm, x1_s.shape[0] // tm
    tiles_per_seq = n_p // mk_wide.shape[0]
    t = (n_p + n_s) * tm
    mem_w = MEM_HEADS * MEM_HD
    row = lambda n: pl.BlockSpec((tm, n), lambda i: (i, 0))
    row_p = lambda n: pl.BlockSpec((tm, n), lambda i: (jnp.minimum(i, n_p - 1), 0))
    row_s = lambda n: pl.BlockSpec((tm, n), lambda i: (jnp.maximum(i - n_p, 0), 0))
    mem = pl.BlockSpec((1, MEM_TOKENS, mem_w), lambda i: (jnp.minimum(i, n_p - 1) // tiles_per_seq, 0, 0))
    return pl.pallas_call(
        functools.partial(_post_b_kernel, n_p), grid=(n_p + n_s,),
        in_specs=[row_p(D_MODEL), row_p(CONV_CH), row_p(N_HEADS * V_DIM), mem, mem,
                  _const_spec((D_MODEL, D_MODEL)), _const_spec((1, D_MODEL)), _const_spec((D_MODEL, mem_w)),
                  row_s(D_MODEL), row_s(mem_w),
                  _const_spec((mem_w, D_MODEL)), _const_spec((1, D_MODEL)),
                  _const_spec((D_MODEL, LANES)), _const_spec((1, LANES))],
        out_specs=[row(D_MODEL), pl.BlockSpec((tm * SLAB, LANES), lambda i: (i, 0)),
                   row(LANES), row(LANES), row(LANES), _const_spec((1, LANES))],
        out_shape=[jax.ShapeDtypeStruct((t, D_MODEL), F32), jax.ShapeDtypeStruct((t * SLAB, LANES), F32),
                   jax.ShapeDtypeStruct((t, LANES), jnp.int32), jax.ShapeDtypeStruct((t, LANES), F32),
                   jax.ShapeDtypeStruct((t, LANES), jnp.int32), jax.ShapeDtypeStruct((1, LANES), jnp.int32)],
        scratch_shapes=[pltpu.VMEM((1, LANES), F32), pltpu.VMEM((tm, D_MODEL), F32), pltpu.VMEM((tm, mem_w), F32)],
        compiler_params=_params("arbitrary", vmem=VMEM_LIMIT),
        name="post_b")(x_p, conv_p, attn_p, mk_wide, mv_wide, w['w_out'], w['g_mem_q'], w['w_mq'],
                       x1_s, om_s, w['w_mo'], w['g_ffn'], w['w_router'], w['b_router'])


def _slab(row):
    return pl.ds(pl.multiple_of(row * SLAB, SLAB), SLAB)


def _dispatch_kernel(start_ref, lo_ref, hi_ref, idx_ref, rank_ref, h_hbm, xs_hbm, dest_ref,
                     hbuf, zero_ref, sem_in, sem_out, sem_z):
    i = pl.program_id(0)
    n = pl.num_programs(0)
    rows_per_tile = hbuf.shape[1]
    tm = rows_per_tile // SLAB

    def load(tile, slot):
        rows = pl.ds(pl.multiple_of(tile * rows_per_tile, SLAB), rows_per_tile)
        return pltpu.make_async_copy(h_hbm.at[rows], hbuf.at[slot], sem_in.at[slot])

    def wait_scatter(slot):
        for _ in range(TOP_K):
            pltpu.make_async_copy(hbuf.at[0], xs_hbm.at[pl.ds(0, rows_per_tile)], sem_out.at[slot]).wait()

    @pl.when(i == 0)
    def _():
        load(0, 0).start()

    @pl.when(i + 1 < n)
    def _():
        load(i + 1, (i + 1) % 3).start()

    cur = i % 3
    load(i, cur).wait()

    def issue(t, _):
        for k in range(TOP_K):
            r = t * TOP_K + k
            d = start_ref[idx_ref[r]] + rank_ref[r]
            dest_ref[r] = d
            pltpu.make_async_copy(hbuf.at[cur, _slab(t)], xs_hbm.at[_slab(d)],
                                  sem_out.at[i % 2]).start(priority=k % 2)
        return 0

    lax.fori_loop(0, tm, issue, 0, unroll=4)

    @pl.when(i > 0)
    def _():
        wait_scatter((i - 1) % 2)

    @pl.when(i == n - 1)
    def _():
        wait_scatter(i % 2)
        zero_ref[...] = jnp.zeros(zero_ref.shape, F32)
        bm = zero_ref.shape[0] // SLAB
        for e in range(N_EXPERTS):
            lo = lo_ref[e]
            pad = hi_ref[e] - lo
            for bit in range(bm.bit_length() - 1):
                size = 1 << bit

                @pl.when((pad >> bit) & 1 == 1)
                def _():
                    first = lo + (pad & (size - 1))
                    rows = pl.ds(pl.multiple_of(first * SLAB, SLAB), size * SLAB)
                    cp = pltpu.make_async_copy(zero_ref.at[pl.ds(0, size * SLAB)], xs_hbm.at[rows], sem_z)
                    cp.start()
                    cp.wait()

        def tail(blk, _):
            rows = pl.ds(pl.multiple_of(blk * bm * SLAB, SLAB), bm * SLAB)
            cp = pltpu.make_async_copy(zero_ref, xs_hbm.at[rows], sem_z)
            cp.start()
            cp.wait()
            return 0

        lax.fori_loop(hi_ref[N_EXPERTS - 1] // bm, xs_hbm.shape[0] // (bm * SLAB), tail, 0)


def _dispatch(h3s, idx_flat, rank_flat, pad_start, pad_lo, pad_hi, n_rows):
    tm = TOKEN_TILE
    t = h3s.shape[0] // SLAB
    grid_spec = pltpu.PrefetchScalarGridSpec(
        num_scalar_prefetch=3, grid=(t // tm,),
        in_specs=[pl.BlockSpec((tm * TOP_K,), lambda i, *_: (i,), memory_space=pltpu.SMEM),
                  pl.BlockSpec((tm * TOP_K,), lambda i, *_: (i,), memory_space=pltpu.SMEM),
                  pl.BlockSpec(memory_space=pl.ANY)],
        out_specs=[pl.BlockSpec(memory_space=pl.ANY),
                   pl.BlockSpec((tm * TOP_K,), lambda i, *_: (i,), memory_space=pltpu.SMEM)],
        scratch_shapes=[pltpu.VMEM((3, tm * SLAB, LANES), F32), pltpu.VMEM((MOE_ROWS * SLAB, LANES), F32),
                        pltpu.SemaphoreType.DMA((3,)), pltpu.SemaphoreType.DMA((2,)), pltpu.SemaphoreType.DMA])
    return pl.pallas_call(
        _dispatch_kernel, grid_spec=grid_spec,
        out_shape=[jax.ShapeDtypeStruct((n_rows * SLAB, LANES), F32),
                   jax.ShapeDtypeStruct((t * TOP_K,), jnp.int32)],
        compiler_params=pltpu.CompilerParams(dimension_semantics=("arbitrary",), has_side_effects=True),
        name="moe_dispatch")(pad_start, pad_lo, pad_hi, idx_flat, rank_flat, h3s)


def _expert_kernel(src_ref, exp_ref, first_ref, valid_ref, next_ref, x_ref, wg_hbm, wu_hbm, wd_hbm,
                   bg_ref, bu_ref, bd_ref, y_ref, wf_ref, wb_ref, xb_ref, sem):
    b = pl.program_id(0)
    bm = xb_ref.shape[0]
    w_hbm = (wg_hbm, wu_hbm, wd_hbm)

    def fetch(e):
        for i in range(3):
            pltpu.make_async_copy(w_hbm[i].at[e], wf_ref.at[i], sem).start()

    @pl.when(b == 0)
    def _():
        fetch(exp_ref[0])

    @pl.when(first_ref[b] == 1)
    def _():
        for i in range(3):
            pltpu.make_async_copy(w_hbm[i].at[0], wf_ref.at[i], sem).wait()
        def cast(r, _):
            rows = pl.ds(pl.multiple_of(r * CAST_ROWS, CAST_ROWS), CAST_ROWS)
            for i in range(3):
                wb_ref[i, rows, :] = wf_ref[i, rows, :].astype(BF16)
            return 0

        lax.fori_loop(0, D_MODEL // CAST_ROWS, cast, 0)

        @pl.when(next_ref[b] >= 0)
        def _():
            fetch(next_ref[b])

    @pl.when(valid_ref[b] == 1)
    def _():
        for j in range(SLAB):
            xb_ref[:, j * LANES:(j + 1) * LANES] = x_ref[_slab_rows(j, bm), :].astype(BF16)
        x = xb_ref[...]
        g = _dot(x, wb_ref[0]) + bg_ref[0]
        u = _dot(x, wb_ref[1]) + bu_ref[0]
        g = jnp.minimum(g, SWIGLU_LIMIT)
        u = jnp.clip(u, -SWIGLU_LIMIT, SWIGLU_LIMIT)
        a = (u + 1.0) * (g * jax.nn.sigmoid(SWIGLU_ALPHA * g))
        y = _dot(a.astype(BF16), wb_ref[2]) + bd_ref[0]
        for j in range(SLAB):
            y_ref[_slab_rows(j, bm), :] = y[:, j * LANES:(j + 1) * LANES]

    @pl.when(valid_ref[b] == 0)
    def _():
        y_ref[...] = jnp.zeros(y_ref.shape, F32)


def _experts(xs, blk_src, blk_exp, blk_first, blk_valid, blk_next, w):
    bm = MOE_ROWS
    n_blocks = xs.shape[0] // (bm * SLAB)
    hbm = pl.BlockSpec(memory_space=pl.ANY)
    bspec = pl.BlockSpec((1, 1, D_MODEL), lambda b, src, ex, *_: (ex[b], 0, 0))
    grid_spec = pltpu.PrefetchScalarGridSpec(
        num_scalar_prefetch=5, grid=(n_blocks,),
        in_specs=[pl.BlockSpec((bm * SLAB, LANES), lambda b, src, *_: (src[b], 0)),
                  hbm, hbm, hbm, bspec, bspec, bspec],
        out_specs=pl.BlockSpec((bm * SLAB, LANES), lambda b, *_: (b, 0)),
        scratch_shapes=[pltpu.VMEM((3, D_MODEL, D_MODEL), F32), pltpu.VMEM((3, D_MODEL, D_MODEL), BF16),
                        pltpu.VMEM((bm, D_MODEL), BF16), pltpu.SemaphoreType.DMA])
    return pl.pallas_call(
        _expert_kernel, grid_spec=grid_spec,
        out_shape=jax.ShapeDtypeStruct(xs.shape, F32),
        compiler_params=_params("arbitrary", vmem=VMEM_LIMIT),
        name="moe_experts")(blk_src, blk_exp, blk_first, blk_valid, blk_next, xs,
                            w['w_gate'], w['w_up'], w['w_down'], w['b_gate'], w['b_up'], w['b_down'])


def _combine_kernel(n_p, dest_ref, dest_nx_ref, gate_ref, yb_hbm, x2_ref, g_ref,
                    yp_ref, ys_ref, rows_ref, ysum_ref, y_ref, sem):
    i = pl.program_id(0)
    tm = x2_ref.shape[0]

    def gather(d_ref, slot):
        def issue(t, _):
            for k in range(TOP_K):
                r = t * TOP_K + k
                pltpu.make_async_copy(yb_hbm.at[_slab(d_ref[r])], rows_ref.at[slot, _slab(r)],
                                      sem.at[slot]).start(priority=k % 2)
            return 0
        lax.fori_loop(0, tm, issue, 0, unroll=4)

    slot = i % 2

    @pl.when(i == 0)
    def _():
        gather(dest_ref, 0)

    @pl.when(i + 1 < pl.num_programs(0))
    def _():
        gather(dest_nx_ref, 1 - slot)

    for _ in range(TOP_K):
        pltpu.make_async_copy(yb_hbm.at[pl.ds(0, tm * SLAB)], rows_ref.at[slot, pl.ds(0, tm * SLAB)],
                              sem.at[slot]).wait()

    def token(t, _):
        acc = rows_ref[slot, _slab(t * TOP_K)] * gate_ref[t * TOP_K]
        for k in range(1, TOP_K):
            acc = acc + rows_ref[slot, _slab(t * TOP_K + k)] * gate_ref[t * TOP_K + k]
        ysum_ref[_slab(t)] = acc
        return 0

    lax.fori_loop(0, tm, token, 0, unroll=4)

    ss = jnp.zeros((tm, 1), F32)
    for j in range(SLAB):
        y = x2_ref[:, j * LANES:(j + 1) * LANES] + ysum_ref[_slab_rows(j, tm), :]
        y_ref[:, j * LANES:(j + 1) * LANES] = y
        ss = ss + jnp.sum(y * y, axis=-1, keepdims=True)
    out = y_ref[...] * lax.rsqrt(ss * (1.0 / D_MODEL) + EPS) * g_ref[...]

    @pl.when(i < n_p)
    def _():
        yp_ref[...] = out

    @pl.when(i >= n_p)
    def _():
        ys_ref[...] = out


def _combine(yb, dest_flat, gate_flat, x2, g_final, n_p):
    t = x2.shape[0]
    tm = TOKEN_TILE
    n = t // tm
    n_s = n - n_p
    cur = lambda i: (i,)
    nxt = lambda i: (jnp.minimum(i + 1, n - 1),)
    smem = lambda index_map: pl.BlockSpec((tm * TOP_K,), index_map, memory_space=pltpu.SMEM)
    return pl.pallas_call(
        functools.partial(_combine_kernel, n_p), grid=(n,),
        in_specs=[smem(cur), smem(nxt), smem(cur),
                  pl.BlockSpec(memory_space=pl.ANY),
                  pl.BlockSpec((tm, D_MODEL), lambda i: (i, 0)),
                  pl.BlockSpec((1, D_MODEL), lambda i: (0, 0))],
        out_specs=[pl.BlockSpec((tm, D_MODEL), lambda i: (jnp.minimum(i, n_p - 1), 0)),
                   pl.BlockSpec((tm, D_MODEL), lambda i: (jnp.maximum(i - n_p, 0), 0))],
        out_shape=[jax.ShapeDtypeStruct((n_p * tm, D_MODEL), F32), jax.ShapeDtypeStruct((n_s * tm, D_MODEL), F32)],
        scratch_shapes=[pltpu.VMEM((2, tm * TOP_K * SLAB, LANES), F32), pltpu.VMEM((tm * SLAB, LANES), F32),
                        pltpu.VMEM((tm, D_MODEL), F32), pltpu.SemaphoreType.DMA((2,))],
        compiler_params=_params("arbitrary", vmem=VMEM_LIMIT),
        name="moe_combine")(dest_flat, dest_flat, gate_flat, yb, x2, g_final)


def _moe_and_final(x2, h3s, idx128, gate128, rank128, cnt128, w, n_p):
    t = x2.shape[0]
    bm = MOE_ROWS
    n_assign = t * TOP_K
    n_blocks = (n_assign + N_EXPERTS * (bm - 1) + bm - 1) // bm
    counts = cnt128[0, :N_EXPERTS]
    padded = ((counts + bm - 1) // bm) * bm
    pad_end = jnp.cumsum(padded).astype(jnp.int32)
    pad_start = pad_end - padded
    idx_flat = idx128[:, :TOP_K].reshape(-1)
    rank_flat = rank128[:, :TOP_K].reshape(-1)
    blk = jnp.arange(n_blocks, dtype=jnp.int32)
    blk_valid = (blk * bm < pad_end[-1]).astype(jnp.int32)
    blk_src = jnp.minimum(blk, jnp.maximum(pad_end[-1] // bm - 1, 0))
    blk_exp = jnp.sum((pad_end[None, :] <= (blk_src * bm)[:, None]).astype(jnp.int32), axis=1)
    blk_exp = jnp.minimum(blk_exp, N_EXPERTS - 1)
    blk_first = jnp.concatenate([jnp.ones((1,), jnp.int32),
                                 (blk_exp[1:] != blk_exp[:-1]).astype(jnp.int32)])
    later_start = (blk[None, :] > blk[:, None]) & (blk_first[None, :] == 1)
    next_pos = jnp.min(jnp.where(later_start, blk[None, :], n_blocks), axis=1)
    blk_next = jnp.where(next_pos < n_blocks, blk_exp[jnp.minimum(next_pos, n_blocks - 1)], -1)
    xs, dest_flat = _dispatch(h3s, idx_flat, rank_flat, pad_start, pad_start + counts, pad_end, n_blocks * bm)
    yb = _experts(xs, blk_src, blk_exp, blk_first, blk_valid, blk_next.astype(jnp.int32), w)
    gate_flat = gate128[:, :TOP_K].reshape(-1)
    return _combine(yb, dest_flat, gate_flat, x2, w['g_final'], n_p)


def _swap_halves(wcols):
    half = QK_ROPE // 2
    return jnp.concatenate([-wcols[..., half:], wcols[..., :half]], axis=-1)


def _prep_weights(g_mix, w_in, conv_w, conv_b, conv_ln_g, conv_ln_b, q_norm_g, w_q_up, kv_norm_g, w_kv_up,
                  w_out, g_mem_q, g_mem_kv, w_mq, w_mk, w_mv, w_mo, g_ffn, w_router, b_router,
                  w_gate, b_gate, w_up, b_up, w_down, b_down, g_final):
    l = 0
    w_kpe = w_in[l][:, C_KPE:]
    zpad = jnp.zeros((D_MODEL, HEAD_PAD - QK_ROPE), F32)
    w_in_ext = jnp.concatenate([w_in[l][:, :C_KPE], w_kpe, zpad, _swap_halves(w_kpe), zpad], axis=1)
    wq3 = w_q_up[l].reshape(Q_RANK, N_HEADS, QK_NOPE + QK_ROPE)
    q_nope, q_rope = wq3[..., :QK_NOPE], wq3[..., QK_NOPE:]
    z32 = jnp.zeros((Q_RANK, N_HEADS, HEAD_PAD - QK_NOPE - QK_ROPE), F32)
    wq = jnp.concatenate([q_rope, q_nope, z32], axis=-1).reshape(Q_RANK, N_HEADS * HEAD_PAD)
    wq_sw = jnp.concatenate([_swap_halves(q_rope), jnp.zeros_like(q_nope), z32], axis=-1)
    wq_sw = wq_sw.reshape(Q_RANK, N_HEADS * HEAD_PAD)
    w_uk = w_kv_up[l][:, :, :QK_NOPE]
    w_uv = w_kv_up[l][:, :, QK_NOPE:]
    wuk_pad = jnp.concatenate([jnp.zeros((KV_RANK, N_HEADS, QK_ROPE), F32), w_uk,
                               jnp.zeros((KV_RANK, N_HEADS, HEAD_PAD - QK_NOPE - QK_ROPE), F32)], axis=-1)
    wukt_pad = jnp.transpose(wuk_pad, (1, 2, 0))
    odd_head = (jnp.arange(N_HEADS) % 2 == 1)[None, :, None]
    zv = jnp.zeros_like(w_uv)
    wuv_slot = jnp.where(odd_head, jnp.concatenate([zv, w_uv], axis=-1), jnp.concatenate([w_uv, zv], axis=-1))
    lane_id = jnp.arange(HEAD_PAD)[None, :]
    v_ones = jnp.where(odd_head[0], lane_id == 0, lane_id == V_DIM).astype(F32)
    eye = jnp.eye(N_HEADS, dtype=F32)
    wuv_pad = (w_uv.transpose(1, 0, 2)[:, :, None, :] * eye[:, None, :, None])
    conv_w_pad = jnp.concatenate([conv_w[l], jnp.zeros((CONV_HALO - CONV_WIDTH, CONV_CH), F32)], axis=0)
    w_router_pad = jnp.concatenate([w_router[l], jnp.zeros((D_MODEL, LANES - N_EXPERTS), F32)], axis=1)
    b_router_pad = jnp.concatenate([b_router[l], jnp.full((LANES - N_EXPERTS,), NEG_INF, F32)])
    return {
        'g_mix': g_mix[l][None], 'w_in_ext': w_in_ext.astype(BF16),
        'q_norm_g': q_norm_g[l][None], 'wq': wq.astype(BF16), 'wq_sw': wq_sw.astype(BF16),
        'kv_norm_g': kv_norm_g[l][None],
        'wuk_pad': wuk_pad.reshape(KV_RANK, N_HEADS * HEAD_PAD).astype(BF16),
        'wuv_slot': wuv_slot.reshape(KV_RANK, N_HEADS * HEAD_PAD).astype(BF16),
        'v_ones': v_ones.reshape(1, N_HEADS * HEAD_PAD),
        'wukt_pad': wukt_pad.astype(BF16),
        'wuv_pad': wuv_pad.reshape(N_HEADS, KV_RANK, N_HEADS * V_DIM).astype(BF16),
        'conv_w': conv_w_pad, 'conv_b': conv_b[l][None],
        'conv_ln_g': conv_ln_g[l][None], 'conv_ln_b': conv_ln_b[l][None],
        'w_out': w_out[l].astype(BF16), 'g_mem_q': g_mem_q[l][None], 'w_mq': w_mq[l].astype(BF16),
        'g_mem_kv': g_mem_kv[l][None], 'w_mk': w_mk[l].astype(BF16), 'w_mv': w_mv[l].astype(BF16),
        'w_mo': w_mo[l].astype(BF16), 'g_ffn': g_ffn[l][None],
        'w_router': w_router_pad.astype(BF16), 'b_router': b_router_pad[None],
        'w_gate': w_gate[l], 'w_up': w_up[l], 'w_down': w_down[l],
        'b_gate': b_gate[l][:, None, :], 'b_up': b_up[l][:, None, :], 'b_down': b_down[l][:, None, :],
        'g_final': g_final[None],
    }


def _rope_table(pos, q_scale):
    half = QK_ROPE // 2
    inv = ROPE_THETA ** (-jnp.arange(half, dtype=F32) / half)
    ang = pos.astype(F32)[:, None] * inv[None, :]
    cos, sin = jnp.cos(ang), jnp.sin(ang)
    n = pos.shape[0]
    ones = jnp.ones((n, QK_NOPE), F32)
    z = lambda k: jnp.zeros((n, k), F32)
    cq = q_scale * jnp.concatenate([cos, cos, ones, z(HEAD_PAD - QK_NOPE - QK_ROPE)], axis=1)
    sq = q_scale * jnp.concatenate([sin, sin, z(HEAD_PAD - QK_ROPE)], axis=1)
    ck = jnp.concatenate([cos, cos, z(HEAD_PAD - QK_ROPE)], axis=1)
    sk = jnp.concatenate([sin, sin, z(HEAD_PAD - QK_ROPE)], axis=1)
    return jnp.stack([cq, sq, ck, sk])


def _front(x, conv_prev, mem_k, mem_v, pos, w, paged):
    b, s, _ = x.shape
    t = b * s
    x2d = x.reshape(t, D_MODEL)
    sample = paged is not None
    if sample:
        tab = _rope_table(jnp.tile(pos, TOKEN_TILE // s), MLA_SCALE)
    else:
        tab = _rope_table(pos, MLA_SCALE * LOG2_E)
    tail = CONV_WIDTH - 1
    prev_pad = jnp.concatenate([jnp.zeros((b, CONV_HALO - tail, CONV_CH), F32), conv_prev], axis=1)
    outs = _inproj(x2d, tab, w, sample, prev_pad)
    u, ckv, kpe = outs[0], outs[1], outs[2]
    u3 = u.reshape(b, s, CONV_CH)
    if s >= tail:
        conv_tail = u3[:, s - tail:]
    else:
        conv_tail = jnp.concatenate([conv_prev[:, s:], u3], axis=1)
    ckv3 = ckv.reshape(b, s, KV_RANK)
    kpe3 = kpe.reshape(b, s, QK_ROPE)
    if sample:
        page_table, cache_ckv, cache_kpe_t = paged
        conv_out = _conv_sample(jnp.concatenate([prev_pad, u3], axis=1), w)
        attn = _attn_sample(page_table, outs[3], outs[4], ckv3, kpe3, cache_ckv, cache_kpe_t)
    else:
        conv_out = outs[6]
        q, k, v = (a.reshape(b, s, -1) for a in outs[3:6])
        attn = _attn_prompt(q, k, v).reshape(t, N_HEADS * V_DIM)
    conv2d = conv_out.reshape(t, CONV_CH)
    if not sample:
        return (x2d, conv2d, attn), conv_tail, ckv3, kpe3
    x1, qm = _post_a(x2d, conv2d, attn, w)
    om = _mem_attn_rows(qm.reshape(b, s, -1), mem_k, mem_v)
    return (x1, om.reshape(t, -1)), conv_tail, ckv3, kpe3


def kernel(x_prompt, x_sample, mem_prompt, cache_ckv, cache_kpe, page_table, cache_mem_k, cache_mem_v, state_conv, g_mix, w_in, conv_w, conv_b, conv_ln_g, conv_ln_b, q_norm_g, w_q_up, kv_norm_g, w_kv_up, w_out, g_mem_q, g_mem_kv, w_mq, w_mk, w_mv, w_mo, g_ffn, w_router, b_router, w_gate, b_gate, w_up, b_up, w_down, b_down, g_final):
    assert g_mix.shape[0] == 1, "single-layer step"
    w = _prep_weights(g_mix, w_in, conv_w, conv_b, conv_ln_g, conv_ln_b, q_norm_g, w_q_up, kv_norm_g, w_kv_up,
                      w_out, g_mem_q, g_mem_kv, w_mq, w_mk, w_mv, w_mo, g_ffn, w_router, b_router,
                      w_gate, b_gate, w_up, b_up, w_down, b_down, g_final)
    b_p, s_p, _ = x_prompt.shape
    b_s, t_s, _ = x_sample.shape
    past = page_table.shape[1] * PAGE_SIZE

    mk, mv, mk_wide, mv_wide = _mem_kv(mem_prompt.reshape(-1, D_MODEL), w)
    conv0 = jnp.zeros((b_p, CONV_WIDTH - 1, CONV_CH), F32)
    (x_p, conv_out_p, attn_p), conv_p, ckv_p, kpe_p = _front(
        x_prompt, conv0, None, None, jnp.arange(s_p, dtype=jnp.int32), w, None)
    (x1_s, om_s), conv_s, ckv_s, kpe_s = _front(
        x_sample, state_conv[0], cache_mem_k[0].reshape(-1, MEM_HD), cache_mem_v[0].reshape(-1, MEM_HD),
        past + jnp.arange(t_s, dtype=jnp.int32), w,
        (page_table, cache_ckv[0], jnp.swapaxes(cache_kpe[0], 1, 2)))

    x2, h3s, idx128, gate128, rank128, cnt128 = _post_b(
        x_p, conv_out_p, attn_p, mk_wide.reshape(b_p, MEM_TOKENS, -1), mv_wide.reshape(b_p, MEM_TOKENS, -1),
        x1_s, om_s, w)
    y_p, y_s = _moe_and_final(x2, h3s, idx128, gate128, rank128, cnt128, w, x_p.shape[0] // TOKEN_TILE)

    mem_shape = (1, b_p, MEM_TOKENS, MEM_HEADS, MEM_HD)
    return (y_p.reshape(b_p, s_p, D_MODEL), y_s.reshape(b_s, t_s, D_MODEL), ckv_p[None], kpe_p[None],
            mk.reshape(mem_shape), mv.reshape(mem_shape), conv_p[None], ckv_s[None], kpe_s[None], conv_s[None])
```

```python
import functools

import jax
import jax.numpy as jnp
from jax import lax
from jax.experimental import pallas as pl
from jax.experimental.pallas import tpu as pltpu

F32 = jnp.float32
BF16 = jnp.bfloat16

D_MODEL = 1024
PAGE_SIZE = 128
CONV_CH = 512
CONV_WIDTH = 31
N_HEADS = 8
QK_NOPE = 64
QK_ROPE = 32
V_DIM = 64
Q_RANK = 384
KV_RANK = 256
ROPE_THETA = 10000.0
MLA_SCALE = (QK_NOPE + QK_ROPE) ** -0.5
LOG2_E = 1.4426950408889634
MEM_TOKENS = 256
MEM_HEADS = 4
MEM_HD = 128
MEM_SCALE = MEM_HD ** -0.5
N_EXPERTS = 32
TOP_K = 4
SWIGLU_LIMIT = 7.0
SWIGLU_ALPHA = 1.702
EPS = 1e-6
NEG_INF = -1e30

LANES = 128
SUBLANES = 8
HEAD_PAD = 128
C_VAL, C_GATE, C_Q, C_CKV, C_KPE, C_KPE_SW, C_END = 0, 512, 1024, 1408, 1664, 1792, 1920
TOKEN_TILE = 256
ATTN_TILE = 512
ATTN_HEADS = 4
CONV_HALO = 32
CONV_CHUNK = 32
MEM_ATTN_ROWS = 32
PAGES_PER_CHUNK = 32
SLAB = 8
MOE_ROWS = 512
CAST_ROWS = 32
VMEM_LIMIT = 48 * 1024 * 1024


def _rms(x, g):
    return x * lax.rsqrt(jnp.mean(x * x, axis=-1, keepdims=True) + EPS) * g


def _dot(a, b):
    return jnp.dot(a, b, preferred_element_type=F32)


def _dot_t(a, b):
    return lax.dot_general(a, b, (((1,), (1,)), ((), ())), preferred_element_type=F32)


def _params(*sem, vmem=None):
    return pltpu.CompilerParams(dimension_semantics=sem, vmem_limit_bytes=vmem)


def _const_spec(shape):
    nd = len(shape)
    return pl.BlockSpec(shape, lambda *_: (0,) * nd)


def _inproj_common(x_ref, gmix_ref, win_ref, qg_ref, wq_ref, wqsw_ref, kvg_ref, tab_ref,
                   u_ref, ckv_ref, kpe_ref, after_u=None, after_q=None):
    h = _rms(x_ref[...], gmix_ref[...]).astype(BF16)
    glu = _dot(h, win_ref[:, C_VAL:C_Q])
    u = glu[:, :CONV_CH] * jax.nn.sigmoid(glu[:, CONV_CH:])
    u_ref[...] = u
    proj = _dot(h, win_ref[:, C_Q:C_END])
    if after_u is not None:
        after_u(u)
    qn = _rms(proj[:, 0:C_CKV - C_Q], qg_ref[...]).astype(BF16)
    ckv = _rms(proj[:, C_CKV - C_Q:C_KPE - C_Q], kvg_ref[...])
    ckv_ref[...] = ckv
    cq, sq, ck, sk = tab_ref[0], tab_ref[1], tab_ref[2], tab_ref[3]
    kpe_rot = proj[:, C_KPE - C_Q:C_KPE_SW - C_Q] * ck + proj[:, C_KPE_SW - C_Q:C_END - C_Q] * sk
    kpe_ref[...] = kpe_rot[:, :QK_ROPE]
    q = _dot(qn, wq_ref[...])
    qs = _dot(qn, wqsw_ref[...])
    if after_q is not None:
        after_q()
    q_heads = []
    for hd in range(N_HEADS):
        sl = slice(hd * HEAD_PAD, (hd + 1) * HEAD_PAD)
        q_heads.append(q[:, sl] * cq + qs[:, sl] * sq)
    return ckv, kpe_rot, q_heads


def _conv_fill_window(win_ref, halo, tile):
    tt = tile.shape[0]
    win_ref[0, 0:CONV_HALO, :] = halo
    win_ref[0, CONV_HALO:CONV_HALO + tt, :] = tile
    n = tt + CONV_HALO - SUBLANES
    for s in range(1, SUBLANES):
        win_ref[s, 0:n, :] = win_ref[0, s:s + n, :]


def _conv_chunk(win_ref, c, w_ref, b_ref, g_ref, lb_ref):
    acc = None
    for j in range(CONV_WIDTH):
        q, s = divmod(c * CONV_CHUNK + 2 + j, SUBLANES)
        term = win_ref[s, q * SUBLANES:q * SUBLANES + CONV_CHUNK, :] * w_ref[j:j + 1, :]
        acc = term if acc is None else acc + term
    return _ln_swish(acc + b_ref[...], g_ref[...], lb_ref[...])


def _inproj_prompt_kernel(tiles_per_seq, x_ref, gmix_ref, win_ref, qg_ref, wq_ref, wqsw_ref, kvg_ref, tab_ref,
                          wuk_ref, wuv_ref, vone_ref, prev_ref, cw_ref, cb_ref, cg_ref, clb_ref,
                          u_ref, ckv_ref, kpe_ref, q_ref, k_ref, v_ref, conv_ref, win_scr, carry_scr):
    tm = x_ref.shape[0]
    n_chunks = tm // CONV_CHUNK
    first = pl.program_id(0) % tiles_per_seq == 0

    def conv_chunks(lo, hi):
        for c in range(lo, hi):
            y = _conv_chunk(win_scr, c, cw_ref, cb_ref, cg_ref, clb_ref)
            conv_ref[c * CONV_CHUNK:(c + 1) * CONV_CHUNK, :] = y.astype(BF16)

    def after_u(u):
        _conv_fill_window(win_scr, jnp.where(first, prev_ref[0], carry_scr[...]), u)
        carry_scr[...] = u[tm - CONV_HALO:, :]
        conv_chunks(0, n_chunks * 3 // 4)

    ckv, kpe_rot, q_heads = _inproj_common(x_ref, gmix_ref, win_ref, qg_ref, wq_ref, wqsw_ref,
                                           kvg_ref, tab_ref, u_ref, ckv_ref, kpe_ref,
                                           after_u=after_u, after_q=lambda: conv_chunks(n_chunks * 3 // 4, n_chunks))
    ckv_b = ckv.astype(BF16)
    k_nope = _dot(ckv_b, wuk_ref[...])
    for hd in range(N_HEADS):
        sl = slice(hd * HEAD_PAD, (hd + 1) * HEAD_PAD)
        q_ref[:, sl] = q_heads[hd].astype(BF16)
        k_ref[:, sl] = (k_nope[:, sl] + kpe_rot).astype(BF16)
    v_ref[...] = (_dot(ckv_b, wuv_ref[...]) + vone_ref[...]).astype(BF16)


def _inproj_sample_kernel(x_ref, gmix_ref, win_ref, qg_ref, wq_ref, wqsw_ref, kvg_ref, tab_ref,
                          wukt_ref,
                          u_ref, ckv_ref, kpe_ref, qlat_ref, qpe_ref):
    _, _, q_heads = _inproj_common(x_ref, gmix_ref, win_ref, qg_ref, wq_ref, wqsw_ref,
                                   kvg_ref, tab_ref, u_ref, ckv_ref, kpe_ref)
    for hd in range(N_HEADS):
        qlat_ref[hd] = _dot(q_heads[hd].astype(BF16), wukt_ref[hd])
        qpe_ref[hd] = q_heads[hd][:, :QK_ROPE]


def _inproj(x2d, tab, w, sample, prev_pad=None):
    t = x2d.shape[0]
    tm = TOKEN_TILE
    n_tab = tab.shape[1] // tm
    row = lambda n: pl.BlockSpec((tm, n), lambda i: (i, 0))
    in_specs = [row(D_MODEL), _const_spec((1, D_MODEL)), _const_spec((D_MODEL, C_END)),
                _const_spec((1, Q_RANK)), _const_spec((Q_RANK, N_HEADS * HEAD_PAD)),
                _const_spec((Q_RANK, N_HEADS * HEAD_PAD)), _const_spec((1, KV_RANK)),
                pl.BlockSpec((4, tm, LANES), lambda i: (0, i % n_tab, 0))]
    args = [x2d, w['g_mix'], w['w_in_ext'], w['q_norm_g'], w['wq'], w['wq_sw'], w['kv_norm_g'], tab]
    out_shape = [jax.ShapeDtypeStruct((t, CONV_CH), F32), jax.ShapeDtypeStruct((t, KV_RANK), F32),
                 jax.ShapeDtypeStruct((t, QK_ROPE), F32)]
    out_specs = [row(CONV_CH), row(KV_RANK), row(QK_ROPE)]
    scratch = []
    if sample:
        body = _inproj_sample_kernel
        in_specs += [_const_spec((N_HEADS, HEAD_PAD, KV_RANK))]
        args += [w['wukt_pad']]
        out_shape += [jax.ShapeDtypeStruct((N_HEADS, t, KV_RANK), F32),
                      jax.ShapeDtypeStruct((N_HEADS, t, QK_ROPE), F32)]
        out_specs += [pl.BlockSpec((N_HEADS, tm, KV_RANK), lambda i: (0, i, 0)),
                      pl.BlockSpec((N_HEADS, tm, QK_ROPE), lambda i: (0, i, 0))]
    else:
        tiles_per_seq = t // prev_pad.shape[0] // tm
        body = functools.partial(_inproj_prompt_kernel, tiles_per_seq)
        in_specs += [_const_spec((KV_RANK, N_HEADS * HEAD_PAD)), _const_spec((KV_RANK, N_HEADS * HEAD_PAD)),
                     _const_spec((1, N_HEADS * HEAD_PAD)),
                     pl.BlockSpec((1, CONV_HALO, CONV_CH), lambda i: (i // tiles_per_seq, 0, 0)),
                     _const_spec((CONV_HALO, CONV_CH)), _const_spec((1, CONV_CH)),
                     _const_spec((1, CONV_CH)), _const_spec((1, CONV_CH))]
        args += [w['wuk_pad'], w['wuv_slot'], w['v_ones'], prev_pad,
                 w['conv_w'], w['conv_b'], w['conv_ln_g'], w['conv_ln_b']]
        out_shape += [jax.ShapeDtypeStruct((t, N_HEADS * HEAD_PAD), BF16)] * 3
        out_shape += [jax.ShapeDtypeStruct((t, CONV_CH), BF16)]
        out_specs += [row(N_HEADS * HEAD_PAD)] * 3 + [row(CONV_CH)]
        scratch = [pltpu.VMEM((SUBLANES, tm + CONV_HALO, CONV_CH), F32), pltpu.VMEM((CONV_HALO, CONV_CH), F32)]
    return pl.pallas_call(
        body, grid=(t // tm,), in_specs=in_specs, out_specs=out_specs, out_shape=out_shape,
        scratch_shapes=scratch,
        compiler_params=_params("parallel" if sample else "arbitrary", vmem=VMEM_LIMIT),
        name="inproj_sample" if sample else "inproj_prompt")(*args)


def _ln_swish(conv, g, b):
    mu = jnp.mean(conv, axis=-1, keepdims=True)
    xc = conv - mu
    var = jnp.mean(xc * xc, axis=-1, keepdims=True)
    y = xc * lax.rsqrt(var + EPS) * g + b
    return y * jax.nn.sigmoid(y)


def _conv_sample_kernel(win_ref, w_ref, b_ref, g_ref, lb_ref, o_ref):
    t = o_ref.shape[1]
    acc = win_ref[:, 2:2 + t, :] * w_ref[0:1, :]
    for j in range(1, CONV_WIDTH):
        acc = acc + win_ref[:, 2 + j:2 + j + t, :] * w_ref[j:j + 1, :]
    y = _ln_swish(acc + b_ref[...], g_ref[...], lb_ref[...])
    o_ref[...] = y


def _conv_sample(upad, w):
    b, s_pad, _ = upad.shape
    t = s_pad - CONV_HALO
    bb = 8
    return pl.pallas_call(
        _conv_sample_kernel, grid=(b // bb,),
        in_specs=[pl.BlockSpec((bb, s_pad, CONV_CH), lambda i: (i, 0, 0)),
                  _const_spec((CONV_HALO, CONV_CH)), _const_spec((1, CONV_CH)),
                  _const_spec((1, CONV_CH)), _const_spec((1, CONV_CH))],
        out_specs=pl.BlockSpec((bb, t, CONV_CH), lambda i: (i, 0, 0)),
        out_shape=jax.ShapeDtypeStruct((b, t, CONV_CH), F32),
        compiler_params=_params("parallel"),
        name="conv_sample")(upad, w['conv_w'], w['conv_b'], w['conv_ln_g'], w['conv_ln_b'])


def _attn_prompt_kernel(q_ref, k_ref, v_ref, o_ref, s_ref, m_ref):
    tq = q_ref.shape[1]
    qi = pl.program_id(2)
    nh = ATTN_HEADS
    head = lambda hd: slice(hd * HEAD_PAD, (hd + 1) * HEAD_PAD)
    rows = lax.broadcasted_iota(jnp.int32, (tq, tq), 0)
    cols = lax.broadcasted_iota(jnp.int32, (tq, tq), 1)
    lane = lax.broadcasted_iota(jnp.int32, (tq, LANES), 1)

    def scores(j, m_all, masked):
        start = pl.multiple_of(j * tq, tq)
        for hd in range(nh):
            s = _dot_t(q_ref[0, :, head(hd)], k_ref[0, pl.ds(start, tq), head(hd)])
            if masked:
                s = jnp.where(cols <= rows, s, NEG_INF)
            s_ref[hd, j] = s
            m_all = jnp.where(lane == hd, jnp.maximum(m_all, jnp.max(s, axis=-1, keepdims=True)), m_all)
        return m_all

    m_all = jnp.full((tq, LANES), NEG_INF, F32)
    m_all = lax.fori_loop(0, qi, functools.partial(scores, masked=False), m_all)
    m_all = scores(qi, m_all, True)
    for hd in range(nh):
        m_ref[hd] = jnp.broadcast_to(m_all[:, hd:hd + 1], (tq, LANES))

    def values(j, accs):
        start = pl.multiple_of(j * tq, tq)
        out = []
        for hd in range(nh):
            m = m_ref[hd]
            p = jnp.exp2(s_ref[hd, j] - jnp.concatenate([m] * (tq // LANES), axis=1)).astype(BF16)
            out.append(accs[hd] + _dot(p, v_ref[0, pl.ds(start, tq), head(hd)]))
        return tuple(out)

    accs = tuple(jnp.zeros((tq, HEAD_PAD), F32) for _ in range(nh))
    accs = lax.fori_loop(0, qi + 1, values, accs)
    for hp in range(nh // 2):
        even, odd = accs[2 * hp], accs[2 * hp + 1]
        o_even = even / even[:, V_DIM:V_DIM + 1]
        o_odd = odd / odd[:, 0:1]
        o_ref[0, :, hp * LANES:(hp + 1) * LANES] = jnp.where(lane < V_DIM, o_even, o_odd).astype(BF16)


def _attn_prompt(q, k, v):
    b, s, _ = q.shape
    tq = ATTN_TILE
    nh = ATTN_HEADS
    return pl.pallas_call(
        _attn_prompt_kernel, grid=(b, N_HEADS // nh, s // tq),
        in_specs=[pl.BlockSpec((1, tq, nh * HEAD_PAD), lambda bi, hq, qi: (bi, qi, hq)),
                  pl.BlockSpec((1, s, nh * HEAD_PAD), lambda bi, hq, qi: (bi, 0, hq)),
                  pl.BlockSpec((1, s, nh * HEAD_PAD), lambda bi, hq, qi: (bi, 0, hq))],
        out_specs=pl.BlockSpec((1, tq, nh * V_DIM), lambda bi, hq, qi: (bi, qi, hq)),
        out_shape=jax.ShapeDtypeStruct((b, s, N_HEADS * V_DIM), BF16),
        scratch_shapes=[pltpu.VMEM((nh, s // tq, tq, tq), F32), pltpu.VMEM((nh, tq, LANES), F32)],
        compiler_params=_params("parallel", "parallel", "arbitrary", vmem=VMEM_LIMIT),
        name="attn_prompt")(q, k, v)


def _attn_sample_kernel(pt_ref, ql_ref, qp_ref, cn_ref, kn_ref, ckv_hbm, kpe_hbm, o_ref,
                        ckv_buf, kpe_buf, sem):
    b = pl.program_id(0)
    n_pages = kpe_buf.shape[1]
    n_chunks = n_pages // PAGES_PER_CHUNK
    chunk = PAGES_PER_CHUNK * PAGE_SIZE
    t_new = cn_ref.shape[1]
    rows_q = N_HEADS * t_new

    def fetch(batch, slot):
        def one(p, _):
            page = pt_ref[batch * n_pages + p]
            rows = pl.ds(pl.multiple_of(p * PAGE_SIZE, PAGE_SIZE), PAGE_SIZE)
            pltpu.make_async_copy(ckv_hbm.at[page], ckv_buf.at[slot, rows], sem.at[0, slot]).start()
            pltpu.make_async_copy(kpe_hbm.at[page], kpe_buf.at[slot, p], sem.at[1, slot]).start()
            return 0
        lax.fori_loop(0, n_pages, one, 0, unroll=4)

    slot = b % 2

    @pl.when(b == 0)
    def _():
        fetch(0, 0)

    @pl.when(b + 1 < pl.num_programs(0))
    def _():
        fetch(b + 1, 1 - slot)

    pltpu.make_async_copy(ckv_buf.at[slot], ckv_buf.at[slot], sem.at[0, slot]).wait()
    pltpu.make_async_copy(kpe_buf.at[slot], kpe_buf.at[slot], sem.at[1, slot]).wait()

    ql = ql_ref[:, 0].reshape(rows_q, KV_RANK).astype(BF16)
    qp = qp_ref[:, 0].reshape(rows_q, QK_ROPE).astype(BF16)

    pad = PAGE_SIZE - t_new
    kc_new = jnp.concatenate([cn_ref[0], jnp.zeros((pad, KV_RANK), F32)], axis=0).astype(BF16)
    kp_new = jnp.concatenate([kn_ref[0], jnp.zeros((pad, QK_ROPE), F32)], axis=0).astype(BF16)
    s_new = _dot_t(ql, kc_new) + _dot_t(qp, kp_new)
    t_q = lax.broadcasted_iota(jnp.int32, s_new.shape, 0) % t_new
    cols = lax.broadcasted_iota(jnp.int32, s_new.shape, 1)
    s_new = jnp.where(cols <= t_q, s_new, NEG_INF)
    def part(s, values):
        m = jnp.max(s, axis=-1, keepdims=True)
        p = jnp.exp(s - m)
        return m, jnp.sum(p, axis=-1, keepdims=True), _dot(p.astype(BF16), values)

    def scores(c):
        kc = ckv_buf[slot, c * chunk:(c + 1) * chunk, :].astype(BF16)
        kpt = jnp.concatenate([kpe_buf[slot, c * PAGES_PER_CHUNK + i] for i in range(PAGES_PER_CHUNK)],
                              axis=1).astype(BF16)
        return _dot_t(ql, kc) + _dot(qp, kpt), kc

    parts = [part(s_new, kc_new)]
    nxt = scores(0)
    for c in range(n_chunks):
        cur = nxt
        if c + 1 < n_chunks:
            nxt = scores(c + 1)
        parts.append(part(*cur))
    m = parts[0][0]
    for mp, _, _ in parts[1:]:
        m = jnp.maximum(m, mp)
    l = jnp.zeros_like(m)
    acc = jnp.zeros((rows_q, KV_RANK), F32)
    for mp, lp, ap in parts:
        w = jnp.exp(mp - m)
        l = l + w * lp
        acc = acc + w * ap
    o_ref[:, 0] = (acc / l).reshape(N_HEADS, t_new, KV_RANK)


def _attn_sample(page_table, qlat, qpe, ckv_new, kpe_new, cache_ckv, cache_kpe_t):
    bs, n_pages = page_table.shape
    t_new = ckv_new.shape[1]
    qlat4 = qlat.reshape(N_HEADS, bs, t_new, KV_RANK)
    qpe4 = qpe.reshape(N_HEADS, bs, t_new, QK_ROPE)
    past = n_pages * PAGE_SIZE
    in_specs = [pl.BlockSpec((N_HEADS, 1, t_new, KV_RANK), lambda b, pt: (0, b, 0, 0)),
                pl.BlockSpec((N_HEADS, 1, t_new, QK_ROPE), lambda b, pt: (0, b, 0, 0)),
                pl.BlockSpec((1, t_new, KV_RANK), lambda b, pt: (b, 0, 0)),
                pl.BlockSpec((1, t_new, QK_ROPE), lambda b, pt: (b, 0, 0)),
                pl.BlockSpec(memory_space=pl.ANY), pl.BlockSpec(memory_space=pl.ANY)]
    grid_spec = pltpu.PrefetchScalarGridSpec(
        num_scalar_prefetch=1, grid=(bs,), in_specs=in_specs,
        out_specs=pl.BlockSpec((N_HEADS, 1, t_new, KV_RANK), lambda b, pt: (0, b, 0, 0)),
        scratch_shapes=[pltpu.VMEM((2, past, KV_RANK), F32),
                        pltpu.VMEM((2, n_pages, QK_ROPE, PAGE_SIZE), F32),
                        pltpu.SemaphoreType.DMA((2, 2))])
    o = pl.pallas_call(
        _attn_sample_kernel, grid_spec=grid_spec,
        out_shape=jax.ShapeDtypeStruct((N_HEADS, bs, t_new, KV_RANK), F32),
        compiler_params=_params("arbitrary", vmem=VMEM_LIMIT),
        name="attn_sample")(page_table.reshape(-1), qlat4, qpe4, ckv_new, kpe_new, cache_ckv, cache_kpe_t)
    return o.reshape(N_HEADS, bs * t_new, KV_RANK)


def _post_a_kernel(x_ref, conv_ref, olat_ref, wuv_ref, wout_ref, g_ref, wmq_ref, x1_ref, qm_ref):
    attn = None
    for hd in range(N_HEADS):
        d = _dot(olat_ref[hd].astype(BF16), wuv_ref[hd])
        attn = d if attn is None else attn + d
    mix = (_dot(conv_ref[...].astype(BF16), wout_ref[0:CONV_CH, :])
           + _dot(attn.astype(BF16), wout_ref[CONV_CH:, :]))
    x1 = x_ref[...] + mix
    x1_ref[...] = x1
    qm_ref[...] = _dot(_rms(x1, g_ref[...]).astype(BF16), wmq_ref[...])


def _post_a(x2d, conv2d, o_lat, w):
    t = x2d.shape[0]
    tm = TOKEN_TILE
    row = lambda n: pl.BlockSpec((tm, n), lambda i: (i, 0))
    mem_w = MEM_HEADS * MEM_HD
    return pl.pallas_call(
        _post_a_kernel, grid=(t // tm,),
        in_specs=[row(D_MODEL), row(CONV_CH), pl.BlockSpec((N_HEADS, tm, KV_RANK), lambda i: (0, i, 0)),
                  _const_spec((N_HEADS, KV_RANK, N_HEADS * V_DIM)),
                  _const_spec((D_MODEL, D_MODEL)), _const_spec((1, D_MODEL)), _const_spec((D_MODEL, mem_w))],
        out_specs=[row(D_MODEL), row(mem_w)],
        out_shape=[jax.ShapeDtypeStruct((t, D_MODEL), F32), jax.ShapeDtypeStruct((t, mem_w), F32)],
        compiler_params=_params("parallel", vmem=VMEM_LIMIT),
        name="post_a_sample")(x2d, conv2d, o_lat, w['wuv_pad'], w['w_out'], w['g_mem_q'], w['w_mq'])


def _mem_kv_kernel(m_ref, g_ref, wk_ref, wv_ref, k_ref, v_ref, kw_ref, vw_ref):
    tm = m_ref.shape[0]
    m = _rms(m_ref[...], g_ref[...]).astype(BF16)
    k = _dot(m, wk_ref[...])
    v = _dot(m, wv_ref[...])
    kw_ref[...] = k
    vw_ref[...] = v
    for hd in range(MEM_HEADS):
        sl = slice(hd * MEM_HD, (hd + 1) * MEM_HD)
        k_ref[pl.ds(hd, tm, stride=MEM_HEADS), :] = k[:, sl]
        v_ref[pl.ds(hd, tm, stride=MEM_HEADS), :] = v[:, sl]


def _mem_kv(mem2d, w):
    t = mem2d.shape[0]
    tm = TOKEN_TILE
    mem_w = MEM_HEADS * MEM_HD
    rows = pl.BlockSpec((tm * MEM_HEADS, MEM_HD), lambda i: (i, 0))
    wide = pl.BlockSpec((tm, mem_w), lambda i: (i, 0))
    return pl.pallas_call(
        _mem_kv_kernel, grid=(t // tm,),
        in_specs=[pl.BlockSpec((tm, D_MODEL), lambda i: (i, 0)), _const_spec((1, D_MODEL)),
                  _const_spec((D_MODEL, mem_w)), _const_spec((D_MODEL, mem_w))],
        out_specs=[rows, rows, wide, wide],
        out_shape=[jax.ShapeDtypeStruct((t * MEM_HEADS, MEM_HD), F32)] * 2
                  + [jax.ShapeDtypeStruct((t, mem_w), F32)] * 2,
        compiler_params=_params("parallel"),
        name="mem_kv")(mem2d, w['g_mem_kv'], w['w_mk'], w['w_mv'])


def _mem_attn_rows_kernel(q_ref, k_ref, v_ref, o_ref):
    bb, tq, _ = q_ref.shape
    rows = MEM_TOKENS * MEM_HEADS
    head = lambda hd: slice(hd * MEM_HD, (hd + 1) * MEM_HD)
    q_head = lax.broadcasted_iota(jnp.int32, (MEM_HEADS * tq, rows), 0) // tq
    k_head = lax.broadcasted_iota(jnp.int32, (MEM_HEADS * tq, rows), 1) % MEM_HEADS
    own = q_head == k_head
    for bi in range(bb):
        q = jnp.concatenate([q_ref[bi, :, head(hd)] for hd in range(MEM_HEADS)], axis=0).astype(BF16)
        k = k_ref[bi * rows:(bi + 1) * rows, :].astype(BF16)
        v = v_ref[bi * rows:(bi + 1) * rows, :].astype(BF16)
        s = jnp.where(own, _dot_t(q, k) * MEM_SCALE, NEG_INF)
        p = jnp.exp(s - jnp.max(s, axis=-1, keepdims=True))
        p = p / jnp.sum(p, axis=-1, keepdims=True)
        o = _dot(p.astype(BF16), v)
        for hd in range(MEM_HEADS):
            o_ref[bi, :, head(hd)] = o[hd * tq:(hd + 1) * tq, :]


def _mem_attn_rows(qm, mem_k, mem_v):
    b, s, mem_w = qm.shape
    bb = max(1, MEM_ATTN_ROWS // s)
    kv = pl.BlockSpec((bb * MEM_TOKENS * MEM_HEADS, MEM_HD), lambda bi: (bi, 0))
    return pl.pallas_call(
        _mem_attn_rows_kernel, grid=(b // bb,),
        in_specs=[pl.BlockSpec((bb, s, mem_w), lambda bi: (bi, 0, 0)), kv, kv],
        out_specs=pl.BlockSpec((bb, s, mem_w), lambda bi: (bi, 0, 0)),
        out_shape=jax.ShapeDtypeStruct((b, s, mem_w), F32),
        compiler_params=_params("parallel"),
        name="mem_attn_rows")(qm, mem_k, mem_v)


def _slab_rows(j, n):
    return pl.ds(j, n, stride=SLAB)


def _post_b_kernel(n_p, x_ref, conv_ref, attn_ref, mk_ref, mv_ref, wout_ref, gq_ref, wmq_ref,
                   x1s_ref, oms_ref, wmo_ref, g_ref, wr_ref, br_ref,
                   x2_ref, h3_ref, idx_ref, gate_ref, rank_ref, cnt_ref, carry_ref, x1_scr, om_scr):
    tm = x2_ref.shape[0]
    i = pl.program_id(0)

    @pl.when(i == 0)
    def _():
        carry_ref[...] = jnp.zeros(carry_ref.shape, F32)

    @pl.when(i < n_p)
    def _():
        mix = _dot(conv_ref[...], wout_ref[0:CONV_CH, :]) + _dot(attn_ref[...], wout_ref[CONV_CH:, :])
        x1p = x_ref[...] + mix
        x1_scr[...] = x1p
        qm = _dot(_rms(x1p, gq_ref[...]).astype(BF16), wmq_ref[...])
        for hd in range(MEM_HEADS):
            sl = slice(hd * MEM_HD, (hd + 1) * MEM_HD)
            s = _dot_t(qm[:, sl].astype(BF16), mk_ref[0, :, sl].astype(BF16)) * MEM_SCALE
            p = jnp.exp(s - jnp.max(s, axis=-1, keepdims=True))
            p = p / jnp.sum(p, axis=-1, keepdims=True)
            om_scr[:, sl] = _dot(p.astype(BF16), mv_ref[0, :, sl].astype(BF16))

    @pl.when(i >= n_p)
    def _():
        x1_scr[...] = x1s_ref[...]
        om_scr[...] = oms_ref[...]

    x1 = x1_scr[...]
    om = om_scr[...]
    x2 = x1 + _dot(om.astype(BF16), wmo_ref[...])
    x2_ref[...] = x2
    h3 = _rms(x2, g_ref[...])
    for j in range(SLAB):
        h3_ref[_slab_rows(j, tm), :] = h3[:, j * LANES:(j + 1) * LANES]
    logits = _dot(h3.astype(BF16), wr_ref[...]) + br_ref[...]
    lane = lax.broadcasted_iota(jnp.int32, logits.shape, 1)
    lane_f = lane.astype(F32)
    vals, hots = [], []
    idx_out = jnp.zeros(logits.shape, F32)
    for kk in range(TOP_K):
        mx = jnp.max(logits, axis=-1, keepdims=True)
        first = jnp.min(jnp.where(logits == mx, lane_f, float(LANES)), axis=-1, keepdims=True)
        hot = lane_f == first
        logits = jnp.where(hot, -jnp.inf, logits)
        vals.append(mx)
        hots.append(hot)
        idx_out = jnp.where(lane == kk, first, idx_out)
    exps = [jnp.exp(v - vals[0]) for v in vals]
    denom = exps[0] + exps[1] + exps[2] + exps[3]
    chosen = jnp.zeros(logits.shape, F32)
    gate_out = jnp.zeros(logits.shape, F32)
    for kk in range(TOP_K):
        chosen = chosen + hots[kk].astype(F32)
        gate_out = jnp.where(lane == kk, exps[kk] / denom, gate_out)
    r_i = lax.broadcasted_iota(jnp.int32, (tm, tm), 0)
    c_i = lax.broadcasted_iota(jnp.int32, (tm, tm), 1)
    tril = (c_i < r_i).astype(BF16)
    before = _dot(tril, chosen.astype(BF16)) + carry_ref[...]
    rank_out = jnp.zeros(logits.shape, F32)
    for kk in range(TOP_K):
        rk = jnp.sum(jnp.where(hots[kk], before, 0.0), axis=-1, keepdims=True)
        rank_out = jnp.where(lane == kk, rk, rank_out)
    carry = carry_ref[...] + jnp.sum(chosen, axis=0, keepdims=True)
    carry_ref[...] = carry
    idx_ref[...] = idx_out.astype(jnp.int32)
    gate_ref[...] = gate_out
    rank_ref[...] = rank_out.astype(jnp.int32)
    cnt_ref[...] = carry.astype(jnp.int32)


def _post_b(x_p, conv_p, attn_p, mk_wide, mv_wide, x1_s, om_s, w):
    tm = TOKEN_TILE
    n_p, n_s = x_p.shape[0] // tm, x1_s.shape[0] // tm
    tiles_per_seq = n_p // mk_wide.shape[0]
    t = (n_p + n_s) * tm
    mem_w = MEM_HEADS * MEM_HD
    row = lambda n: pl.BlockSpec((tm, n), lambda i: (i, 0))
    row_p = lambda n: pl.BlockSpec((tm, n), lambda i: (jnp.minimum(i, n_p - 1), 0))
    row_s = lambda n: pl.BlockSpec((tm, n), lambda i: (jnp.maximum(i - n_p, 0), 0))
    mem = pl.BlockSpec((1, MEM_TOKENS, mem_w), lambda i: (jnp.minimum(i, n_p - 1) // tiles_per_seq, 0, 0))
    return pl.pallas_call(
        functools.partial(_post_b_kernel, n_p), grid=(n_p + n_s,),
        in_specs=[row_p(D_MODEL), row_p(CONV_CH), row_p(N_HEADS * V_DIM), mem, mem,
                  _const_spec((D_MODEL, D_MODEL)), _const_spec((1, D_MODEL)), _const_spec((D_MODEL, mem_w)),
                  row_s(D_MODEL), row_s(mem_w),
                  _const_spec((mem_w, D_MODEL)), _const_spec((1, D_MODEL)),
                  _const_spec((D_MODEL, LANES)), _const_spec((1, LANES))],
        out_specs=[row(D_MODEL), pl.BlockSpec((tm * SLAB, LANES), lambda i: (i, 0)),
                   row(LANES), row(LANES), row(LANES), _const_spec((1, LANES))],
        out_shape=[jax.ShapeDtypeStruct((t, D_MODEL), F32), jax.ShapeDtypeStruct((t * SLAB, LANES), F32),
                   jax.ShapeDtypeStruct((t, LANES), jnp.int32), jax.ShapeDtypeStruct((t, LANES), F32),
                   jax.ShapeDtypeStruct((t, LANES), jnp.int32), jax.ShapeDtypeStruct((1, LANES), jnp.int32)],
        scratch_shapes=[pltpu.VMEM((1, LANES), F32), pltpu.VMEM((tm, D_MODEL), F32), pltpu.VMEM((tm, mem_w), F32)],
        compiler_params=_params("arbitrary", vmem=VMEM_LIMIT),
        name="post_b")(x_p, conv_p, attn_p, mk_wide, mv_wide, w['w_out'], w['g_mem_q'], w['w_mq'],
                       x1_s, om_s, w['w_mo'], w['g_ffn'], w['w_router'], w['b_router'])


def _slab(row):
    return pl.ds(pl.multiple_of(row * SLAB, SLAB), SLAB)


def _dispatch_kernel(start_ref, lo_ref, hi_ref, idx_ref, rank_ref, h_hbm, xs_hbm, dest_ref,
                     hbuf, zero_ref, sem_in, sem_out, sem_z):
    i = pl.program_id(0)
    n = pl.num_programs(0)
    rows_per_tile = hbuf.shape[1]
    tm = rows_per_tile // SLAB

    def load(tile, slot):
        rows = pl.ds(pl.multiple_of(tile * rows_per_tile, SLAB), rows_per_tile)
        return pltpu.make_async_copy(h_hbm.at[rows], hbuf.at[slot], sem_in.at[slot])

    def wait_scatter(slot):
        for _ in range(TOP_K):
            pltpu.make_async_copy(hbuf.at[0], xs_hbm.at[pl.ds(0, rows_per_tile)], sem_out.at[slot]).wait()

    @pl.when(i == 0)
    def _():
        load(0, 0).start()

    @pl.when(i + 1 < n)
    def _():
        load(i + 1, (i + 1) % 3).start()

    cur = i % 3
    load(i, cur).wait()

    def issue(t, _):
        for k in range(TOP_K):
            r = t * TOP_K + k
            d = start_ref[idx_ref[r]] + rank_ref[r]
            dest_ref[r] = d
            pltpu.make_async_copy(hbuf.at[cur, _slab(t)], xs_hbm.at[_slab(d)],
                                  sem_out.at[i % 2]).start(priority=k % 2)
        return 0

    lax.fori_loop(0, tm, issue, 0, unroll=4)

    @pl.when(i > 0)
    def _():
        wait_scatter((i - 1) % 2)

    @pl.when(i == n - 1)
    def _():
        wait_scatter(i % 2)
        zero_ref[...] = jnp.zeros(zero_ref.shape, F32)
        bm = zero_ref.shape[0] // SLAB
        for e in range(N_EXPERTS):
            lo = lo_ref[e]
            pad = hi_ref[e] - lo
            for bit in range(bm.bit_length() - 1):
                size = 1 << bit

                @pl.when((pad >> bit) & 1 == 1)
                def _():
                    first = lo + (pad & (size - 1))
                    rows = pl.ds(pl.multiple_of(first * SLAB, SLAB), size * SLAB)
                    cp = pltpu.make_async_copy(zero_ref.at[pl.ds(0, size * SLAB)], xs_hbm.at[rows], sem_z)
                    cp.start()
                    cp.wait()

        def tail(blk, _):
            rows = pl.ds(pl.multiple_of(blk * bm * SLAB, SLAB), bm * SLAB)
            cp = pltpu.make_async_copy(zero_ref, xs_hbm.at[rows], sem_z)
            cp.start()
            cp.wait()
            return 0

        lax.fori_loop(hi_ref[N_EXPERTS - 1] // bm, xs_hbm.shape[0] // (bm * SLAB), tail, 0)


def _dispatch(h3s, idx_flat, rank_flat, pad_start, pad_lo, pad_hi, n_rows):
    tm = TOKEN_TILE
    t = h3s.shape[0] // SLAB
    grid_spec = pltpu.PrefetchScalarGridSpec(
        num_scalar_prefetch=3, grid=(t // tm,),
        in_specs=[pl.BlockSpec((tm * TOP_K,), lambda i, *_: (i,), memory_space=pltpu.SMEM),
                  pl.BlockSpec((tm * TOP_K,), lambda i, *_: (i,), memory_space=pltpu.SMEM),
                  pl.BlockSpec(memory_space=pl.ANY)],
        out_specs=[pl.BlockSpec(memory_space=pl.ANY),
                   pl.BlockSpec((tm * TOP_K,), lambda i, *_: (i,), memory_space=pltpu.SMEM)],
        scratch_shapes=[pltpu.VMEM((3, tm * SLAB, LANES), F32), pltpu.VMEM((MOE_ROWS * SLAB, LANES), F32),
                        pltpu.SemaphoreType.DMA((3,)), pltpu.SemaphoreType.DMA((2,)), pltpu.SemaphoreType.DMA])
    return pl.pallas_call(
        _dispatch_kernel, grid_spec=grid_spec,
        out_shape=[jax.ShapeDtypeStruct((n_rows * SLAB, LANES), F32),
                   jax.ShapeDtypeStruct((t * TOP_K,), jnp.int32)],
        compiler_params=pltpu.CompilerParams(dimension_semantics=("arbitrary",), has_side_effects=True),
        name="moe_dispatch")(pad_start, pad_lo, pad_hi, idx_flat, rank_flat, h3s)


def _expert_kernel(src_ref, exp_ref, first_ref, valid_ref, next_ref, x_ref, wg_hbm, wu_hbm, wd_hbm,
                   bg_ref, bu_ref, bd_ref, y_ref, wf_ref, wb_ref, xb_ref, sem):
    b = pl.program_id(0)
    bm = xb_ref.shape[0]
    w_hbm = (wg_hbm, wu_hbm, wd_hbm)

    def fetch(e):
        for i in range(3):
            pltpu.make_async_copy(w_hbm[i].at[e], wf_ref.at[i], sem).start()

    @pl.when(b == 0)
    def _():
        fetch(exp_ref[0])

    @pl.when(first_ref[b] == 1)
    def _():
        for i in range(3):
            pltpu.make_async_copy(w_hbm[i].at[0], wf_ref.at[i], sem).wait()
        def cast(r, _):
            rows = pl.ds(pl.multiple_of(r * CAST_ROWS, CAST_ROWS), CAST_ROWS)
            for i in range(3):
                wb_ref[i, rows, :] = wf_ref[i, rows, :].astype(BF16)
            return 0

        lax.fori_loop(0, D_MODEL // CAST_ROWS, cast, 0)

        @pl.when(next_ref[b] >= 0)
        def _():
            fetch(next_ref[b])

    @pl.when(valid_ref[b] == 1)
    def _():
        for j in range(SLAB):
            xb_ref[:, j * LANES:(j + 1) * LANES] = x_ref[_slab_rows(j, bm), :].astype(BF16)
        x = xb_ref[...]
        g = _dot(x, wb_ref[0]) + bg_ref[0]
        u = _dot(x, wb_ref[1]) + bu_ref[0]
        g = jnp.minimum(g, SWIGLU_LIMIT)
        u = jnp.clip(u, -SWIGLU_LIMIT, SWIGLU_LIMIT)
        a = (u + 1.0) * (g * jax.nn.sigmoid(SWIGLU_ALPHA * g))
        y = _dot(a.astype(BF16), wb_ref[2]) + bd_ref[0]
        for j in range(SLAB):
            y_ref[_slab_rows(j, bm), :] = y[:, j * LANES:(j + 1) * LANES]

    @pl.when(valid_ref[b] == 0)
    def _():
        y_ref[...] = jnp.zeros(y_ref.shape, F32)


def _experts(xs, blk_src, blk_exp, blk_first, blk_valid, blk_next, w):
    bm = MOE_ROWS
    n_blocks = xs.shape[0] // (bm * SLAB)
    hbm = pl.BlockSpec(memory_space=pl.ANY)
    bspec = pl.BlockSpec((1, 1, D_MODEL), lambda b, src, ex, *_: (ex[b], 0, 0))
    grid_spec = pltpu.PrefetchScalarGridSpec(
        num_scalar_prefetch=5, grid=(n_blocks,),
        in_specs=[pl.BlockSpec((bm * SLAB, LANES), lambda b, src, *_: (src[b], 0)),
                  hbm, hbm, hbm, bspec, bspec, bspec],
        out_specs=pl.BlockSpec((bm * SLAB, LANES), lambda b, *_: (b, 0)),
        scratch_shapes=[pltpu.VMEM((3, D_MODEL, D_MODEL), F32), pltpu.VMEM((3, D_MODEL, D_MODEL), BF16),
                        pltpu.VMEM((bm, D_MODEL), BF16), pltpu.SemaphoreType.DMA])
    return pl.pallas_call(
        _expert_kernel, grid_spec=grid_spec,
        out_shape=jax.ShapeDtypeStruct(xs.shape, F32),
        compiler_params=_params("arbitrary", vmem=VMEM_LIMIT),
        name="moe_experts")(blk_src, blk_exp, blk_first, blk_valid, blk_next, xs,
                            w['w_gate'], w['w_up'], w['w_down'], w['b_gate'], w['b_up'], w['b_down'])


def _combine_kernel(n_p, dest_ref, dest_nx_ref, gate_ref, yb_hbm, x2_ref, g_ref,
                    yp_ref, ys_ref, rows_ref, ysum_ref, y_ref, sem):
    i = pl.program_id(0)
    tm = x2_ref.shape[0]

    def gather(d_ref, slot):
        def issue(t, _):
            for k in range(TOP_K):
                r = t * TOP_K + k
                pltpu.make_async_copy(yb_hbm.at[_slab(d_ref[r])], rows_ref.at[slot, _slab(r)],
                                      sem.at[slot]).start(priority=k % 2)
            return 0
        lax.fori_loop(0, tm, issue, 0, unroll=4)

    slot = i % 2

    @pl.when(i == 0)
    def _():
        gather(dest_ref, 0)

    @pl.when(i + 1 < pl.num_programs(0))
    def _():
        gather(dest_nx_ref, 1 - slot)

    for _ in range(TOP_K):
        pltpu.make_async_copy(yb_hbm.at[pl.ds(0, tm * SLAB)], rows_ref.at[slot, pl.ds(0, tm * SLAB)],
                              sem.at[slot]).wait()

    def token(t, _):
        acc = rows_ref[slot, _slab(t * TOP_K)] * gate_ref[t * TOP_K]
        for k in range(1, TOP_K):
            acc = acc + rows_ref[slot, _slab(t * TOP_K + k)] * gate_ref[t * TOP_K + k]
        ysum_ref[_slab(t)] = acc
        return 0

    lax.fori_loop(0, tm, token, 0, unroll=4)

    ss = jnp.zeros((tm, 1), F32)
    for j in range(SLAB):
        y = x2_ref[:, j * LANES:(j + 1) * LANES] + ysum_ref[_slab_rows(j, tm), :]
        y_ref[:, j * LANES:(j + 1) * LANES] = y
        ss = ss + jnp.sum(y * y, axis=-1, keepdims=True)
    out = y_ref[...] * lax.rsqrt(ss * (1.0 / D_MODEL) + EPS) * g_ref[...]

    @pl.when(i < n_p)
    def _():
        yp_ref[...] = out

    @pl.when(i >= n_p)
    def _():
        ys_ref[...] = out


def _combine(yb, dest_flat, gate_flat, x2, g_final, n_p):
    t = x2.shape[0]
    tm = TOKEN_TILE
    n = t // tm
    n_s = n - n_p
    cur = lambda i: (i,)
    nxt = lambda i: (jnp.minimum(i + 1, n - 1),)
    smem = lambda index_map: pl.BlockSpec((tm * TOP_K,), index_map, memory_space=pltpu.SMEM)
    return pl.pallas_call(
        functools.partial(_combine_kernel, n_p), grid=(n,),
        in_specs=[smem(cur), smem(nxt), smem(cur),
                  pl.BlockSpec(memory_space=pl.ANY),
                  pl.BlockSpec((tm, D_MODEL), lambda i: (i, 0)),
                  pl.BlockSpec((1, D_MODEL), lambda i: (0, 0))],
        out_specs=[pl.BlockSpec((tm, D_MODEL), lambda i: (jnp.minimum(i, n_p - 1), 0)),
                   pl.BlockSpec((tm, D_MODEL), lambda i: (jnp.maximum(i - n_p, 0), 0))],
        out_shape=[jax.ShapeDtypeStruct((n_p * tm, D_MODEL), F32), jax.ShapeDtypeStruct((n_s * tm, D_MODEL), F32)],
        scratch_shapes=[pltpu.VMEM((2, tm * TOP_K * SLAB, LANES), F32), pltpu.VMEM((tm * SLAB, LANES), F32),
                        pltpu.VMEM((tm, D_MODEL), F32), pltpu.SemaphoreType.DMA((2,))],
        compiler_params=_params("arbitrary", vmem=VMEM_LIMIT),
        name="moe_combine")(dest_flat, dest_flat, gate_flat, yb, x2, g_final)


def _moe_and_final(x2, h3s, idx128, gate128, rank128, cnt128, w, n_p):
    t = x2.shape[0]
    bm = MOE_ROWS
    n_assign = t * TOP_K
    n_blocks = (n_assign + N_EXPERTS * (bm - 1) + bm - 1) // bm
    counts = cnt128[0, :N_EXPERTS]
    padded = ((counts + bm - 1) // bm) * bm
    pad_end = jnp.cumsum(padded).astype(jnp.int32)
    pad_start = pad_end - padded
    idx_flat = idx128[:, :TOP_K].reshape(-1)
    rank_flat = rank128[:, :TOP_K].reshape(-1)
    blk = jnp.arange(n_blocks, dtype=jnp.int32)
    blk_valid = (blk * bm < pad_end[-1]).astype(jnp.int32)
    blk_src = jnp.minimum(blk, jnp.maximum(pad_end[-1] // bm - 1, 0))
    blk_exp = jnp.sum((pad_end[None, :] <= (blk_src * bm)[:, None]).astype(jnp.int32), axis=1)
    blk_exp = jnp.minimum(blk_exp, N_EXPERTS - 1)
    blk_first = jnp.concatenate([jnp.ones((1,), jnp.int32),
                                 (blk_exp[1:] != blk_exp[:-1]).astype(jnp.int32)])
    later_start = (blk[None, :] > blk[:, None]) & (blk_first[None, :] == 1)
    next_pos = jnp.min(jnp.where(later_start, blk[None, :], n_blocks), axis=1)
    blk_next = jnp.where(next_pos < n_blocks, blk_exp[jnp.minimum(next_pos, n_blocks - 1)], -1)
    xs, dest_flat = _dispatch(h3s, idx_flat, rank_flat, pad_start, pad_start + counts, pad_end, n_blocks * bm)
    yb = _experts(xs, blk_src, blk_exp, blk_first, blk_valid, blk_next.astype(jnp.int32), w)
    gate_flat = gate128[:, :TOP_K].reshape(-1)
    return _combine(yb, dest_flat, gate_flat, x2, w['g_final'], n_p)


def _swap_halves(wcols):
    half = QK_ROPE // 2
    return jnp.concatenate([-wcols[..., half:], wcols[..., :half]], axis=-1)


def _prep_weights(g_mix, w_in, conv_w, conv_b, conv_ln_g, conv_ln_b, q_norm_g, w_q_up, kv_norm_g, w_kv_up,
                  w_out, g_mem_q, g_mem_kv, w_mq, w_mk, w_mv, w_mo, g_ffn, w_router, b_router,
                  w_gate, b_gate, w_up, b_up, w_down, b_down, g_final):
    l = 0
    w_kpe = w_in[l][:, C_KPE:]
    zpad = jnp.zeros((D_MODEL, HEAD_PAD - QK_ROPE), F32)
    w_in_ext = jnp.concatenate([w_in[l][:, :C_KPE], w_kpe, zpad, _swap_halves(w_kpe), zpad], axis=1)
    wq3 = w_q_up[l].reshape(Q_RANK, N_HEADS, QK_NOPE + QK_ROPE)
    q_nope, q_rope = wq3[..., :QK_NOPE], wq3[..., QK_NOPE:]
    z32 = jnp.zeros((Q_RANK, N_HEADS, HEAD_PAD - QK_NOPE - QK_ROPE), F32)
    wq = jnp.concatenate([q_rope, q_nope, z32], axis=-1).reshape(Q_RANK, N_HEADS * HEAD_PAD)
    wq_sw = jnp.concatenate([_swap_halves(q_rope), jnp.zeros_like(q_nope), z32], axis=-1)
    wq_sw = wq_sw.reshape(Q_RANK, N_HEADS * HEAD_PAD)
    w_uk = w_kv_up[l][:, :, :QK_NOPE]
    w_uv = w_kv_up[l][:, :, QK_NOPE:]
    wuk_pad = jnp.concatenate([jnp.zeros((KV_RANK, N_HEADS, QK_ROPE), F32), w_uk,
                               jnp.zeros((KV_RANK, N_HEADS, HEAD_PAD - QK_NOPE - QK_ROPE), F32)], axis=-1)
    wukt_pad = jnp.transpose(wuk_pad, (1, 2, 0))
    odd_head = (jnp.arange(N_HEADS) % 2 == 1)[None, :, None]
    zv = jnp.zeros_like(w_uv)
    wuv_slot = jnp.where(odd_head, jnp.concatenate([zv, w_uv], axis=-1), jnp.concatenate([w_uv, zv], axis=-1))
    lane_id = jnp.arange(HEAD_PAD)[None, :]
    v_ones = jnp.where(odd_head[0], lane_id == 0, lane_id == V_DIM).astype(F32)
    eye = jnp.eye(N_HEADS, dtype=F32)
    wuv_pad = (w_uv.transpose(1, 0, 2)[:, :, None, :] * eye[:, None, :, None])
    conv_w_pad = jnp.concatenate([conv_w[l], jnp.zeros((CONV_HALO - CONV_WIDTH, CONV_CH), F32)], axis=0)
    w_router_pad = jnp.concatenate([w_router[l], jnp.zeros((D_MODEL, LANES - N_EXPERTS), F32)], axis=1)
    b_router_pad = jnp.concatenate([b_router[l], jnp.full((LANES - N_EXPERTS,), NEG_INF, F32)])
    return {
        'g_mix': g_mix[l][None], 'w_in_ext': w_in_ext.astype(BF16),
        'q_norm_g': q_norm_g[l][None], 'wq': wq.astype(BF16), 'wq_sw': wq_sw.astype(BF16),
        'kv_norm_g': kv_norm_g[l][None],
        'wuk_pad': wuk_pad.reshape(KV_RANK, N_HEADS * HEAD_PAD).astype(BF16),
        'wuv_slot': wuv_slot.reshape(KV_RANK, N_HEADS * HEAD_PAD).astype(BF16),
        'v_ones': v_ones.reshape(1, N_HEADS * HEAD_PAD),
        'wukt_pad': wukt_pad.astype(BF16),
        'wuv_pad': wuv_pad.reshape(N_HEADS, KV_RANK, N_HEADS * V_DIM).astype(BF16),
        'conv_w': conv_w_pad, 'conv_b': conv_b[l][None],
        'conv_ln_g': conv_ln_g[l][None], 'conv_ln_b': conv_ln_b[l][None],
        'w_out': w_out[l].astype(BF16), 'g_mem_q': g_mem_q[l][None], 'w_mq': w_mq[l].astype(BF16),
        'g_mem_kv': g_mem_kv[l][None], 'w_mk': w_mk[l].astype(BF16), 'w_mv': w_mv[l].astype(BF16),
        'w_mo': w_mo[l].astype(BF16), 'g_ffn': g_ffn[l][None],
        'w_router': w_router_pad.astype(BF16), 'b_router': b_router_pad[None],
        'w_gate': w_gate[l], 'w_up': w_up[l], 'w_down': w_down[l],
        'b_gate': b_gate[l][:, None, :], 'b_up': b_up[l][:, None, :], 'b_down': b_down[l][:, None, :],
        'g_final': g_final[None],
    }


def _rope_table(pos, q_scale):
    half = QK_ROPE // 2
    inv = ROPE_THETA ** (-jnp.arange(half, dtype=F32) / half)
    ang = pos.astype(F32)[:, None] * inv[None, :]
    cos, sin = jnp.cos(ang), jnp.sin(ang)
    n = pos.shape[0]
    ones = jnp.ones((n, QK_NOPE), F32)
    z = lambda k: jnp.zeros((n, k), F32)
    cq = q_scale * jnp.concatenate([cos, cos, ones, z(HEAD_PAD - QK_NOPE - QK_ROPE)], axis=1)
    sq = q_scale * jnp.concatenate([sin, sin, z(HEAD_PAD - QK_ROPE)], axis=1)
    ck = jnp.concatenate([cos, cos, z(HEAD_PAD - QK_ROPE)], axis=1)
    sk = jnp.concatenate([sin, sin, z(HEAD_PAD - QK_ROPE)], axis=1)
    return jnp.stack([cq, sq, ck, sk])


def _front(x, conv_prev, mem_k, mem_v, pos, w, paged):
    b, s, _ = x.shape
    t = b * s
    x2d = x.reshape(t, D_MODEL)
    sample = paged is not None
    if sample:
        tab = _rope_table(jnp.tile(pos, TOKEN_TILE // s), MLA_SCALE)
    else:
        tab = _rope_table(pos, MLA_SCALE * LOG2_E)
    tail = CONV_WIDTH - 1
    prev_pad = jnp.concatenate([jnp.zeros((b, CONV_HALO - tail, CONV_CH), F32), conv_prev], axis=1)
    outs = _inproj(x2d, tab, w, sample, prev_pad)
    u, ckv, kpe = outs[0], outs[1], outs[2]
    u3 = u.reshape(b, s, CONV_CH)
    if s >= tail:
        conv_tail = u3[:, s - tail:]
    else:
        conv_tail = jnp.concatenate([conv_prev[:, s:], u3], axis=1)
    ckv3 = ckv.reshape(b, s, KV_RANK)
    kpe3 = kpe.reshape(b, s, QK_ROPE)
    if sample:
        page_table, cache_ckv, cache_kpe_t = paged
        conv_out = _conv_sample(jnp.concatenate([prev_pad, u3], axis=1), w)
        attn = _attn_sample(page_table, outs[3], outs[4], ckv3, kpe3, cache_ckv, cache_kpe_t)
    else:
        conv_out = outs[6]
        q, k, v = (a.reshape(b, s, -1) for a in outs[3:6])
        attn = _attn_prompt(q, k, v).reshape(t, N_HEADS * V_DIM)
    conv2d = conv_out.reshape(t, CONV_CH)
    if not sample:
        return (x2d, conv2d, attn), conv_tail, ckv3, kpe3
    x1, qm = _post_a(x2d, conv2d, attn, w)
    om = _mem_attn_rows(qm.reshape(b, s, -1), mem_k, mem_v)
    return (x1, om.reshape(t, -1)), conv_tail, ckv3, kpe3


def kernel(x_prompt, x_sample, mem_prompt, cache_ckv, cache_kpe, page_table, cache_mem_k, cache_mem_v, state_conv, g_mix, w_in, conv_w, conv_b, conv_ln_g, conv_ln_b, q_norm_g, w_q_up, kv_norm_g, w_kv_up, w_out, g_mem_q, g_mem_kv, w_mq, w_mk, w_mv, w_mo, g_ffn, w_router, b_router, w_gate, b_gate, w_up, b_up, w_down, b_down, g_final):
    assert g_mix.shape[0] == 1, "single-layer step"
    w = _prep_weights(g_mix, w_in, conv_w, conv_b, conv_ln_g, conv_ln_b, q_norm_g, w_q_up, kv_norm_g, w_kv_up,
                      w_out, g_mem_q, g_mem_kv, w_mq, w_mk, w_mv, w_mo, g_ffn, w_router, b_router,
                      w_gate, b_gate, w_up, b_up, w_down, b_down, g_final)
    b_p, s_p, _ = x_prompt.shape
    b_s, t_s, _ = x_sample.shape
    past = page_table.shape[1] * PAGE_SIZE

    mk, mv, mk_wide, mv_wide = _mem_kv(mem_prompt.reshape(-1, D_MODEL), w)
    conv0 = jnp.zeros((b_p, CONV_WIDTH - 1, CONV_CH), F32)
    (x_p, conv_out_p, attn_p), conv_p, ckv_p, kpe_p = _front(
        x_prompt, conv0, None, None, jnp.arange(s_p, dtype=jnp.int32), w, None)
    (x1_s, om_s), conv_s, ckv_s, kpe_s = _front(
        x_sample, state_conv[0], cache_mem_k[0].reshape(-1, MEM_HD), cache_mem_v[0].reshape(-1, MEM_HD),
        past + jnp.arange(t_s, dtype=jnp.int32), w,
        (page_table, cache_ckv[0], jnp.swapaxes(cache_kpe[0], 1, 2)))

    x2, h3s, idx128, gate128, rank128, cnt128 = _post_b(
        x_p, conv_out_p, attn_p, mk_wide.reshape(b_p, MEM_TOKENS, -1), mv_wide.reshape(b_p, MEM_TOKENS, -1),
        x1_s, om_s, w)
    y_p, y_s = _moe_and_final(x2, h3s, idx128, gate128, rank128, cnt128, w, x_p.shape[0] // TOKEN_TILE)

    mem_shape = (1, b_p, MEM_TOKENS, MEM_HEADS, MEM_HD)
    return (y_p.reshape(b_p, s_p, D_MODEL), y_s.reshape(b_s, t_s, D_MODEL), ckv_p[None], kpe_p[None],
            mk.reshape(mem_shape), mv.reshape(mem_shape), conv_p[None], ckv_s[None], kpe_s[None], conv_s[None])
```

```python
import functools

import jax
import jax.numpy as jnp
from jax import lax
from jax.experimental import pallas as pl
from jax.experimental.pallas import tpu as pltpu

F32 = jnp.float32
BF16 = jnp.bfloat16

D_MODEL = 1024
PAGE_SIZE = 128
CONV_CH = 512
CONV_WIDTH = 31
N_HEADS = 8
QK_NOPE = 64
QK_ROPE = 32
V_DIM = 64
Q_RANK = 384
KV_RANK = 256
ROPE_THETA = 10000.0
MLA_SCALE = (QK_NOPE + QK_ROPE) ** -0.5
LOG2_E = 1.4426950408889634
MEM_TOKENS = 256
MEM_HEADS = 4
MEM_HD = 128
MEM_SCALE = MEM_HD ** -0.5
N_EXPERTS = 32
TOP_K = 4
SWIGLU_LIMIT = 7.0
SWIGLU_ALPHA = 1.702
EPS = 1e-6
NEG_INF = -1e30

LANES = 128
SUBLANES = 8
HEAD_PAD = 128
C_VAL, C_GATE, C_Q, C_CKV, C_KPE, C_KPE_SW, C_END = 0, 512, 1024, 1408, 1664, 1792, 1920
TOKEN_TILE = 256
ATTN_TILE = 512
ATTN_HEADS = 4
CONV_HALO = 32
CONV_CHUNK = 32
MEM_ATTN_ROWS = 32
PAGES_PER_CHUNK = 32
SLAB = 8
MOE_ROWS = 512
CAST_ROWS = 32
VMEM_LIMIT = 48 * 1024 * 1024


def _rms(x, g):
    return x * lax.rsqrt(jnp.mean(x * x, axis=-1, keepdims=True) + EPS) * g


def _dot(a, b):
    return jnp.dot(a, b, preferred_element_type=F32)


def _dot_t(a, b):
    return lax.dot_general(a, b, (((1,), (1,)), ((), ())), preferred_element_type=F32)


def _params(*sem, vmem=None):
    return pltpu.CompilerParams(dimension_semantics=sem, vmem_limit_bytes=vmem)


def _const_spec(shape):
    nd = len(shape)
    return pl.BlockSpec(shape, lambda *_: (0,) * nd)


def _inproj_common(x_ref, gmix_ref, win_ref, qg_ref, wq_ref, wqsw_ref, kvg_ref, tab_ref,
                   u_ref, ckv_ref, kpe_ref, after_u=None, after_q=None):
    h = _rms(x_ref[...], gmix_ref[...]).astype(BF16)
    glu = _dot(h, win_ref[:, C_VAL:C_Q])
    u = glu[:, :CONV_CH] * jax.nn.sigmoid(glu[:, CONV_CH:])
    u_ref[...] = u
    proj = _dot(h, win_ref[:, C_Q:C_END])
    if after_u is not None:
        after_u(u)
    qn = _rms(proj[:, 0:C_CKV - C_Q], qg_ref[...]).astype(BF16)
    ckv = _rms(proj[:, C_CKV - C_Q:C_KPE - C_Q], kvg_ref[...])
    ckv_ref[...] = ckv
    cq, sq, ck, sk = tab_ref[0], tab_ref[1], tab_ref[2], tab_ref[3]
    kpe_rot = proj[:, C_KPE - C_Q:C_KPE_SW - C_Q] * ck + proj[:, C_KPE_SW - C_Q:C_END - C_Q] * sk
    kpe_ref[...] = kpe_rot[:, :QK_ROPE]
    q = _dot(qn, wq_ref[...])
    qs = _dot(qn, wqsw_ref[...])
    if after_q is not None:
        after_q()
    q_heads = []
    for hd in range(N_HEADS):
        sl = slice(hd * HEAD_PAD, (hd + 1) * HEAD_PAD)
        q_heads.append(q[:, sl] * cq + qs[:, sl] * sq)
    return ckv, kpe_rot, q_heads


def _conv_fill_window(win_ref, halo, tile):
    tt = tile.shape[0]
    win_ref[0, 0:CONV_HALO, :] = halo
    win_ref[0, CONV_HALO:CONV_HALO + tt, :] = tile
    n = tt + CONV_HALO - SUBLANES
    for s in range(1, SUBLANES):
        win_ref[s, 0:n, :] = win_ref[0, s:s + n, :]


def _conv_chunk(win_ref, c, w_ref, b_ref, g_ref, lb_ref):
    acc = None
    for j in range(CONV_WIDTH):
        q, s = divmod(c * CONV_CHUNK + 2 + j, SUBLANES)
        term = win_ref[s, q * SUBLANES:q * SUBLANES + CONV_CHUNK, :] * w_ref[j:j + 1, :]
        acc = term if acc is None else acc + term
    return _ln_swish(acc + b_ref[...], g_ref[...], lb_ref[...])


def _inproj_prompt_kernel(tiles_per_seq, x_ref, gmix_ref, win_ref, qg_ref, wq_ref, wqsw_ref, kvg_ref, tab_ref,
                          wuk_ref, wuv_ref, vone_ref, prev_ref, cw_ref, cb_ref, cg_ref, clb_ref,
                          u_ref, ckv_ref, kpe_ref, q_ref, k_ref, v_ref, conv_ref, win_scr, carry_scr):
    tm = x_ref.shape[0]
    n_chunks = tm // CONV_CHUNK
    first = pl.program_id(0) % tiles_per_seq == 0

    def conv_chunks(lo, hi):
        for c in range(lo, hi):
            y = _conv_chunk(win_scr, c, cw_ref, cb_ref, cg_ref, clb_ref)
            conv_ref[c * CONV_CHUNK:(c + 1) * CONV_CHUNK, :] = y.astype(BF16)

    def after_u(u):
        _conv_fill_window(win_scr, jnp.where(first, prev_ref[0], carry_scr[...]), u)
        carry_scr[...] = u[tm - CONV_HALO:, :]
        conv_chunks(0, n_chunks * 3 // 4)

    ckv, kpe_rot, q_heads = _inproj_common(x_ref, gmix_ref, win_ref, qg_ref, wq_ref, wqsw_ref,
                                           kvg_ref, tab_ref, u_ref, ckv_ref, kpe_ref,
                                           after_u=after_u, after_q=lambda: conv_chunks(n_chunks * 3 // 4, n_chunks))
    ckv_b = ckv.astype(BF16)
    k_nope = _dot(ckv_b, wuk_ref[...])
    for hd in range(N_HEADS):
        sl = slice(hd * HEAD_PAD, (hd + 1) * HEAD_PAD)
        q_ref[:, sl] = q_heads[hd].astype(BF16)
        k_ref[:, sl] = (k_nope[:, sl] + kpe_rot).astype(BF16)
    v_ref[...] = (_dot(ckv_b, wuv_ref[...]) + vone_ref[...]).astype(BF16)


def _inproj_sample_kernel(x_ref, gmix_ref, win_ref, qg_ref, wq_ref, wqsw_ref, kvg_ref, tab_ref,
                          wukt_ref,
                          u_ref, ckv_ref, kpe_ref, qlat_ref, qpe_ref):
    _, _, q_heads = _inproj_common(x_ref, gmix_ref, win_ref, qg_ref, wq_ref, wqsw_ref,
                                   kvg_ref, tab_ref, u_ref, ckv_ref, kpe_ref)
    for hd in range(N_HEADS):
        qlat_ref[hd] = _dot(q_heads[hd].astype(BF16), wukt_ref[hd])
        qpe_ref[hd] = q_heads[hd][:, :QK_ROPE]


def _inproj(x2d, tab, w, sample, prev_pad=None):
    t = x2d.shape[0]
    tm = TOKEN_TILE
    n_tab = tab.shape[1] // tm
    row = lambda n: pl.BlockSpec((tm, n), lambda i: (i, 0))
    in_specs = [row(D_MODEL), _const_spec((1, D_MODEL)), _const_spec((D_MODEL, C_END)),
                _const_spec((1, Q_RANK)), _const_spec((Q_RANK, N_HEADS * HEAD_PAD)),
                _const_spec((Q_RANK, N_HEADS * HEAD_PAD)), _const_spec((1, KV_RANK)),
                pl.BlockSpec((4, tm, LANES), lambda i: (0, i % n_tab, 0))]
    args = [x2d, w['g_mix'], w['w_in_ext'], w['q_norm_g'], w['wq'], w['wq_sw'], w['kv_norm_g'], tab]
    out_shape = [jax.ShapeDtypeStruct((t, CONV_CH), F32), jax.ShapeDtypeStruct((t, KV_RANK), F32),
                 jax.ShapeDtypeStruct((t, QK_ROPE), F32)]
    out_specs = [row(CONV_CH), row(KV_RANK), row(QK_ROPE)]
    scratch = []
    if sample:
        body = _inproj_sample_kernel
        in_specs += [_const_spec((N_HEADS, HEAD_PAD, KV_RANK))]
        args += [w['wukt_pad']]
        out_shape += [jax.ShapeDtypeStruct((N_HEADS, t, KV_RANK), F32),
                      jax.ShapeDtypeStruct((N_HEADS, t, QK_ROPE), F32)]
        out_specs += [pl.BlockSpec((N_HEADS, tm, KV_RANK), lambda i: (0, i, 0)),
                      pl.BlockSpec((N_HEADS, tm, QK_ROPE), lambda i: (0, i, 0))]
    else:
        tiles_per_seq = t // prev_pad.shape[0] // tm
        body = functools.partial(_inproj_prompt_kernel, tiles_per_seq)
        in_specs += [_const_spec((KV_RANK, N_HEADS * HEAD_PAD)), _const_spec((KV_RANK, N_HEADS * HEAD_PAD)),
                     _const_spec((1, N_HEADS * HEAD_PAD)),
                     pl.BlockSpec((1, CONV_HALO, CONV_CH), lambda i: (i // tiles_per_seq, 0, 0)),
                     _const_spec((CONV_HALO, CONV_CH)), _const_spec((1, CONV_CH)),
                     _const_spec((1, CONV_CH)), _const_spec((1, CONV_CH))]
        args += [w['wuk_pad'], w['wuv_slot'], w['v_ones'], prev_pad,
                 w['conv_w'], w['conv_b'], w['conv_ln_g'], w['conv_ln_b']]
        out_shape += [jax.ShapeDtypeStruct((t, N_HEADS * HEAD_PAD), BF16)] * 3
        out_shape += [jax.ShapeDtypeStruct((t, CONV_CH), BF16)]
        out_specs += [row(N_HEADS * HEAD_PAD)] * 3 + [row(CONV_CH)]
        scratch = [pltpu.VMEM((SUBLANES, tm + CONV_HALO, CONV_CH), F32), pltpu.VMEM((CONV_HALO, CONV_CH), F32)]
    return pl.pallas_call(
        body, grid=(t // tm,), in_specs=in_specs, out_specs=out_specs, out_shape=out_shape,
        scratch_shapes=scratch,
        compiler_params=_params("parallel" if sample else "arbitrary", vmem=VMEM_LIMIT),
        name="inproj_sample" if sample else "inproj_prompt")(*args)


def _ln_swish(conv, g, b):
    mu = jnp.mean(conv, axis=-1, keepdims=True)
    xc = conv - mu
    var = jnp.mean(xc * xc, axis=-1, keepdims=True)
    y = xc * lax.rsqrt(var + EPS) * g + b
    return y * jax.nn.sigmoid(y)


def _conv_sample_kernel(win_ref, w_ref, b_ref, g_ref, lb_ref, o_ref):
    t = o_ref.shape[1]
    acc = win_ref[:, 2:2 + t, :] * w_ref[0:1, :]
    for j in range(1, CONV_WIDTH):
        acc = acc + win_ref[:, 2 + j:2 + j + t, :] * w_ref[j:j + 1, :]
    y = _ln_swish(acc + b_ref[...], g_ref[...], lb_ref[...])
    o_ref[...] = y


def _conv_sample(upad, w):
    b, s_pad, _ = upad.shape
    t = s_pad - CONV_HALO
    bb = 8
    return pl.pallas_call(
        _conv_sample_kernel, grid=(b // bb,),
        in_specs=[pl.BlockSpec((bb, s_pad, CONV_CH), lambda i: (i, 0, 0)),
                  _const_spec((CONV_HALO, CONV_CH)), _const_spec((1, CONV_CH)),
                  _const_spec((1, CONV_CH)), _const_spec((1, CONV_CH))],
        out_specs=pl.BlockSpec((bb, t, CONV_CH), lambda i: (i, 0, 0)),
        out_shape=jax.ShapeDtypeStruct((b, t, CONV_CH), F32),
        compiler_params=_params("parallel"),
        name="conv_sample")(upad, w['conv_w'], w['conv_b'], w['conv_ln_g'], w['conv_ln_b'])


def _attn_prompt_kernel(q_ref, k_ref, v_ref, o_ref, s_ref, m_ref):
    tq = q_ref.shape[1]
    qi = pl.program_id(2)
    nh = ATTN_HEADS
    head = lambda hd: slice(hd * HEAD_PAD, (hd + 1) * HEAD_PAD)
    rows = lax.broadcasted_iota(jnp.int32, (tq, tq), 0)
    cols = lax.broadcasted_iota(jnp.int32, (tq, tq), 1)
    lane = lax.broadcasted_iota(jnp.int32, (tq, LANES), 1)

    def scores(j, m_all, masked):
        start = pl.multiple_of(j * tq, tq)
        for hd in range(nh):
            s = _dot_t(q_ref[0, :, head(hd)], k_ref[0, pl.ds(start, tq), head(hd)])
            if masked:
                s = jnp.where(cols <= rows, s, NEG_INF)
            s_ref[hd, j] = s
            m_all = jnp.where(lane == hd, jnp.maximum(m_all, jnp.max(s, axis=-1, keepdims=True)), m_all)
        return m_all

    m_all = jnp.full((tq, LANES), NEG_INF, F32)
    m_all = lax.fori_loop(0, qi, functools.partial(scores, masked=False), m_all)
    m_all = scores(qi, m_all, True)
    for hd in range(nh):
        m_ref[hd] = jnp.broadcast_to(m_all[:, hd:hd + 1], (tq, LANES))

    def values(j, accs):
        start = pl.multiple_of(j * tq, tq)
        out = []
        for hd in range(nh):
            m = m_ref[hd]
            p = jnp.exp2(s_ref[hd, j] - jnp.concatenate([m] * (tq // LANES), axis=1)).astype(BF16)
            out.append(accs[hd] + _dot(p, v_ref[0, pl.ds(start, tq), head(hd)]))
        return tuple(out)

    accs = tuple(jnp.zeros((tq, HEAD_PAD), F32) for _ in range(nh))
    accs = lax.fori_loop(0, qi + 1, values, accs)
    for hp in range(nh // 2):
        even, odd = accs[2 * hp], accs[2 * hp + 1]
        o_even = even / even[:, V_DIM:V_DIM + 1]
        o_odd = odd / odd[:, 0:1]
        o_ref[0, :, hp * LANES:(hp + 1) * LANES] = jnp.where(lane < V_DIM, o_even, o_odd).astype(BF16)


def _attn_prompt(q, k, v):
    b, s, _ = q.shape
    tq = ATTN_TILE
    nh = ATTN_HEADS
    return pl.pallas_call(
        _attn_prompt_kernel, grid=(b, N_HEADS // nh, s // tq),
        in_specs=[pl.BlockSpec((1, tq, nh * HEAD_PAD), lambda bi, hq, qi: (bi, qi, hq)),
                  pl.BlockSpec((1, s, nh * HEAD_PAD), lambda bi, hq, qi: (bi, 0, hq)),
                  pl.BlockSpec((1, s, nh * HEAD_PAD), lambda bi, hq, qi: (bi, 0, hq))],
        out_specs=pl.BlockSpec((1, tq, nh * V_DIM), lambda bi, hq, qi: (bi, qi, hq)),
        out_shape=jax.ShapeDtypeStruct((b, s, N_HEADS * V_DIM), BF16),
        scratch_shapes=[pltpu.VMEM((nh, s // tq, tq, tq), F32), pltpu.VMEM((nh, tq, LANES), F32)],
        compiler_params=_params("parallel", "parallel", "arbitrary", vmem=VMEM_LIMIT),
        name="attn_prompt")(q, k, v)


def _attn_sample_kernel(pt_ref, ql_ref, qp_ref, cn_ref, kn_ref, ckv_hbm, kpe_hbm, o_ref,
                        ckv_buf, kpe_buf, sem):
    b = pl.program_id(0)
    n_pages = kpe_buf.shape[1]
    n_chunks = n_pages // PAGES_PER_CHUNK
    chunk = PAGES_PER_CHUNK * PAGE_SIZE
    t_new = cn_ref.shape[1]
    rows_q = N_HEADS * t_new

    def fetch(batch, slot):
        def one(p, _):
            page = pt_ref[batch * n_pages + p]
            rows = pl.ds(pl.multiple_of(p * PAGE_SIZE, PAGE_SIZE), PAGE_SIZE)
            pltpu.make_async_copy(ckv_hbm.at[page], ckv_buf.at[slot, rows], sem.at[0, slot]).start()
            pltpu.make_async_copy(kpe_hbm.at[page], kpe_buf.at[slot, p], sem.at[1, slot]).start()
            return 0
        lax.fori_loop(0, n_pages, one, 0, unroll=4)

    slot = b % 2

    @pl.when(b == 0)
    def _():
        fetch(0, 0)

    @pl.when(b + 1 < pl.num_programs(0))
    def _():
        fetch(b + 1, 1 - slot)

    pltpu.make_async_copy(ckv_buf.at[slot], ckv_buf.at[slot], sem.at[0, slot]).wait()
    pltpu.make_async_copy(kpe_buf.at[slot], kpe_buf.at[slot], sem.at[1, slot]).wait()

    ql = ql_ref[:, 0].reshape(rows_q, KV_RANK).astype(BF16)
    qp = qp_ref[:, 0].reshape(rows_q, QK_ROPE).astype(BF16)

    pad = PAGE_SIZE - t_new
    kc_new = jnp.concatenate([cn_ref[0], jnp.zeros((pad, KV_RANK), F32)], axis=0).astype(BF16)
    kp_new = jnp.concatenate([kn_ref[0], jnp.zeros((pad, QK_ROPE), F32)], axis=0).astype(BF16)
    s_new = _dot_t(ql, kc_new) + _dot_t(qp, kp_new)
    t_q = lax.broadcasted_iota(jnp.int32, s_new.shape, 0) % t_new
    cols = lax.broadcasted_iota(jnp.int32, s_new.shape, 1)
    s_new = jnp.where(cols <= t_q, s_new, NEG_INF)
    def part(s, values):
        m = jnp.max(s, axis=-1, keepdims=True)
        p = jnp.exp(s - m)
        return m, jnp.sum(p, axis=-1, keepdims=True), _dot(p.astype(BF16), values)

    def scores(c):
        kc = ckv_buf[slot, c * chunk:(c + 1) * chunk, :].astype(BF16)
        kpt = jnp.concatenate([kpe_buf[slot, c * PAGES_PER_CHUNK + i] for i in range(PAGES_PER_CHUNK)],
                              axis=1).astype(BF16)
        return _dot_t(ql, kc) + _dot(qp, kpt), kc

    parts = [part(s_new, kc_new)]
    nxt = scores(0)
    for c in range(n_chunks):
        cur = nxt
        if c + 1 < n_chunks:
            nxt = scores(c + 1)
        parts.append(part(*cur))
    m = parts[0][0]
    for mp, _, _ in parts[1:]:
        m = jnp.maximum(m, mp)
    l = jnp.zeros_like(m)
    acc = jnp.zeros((rows_q, KV_RANK), F32)
    for mp, lp, ap in parts:
        w = jnp.exp(mp - m)
        l = l + w * lp
        acc = acc + w * ap
    o_ref[:, 0] = (acc / l).reshape(N_HEADS, t_new, KV_RANK)


def _attn_sample(page_table, qlat, qpe, ckv_new, kpe_new, cache_ckv, cache_kpe_t):
    bs, n_pages = page_table.shape
    t_new = ckv_new.shape[1]
    qlat4 = qlat.reshape(N_HEADS, bs, t_new, KV_RANK)
    qpe4 = qpe.reshape(N_HEADS, bs, t_new, QK_ROPE)
    past = n_pages * PAGE_SIZE
    in_specs = [pl.BlockSpec((N_HEADS, 1, t_new, KV_RANK), lambda b, pt: (0, b, 0, 0)),
                pl.BlockSpec((N_HEADS, 1, t_new, QK_ROPE), lambda b, pt: (0, b, 0, 0)),
                pl.BlockSpec((1, t_new, KV_RANK), lambda b, pt: (b, 0, 0)),
                pl.BlockSpec((1, t_new, QK_ROPE), lambda b, pt: (b, 0, 0)),
                pl.BlockSpec(memory_space=pl.ANY), pl.BlockSpec(memory_space=pl.ANY)]
    grid_spec = pltpu.PrefetchScalarGridSpec(
        num_scalar_prefetch=1, grid=(bs,), in_specs=in_specs,
        out_specs=pl.BlockSpec((N_HEADS, 1, t_new, KV_RANK), lambda b, pt: (0, b, 0, 0)),
        scratch_shapes=[pltpu.VMEM((2, past, KV_RANK), F32),
                        pltpu.VMEM((2, n_pages, QK_ROPE, PAGE_SIZE), F32),
                        pltpu.SemaphoreType.DMA((2, 2))])
    o = pl.pallas_call(
        _attn_sample_kernel, grid_spec=grid_spec,
        out_shape=jax.ShapeDtypeStruct((N_HEADS, bs, t_new, KV_RANK), F32),
        compiler_params=_params("arbitrary", vmem=VMEM_LIMIT),
        name="attn_sample")(page_table.reshape(-1), qlat4, qpe4, ckv_new, kpe_new, cache_ckv, cache_kpe_t)
    return o.reshape(N_HEADS, bs * t_new, KV_RANK)


def _post_a_kernel(x_ref, conv_ref, olat_ref, wuv_ref, wout_ref, g_ref, wmq_ref, x1_ref, qm_ref):
    attn = None
    for hd in range(N_HEADS):
        d = _dot(olat_ref[hd].astype(BF16), wuv_ref[hd])
        attn = d if attn is None else attn + d
    mix = (_dot(conv_ref[...].astype(BF16), wout_ref[0:CONV_CH, :])
           + _dot(attn.astype(BF16), wout_ref[CONV_CH:, :]))
    x1 = x_ref[...] + mix
    x1_ref[...] = x1
    qm_ref[...] = _dot(_rms(x1, g_ref[...]).astype(BF16), wmq_ref[...])


def _post_a(x2d, conv2d, o_lat, w):
    t = x2d.shape[0]
    tm = TOKEN_TILE
    row = lambda n: pl.BlockSpec((tm, n), lambda i: (i, 0))
    mem_w = MEM_HEADS * MEM_HD
    return pl.pallas_call(
        _post_a_kernel, grid=(t // tm,),
        in_specs=[row(D_MODEL), row(CONV_CH), pl.BlockSpec((N_HEADS, tm, KV_RANK), lambda i: (0, i, 0)),
                  _const_spec((N_HEADS, KV_RANK, N_HEADS * V_DIM)),
                  _const_spec((D_MODEL, D_MODEL)), _const_spec((1, D_MODEL)), _const_spec((D_MODEL, mem_w))],
        out_specs=[row(D_MODEL), row(mem_w)],
        out_shape=[jax.ShapeDtypeStruct((t, D_MODEL), F32), jax.ShapeDtypeStruct((t, mem_w), F32)],
        compiler_params=_params("parallel", vmem=VMEM_LIMIT),
        name="post_a_sample")(x2d, conv2d, o_lat, w['wuv_pad'], w['w_out'], w['g_mem_q'], w['w_mq'])


def _mem_kv_kernel(m_ref, g_ref, wk_ref, wv_ref, k_ref, v_ref, kw_ref, vw_ref):
    tm = m_ref.shape[0]
    m = _rms(m_ref[...], g_ref[...]).astype(BF16)
    k = _dot(m, wk_ref[...])
    v = _dot(m, wv_ref[...])
    kw_ref[...] = k
    vw_ref[...] = v
    for hd in range(MEM_HEADS):
        sl = slice(hd * MEM_HD, (hd + 1) * MEM_HD)
        k_ref[pl.ds(hd, tm, stride=MEM_HEADS), :] = k[:, sl]
        v_ref[pl.ds(hd, tm, stride=MEM_HEADS), :] = v[:, sl]


def _mem_kv(mem2d, w):
    t = mem2d.shape[0]
    tm = TOKEN_TILE
    mem_w = MEM_HEADS * MEM_HD
    rows = pl.BlockSpec((tm * MEM_HEADS, MEM_HD), lambda i: (i, 0))
    wide = pl.BlockSpec((tm, mem_w), lambda i: (i, 0))
    return pl.pallas_call(
        _mem_kv_kernel, grid=(t // tm,),
        in_specs=[pl.BlockSpec((tm, D_MODEL), lambda i: (i, 0)), _const_spec((1, D_MODEL)),
                  _const_spec((D_MODEL, mem_w)), _const_spec((D_MODEL, mem_w))],
        out_specs=[rows, rows, wide, wide],
        out_shape=[jax.ShapeDtypeStruct((t * MEM_HEADS, MEM_HD), F32)] * 2
                  + [jax.ShapeDtypeStruct((t, mem_w), F32)] * 2,
        compiler_params=_params("parallel"),
        name="mem_kv")(mem2d, w['g_mem_kv'], w['w_mk'], w['w_mv'])


def _mem_attn_rows_kernel(q_ref, k_ref, v_ref, o_ref):
    bb, tq, _ = q_ref.shape
    rows = MEM_TOKENS * MEM_HEADS
    head = lambda hd: slice(hd * MEM_HD, (hd + 1) * MEM_HD)
    q_head = lax.broadcasted_iota(jnp.int32, (MEM_HEADS * tq, rows), 0) // tq
    k_head = lax.broadcasted_iota(jnp.int32, (MEM_HEADS * tq, rows), 1) % MEM_HEADS
    own = q_head == k_head
    for bi in range(bb):
        q = jnp.concatenate([q_ref[bi, :, head(hd)] for hd in range(MEM_HEADS)], axis=0).astype(BF16)
        k = k_ref[bi * rows:(bi + 1) * rows, :].astype(BF16)
        v = v_ref[bi * rows:(bi + 1) * rows, :].astype(BF16)
        s = jnp.where(own, _dot_t(q, k) * MEM_SCALE, NEG_INF)
        p = jnp.exp(s - jnp.max(s, axis=-1, keepdims=True))
        p = p / jnp.sum(p, axis=-1, keepdims=True)
        o = _dot(p.astype(BF16), v)
        for hd in range(MEM_HEADS):
            o_ref[bi, :, head(hd)] = o[hd * tq:(hd + 1) * tq, :]


def _mem_attn_rows(qm, mem_k, mem_v):
    b, s, mem_w = qm.shape
    bb = max(1, MEM_ATTN_ROWS // s)
    kv = pl.BlockSpec((bb * MEM_TOKENS * MEM_HEADS, MEM_HD), lambda bi: (bi, 0))
    return pl.pallas_call(
        _mem_attn_rows_kernel, grid=(b // bb,),
        in_specs=[pl.BlockSpec((bb, s, mem_w), lambda bi: (bi, 0, 0)), kv, kv],
        out_specs=pl.BlockSpec((bb, s, mem_w), lambda bi: (bi, 0, 0)),
        out_shape=jax.ShapeDtypeStruct((b, s, mem_w), F32),
        compiler_params=_params("parallel"),
        name="mem_attn_rows")(qm, mem_k, mem_v)


def _slab_rows(j, n):
    return pl.ds(j, n, stride=SLAB)


def _post_b_kernel(n_p, x_ref, conv_ref, attn_ref, mk_ref, mv_ref, wout_ref, gq_ref, wmq_ref,
                   x1s_ref, oms_ref, wmo_ref, g_ref, wr_ref, br_ref,
                   x2_ref, h3_ref, idx_ref, gate_ref, rank_ref, cnt_ref, carry_ref, x1_scr, om_scr):
    tm = x2_ref.shape[0]
    i = pl.program_id(0)

    @pl.when(i == 0)
    def _():
        carry_ref[...] = jnp.zeros(carry_ref.shape, F32)

    @pl.when(i < n_p)
    def _():
        mix = _dot(conv_ref[...], wout_ref[0:CONV_CH, :]) + _dot(attn_ref[...], wout_ref[CONV_CH:, :])
        x1p = x_ref[...] + mix
        x1_scr[...] = x1p
        qm = _dot(_rms(x1p, gq_ref[...]).astype(BF16), wmq_ref[...])
        for hd in range(MEM_HEADS):
            sl = slice(hd * MEM_HD, (hd + 1) * MEM_HD)
            s = _dot_t(qm[:, sl].astype(BF16), mk_ref[0, :, sl].astype(BF16)) * MEM_SCALE
            p = jnp.exp(s - jnp.max(s, axis=-1, keepdims=True))
            p = p / jnp.sum(p, axis=-1, keepdims=True)
            om_scr[:, sl] = _dot(p.astype(BF16), mv_ref[0, :, sl].astype(BF16))

    @pl.when(i >= n_p)
    def _():
        x1_scr[...] = x1s_ref[...]
        om_scr[...] = oms_ref[...]

    x1 = x1_scr[...]
    om = om_scr[...]
    x2 = x1 + _dot(om.astype(BF16), wmo_ref[...])
    x2_ref[...] = x2
    h3 = _rms(x2, g_ref[...])
    for j in range(SLAB):
        h3_ref[_slab_rows(j, tm), :] = h3[:, j * LANES:(j + 1) * LANES]
    logits = _dot(h3.astype(BF16), wr_ref[...]) + br_ref[...]
    lane = lax.broadcasted_iota(jnp.int32, logits.shape, 1)
    lane_f = lane.astype(F32)
    vals, hots = [], []
    idx_out = jnp.zeros(logits.shape, F32)
    for kk in range(TOP_K):
        mx = jnp.max(logits, axis=-1, keepdims=True)
        first = jnp.min(jnp.where(logits == mx, lane_f, float(LANES)), axis=-1, keepdims=True)
        hot = lane_f == first
        logits = jnp.where(hot, -jnp.inf, logits)
        vals.append(mx)
        hots.append(hot)
        idx_out = jnp.where(lane == kk, first, idx_out)
    exps = [jnp.exp(v - vals[0]) for v in vals]
    denom = exps[0] + exps[1] + exps[2] + exps[3]
    chosen = jnp.zeros(logits.shape, F32)
    gate_out = jnp.zeros(logits.shape, F32)
    for kk in range(TOP_K):
        chosen = chosen + hots[kk].astype(F32)
        gate_out = jnp.where(lane == kk, exps[kk] / denom, gate_out)
    r_i = lax.broadcasted_iota(jnp.int32, (tm, tm), 0)
    c_i = lax.broadcasted_iota(jnp.int32, (tm, tm), 1)
    tril = (c_i < r_i).astype(BF16)
    before = _dot(tril, chosen.astype(BF16)) + carry_ref[...]
    rank_out = jnp.zeros(logits.shape, F32)
    for kk in range(TOP_K):
        rk = jnp.sum(jnp.where(hots[kk], before, 0.0), axis=-1, keepdims=True)
        rank_out = jnp.where(lane == kk, rk, rank_out)
    carry = carry_ref[...] + jnp.sum(chosen, axis=0, keepdims=True)
    carry_ref[...] = carry
    idx_ref[...] = idx_out.astype(jnp.int32)
    gate_ref[...] = gate_out
    rank_ref[...] = rank_out.astype(jnp.int32)
    cnt_ref[...] = carry.astype(jnp.int32)


def _post_b(x_p, conv_p, attn_p, mk_wide, mv_wide, x1_s, om_s, w):
    tm = TOKEN_TILE
    n_p, n_s = x_p.shape[0] // tm, x1_s.shape[0] // tm
    tiles_per_seq = n_p // mk_wide.shape[0]
    t = (n_p + n_s) * tm
    mem_w = MEM_HEADS * MEM_HD
    row = lambda n: pl.BlockSpec((tm, n), lambda i: (i, 0))
    row_p = lambda n: pl.BlockSpec((tm, n), lambda i: (jnp.minimum(i, n_p - 1), 0))
    row_s = lambda n: pl.BlockSpec((tm, n), lambda i: (jnp.maximum(i - n_p, 0), 0))
    mem = pl.BlockSpec((1, MEM_TOKENS, mem_w), lambda i: (jnp.minimum(i, n_p - 1) // tiles_per_seq, 0, 0))
    return pl.pallas_call(
        functools.partial(_post_b_kernel, n_p), grid=(n_p + n_s,),
        in_specs=[row_p(D_MODEL), row_p(CONV_CH), row_p(N_HEADS * V_DIM), mem, mem,
                  _const_spec((D_MODEL, D_MODEL)), _const_spec((1, D_MODEL)), _const_spec((D_MODEL, mem_w)),
                  row_s(D_MODEL), row_s(mem_w),
                  _const_spec((mem_w, D_MODEL)), _const_spec((1, D_MODEL)),
                  _const_spec((D_MODEL, LANES)), _const_spec((1, LANES))],
        out_specs=[row(D_MODEL), pl.BlockSpec((tm * SLAB, LANES), lambda i: (i, 0)),
                   row(LANES), row(LANES), row(LANES), _const_spec((1, LANES))],
        out_shape=[jax.ShapeDtypeStruct((t, D_MODEL), F32), jax.ShapeDtypeStruct((t * SLAB, LANES), F32),
                   jax.ShapeDtypeStruct((t, LANES), jnp.int32), jax.ShapeDtypeStruct((t, LANES), F32),
                   jax.ShapeDtypeStruct((t, LANES), jnp.int32), jax.ShapeDtypeStruct((1, LANES), jnp.int32)],
        scratch_shapes=[pltpu.VMEM((1, LANES), F32), pltpu.VMEM((tm, D_MODEL), F32), pltpu.VMEM((tm, mem_w), F32)],
        compiler_params=_params("arbitrary", vmem=VMEM_LIMIT),
        name="post_b")(x_p, conv_p, attn_p, mk_wide, mv_wide, w['w_out'], w['g_mem_q'], w['w_mq'],
                       x1_s, om_s, w['w_mo'], w['g_ffn'], w['w_router'], w['b_router'])


def _slab(row):
    return pl.ds(pl.multiple_of(row * SLAB, SLAB), SLAB)


def _dest_kernel(start_ref, idx_ref, rank_ref, dest_ref):
    idx = idx_ref[...]
    dest = rank_ref[...]
    for e in range(N_EXPERTS):
        dest = dest + jnp.where(idx == e, start_ref[e], 0)
    dest_ref[...] = dest


def _dest_rows(idx128, rank128, pad_start):
    t = idx128.shape[0]
    tm = TOKEN_TILE
    spec = pl.BlockSpec((tm, LANES), lambda i, *_: (i, 0))
    grid_spec = pltpu.PrefetchScalarGridSpec(num_scalar_prefetch=1, grid=(t // tm,),
                                             in_specs=[spec, spec], out_specs=spec)
    return pl.pallas_call(
        _dest_kernel, grid_spec=grid_spec, out_shape=jax.ShapeDtypeStruct((t, LANES), jnp.int32),
        compiler_params=_params("parallel"), name="moe_dest")(pad_start, idx128, rank128)


def _dispatch_kernel(lo_ref, hi_ref, dest_ref, h_hbm, xs_hbm, hbuf, zero_ref, sem_in, sem_out, sem_z):
    i = pl.program_id(0)
    n = pl.num_programs(0)
    rows_per_tile = hbuf.shape[1]
    tm = rows_per_tile // SLAB

    def load(tile, slot):
        rows = pl.ds(pl.multiple_of(tile * rows_per_tile, SLAB), rows_per_tile)
        return pltpu.make_async_copy(h_hbm.at[rows], hbuf.at[slot], sem_in.at[slot])

    def wait_scatter(slot):
        for _ in range(TOP_K):
            pltpu.make_async_copy(hbuf.at[0], xs_hbm.at[pl.ds(0, rows_per_tile)], sem_out.at[slot]).wait()

    @pl.when(i == 0)
    def _():
        load(0, 0).start()

    @pl.when(i + 1 < n)
    def _():
        load(i + 1, (i + 1) % 3).start()

    cur = i % 3
    load(i, cur).wait()

    def issue(t, _):
        for k in range(TOP_K):
            d = dest_ref[t * TOP_K + k]
            pltpu.make_async_copy(hbuf.at[cur, _slab(t)], xs_hbm.at[_slab(d)],
                                  sem_out.at[i % 2]).start(priority=k % 2)
        return 0

    lax.fori_loop(0, tm, issue, 0, unroll=4)

    @pl.when(i > 0)
    def _():
        wait_scatter((i - 1) % 2)

    @pl.when(i == n - 1)
    def _():
        wait_scatter(i % 2)
        zero_ref[...] = jnp.zeros(zero_ref.shape, F32)
        bm = zero_ref.shape[0] // SLAB
        for e in range(N_EXPERTS):
            lo = lo_ref[e]
            pad = hi_ref[e] - lo
            for bit in range(bm.bit_length() - 1):
                size = 1 << bit

                @pl.when((pad >> bit) & 1 == 1)
                def _():
                    first = lo + (pad & (size - 1))
                    rows = pl.ds(pl.multiple_of(first * SLAB, SLAB), size * SLAB)
                    cp = pltpu.make_async_copy(zero_ref.at[pl.ds(0, size * SLAB)], xs_hbm.at[rows], sem_z)
                    cp.start()
                    cp.wait()

        def tail(blk, _):
            rows = pl.ds(pl.multiple_of(blk * bm * SLAB, SLAB), bm * SLAB)
            cp = pltpu.make_async_copy(zero_ref, xs_hbm.at[rows], sem_z)
            cp.start()
            cp.wait()
            return 0

        lax.fori_loop(hi_ref[N_EXPERTS - 1] // bm, xs_hbm.shape[0] // (bm * SLAB), tail, 0)


def _dispatch(h3s, dest_flat, pad_lo, pad_hi, n_rows):
    tm = TOKEN_TILE
    t = h3s.shape[0] // SLAB
    grid_spec = pltpu.PrefetchScalarGridSpec(
        num_scalar_prefetch=2, grid=(t // tm,),
        in_specs=[pl.BlockSpec((tm * TOP_K,), lambda i, *_: (i,), memory_space=pltpu.SMEM),
                  pl.BlockSpec(memory_space=pl.ANY)],
        out_specs=pl.BlockSpec(memory_space=pl.ANY),
        scratch_shapes=[pltpu.VMEM((3, tm * SLAB, LANES), F32), pltpu.VMEM((MOE_ROWS * SLAB, LANES), F32),
                        pltpu.SemaphoreType.DMA((3,)), pltpu.SemaphoreType.DMA((2,)), pltpu.SemaphoreType.DMA])
    return pl.pallas_call(
        _dispatch_kernel, grid_spec=grid_spec,
        out_shape=jax.ShapeDtypeStruct((n_rows * SLAB, LANES), F32),
        compiler_params=pltpu.CompilerParams(dimension_semantics=("arbitrary",), has_side_effects=True),
        name="moe_dispatch")(pad_lo, pad_hi, dest_flat, h3s)


def _expert_kernel(src_ref, exp_ref, first_ref, valid_ref, next_ref, x_ref, wg_hbm, wu_hbm, wd_hbm,
                   bg_ref, bu_ref, bd_ref, y_ref, wf_ref, wb_ref, xb_ref, sem):
    b = pl.program_id(0)
    bm = xb_ref.shape[0]
    w_hbm = (wg_hbm, wu_hbm, wd_hbm)

    def fetch(e):
        for i in range(3):
            pltpu.make_async_copy(w_hbm[i].at[e], wf_ref.at[i], sem).start()

    @pl.when(b == 0)
    def _():
        fetch(exp_ref[0])

    @pl.when(first_ref[b] == 1)
    def _():
        for i in range(3):
            pltpu.make_async_copy(w_hbm[i].at[0], wf_ref.at[i], sem).wait()
        def cast(r, _):
            rows = pl.ds(pl.multiple_of(r * CAST_ROWS, CAST_ROWS), CAST_ROWS)
            for i in range(3):
                wb_ref[i, rows, :] = wf_ref[i, rows, :].astype(BF16)
            return 0

        lax.fori_loop(0, D_MODEL // CAST_ROWS, cast, 0)

        @pl.when(next_ref[b] >= 0)
        def _():
            fetch(next_ref[b])

    @pl.when(valid_ref[b] == 1)
    def _():
        for j in range(SLAB):
            xb_ref[:, j * LANES:(j + 1) * LANES] = x_ref[_slab_rows(j, bm), :].astype(BF16)
        x = xb_ref[...]
        g = _dot(x, wb_ref[0]) + bg_ref[0]
        u = _dot(x, wb_ref[1]) + bu_ref[0]
        g = jnp.minimum(g, SWIGLU_LIMIT)
        u = jnp.clip(u, -SWIGLU_LIMIT, SWIGLU_LIMIT)
        a = (u + 1.0) * (g * jax.nn.sigmoid(SWIGLU_ALPHA * g))
        y = _dot(a.astype(BF16), wb_ref[2]) + bd_ref[0]
        for j in range(SLAB):
            y_ref[_slab_rows(j, bm), :] = y[:, j * LANES:(j + 1) * LANES]

    @pl.when(valid_ref[b] == 0)
    def _():
        y_ref[...] = jnp.zeros(y_ref.shape, F32)


def _experts(xs, blk_src, blk_exp, blk_first, blk_valid, blk_next, w):
    bm = MOE_ROWS
    n_blocks = xs.shape[0] // (bm * SLAB)
    hbm = pl.BlockSpec(memory_space=pl.ANY)
    bspec = pl.BlockSpec((1, 1, D_MODEL), lambda b, src, ex, *_: (ex[b], 0, 0))
    grid_spec = pltpu.PrefetchScalarGridSpec(
        num_scalar_prefetch=5, grid=(n_blocks,),
        in_specs=[pl.BlockSpec((bm * SLAB, LANES), lambda b, src, *_: (src[b], 0)),
                  hbm, hbm, hbm, bspec, bspec, bspec],
        out_specs=pl.BlockSpec((bm * SLAB, LANES), lambda b, *_: (b, 0)),
        scratch_shapes=[pltpu.VMEM((3, D_MODEL, D_MODEL), F32), pltpu.VMEM((3, D_MODEL, D_MODEL), BF16),
                        pltpu.VMEM((bm, D_MODEL), BF16), pltpu.SemaphoreType.DMA])
    return pl.pallas_call(
        _expert_kernel, grid_spec=grid_spec,
        out_shape=jax.ShapeDtypeStruct(xs.shape, F32),
        compiler_params=_params("arbitrary", vmem=VMEM_LIMIT),
        name="moe_experts")(blk_src, blk_exp, blk_first, blk_valid, blk_next, xs,
                            w['w_gate'], w['w_up'], w['w_down'], w['b_gate'], w['b_up'], w['b_down'])


def _combine_kernel(n_p, dest_ref, dest_nx_ref, gate_ref, yb_hbm, x2_ref, g_ref,
                    yp_ref, ys_ref, rows_ref, ysum_ref, y_ref, sem):
    i = pl.program_id(0)
    tm = x2_ref.shape[0]

    def gather(d_ref, slot):
        def issue(t, _):
            for k in range(TOP_K):
                r = t * TOP_K + k
                pltpu.make_async_copy(yb_hbm.at[_slab(d_ref[r])], rows_ref.at[slot, _slab(r)],
                                      sem.at[slot]).start(priority=k % 2)
            return 0
        lax.fori_loop(0, tm, issue, 0, unroll=4)

    slot = i % 2

    @pl.when(i == 0)
    def _():
        gather(dest_ref, 0)

    @pl.when(i + 1 < pl.num_programs(0))
    def _():
        gather(dest_nx_ref, 1 - slot)

    for _ in range(TOP_K):
        pltpu.make_async_copy(yb_hbm.at[pl.ds(0, tm * SLAB)], rows_ref.at[slot, pl.ds(0, tm * SLAB)],
                              sem.at[slot]).wait()

    def token(t, _):
        acc = rows_ref[slot, _slab(t * TOP_K)] * gate_ref[t * TOP_K]
        for k in range(1, TOP_K):
            acc = acc + rows_ref[slot, _slab(t * TOP_K + k)] * gate_ref[t * TOP_K + k]
        ysum_ref[_slab(t)] = acc
        return 0

    lax.fori_loop(0, tm, token, 0, unroll=4)

    ss = jnp.zeros((tm, 1), F32)
    for j in range(SLAB):
        y = x2_ref[:, j * LANES:(j + 1) * LANES] + ysum_ref[_slab_rows(j, tm), :]
        y_ref[:, j * LANES:(j + 1) * LANES] = y
        ss = ss + jnp.sum(y * y, axis=-1, keepdims=True)
    out = y_ref[...] * lax.rsqrt(ss * (1.0 / D_MODEL) + EPS) * g_ref[...]

    @pl.when(i < n_p)
    def _():
        yp_ref[...] = out

    @pl.when(i >= n_p)
    def _():
        ys_ref[...] = out


def _combine(yb, dest_flat, gate_flat, x2, g_final, n_p):
    t = x2.shape[0]
    tm = TOKEN_TILE
    n = t // tm
    n_s = n - n_p
    cur = lambda i: (i,)
    nxt = lambda i: (jnp.minimum(i + 1, n - 1),)
    smem = lambda index_map: pl.BlockSpec((tm * TOP_K,), index_map, memory_space=pltpu.SMEM)
    return pl.pallas_call(
        functools.partial(_combine_kernel, n_p), grid=(n,),
        in_specs=[smem(cur), smem(nxt), smem(cur),
                  pl.BlockSpec(memory_space=pl.ANY),
                  pl.BlockSpec((tm, D_MODEL), lambda i: (i, 0)),
                  pl.BlockSpec((1, D_MODEL), lambda i: (0, 0))],
        out_specs=[pl.BlockSpec((tm, D_MODEL), lambda i: (jnp.minimum(i, n_p - 1), 0)),
                   pl.BlockSpec((tm, D_MODEL), lambda i: (jnp.maximum(i - n_p, 0), 0))],
        out_shape=[jax.ShapeDtypeStruct((n_p * tm, D_MODEL), F32), jax.ShapeDtypeStruct((n_s * tm, D_MODEL), F32)],
        scratch_shapes=[pltpu.VMEM((2, tm * TOP_K * SLAB, LANES), F32), pltpu.VMEM((tm * SLAB, LANES), F32),
                        pltpu.VMEM((tm, D_MODEL), F32), pltpu.SemaphoreType.DMA((2,))],
        compiler_params=_params("arbitrary", vmem=VMEM_LIMIT),
        name="moe_combine")(dest_flat, dest_flat, gate_flat, yb, x2, g_final)


def _moe_and_final(x2, h3s, idx128, gate128, rank128, cnt128, w, n_p):
    t = x2.shape[0]
    bm = MOE_ROWS
    n_assign = t * TOP_K
    n_blocks = (n_assign + N_EXPERTS * (bm - 1) + bm - 1) // bm
    counts = cnt128[0, :N_EXPERTS]
    padded = ((counts + bm - 1) // bm) * bm
    pad_end = jnp.cumsum(padded).astype(jnp.int32)
    pad_start = pad_end - padded
    dest_flat = _dest_rows(idx128, rank128, pad_start)[:, :TOP_K].reshape(-1)
    blk = jnp.arange(n_blocks, dtype=jnp.int32)
    blk_valid = (blk * bm < pad_end[-1]).astype(jnp.int32)
    blk_src = jnp.minimum(blk, jnp.maximum(pad_end[-1] // bm - 1, 0))
    blk_exp = jnp.sum((pad_end[None, :] <= (blk_src * bm)[:, None]).astype(jnp.int32), axis=1)
    blk_exp = jnp.minimum(blk_exp, N_EXPERTS - 1)
    blk_first = jnp.concatenate([jnp.ones((1,), jnp.int32),
                                 (blk_exp[1:] != blk_exp[:-1]).astype(jnp.int32)])
    later_start = (blk[None, :] > blk[:, None]) & (blk_first[None, :] == 1)
    next_pos = jnp.min(jnp.where(later_start, blk[None, :], n_blocks), axis=1)
    blk_next = jnp.where(next_pos < n_blocks, blk_exp[jnp.minimum(next_pos, n_blocks - 1)], -1)
    xs = _dispatch(h3s, dest_flat, pad_start + counts, pad_end, n_blocks * bm)
    yb = _experts(xs, blk_src, blk_exp, blk_first, blk_valid, blk_next.astype(jnp.int32), w)
    gate_flat = gate128[:, :TOP_K].reshape(-1)
    return _combine(yb, dest_flat, gate_flat, x2, w['g_final'], n_p)


def _swap_halves(wcols):
    half = QK_ROPE // 2
    return jnp.concatenate([-wcols[..., half:], wcols[..., :half]], axis=-1)


def _prep_weights(g_mix, w_in, conv_w, conv_b, conv_ln_g, conv_ln_b, q_norm_g, w_q_up, kv_norm_g, w_kv_up,
                  w_out, g_mem_q, g_mem_kv, w_mq, w_mk, w_mv, w_mo, g_ffn, w_router, b_router,
                  w_gate, b_gate, w_up, b_up, w_down, b_down, g_final):
    l = 0
    w_kpe = w_in[l][:, C_KPE:]
    zpad = jnp.zeros((D_MODEL, HEAD_PAD - QK_ROPE), F32)
    w_in_ext = jnp.concatenate([w_in[l][:, :C_KPE], w_kpe, zpad, _swap_halves(w_kpe), zpad], axis=1)
    wq3 = w_q_up[l].reshape(Q_RANK, N_HEADS, QK_NOPE + QK_ROPE)
    q_nope, q_rope = wq3[..., :QK_NOPE], wq3[..., QK_NOPE:]
    z32 = jnp.zeros((Q_RANK, N_HEADS, HEAD_PAD - QK_NOPE - QK_ROPE), F32)
    wq = jnp.concatenate([q_rope, q_nope, z32], axis=-1).reshape(Q_RANK, N_HEADS * HEAD_PAD)
    wq_sw = jnp.concatenate([_swap_halves(q_rope), jnp.zeros_like(q_nope), z32], axis=-1)
    wq_sw = wq_sw.reshape(Q_RANK, N_HEADS * HEAD_PAD)
    w_uk = w_kv_up[l][:, :, :QK_NOPE]
    w_uv = w_kv_up[l][:, :, QK_NOPE:]
    wuk_pad = jnp.concatenate([jnp.zeros((KV_RANK, N_HEADS, QK_ROPE), F32), w_uk,
                               jnp.zeros((KV_RANK, N_HEADS, HEAD_PAD - QK_NOPE - QK_ROPE), F32)], axis=-1)
    wukt_pad = jnp.transpose(wuk_pad, (1, 2, 0))
    odd_head = (jnp.arange(N_HEADS) % 2 == 1)[None, :, None]
    zv = jnp.zeros_like(w_uv)
    wuv_slot = jnp.where(odd_head, jnp.concatenate([zv, w_uv], axis=-1), jnp.concatenate([w_uv, zv], axis=-1))
    lane_id = jnp.arange(HEAD_PAD)[None, :]
    v_ones = jnp.where(odd_head[0], lane_id == 0, lane_id == V_DIM).astype(F32)
    eye = jnp.eye(N_HEADS, dtype=F32)
    wuv_pad = (w_uv.transpose(1, 0, 2)[:, :, None, :] * eye[:, None, :, None])
    conv_w_pad = jnp.concatenate([conv_w[l], jnp.zeros((CONV_HALO - CONV_WIDTH, CONV_CH), F32)], axis=0)
    w_router_pad = jnp.concatenate([w_router[l], jnp.zeros((D_MODEL, LANES - N_EXPERTS), F32)], axis=1)
    b_router_pad = jnp.concatenate([b_router[l], jnp.full((LANES - N_EXPERTS,), NEG_INF, F32)])
    return {
        'g_mix': g_mix[l][None], 'w_in_ext': w_in_ext.astype(BF16),
        'q_norm_g': q_norm_g[l][None], 'wq': wq.astype(BF16), 'wq_sw': wq_sw.astype(BF16),
        'kv_norm_g': kv_norm_g[l][None],
        'wuk_pad': wuk_pad.reshape(KV_RANK, N_HEADS * HEAD_PAD).astype(BF16),
        'wuv_slot': wuv_slot.reshape(KV_RANK, N_HEADS * HEAD_PAD).astype(BF16),
        'v_ones': v_ones.reshape(1, N_HEADS * HEAD_PAD),
        'wukt_pad': wukt_pad.astype(BF16),
        'wuv_pad': wuv_pad.reshape(N_HEADS, KV_RANK, N_HEADS * V_DIM).astype(BF16),
        'conv_w': conv_w_pad, 'conv_b': conv_b[l][None],
        'conv_ln_g': conv_ln_g[l][None], 'conv_ln_b': conv_ln_b[l][None],
        'w_out': w_out[l].astype(BF16), 'g_mem_q': g_mem_q[l][None], 'w_mq': w_mq[l].astype(BF16),
        'g_mem_kv': g_mem_kv[l][None], 'w_mk': w_mk[l].astype(BF16), 'w_mv': w_mv[l].astype(BF16),
        'w_mo': w_mo[l].astype(BF16), 'g_ffn': g_ffn[l][None],
        'w_router': w_router_pad.astype(BF16), 'b_router': b_router_pad[None],
        'w_gate': w_gate[l], 'w_up': w_up[l], 'w_down': w_down[l],
        'b_gate': b_gate[l][:, None, :], 'b_up': b_up[l][:, None, :], 'b_down': b_down[l][:, None, :],
        'g_final': g_final[None],
    }


def _rope_table(pos, q_scale):
    half = QK_ROPE // 2
    inv = ROPE_THETA ** (-jnp.arange(half, dtype=F32) / half)
    ang = pos.astype(F32)[:, None] * inv[None, :]
    cos, sin = jnp.cos(ang), jnp.sin(ang)
    n = pos.shape[0]
    ones = jnp.ones((n, QK_NOPE), F32)
    z = lambda k: jnp.zeros((n, k), F32)
    cq = q_scale * jnp.concatenate([cos, cos, ones, z(HEAD_PAD - QK_NOPE - QK_ROPE)], axis=1)
    sq = q_scale * jnp.concatenate([sin, sin, z(HEAD_PAD - QK_ROPE)], axis=1)
    ck = jnp.concatenate([cos, cos, z(HEAD_PAD - QK_ROPE)], axis=1)
    sk = jnp.concatenate([sin, sin, z(HEAD_PAD - QK_ROPE)], axis=1)
    return jnp.stack([cq, sq, ck, sk])


def _front(x, conv_prev, mem_k, mem_v, pos, w, paged):
    b, s, _ = x.shape
    t = b * s
    x2d = x.reshape(t, D_MODEL)
    sample = paged is not None
    if sample:
        tab = _rope_table(jnp.tile(pos, TOKEN_TILE // s), MLA_SCALE)
    else:
        tab = _rope_table(pos, MLA_SCALE * LOG2_E)
    tail = CONV_WIDTH - 1
    prev_pad = jnp.concatenate([jnp.zeros((b, CONV_HALO - tail, CONV_CH), F32), conv_prev], axis=1)
    outs = _inproj(x2d, tab, w, sample, prev_pad)
    u, ckv, kpe = outs[0], outs[1], outs[2]
    u3 = u.reshape(b, s, CONV_CH)
    if s >= tail:
        conv_tail = u3[:, s - tail:]
    else:
        conv_tail = jnp.concatenate([conv_prev[:, s:], u3], axis=1)
    ckv3 = ckv.reshape(b, s, KV_RANK)
    kpe3 = kpe.reshape(b, s, QK_ROPE)
    if sample:
        page_table, cache_ckv, cache_kpe_t = paged
        conv_out = _conv_sample(jnp.concatenate([prev_pad, u3], axis=1), w)
        attn = _attn_sample(page_table, outs[3], outs[4], ckv3, kpe3, cache_ckv, cache_kpe_t)
    else:
        conv_out = outs[6]
        q, k, v = (a.reshape(b, s, -1) for a in outs[3:6])
        attn = _attn_prompt(q, k, v).reshape(t, N_HEADS * V_DIM)
    conv2d = conv_out.reshape(t, CONV_CH)
    if not sample:
        return (x2d, conv2d, attn), conv_tail, ckv3, kpe3
    x1, qm = _post_a(x2d, conv2d, attn, w)
    om = _mem_attn_rows(qm.reshape(b, s, -1), mem_k, mem_v)
    return (x1, om.reshape(t, -1)), conv_tail, ckv3, kpe3


def kernel(x_prompt, x_sample, mem_prompt, cache_ckv, cache_kpe, page_table, cache_mem_k, cache_mem_v, state_conv, g_mix, w_in, conv_w, conv_b, conv_ln_g, conv_ln_b, q_norm_g, w_q_up, kv_norm_g, w_kv_up, w_out, g_mem_q, g_mem_kv, w_mq, w_mk, w_mv, w_mo, g_ffn, w_router, b_router, w_gate, b_gate, w_up, b_up, w_down, b_down, g_final):
    assert g_mix.shape[0] == 1, "single-layer step"
    w = _prep_weights(g_mix, w_in, conv_w, conv_b, conv_ln_g, conv_ln_b, q_norm_g, w_q_up, kv_norm_g, w_kv_up,
                      w_out, g_mem_q, g_mem_kv, w_mq, w_mk, w_mv, w_mo, g_ffn, w_router, b_router,
                      w_gate, b_gate, w_up, b_up, w_down, b_down, g_final)
    b_p, s_p, _ = x_prompt.shape
    b_s, t_s, _ = x_sample.shape
    past = page_table.shape[1] * PAGE_SIZE

    mk, mv, mk_wide, mv_wide = _mem_kv(mem_prompt.reshape(-1, D_MODEL), w)
    conv0 = jnp.zeros((b_p, CONV_WIDTH - 1, CONV_CH), F32)
    (x_p, conv_out_p, attn_p), conv_p, ckv_p, kpe_p = _front(
        x_prompt, conv0, None, None, jnp.arange(s_p, dtype=jnp.int32), w, None)
    (x1_s, om_s), conv_s, ckv_s, kpe_s = _front(
        x_sample, state_conv[0], cache_mem_k[0].reshape(-1, MEM_HD), cache_mem_v[0].reshape(-1, MEM_HD),
        past + jnp.arange(t_s, dtype=jnp.int32), w,
        (page_table, cache_ckv[0], jnp.swapaxes(cache_kpe[0], 1, 2)))

    x2, h3s, idx128, gate128, rank128, cnt128 = _post_b(
        x_p, conv_out_p, attn_p, mk_wide.reshape(b_p, MEM_TOKENS, -1), mv_wide.reshape(b_p, MEM_TOKENS, -1),
        x1_s, om_s, w)
    y_p, y_s = _moe_and_final(x2, h3s, idx128, gate128, rank128, cnt128, w, x_p.shape[0] // TOKEN_TILE)

    mem_shape = (1, b_p, MEM_TOKENS, MEM_HEADS, MEM_HD)
    return (y_p.reshape(b_p, s_p, D_MODEL), y_s.reshape(b_s, t_s, D_MODEL), ckv_p[None], kpe_p[None],
            mk.reshape(mem_shape), mv.reshape(mem_shape), conv_p[None], ckv_s[None], kpe_s[None], conv_s[None])
```

```python
import functools

import jax
import jax.numpy as jnp
from jax import lax
from jax.experimental import pallas as pl
from jax.experimental.pallas import tpu as pltpu

F32 = jnp.float32
BF16 = jnp.bfloat16

D_MODEL = 1024
PAGE_SIZE = 128
CONV_CH = 512
CONV_WIDTH = 31
N_HEADS = 8
QK_NOPE = 64
QK_ROPE = 32
V_DIM = 64
Q_RANK = 384
KV_RANK = 256
ROPE_THETA = 10000.0
MLA_SCALE = (QK_NOPE + QK_ROPE) ** -0.5
LOG2_E = 1.4426950408889634
MEM_TOKENS = 256
MEM_HEADS = 4
MEM_HD = 128
MEM_SCALE = MEM_HD ** -0.5
N_EXPERTS = 32
TOP_K = 4
SWIGLU_LIMIT = 7.0
SWIGLU_ALPHA = 1.702
EPS = 1e-6
NEG_INF = -1e30

LANES = 128
SUBLANES = 8
HEAD_PAD = 128
C_VAL, C_GATE, C_Q, C_CKV, C_KPE, C_KPE_SW, C_END = 0, 512, 1024, 1408, 1664, 1792, 1920
TOKEN_TILE = 256
ATTN_TILE = 512
ATTN_HEADS = 4
CONV_HALO = 32
CONV_CHUNK = 32
MEM_ATTN_ROWS = 32
PAGES_PER_CHUNK = 32
SLAB = 8
MOE_ROWS = 512
CAST_ROWS = 32
VMEM_LIMIT = 48 * 1024 * 1024


def _rms(x, g):
    return x * lax.rsqrt(jnp.mean(x * x, axis=-1, keepdims=True) + EPS) * g


def _dot(a, b):
    return jnp.dot(a, b, preferred_element_type=F32)


def _dot_t(a, b):
    return lax.dot_general(a, b, (((1,), (1,)), ((), ())), preferred_element_type=F32)


def _params(*sem, vmem=None):
    return pltpu.CompilerParams(dimension_semantics=sem, vmem_limit_bytes=vmem)


def _const_spec(shape):
    nd = len(shape)
    return pl.BlockSpec(shape, lambda *_: (0,) * nd)


def _inproj_common(x_ref, gmix_ref, win_ref, qg_ref, wq_ref, wqsw_ref, kvg_ref, tab_ref,
                   u_ref, ckv_ref, kpe_ref, after_u=None, after_q=None):
    h = _rms(x_ref[...], gmix_ref[...]).astype(BF16)
    glu = _dot(h, win_ref[:, C_VAL:C_Q])
    u = glu[:, :CONV_CH] * jax.nn.sigmoid(glu[:, CONV_CH:])
    u_ref[...] = u
    proj = _dot(h, win_ref[:, C_Q:C_END])
    if after_u is not None:
        after_u(u)
    qn = _rms(proj[:, 0:C_CKV - C_Q], qg_ref[...]).astype(BF16)
    ckv = _rms(proj[:, C_CKV - C_Q:C_KPE - C_Q], kvg_ref[...])
    ckv_ref[...] = ckv
    cq, sq, ck, sk = tab_ref[0], tab_ref[1], tab_ref[2], tab_ref[3]
    kpe_rot = proj[:, C_KPE - C_Q:C_KPE_SW - C_Q] * ck + proj[:, C_KPE_SW - C_Q:C_END - C_Q] * sk
    kpe_ref[...] = kpe_rot[:, :QK_ROPE]
    q = _dot(qn, wq_ref[...])
    qs = _dot(qn, wqsw_ref[...])
    if after_q is not None:
        after_q()
    q_heads = []
    for hd in range(N_HEADS):
        sl = slice(hd * HEAD_PAD, (hd + 1) * HEAD_PAD)
        q_heads.append(q[:, sl] * cq + qs[:, sl] * sq)
    return ckv, kpe_rot, q_heads


def _conv_fill_window(win_ref, halo, tile):
    tt = tile.shape[0]
    win_ref[0, 0:CONV_HALO, :] = halo
    win_ref[0, CONV_HALO:CONV_HALO + tt, :] = tile
    n = tt + CONV_HALO - SUBLANES
    for s in range(1, SUBLANES):
        win_ref[s, 0:n, :] = win_ref[0, s:s + n, :]


def _conv_chunk(win_ref, c, w_ref, b_ref, g_ref, lb_ref):
    acc = None
    for j in range(CONV_WIDTH):
        q, s = divmod(c * CONV_CHUNK + 2 + j, SUBLANES)
        term = win_ref[s, q * SUBLANES:q * SUBLANES + CONV_CHUNK, :] * w_ref[j:j + 1, :]
        acc = term if acc is None else acc + term
    return _ln_swish(acc + b_ref[...], g_ref[...], lb_ref[...])


def _inproj_prompt_kernel(tiles_per_seq, x_ref, gmix_ref, win_ref, qg_ref, wq_ref, wqsw_ref, kvg_ref, tab_ref,
                          wuk_ref, wuv_ref, vone_ref, prev_ref, cw_ref, cb_ref, cg_ref, clb_ref,
                          u_ref, ckv_ref, kpe_ref, q_ref, k_ref, v_ref, conv_ref, win_scr, carry_scr):
    tm = x_ref.shape[0]
    n_chunks = tm // CONV_CHUNK
    first = pl.program_id(0) % tiles_per_seq == 0

    def conv_chunks(lo, hi):
        for c in range(lo, hi):
            y = _conv_chunk(win_scr, c, cw_ref, cb_ref, cg_ref, clb_ref)
            conv_ref[c * CONV_CHUNK:(c + 1) * CONV_CHUNK, :] = y.astype(BF16)

    def after_u(u):
        _conv_fill_window(win_scr, jnp.where(first, prev_ref[0], carry_scr[...]), u)
        carry_scr[...] = u[tm - CONV_HALO:, :]
        conv_chunks(0, n_chunks * 3 // 4)

    ckv, kpe_rot, q_heads = _inproj_common(x_ref, gmix_ref, win_ref, qg_ref, wq_ref, wqsw_ref,
                                           kvg_ref, tab_ref, u_ref, ckv_ref, kpe_ref,
                                           after_u=after_u, after_q=lambda: conv_chunks(n_chunks * 3 // 4, n_chunks))
    ckv_b = ckv.astype(BF16)
    k_nope = _dot(ckv_b, wuk_ref[...])
    for hd in range(N_HEADS):
        sl = slice(hd * HEAD_PAD, (hd + 1) * HEAD_PAD)
        q_ref[:, sl] = q_heads[hd].astype(BF16)
        k_ref[:, sl] = (k_nope[:, sl] + kpe_rot).astype(BF16)
    v_ref[...] = (_dot(ckv_b, wuv_ref[...]) + vone_ref[...]).astype(BF16)


def _inproj_sample_kernel(x_ref, gmix_ref, win_ref, qg_ref, wq_ref, wqsw_ref, kvg_ref, tab_ref,
                          wukt_ref,
                          u_ref, ckv_ref, kpe_ref, qlat_ref, qpe_ref):
    _, _, q_heads = _inproj_common(x_ref, gmix_ref, win_ref, qg_ref, wq_ref, wqsw_ref,
                                   kvg_ref, tab_ref, u_ref, ckv_ref, kpe_ref)
    for hd in range(N_HEADS):
        qlat_ref[hd] = _dot(q_heads[hd].astype(BF16), wukt_ref[hd])
        qpe_ref[hd] = q_heads[hd][:, :QK_ROPE]


def _inproj(x2d, tab, w, sample, prev_pad=None):
    t = x2d.shape[0]
    tm = TOKEN_TILE
    n_tab = tab.shape[1] // tm
    row = lambda n: pl.BlockSpec((tm, n), lambda i: (i, 0))
    in_specs = [row(D_MODEL), _const_spec((1, D_MODEL)), _const_spec((D_MODEL, C_END)),
                _const_spec((1, Q_RANK)), _const_spec((Q_RANK, N_HEADS * HEAD_PAD)),
                _const_spec((Q_RANK, N_HEADS * HEAD_PAD)), _const_spec((1, KV_RANK)),
                pl.BlockSpec((4, tm, LANES), lambda i: (0, i % n_tab, 0))]
    args = [x2d, w['g_mix'], w['w_in_ext'], w['q_norm_g'], w['wq'], w['wq_sw'], w['kv_norm_g'], tab]
    out_shape = [jax.ShapeDtypeStruct((t, CONV_CH), F32), jax.ShapeDtypeStruct((t, KV_RANK), F32),
                 jax.ShapeDtypeStruct((t, QK_ROPE), F32)]
    out_specs = [row(CONV_CH), row(KV_RANK), row(QK_ROPE)]
    scratch = []
    if sample:
        body = _inproj_sample_kernel
        in_specs += [_const_spec((N_HEADS, HEAD_PAD, KV_RANK))]
        args += [w['wukt_pad']]
        out_shape += [jax.ShapeDtypeStruct((N_HEADS, t, KV_RANK), F32),
                      jax.ShapeDtypeStruct((N_HEADS, t, QK_ROPE), F32)]
        out_specs += [pl.BlockSpec((N_HEADS, tm, KV_RANK), lambda i: (0, i, 0)),
                      pl.BlockSpec((N_HEADS, tm, QK_ROPE), lambda i: (0, i, 0))]
    else:
        tiles_per_seq = t // prev_pad.shape[0] // tm
        body = functools.partial(_inproj_prompt_kernel, tiles_per_seq)
        in_specs += [_const_spec((KV_RANK, N_HEADS * HEAD_PAD)), _const_spec((KV_RANK, N_HEADS * HEAD_PAD)),
                     _const_spec((1, N_HEADS * HEAD_PAD)),
                     pl.BlockSpec((1, CONV_HALO, CONV_CH), lambda i: (i // tiles_per_seq, 0, 0)),
                     _const_spec((CONV_HALO, CONV_CH)), _const_spec((1, CONV_CH)),
                     _const_spec((1, CONV_CH)), _const_spec((1, CONV_CH))]
        args += [w['wuk_pad'], w['wuv_slot'], w['v_ones'], prev_pad,
                 w['conv_w'], w['conv_b'], w['conv_ln_g'], w['conv_ln_b']]
        out_shape += [jax.ShapeDtypeStruct((t, N_HEADS * HEAD_PAD), BF16)] * 3
        out_shape += [jax.ShapeDtypeStruct((t, CONV_CH), BF16)]
        out_specs += [row(N_HEADS * HEAD_PAD)] * 3 + [row(CONV_CH)]
        scratch = [pltpu.VMEM((SUBLANES, tm + CONV_HALO, CONV_CH), F32), pltpu.VMEM((CONV_HALO, CONV_CH), F32)]
    return pl.pallas_call(
        body, grid=(t // tm,), in_specs=in_specs, out_specs=out_specs, out_shape=out_shape,
        scratch_shapes=scratch,
        compiler_params=_params("parallel" if sample else "arbitrary", vmem=VMEM_LIMIT),
        name="inproj_sample" if sample else "inproj_prompt")(*args)


def _ln_swish(conv, g, b):
    mu = jnp.mean(conv, axis=-1, keepdims=True)
    xc = conv - mu
    var = jnp.mean(xc * xc, axis=-1, keepdims=True)
    y = xc * lax.rsqrt(var + EPS) * g + b
    return y * jax.nn.sigmoid(y)


def _conv_sample_kernel(win_ref, w_ref, b_ref, g_ref, lb_ref, o_ref):
    t = o_ref.shape[1]
    acc = win_ref[:, 2:2 + t, :] * w_ref[0:1, :]
    for j in range(1, CONV_WIDTH):
        acc = acc + win_ref[:, 2 + j:2 + j + t, :] * w_ref[j:j + 1, :]
    y = _ln_swish(acc + b_ref[...], g_ref[...], lb_ref[...])
    o_ref[...] = y


def _conv_sample(upad, w):
    b, s_pad, _ = upad.shape
    t = s_pad - CONV_HALO
    bb = 8
    return pl.pallas_call(
        _conv_sample_kernel, grid=(b // bb,),
        in_specs=[pl.BlockSpec((bb, s_pad, CONV_CH), lambda i: (i, 0, 0)),
                  _const_spec((CONV_HALO, CONV_CH)), _const_spec((1, CONV_CH)),
                  _const_spec((1, CONV_CH)), _const_spec((1, CONV_CH))],
        out_specs=pl.BlockSpec((bb, t, CONV_CH), lambda i: (i, 0, 0)),
        out_shape=jax.ShapeDtypeStruct((b, t, CONV_CH), F32),
        compiler_params=_params("parallel"),
        name="conv_sample")(upad, w['conv_w'], w['conv_b'], w['conv_ln_g'], w['conv_ln_b'])


def _attn_prompt_kernel(q_ref, k_ref, v_ref, o_ref, s_ref, m_ref):
    tq = q_ref.shape[1]
    qi = pl.program_id(2)
    nh = ATTN_HEADS
    head = lambda hd: slice(hd * HEAD_PAD, (hd + 1) * HEAD_PAD)
    rows = lax.broadcasted_iota(jnp.int32, (tq, tq), 0)
    cols = lax.broadcasted_iota(jnp.int32, (tq, tq), 1)
    lane = lax.broadcasted_iota(jnp.int32, (tq, LANES), 1)

    def scores(j, m_all, masked):
        start = pl.multiple_of(j * tq, tq)
        for hd in range(nh):
            s = _dot_t(q_ref[0, :, head(hd)], k_ref[0, pl.ds(start, tq), head(hd)])
            if masked:
                s = jnp.where(cols <= rows, s, NEG_INF)
            s_ref[hd, j] = s
            m_all = jnp.where(lane == hd, jnp.maximum(m_all, jnp.max(s, axis=-1, keepdims=True)), m_all)
        return m_all

    m_all = jnp.full((tq, LANES), NEG_INF, F32)
    m_all = lax.fori_loop(0, qi, functools.partial(scores, masked=False), m_all)
    m_all = scores(qi, m_all, True)
    for hd in range(nh):
        m_ref[hd] = jnp.broadcast_to(m_all[:, hd:hd + 1], (tq, LANES))

    def values(j, accs):
        start = pl.multiple_of(j * tq, tq)
        out = []
        for hd in range(nh):
            m = m_ref[hd]
            p = jnp.exp2(s_ref[hd, j] - jnp.concatenate([m] * (tq // LANES), axis=1)).astype(BF16)
            out.append(accs[hd] + _dot(p, v_ref[0, pl.ds(start, tq), head(hd)]))
        return tuple(out)

    accs = tuple(jnp.zeros((tq, HEAD_PAD), F32) for _ in range(nh))
    accs = lax.fori_loop(0, qi + 1, values, accs)
    for hp in range(nh // 2):
        even, odd = accs[2 * hp], accs[2 * hp + 1]
        o_even = even / even[:, V_DIM:V_DIM + 1]
        o_odd = odd / odd[:, 0:1]
        o_ref[0, :, hp * LANES:(hp + 1) * LANES] = jnp.where(lane < V_DIM, o_even, o_odd).astype(BF16)


def _attn_prompt(q, k, v):
    b, s, _ = q.shape
    tq = ATTN_TILE
    nh = ATTN_HEADS
    return pl.pallas_call(
        _attn_prompt_kernel, grid=(b, N_HEADS // nh, s // tq),
        in_specs=[pl.BlockSpec((1, tq, nh * HEAD_PAD), lambda bi, hq, qi: (bi, qi, hq)),
                  pl.BlockSpec((1, s, nh * HEAD_PAD), lambda bi, hq, qi: (bi, 0, hq)),
                  pl.BlockSpec((1, s, nh * HEAD_PAD), lambda bi, hq, qi: (bi, 0, hq))],
        out_specs=pl.BlockSpec((1, tq, nh * V_DIM), lambda bi, hq, qi: (bi, qi, hq)),
        out_shape=jax.ShapeDtypeStruct((b, s, N_HEADS * V_DIM), BF16),
        scratch_shapes=[pltpu.VMEM((nh, s // tq, tq, tq), F32), pltpu.VMEM((nh, tq, LANES), F32)],
        compiler_params=_params("parallel", "parallel", "arbitrary", vmem=VMEM_LIMIT),
        name="attn_prompt")(q, k, v)


def _attn_sample_kernel(pt_ref, ql_ref, qp_ref, cn_ref, kn_ref, ckv_hbm, kpe_hbm, o_ref,
                        ckv_buf, kpe_buf, sem):
    b = pl.program_id(0)
    n_pages = kpe_buf.shape[1]
    n_chunks = n_pages // PAGES_PER_CHUNK
    chunk = PAGES_PER_CHUNK * PAGE_SIZE
    t_new = cn_ref.shape[1]
    rows_q = N_HEADS * t_new

    def fetch(batch, slot):
        def one(p, _):
            page = pt_ref[batch * n_pages + p]
            rows = pl.ds(pl.multiple_of(p * PAGE_SIZE, PAGE_SIZE), PAGE_SIZE)
            pltpu.make_async_copy(ckv_hbm.at[page], ckv_buf.at[slot, rows], sem.at[0, slot]).start()
            pltpu.make_async_copy(kpe_hbm.at[page], kpe_buf.at[slot, p], sem.at[1, slot]).start()
            return 0
        lax.fori_loop(0, n_pages, one, 0, unroll=4)

    slot = b % 2

    @pl.when(b == 0)
    def _():
        fetch(0, 0)

    @pl.when(b + 1 < pl.num_programs(0))
    def _():
        fetch(b + 1, 1 - slot)

    pltpu.make_async_copy(ckv_buf.at[slot], ckv_buf.at[slot], sem.at[0, slot]).wait()
    pltpu.make_async_copy(kpe_buf.at[slot], kpe_buf.at[slot], sem.at[1, slot]).wait()

    ql = ql_ref[:, 0].reshape(rows_q, KV_RANK).astype(BF16)
    qp = qp_ref[:, 0].reshape(rows_q, QK_ROPE).astype(BF16)

    pad = PAGE_SIZE - t_new
    kc_new = jnp.concatenate([cn_ref[0], jnp.zeros((pad, KV_RANK), F32)], axis=0).astype(BF16)
    kp_new = jnp.concatenate([kn_ref[0], jnp.zeros((pad, QK_ROPE), F32)], axis=0).astype(BF16)
    s_new = _dot_t(ql, kc_new) + _dot_t(qp, kp_new)
    t_q = lax.broadcasted_iota(jnp.int32, s_new.shape, 0) % t_new
    cols = lax.broadcasted_iota(jnp.int32, s_new.shape, 1)
    s_new = jnp.where(cols <= t_q, s_new, NEG_INF)
    def part(s, values):
        m = jnp.max(s, axis=-1, keepdims=True)
        p = jnp.exp(s - m)
        return m, jnp.sum(p, axis=-1, keepdims=True), _dot(p.astype(BF16), values)

    def scores(c):
        kc = ckv_buf[slot, c * chunk:(c + 1) * chunk, :].astype(BF16)
        kpt = jnp.concatenate([kpe_buf[slot, c * PAGES_PER_CHUNK + i] for i in range(PAGES_PER_CHUNK)],
                              axis=1).astype(BF16)
        return _dot_t(ql, kc) + _dot(qp, kpt), kc

    parts = [part(s_new, kc_new)]
    nxt = scores(0)
    for c in range(n_chunks):
        cur = nxt
        if c + 1 < n_chunks:
            nxt = scores(c + 1)
        parts.append(part(*cur))
    m = parts[0][0]
    for mp, _, _ in parts[1:]:
        m = jnp.maximum(m, mp)
    l = jnp.zeros_like(m)
    acc = jnp.zeros((rows_q, KV_RANK), F32)
    for mp, lp, ap in parts:
        w = jnp.exp(mp - m)
        l = l + w * lp
        acc = acc + w * ap
    o_ref[:, 0] = (acc / l).reshape(N_HEADS, t_new, KV_RANK)


def _attn_sample(page_table, qlat, qpe, ckv_new, kpe_new, cache_ckv, cache_kpe_t):
    bs, n_pages = page_table.shape
    t_new = ckv_new.shape[1]
    qlat4 = qlat.reshape(N_HEADS, bs, t_new, KV_RANK)
    qpe4 = qpe.reshape(N_HEADS, bs, t_new, QK_ROPE)
    past = n_pages * PAGE_SIZE
    in_specs = [pl.BlockSpec((N_HEADS, 1, t_new, KV_RANK), lambda b, pt: (0, b, 0, 0)),
                pl.BlockSpec((N_HEADS, 1, t_new, QK_ROPE), lambda b, pt: (0, b, 0, 0)),
                pl.BlockSpec((1, t_new, KV_RANK), lambda b, pt: (b, 0, 0)),
                pl.BlockSpec((1, t_new, QK_ROPE), lambda b, pt: (b, 0, 0)),
                pl.BlockSpec(memory_space=pl.ANY), pl.BlockSpec(memory_space=pl.ANY)]
    grid_spec = pltpu.PrefetchScalarGridSpec(
        num_scalar_prefetch=1, grid=(bs,), in_specs=in_specs,
        out_specs=pl.BlockSpec((N_HEADS, 1, t_new, KV_RANK), lambda b, pt: (0, b, 0, 0)),
        scratch_shapes=[pltpu.VMEM((2, past, KV_RANK), F32),
                        pltpu.VMEM((2, n_pages, QK_ROPE, PAGE_SIZE), F32),
                        pltpu.SemaphoreType.DMA((2, 2))])
    o = pl.pallas_call(
        _attn_sample_kernel, grid_spec=grid_spec,
        out_shape=jax.ShapeDtypeStruct((N_HEADS, bs, t_new, KV_RANK), F32),
        compiler_params=_params("arbitrary", vmem=VMEM_LIMIT),
        name="attn_sample")(page_table.reshape(-1), qlat4, qpe4, ckv_new, kpe_new, cache_ckv, cache_kpe_t)
    return o.reshape(N_HEADS, bs * t_new, KV_RANK)


def _post_a_kernel(x_ref, conv_ref, olat_ref, wuv_ref, wout_ref, g_ref, wmq_ref, x1_ref, qm_ref):
    attn = None
    for hd in range(N_HEADS):
        d = _dot(olat_ref[hd].astype(BF16), wuv_ref[hd])
        attn = d if attn is None else attn + d
    mix = (_dot(conv_ref[...].astype(BF16), wout_ref[0:CONV_CH, :])
           + _dot(attn.astype(BF16), wout_ref[CONV_CH:, :]))
    x1 = x_ref[...] + mix
    x1_ref[...] = x1
    qm_ref[...] = _dot(_rms(x1, g_ref[...]).astype(BF16), wmq_ref[...])


def _post_a(x2d, conv2d, o_lat, w):
    t = x2d.shape[0]
    tm = TOKEN_TILE
    row = lambda n: pl.BlockSpec((tm, n), lambda i: (i, 0))
    mem_w = MEM_HEADS * MEM_HD
    return pl.pallas_call(
        _post_a_kernel, grid=(t // tm,),
        in_specs=[row(D_MODEL), row(CONV_CH), pl.BlockSpec((N_HEADS, tm, KV_RANK), lambda i: (0, i, 0)),
                  _const_spec((N_HEADS, KV_RANK, N_HEADS * V_DIM)),
                  _const_spec((D_MODEL, D_MODEL)), _const_spec((1, D_MODEL)), _const_spec((D_MODEL, mem_w))],
        out_specs=[row(D_MODEL), row(mem_w)],
        out_shape=[jax.ShapeDtypeStruct((t, D_MODEL), F32), jax.ShapeDtypeStruct((t, mem_w), F32)],
        compiler_params=_params("parallel", vmem=VMEM_LIMIT),
        name="post_a_sample")(x2d, conv2d, o_lat, w['wuv_pad'], w['w_out'], w['g_mem_q'], w['w_mq'])


def _mem_kv_kernel(m_ref, g_ref, wk_ref, wv_ref, k_ref, v_ref, kw_ref, vw_ref):
    tm = m_ref.shape[0]
    m = _rms(m_ref[...], g_ref[...]).astype(BF16)
    k = _dot(m, wk_ref[...])
    v = _dot(m, wv_ref[...])
    kw_ref[...] = k
    vw_ref[...] = v
    for hd in range(MEM_HEADS):
        sl = slice(hd * MEM_HD, (hd + 1) * MEM_HD)
        k_ref[pl.ds(hd, tm, stride=MEM_HEADS), :] = k[:, sl]
        v_ref[pl.ds(hd, tm, stride=MEM_HEADS), :] = v[:, sl]


def _mem_kv(mem2d, w):
    t = mem2d.shape[0]
    tm = TOKEN_TILE
    mem_w = MEM_HEADS * MEM_HD
    rows = pl.BlockSpec((tm * MEM_HEADS, MEM_HD), lambda i: (i, 0))
    wide = pl.BlockSpec((tm, mem_w), lambda i: (i, 0))
    return pl.pallas_call(
        _mem_kv_kernel, grid=(t // tm,),
        in_specs=[pl.BlockSpec((tm, D_MODEL), lambda i: (i, 0)), _const_spec((1, D_MODEL)),
                  _const_spec((D_MODEL, mem_w)), _const_spec((D_MODEL, mem_w))],
        out_specs=[rows, rows, wide, wide],
        out_shape=[jax.ShapeDtypeStruct((t * MEM_HEADS, MEM_HD), F32)] * 2
                  + [jax.ShapeDtypeStruct((t, mem_w), F32)] * 2,
        compiler_params=_params("parallel"),
        name="mem_kv")(mem2d, w['g_mem_kv'], w['w_mk'], w['w_mv'])


def _mem_attn_rows_kernel(q_ref, k_ref, v_ref, o_ref):
    bb, tq, _ = q_ref.shape
    rows = MEM_TOKENS * MEM_HEADS
    head = lambda hd: slice(hd * MEM_HD, (hd + 1) * MEM_HD)
    q_head = lax.broadcasted_iota(jnp.int32, (MEM_HEADS * tq, rows), 0) // tq
    k_head = lax.broadcasted_iota(jnp.int32, (MEM_HEADS * tq, rows), 1) % MEM_HEADS
    own = q_head == k_head
    for bi in range(bb):
        q = jnp.concatenate([q_ref[bi, :, head(hd)] for hd in range(MEM_HEADS)], axis=0).astype(BF16)
        k = k_ref[bi * rows:(bi + 1) * rows, :].astype(BF16)
        v = v_ref[bi * rows:(bi + 1) * rows, :].astype(BF16)
        s = jnp.where(own, _dot_t(q, k) * MEM_SCALE, NEG_INF)
        p = jnp.exp(s - jnp.max(s, axis=-1, keepdims=True))
        p = p / jnp.sum(p, axis=-1, keepdims=True)
        o = _dot(p.astype(BF16), v)
        for hd in range(MEM_HEADS):
            o_ref[bi, :, head(hd)] = o[hd * tq:(hd + 1) * tq, :]


def _mem_attn_rows(qm, mem_k, mem_v):
    b, s, mem_w = qm.shape
    bb = max(1, MEM_ATTN_ROWS // s)
    kv = pl.BlockSpec((bb * MEM_TOKENS * MEM_HEADS, MEM_HD), lambda bi: (bi, 0))
    return pl.pallas_call(
        _mem_attn_rows_kernel, grid=(b // bb,),
        in_specs=[pl.BlockSpec((bb, s, mem_w), lambda bi: (bi, 0, 0)), kv, kv],
        out_specs=pl.BlockSpec((bb, s, mem_w), lambda bi: (bi, 0, 0)),
        out_shape=jax.ShapeDtypeStruct((b, s, mem_w), F32),
        compiler_params=_params("parallel"),
        name="mem_attn_rows")(qm, mem_k, mem_v)


def _slab_rows(j, n):
    return pl.ds(j, n, stride=SLAB)


def _post_b_kernel(n_p, x_ref, conv_ref, attn_ref, mk_ref, mv_ref, wout_ref, gq_ref, wmq_ref,
                   x1s_ref, oms_ref, wmo_ref, g_ref, wr_ref, br_ref,
                   x2_ref, h3_ref, idx_ref, gate_ref, rank_ref, cnt_ref, carry_ref, x1_scr, om_scr):
    tm = x2_ref.shape[0]
    i = pl.program_id(0)

    @pl.when(i == 0)
    def _():
        carry_ref[...] = jnp.zeros(carry_ref.shape, F32)

    @pl.when(i < n_p)
    def _():
        mix = _dot(conv_ref[...], wout_ref[0:CONV_CH, :]) + _dot(attn_ref[...], wout_ref[CONV_CH:, :])
        x1p = x_ref[...] + mix
        x1_scr[...] = x1p
        qm = _dot(_rms(x1p, gq_ref[...]).astype(BF16), wmq_ref[...])
        for hd in range(MEM_HEADS):
            sl = slice(hd * MEM_HD, (hd + 1) * MEM_HD)
            s = _dot_t(qm[:, sl].astype(BF16), mk_ref[0, :, sl].astype(BF16)) * MEM_SCALE
            p = jnp.exp(s - jnp.max(s, axis=-1, keepdims=True))
            p = p / jnp.sum(p, axis=-1, keepdims=True)
            om_scr[:, sl] = _dot(p.astype(BF16), mv_ref[0, :, sl].astype(BF16))

    @pl.when(i >= n_p)
    def _():
        x1_scr[...] = x1s_ref[...]
        om_scr[...] = oms_ref[...]

    x1 = x1_scr[...]
    om = om_scr[...]
    x2 = x1 + _dot(om.astype(BF16), wmo_ref[...])
    x2_ref[...] = x2
    h3 = _rms(x2, g_ref[...])
    for j in range(SLAB):
        h3_ref[_slab_rows(j, tm), :] = h3[:, j * LANES:(j + 1) * LANES]
    logits = _dot(h3.astype(BF16), wr_ref[...]) + br_ref[...]
    lane = lax.broadcasted_iota(jnp.int32, logits.shape, 1)
    lane_f = lane.astype(F32)
    vals, hots = [], []
    idx_out = jnp.zeros(logits.shape, F32)
    for kk in range(TOP_K):
        mx = jnp.max(logits, axis=-1, keepdims=True)
        first = jnp.min(jnp.where(logits == mx, lane_f, float(LANES)), axis=-1, keepdims=True)
        hot = lane_f == first
        logits = jnp.where(hot, -jnp.inf, logits)
        vals.append(mx)
        hots.append(hot)
        idx_out = jnp.where(lane == kk, first, idx_out)
    exps = [jnp.exp(v - vals[0]) for v in vals]
    denom = exps[0] + exps[1] + exps[2] + exps[3]
    chosen = jnp.zeros(logits.shape, F32)
    gate_out = jnp.zeros(logits.shape, F32)
    for kk in range(TOP_K):
        chosen = chosen + hots[kk].astype(F32)
        gate_out = jnp.where(lane == kk, exps[kk] / denom, gate_out)
    r_i = lax.broadcasted_iota(jnp.int32, (tm, tm), 0)
    c_i = lax.broadcasted_iota(jnp.int32, (tm, tm), 1)
    tril = (c_i < r_i).astype(BF16)
    before = _dot(tril, chosen.astype(BF16)) + carry_ref[...]
    rank_out = jnp.zeros(logits.shape, F32)
    for kk in range(TOP_K):
        rk = jnp.sum(jnp.where(hots[kk], before, 0.0), axis=-1, keepdims=True)
        rank_out = jnp.where(lane == kk, rk, rank_out)
    carry = carry_ref[...] + jnp.sum(chosen, axis=0, keepdims=True)
    carry_ref[...] = carry
    idx_ref[...] = idx_out.astype(jnp.int32)
    gate_ref[...] = gate_out
    rank_ref[...] = rank_out.astype(jnp.int32)
    cnt_ref[...] = carry.astype(jnp.int32)


def _post_b(x_p, conv_p, attn_p, mk_wide, mv_wide, x1_s, om_s, w):
    tm = TOKEN_TILE
    n_p, n_s = x_p.shape[0] // tm, x1_s.shape[0] // tm
    tiles_per_seq = n_p // mk_wide.shape[0]
    t = (n_p + n_s) * tm
    mem_w = MEM_HEADS * MEM_HD
    row = lambda n: pl.BlockSpec((tm, n), lambda i: (i, 0))
    row_p = lambda n: pl.BlockSpec((tm, n), lambda i: (jnp.minimum(i, n_p - 1), 0))
    row_s = lambda n: pl.BlockSpec((tm, n), lambda i: (jnp.maximum(i - n_p, 0), 0))
    mem = pl.BlockSpec((1, MEM_TOKENS, mem_w), lambda i: (jnp.minimum(i, n_p - 1) // tiles_per_seq, 0, 0))
    return pl.pallas_call(
        functools.partial(_post_b_kernel, n_p), grid=(n_p + n_s,),
        in_specs=[row_p(D_MODEL), row_p(CONV_CH), row_p(N_HEADS * V_DIM), mem, mem,
                  _const_spec((D_MODEL, D_MODEL)), _const_spec((1, D_MODEL)), _const_spec((D_MODEL, mem_w)),
                  row_s(D_MODEL), row_s(mem_w),
                  _const_spec((mem_w, D_MODEL)), _const_spec((1, D_MODEL)),
                  _const_spec((D_MODEL, LANES)), _const_spec((1, LANES))],
        out_specs=[row(D_MODEL), pl.BlockSpec((tm * SLAB, LANES), lambda i: (i, 0)),
                   row(LANES), row(LANES), row(LANES), _const_spec((1, LANES))],
        out_shape=[jax.ShapeDtypeStruct((t, D_MODEL), F32), jax.ShapeDtypeStruct((t * SLAB, LANES), F32),
                   jax.ShapeDtypeStruct((t, LANES), jnp.int32), jax.ShapeDtypeStruct((t, LANES), F32),
                   jax.ShapeDtypeStruct((t, LANES), jnp.int32), jax.ShapeDtypeStruct((1, LANES), jnp.int32)],
        scratch_shapes=[pltpu.VMEM((1, LANES), F32), pltpu.VMEM((tm, D_MODEL), F32), pltpu.VMEM((tm, mem_w), F32)],
        compiler_params=_params("arbitrary", vmem=VMEM_LIMIT),
        name="post_b")(x_p, conv_p, attn_p, mk_wide, mv_wide, w['w_out'], w['g_mem_q'], w['w_mq'],
                       x1_s, om_s, w['w_mo'], w['g_ffn'], w['w_router'], w['b_router'])


def _slab(row):
    return pl.ds(pl.multiple_of(row * SLAB, SLAB), SLAB)


def _dest_kernel(start_ref, idx_ref, rank_ref, dest_ref):
    idx = idx_ref[...]
    table = [start_ref[e] for e in range(N_EXPERTS)]
    bit = 0
    while len(table) > 1:
        odd = ((idx >> bit) & 1) == 1
        table = [jnp.where(odd, table[2 * i + 1], table[2 * i]) for i in range(len(table) // 2)]
        bit += 1
    dest_ref[...] = rank_ref[...] + table[0]


def _dest_rows(idx128, rank128, pad_start):
    t = idx128.shape[0]
    tm = TOKEN_TILE
    spec = pl.BlockSpec((tm, LANES), lambda i, *_: (i, 0))
    grid_spec = pltpu.PrefetchScalarGridSpec(num_scalar_prefetch=1, grid=(t // tm,),
                                             in_specs=[spec, spec], out_specs=spec)
    return pl.pallas_call(
        _dest_kernel, grid_spec=grid_spec, out_shape=jax.ShapeDtypeStruct((t, LANES), jnp.int32),
        compiler_params=_params("parallel"), name="moe_dest")(pad_start, idx128, rank128)


def _dispatch_kernel(lo_ref, hi_ref, dest_ref, h_hbm, xs_hbm, hbuf, zero_ref, sem_in, sem_out, sem_z):
    i = pl.program_id(0)
    n = pl.num_programs(0)
    rows_per_tile = hbuf.shape[1]
    tm = rows_per_tile // SLAB

    def load(tile, slot):
        rows = pl.ds(pl.multiple_of(tile * rows_per_tile, SLAB), rows_per_tile)
        return pltpu.make_async_copy(h_hbm.at[rows], hbuf.at[slot], sem_in.at[slot])

    def wait_scatter(slot):
        for _ in range(TOP_K):
            pltpu.make_async_copy(hbuf.at[0], xs_hbm.at[pl.ds(0, rows_per_tile)], sem_out.at[slot]).wait()

    @pl.when(i == 0)
    def _():
        load(0, 0).start()

    @pl.when(i + 1 < n)
    def _():
        load(i + 1, (i + 1) % 3).start()

    cur = i % 3
    load(i, cur).wait()

    def issue(t, _):
        for k in range(TOP_K):
            d = dest_ref[t * TOP_K + k]
            pltpu.make_async_copy(hbuf.at[cur, _slab(t)], xs_hbm.at[_slab(d)],
                                  sem_out.at[i % 2]).start(priority=k % 2)
        return 0

    lax.fori_loop(0, tm, issue, 0, unroll=4)

    @pl.when(i > 0)
    def _():
        wait_scatter((i - 1) % 2)

    @pl.when(i == n - 1)
    def _():
        wait_scatter(i % 2)
        zero_ref[...] = jnp.zeros(zero_ref.shape, F32)
        bm = zero_ref.shape[0] // SLAB
        for e in range(N_EXPERTS):
            lo = lo_ref[e]
            pad = hi_ref[e] - lo
            for bit in range(bm.bit_length() - 1):
                size = 1 << bit

                @pl.when((pad >> bit) & 1 == 1)
                def _():
                    first = lo + (pad & (size - 1))
                    rows = pl.ds(pl.multiple_of(first * SLAB, SLAB), size * SLAB)
                    cp = pltpu.make_async_copy(zero_ref.at[pl.ds(0, size * SLAB)], xs_hbm.at[rows], sem_z)
                    cp.start()
                    cp.wait()

        def tail(blk, _):
            rows = pl.ds(pl.multiple_of(blk * bm * SLAB, SLAB), bm * SLAB)
            cp = pltpu.make_async_copy(zero_ref, xs_hbm.at[rows], sem_z)
            cp.start()
            cp.wait()
            return 0

        lax.fori_loop(hi_ref[N_EXPERTS - 1] // bm, xs_hbm.shape[0] // (bm * SLAB), tail, 0)


def _dispatch(h3s, dest_flat, pad_lo, pad_hi, n_rows):
    tm = TOKEN_TILE
    t = h3s.shape[0] // SLAB
    grid_spec = pltpu.PrefetchScalarGridSpec(
        num_scalar_prefetch=2, grid=(t // tm,),
        in_specs=[pl.BlockSpec((tm * TOP_K,), lambda i, *_: (i,), memory_space=pltpu.SMEM),
                  pl.BlockSpec(memory_space=pl.ANY)],
        out_specs=pl.BlockSpec(memory_space=pl.ANY),
        scratch_shapes=[pltpu.VMEM((3, tm * SLAB, LANES), F32), pltpu.VMEM((MOE_ROWS * SLAB, LANES), F32),
                        pltpu.SemaphoreType.DMA((3,)), pltpu.SemaphoreType.DMA((2,)), pltpu.SemaphoreType.DMA])
    return pl.pallas_call(
        _dispatch_kernel, grid_spec=grid_spec,
        out_shape=jax.ShapeDtypeStruct((n_rows * SLAB, LANES), F32),
        compiler_params=pltpu.CompilerParams(dimension_semantics=("arbitrary",), has_side_effects=True),
        name="moe_dispatch")(pad_lo, pad_hi, dest_flat, h3s)


def _expert_kernel(src_ref, exp_ref, first_ref, valid_ref, next_ref, x_ref, wg_hbm, wu_hbm, wd_hbm,
                   bg_ref, bu_ref, bd_ref, y_ref, wf_ref, wb_ref, xb_ref, sem):
    b = pl.program_id(0)
    bm = xb_ref.shape[0]
    w_hbm = (wg_hbm, wu_hbm, wd_hbm)

    def fetch(e):
        for i in range(3):
            pltpu.make_async_copy(w_hbm[i].at[e], wf_ref.at[i], sem).start()

    @pl.when(b == 0)
    def _():
        fetch(exp_ref[0])

    @pl.when(first_ref[b] == 1)
    def _():
        for i in range(3):
            pltpu.make_async_copy(w_hbm[i].at[0], wf_ref.at[i], sem).wait()
        def cast(r, _):
            rows = pl.ds(pl.multiple_of(r * CAST_ROWS, CAST_ROWS), CAST_ROWS)
            for i in range(3):
                wb_ref[i, rows, :] = wf_ref[i, rows, :].astype(BF16)
            return 0

        lax.fori_loop(0, D_MODEL // CAST_ROWS, cast, 0)

        @pl.when(next_ref[b] >= 0)
        def _():
            fetch(next_ref[b])

    @pl.when(valid_ref[b] == 1)
    def _():
        for j in range(SLAB):
            xb_ref[:, j * LANES:(j + 1) * LANES] = x_ref[_slab_rows(j, bm), :].astype(BF16)
        x = xb_ref[...]
        g = _dot(x, wb_ref[0]) + bg_ref[0]
        u = _dot(x, wb_ref[1]) + bu_ref[0]
        g = jnp.minimum(g, SWIGLU_LIMIT)
        u = jnp.clip(u, -SWIGLU_LIMIT, SWIGLU_LIMIT)
        a = (u + 1.0) * (g * jax.nn.sigmoid(SWIGLU_ALPHA * g))
        y = _dot(a.astype(BF16), wb_ref[2]) + bd_ref[0]
        for j in range(SLAB):
            y_ref[_slab_rows(j, bm), :] = y[:, j * LANES:(j + 1) * LANES]

    @pl.when(valid_ref[b] == 0)
    def _():
        y_ref[...] = jnp.zeros(y_ref.shape, F32)


def _experts(xs, blk_src, blk_exp, blk_first, blk_valid, blk_next, w):
    bm = MOE_ROWS
    n_blocks = xs.shape[0] // (bm * SLAB)
    hbm = pl.BlockSpec(memory_space=pl.ANY)
    bspec = pl.BlockSpec((1, 1, D_MODEL), lambda b, src, ex, *_: (ex[b], 0, 0))
    grid_spec = pltpu.PrefetchScalarGridSpec(
        num_scalar_prefetch=5, grid=(n_blocks,),
        in_specs=[pl.BlockSpec((bm * SLAB, LANES), lambda b, src, *_: (src[b], 0)),
                  hbm, hbm, hbm, bspec, bspec, bspec],
        out_specs=pl.BlockSpec((bm * SLAB, LANES), lambda b, *_: (b, 0)),
        scratch_shapes=[pltpu.VMEM((3, D_MODEL, D_MODEL), F32), pltpu.VMEM((3, D_MODEL, D_MODEL), BF16),
                        pltpu.VMEM((bm, D_MODEL), BF16), pltpu.SemaphoreType.DMA])
    return pl.pallas_call(
        _expert_kernel, grid_spec=grid_spec,
        out_shape=jax.ShapeDtypeStruct(xs.shape, F32),
        compiler_params=_params("arbitrary", vmem=VMEM_LIMIT),
        name="moe_experts")(blk_src, blk_exp, blk_first, blk_valid, blk_next, xs,
                            w['w_gate'], w['w_up'], w['w_down'], w['b_gate'], w['b_up'], w['b_down'])


def _combine_kernel(n_p, dest_ref, dest_nx_ref, gate_ref, yb_hbm, x2_ref, g_ref,
                    yp_ref, ys_ref, rows_ref, ysum_ref, y_ref, sem):
    i = pl.program_id(0)
    tm = x2_ref.shape[0]

    def gather(d_ref, slot):
        def issue(t, _):
            for k in range(TOP_K):
                r = t * TOP_K + k
                pltpu.make_async_copy(yb_hbm.at[_slab(d_ref[r])], rows_ref.at[slot, _slab(r)],
                                      sem.at[slot]).start(priority=k % 2)
            return 0
        lax.fori_loop(0, tm, issue, 0, unroll=4)

    slot = i % 2

    @pl.when(i == 0)
    def _():
        gather(dest_ref, 0)

    @pl.when(i + 1 < pl.num_programs(0))
    def _():
        gather(dest_nx_ref, 1 - slot)

    for _ in range(TOP_K):
        pltpu.make_async_copy(yb_hbm.at[pl.ds(0, tm * SLAB)], rows_ref.at[slot, pl.ds(0, tm * SLAB)],
                              sem.at[slot]).wait()

    def token(t, _):
        acc = rows_ref[slot, _slab(t * TOP_K)] * gate_ref[t * TOP_K]
        for k in range(1, TOP_K):
            acc = acc + rows_ref[slot, _slab(t * TOP_K + k)] * gate_ref[t * TOP_K + k]
        ysum_ref[_slab(t)] = acc
        return 0

    lax.fori_loop(0, tm, token, 0, unroll=4)

    ss = jnp.zeros((tm, 1), F32)
    for j in range(SLAB):
        y = x2_ref[:, j * LANES:(j + 1) * LANES] + ysum_ref[_slab_rows(j, tm), :]
        y_ref[:, j * LANES:(j + 1) * LANES] = y
        ss = ss + jnp.sum(y * y, axis=-1, keepdims=True)
    out = y_ref[...] * lax.rsqrt(ss * (1.0 / D_MODEL) + EPS) * g_ref[...]

    @pl.when(i < n_p)
    def _():
        yp_ref[...] = out

    @pl.when(i >= n_p)
    def _():
        ys_ref[...] = out


def _combine(yb, dest_flat, gate_flat, x2, g_final, n_p):
    t = x2.shape[0]
    tm = TOKEN_TILE
    n = t // tm
    n_s = n - n_p
    cur = lambda i: (i,)
    nxt = lambda i: (jnp.minimum(i + 1, n - 1),)
    smem = lambda index_map: pl.BlockSpec((tm * TOP_K,), index_map, memory_space=pltpu.SMEM)
    return pl.pallas_call(
        functools.partial(_combine_kernel, n_p), grid=(n,),
        in_specs=[smem(cur), smem(nxt), smem(cur),
                  pl.BlockSpec(memory_space=pl.ANY),
                  pl.BlockSpec((tm, D_MODEL), lambda i: (i, 0)),
                  pl.BlockSpec((1, D_MODEL), lambda i: (0, 0))],
        out_specs=[pl.BlockSpec((tm, D_MODEL), lambda i: (jnp.minimum(i, n_p - 1), 0)),
                   pl.BlockSpec((tm, D_MODEL), lambda i: (jnp.maximum(i - n_p, 0), 0))],
        out_shape=[jax.ShapeDtypeStruct((n_p * tm, D_MODEL), F32), jax.ShapeDtypeStruct((n_s * tm, D_MODEL), F32)],
        scratch_shapes=[pltpu.VMEM((2, tm * TOP_K * SLAB, LANES), F32), pltpu.VMEM((tm * SLAB, LANES), F32),
                        pltpu.VMEM((tm, D_MODEL), F32), pltpu.SemaphoreType.DMA((2,))],
        compiler_params=_params("arbitrary", vmem=VMEM_LIMIT),
        name="moe_combine")(dest_flat, dest_flat, gate_flat, yb, x2, g_final)


def _moe_and_final(x2, h3s, idx128, gate128, rank128, cnt128, w, n_p):
    t = x2.shape[0]
    bm = MOE_ROWS
    n_assign = t * TOP_K
    n_blocks = (n_assign + N_EXPERTS * (bm - 1) + bm - 1) // bm
    counts = cnt128[0, :N_EXPERTS]
    padded = ((counts + bm - 1) // bm) * bm
    pad_end = jnp.cumsum(padded).astype(jnp.int32)
    pad_start = pad_end - padded
    dest_flat = _dest_rows(idx128, rank128, pad_start)[:, :TOP_K].reshape(-1)
    blk = jnp.arange(n_blocks, dtype=jnp.int32)
    blk_valid = (blk * bm < pad_end[-1]).astype(jnp.int32)
    blk_src = jnp.minimum(blk, jnp.maximum(pad_end[-1] // bm - 1, 0))
    blk_exp = jnp.sum((pad_end[None, :] <= (blk_src * bm)[:, None]).astype(jnp.int32), axis=1)
    blk_exp = jnp.minimum(blk_exp, N_EXPERTS - 1)
    blk_first = jnp.concatenate([jnp.ones((1,), jnp.int32),
                                 (blk_exp[1:] != blk_exp[:-1]).astype(jnp.int32)])
    later_start = (blk[None, :] > blk[:, None]) & (blk_first[None, :] == 1)
    next_pos = jnp.min(jnp.where(later_start, blk[None, :], n_blocks), axis=1)
    blk_next = jnp.where(next_pos < n_blocks, blk_exp[jnp.minimum(next_pos, n_blocks - 1)], -1)
    xs = _dispatch(h3s, dest_flat, pad_start + counts, pad_end, n_blocks * bm)
    yb = _experts(xs, blk_src, blk_exp, blk_first, blk_valid, blk_next.astype(jnp.int32), w)
    gate_flat = gate128[:, :TOP_K].reshape(-1)
    return _combine(yb, dest_flat, gate_flat, x2, w['g_final'], n_p)


def _swap_halves(wcols):
    half = QK_ROPE // 2
    return jnp.concatenate([-wcols[..., half:], wcols[..., :half]], axis=-1)


def _prep_weights(g_mix, w_in, conv_w, conv_b, conv_ln_g, conv_ln_b, q_norm_g, w_q_up, kv_norm_g, w_kv_up,
                  w_out, g_mem_q, g_mem_kv, w_mq, w_mk, w_mv, w_mo, g_ffn, w_router, b_router,
                  w_gate, b_gate, w_up, b_up, w_down, b_down, g_final):
    l = 0
    w_kpe = w_in[l][:, C_KPE:]
    zpad = jnp.zeros((D_MODEL, HEAD_PAD - QK_ROPE), F32)
    w_in_ext = jnp.concatenate([w_in[l][:, :C_KPE], w_kpe, zpad, _swap_halves(w_kpe), zpad], axis=1)
    wq3 = w_q_up[l].reshape(Q_RANK, N_HEADS, QK_NOPE + QK_ROPE)
    q_nope, q_rope = wq3[..., :QK_NOPE], wq3[..., QK_NOPE:]
    z32 = jnp.zeros((Q_RANK, N_HEADS, HEAD_PAD - QK_NOPE - QK_ROPE), F32)
    wq = jnp.concatenate([q_rope, q_nope, z32], axis=-1).reshape(Q_RANK, N_HEADS * HEAD_PAD)
    wq_sw = jnp.concatenate([_swap_halves(q_rope), jnp.zeros_like(q_nope), z32], axis=-1)
    wq_sw = wq_sw.reshape(Q_RANK, N_HEADS * HEAD_PAD)
    w_uk = w_kv_up[l][:, :, :QK_NOPE]
    w_uv = w_kv_up[l][:, :, QK_NOPE:]
    wuk_pad = jnp.concatenate([jnp.zeros((KV_RANK, N_HEADS, QK_ROPE), F32), w_uk,
                               jnp.zeros((KV_RANK, N_HEADS, HEAD_PAD - QK_NOPE - QK_ROPE), F32)], axis=-1)
    wukt_pad = jnp.transpose(wuk_pad, (1, 2, 0))
    odd_head = (jnp.arange(N_HEADS) % 2 == 1)[None, :, None]
    zv = jnp.zeros_like(w_uv)
    wuv_slot = jnp.where(odd_head, jnp.concatenate([zv, w_uv], axis=-1), jnp.concatenate([w_uv, zv], axis=-1))
    lane_id = jnp.arange(HEAD_PAD)[None, :]
    v_ones = jnp.where(odd_head[0], lane_id == 0, lane_id == V_DIM).astype(F32)
    eye = jnp.eye(N_HEADS, dtype=F32)
    wuv_pad = (w_uv.transpose(1, 0, 2)[:, :, None, :] * eye[:, None, :, None])
    conv_w_pad = jnp.concatenate([conv_w[l], jnp.zeros((CONV_HALO - CONV_WIDTH, CONV_CH), F32)], axis=0)
    w_router_pad = jnp.concatenate([w_router[l], jnp.zeros((D_MODEL, LANES - N_EXPERTS), F32)], axis=1)
    b_router_pad = jnp.concatenate([b_router[l], jnp.full((LANES - N_EXPERTS,), NEG_INF, F32)])
    return {
        'g_mix': g_mix[l][None], 'w_in_ext': w_in_ext.astype(BF16),
        'q_norm_g': q_norm_g[l][None], 'wq': wq.astype(BF16), 'wq_sw': wq_sw.astype(BF16),
        'kv_norm_g': kv_norm_g[l][None],
        'wuk_pad': wuk_pad.reshape(KV_RANK, N_HEADS * HEAD_PAD).astype(BF16),
        'wuv_slot': wuv_slot.reshape(KV_RANK, N_HEADS * HEAD_PAD).astype(BF16),
        'v_ones': v_ones.reshape(1, N_HEADS * HEAD_PAD),
        'wukt_pad': wukt_pad.astype(BF16),
        'wuv_pad': wuv_pad.reshape(N_HEADS, KV_RANK, N_HEADS * V_DIM).astype(BF16),
        'conv_w': conv_w_pad, 'conv_b': conv_b[l][None],
        'conv_ln_g': conv_ln_g[l][None], 'conv_ln_b': conv_ln_b[l][None],
        'w_out': w_out[l].astype(BF16), 'g_mem_q': g_mem_q[l][None], 'w_mq': w_mq[l].astype(BF16),
        'g_mem_kv': g_mem_kv[l][None], 'w_mk': w_mk[l].astype(BF16), 'w_mv': w_mv[l].astype(BF16),
        'w_mo': w_mo[l].astype(BF16), 'g_ffn': g_ffn[l][None],
        'w_router': w_router_pad.astype(BF16), 'b_router': b_router_pad[None],
        'w_gate': w_gate[l], 'w_up': w_up[l], 'w_down': w_down[l],
        'b_gate': b_gate[l][:, None, :], 'b_up': b_up[l][:, None, :], 'b_down': b_down[l][:, None, :],
        'g_final': g_final[None],
    }


def _rope_table(pos, q_scale):
    half = QK_ROPE // 2
    inv = ROPE_THETA ** (-jnp.arange(half, dtype=F32) / half)
    ang = pos.astype(F32)[:, None] * inv[None, :]
    cos, sin = jnp.cos(ang), jnp.sin(ang)
    n = pos.shape[0]
    ones = jnp.ones((n, QK_NOPE), F32)
    z = lambda k: jnp.zeros((n, k), F32)
    cq = q_scale * jnp.concatenate([cos, cos, ones, z(HEAD_PAD - QK_NOPE - QK_ROPE)], axis=1)
    sq = q_scale * jnp.concatenate([sin, sin, z(HEAD_PAD - QK_ROPE)], axis=1)
    ck = jnp.concatenate([cos, cos, z(HEAD_PAD - QK_ROPE)], axis=1)
    sk = jnp.concatenate([sin, sin, z(HEAD_PAD - QK_ROPE)], axis=1)
    return jnp.stack([cq, sq, ck, sk])


def _front(x, conv_prev, mem_k, mem_v, pos, w, paged):
    b, s, _ = x.shape
    t = b * s
    x2d = x.reshape(t, D_MODEL)
    sample = paged is not None
    if sample:
        tab = _rope_table(jnp.tile(pos, TOKEN_TILE // s), MLA_SCALE)
    else:
        tab = _rope_table(pos, MLA_SCALE * LOG2_E)
    tail = CONV_WIDTH - 1
    prev_pad = jnp.concatenate([jnp.zeros((b, CONV_HALO - tail, CONV_CH), F32), conv_prev], axis=1)
    outs = _inproj(x2d, tab, w, sample, prev_pad)
    u, ckv, kpe = outs[0], outs[1], outs[2]
    u3 = u.reshape(b, s, CONV_CH)
    if s >= tail:
        conv_tail = u3[:, s - tail:]
    else:
        conv_tail = jnp.concatenate([conv_prev[:, s:], u3], axis=1)
    ckv3 = ckv.reshape(b, s, KV_RANK)
    kpe3 = kpe.reshape(b, s, QK_ROPE)
    if sample:
        page_table, cache_ckv, cache_kpe_t = paged
        conv_out = _conv_sample(jnp.concatenate([prev_pad, u3], axis=1), w)
        attn = _attn_sample(page_table, outs[3], outs[4], ckv3, kpe3, cache_ckv, cache_kpe_t)
    else:
        conv_out = outs[6]
        q, k, v = (a.reshape(b, s, -1) for a in outs[3:6])
        attn = _attn_prompt(q, k, v).reshape(t, N_HEADS * V_DIM)
    conv2d = conv_out.reshape(t, CONV_CH)
    if not sample:
        return (x2d, conv2d, attn), conv_tail, ckv3, kpe3
    x1, qm = _post_a(x2d, conv2d, attn, w)
    om = _mem_attn_rows(qm.reshape(b, s, -1), mem_k, mem_v)
    return (x1, om.reshape(t, -1)), conv_tail, ckv3, kpe3


def kernel(x_prompt, x_sample, mem_prompt, cache_ckv, cache_kpe, page_table, cache_mem_k, cache_mem_v, state_conv, g_mix, w_in, conv_w, conv_b, conv_ln_g, conv_ln_b, q_norm_g, w_q_up, kv_norm_g, w_kv_up, w_out, g_mem_q, g_mem_kv, w_mq, w_mk, w_mv, w_mo, g_ffn, w_router, b_router, w_gate, b_gate, w_up, b_up, w_down, b_down, g_final):
    assert g_mix.shape[0] == 1, "single-layer step"
    w = _prep_weights(g_mix, w_in, conv_w, conv_b, conv_ln_g, conv_ln_b, q_norm_g, w_q_up, kv_norm_g, w_kv_up,
                      w_out, g_mem_q, g_mem_kv, w_mq, w_mk, w_mv, w_mo, g_ffn, w_router, b_router,
                      w_gate, b_gate, w_up, b_up, w_down, b_down, g_final)
    b_p, s_p, _ = x_prompt.shape
    b_s, t_s, _ = x_sample.shape
    past = page_table.shape[1] * PAGE_SIZE

    mk, mv, mk_wide, mv_wide = _mem_kv(mem_prompt.reshape(-1, D_MODEL), w)
    conv0 = jnp.zeros((b_p, CONV_WIDTH - 1, CONV_CH), F32)
    (x_p, conv_out_p, attn_p), conv_p, ckv_p, kpe_p = _front(
        x_prompt, conv0, None, None, jnp.arange(s_p, dtype=jnp.int32), w, None)
    (x1_s, om_s), conv_s, ckv_s, kpe_s = _front(
        x_sample, state_conv[0], cache_mem_k[0].reshape(-1, MEM_HD), cache_mem_v[0].reshape(-1, MEM_HD),
        past + jnp.arange(t_s, dtype=jnp.int32), w,
        (page_table, cache_ckv[0], jnp.swapaxes(cache_kpe[0], 1, 2)))

    x2, h3s, idx128, gate128, rank128, cnt128 = _post_b(
        x_p, conv_out_p, attn_p, mk_wide.reshape(b_p, MEM_TOKENS, -1), mv_wide.reshape(b_p, MEM_TOKENS, -1),
        x1_s, om_s, w)
    y_p, y_s = _moe_and_final(x2, h3s, idx128, gate128, rank128, cnt128, w, x_p.shape[0] // TOKEN_TILE)

    mem_shape = (1, b_p, MEM_TOKENS, MEM_HEADS, MEM_HD)
    return (y_p.reshape(b_p, s_p, D_MODEL), y_s.reshape(b_s, t_s, D_MODEL), ckv_p[None], kpe_p[None],
            mk.reshape(mem_shape), mv.reshape(mem_shape), conv_p[None], ckv_s[None], kpe_s[None], conv_s[None])
```

```python
import functools

import jax
import jax.numpy as jnp
from jax import lax
from jax.experimental import pallas as pl
from jax.experimental.pallas import tpu as pltpu

F32 = jnp.float32
BF16 = jnp.bfloat16

D_MODEL = 1024
PAGE_SIZE = 128
CONV_CH = 512
CONV_WIDTH = 31
N_HEADS = 8
QK_NOPE = 64
QK_ROPE = 32
V_DIM = 64
Q_RANK = 384
KV_RANK = 256
ROPE_THETA = 10000.0
MLA_SCALE = (QK_NOPE + QK_ROPE) ** -0.5
LOG2_E = 1.4426950408889634
MEM_TOKENS = 256
MEM_HEADS = 4
MEM_HD = 128
MEM_SCALE = MEM_HD ** -0.5
N_EXPERTS = 32
TOP_K = 4
SWIGLU_LIMIT = 7.0
SWIGLU_ALPHA = 1.702
EPS = 1e-6
NEG_INF = -1e30

LANES = 128
SUBLANES = 8
HEAD_PAD = 128
C_VAL, C_GATE, C_Q, C_CKV, C_KPE, C_KPE_SW, C_END = 0, 512, 1024, 1408, 1664, 1792, 1920
TOKEN_TILE = 256
DEST_TILE = 1024
ATTN_TILE = 512
ATTN_HEADS = 4
CONV_HALO = 32
CONV_CHUNK = 32
MEM_ATTN_ROWS = 32
PAGES_PER_CHUNK = 32
SLAB = 8
MOE_ROWS = 512
CAST_ROWS = 32
VMEM_LIMIT = 48 * 1024 * 1024


def _rms(x, g):
    return x * lax.rsqrt(jnp.mean(x * x, axis=-1, keepdims=True) + EPS) * g


def _dot(a, b):
    return jnp.dot(a, b, preferred_element_type=F32)


def _dot_t(a, b):
    return lax.dot_general(a, b, (((1,), (1,)), ((), ())), preferred_element_type=F32)


def _params(*sem, vmem=None):
    return pltpu.CompilerParams(dimension_semantics=sem, vmem_limit_bytes=vmem)


def _const_spec(shape):
    nd = len(shape)
    return pl.BlockSpec(shape, lambda *_: (0,) * nd)


def _inproj_common(x_ref, gmix_ref, win_ref, qg_ref, wq_ref, wqsw_ref, kvg_ref, tab_ref,
                   u_ref, ckv_ref, kpe_ref, after_u=None, after_q=None):
    h = _rms(x_ref[...], gmix_ref[...]).astype(BF16)
    glu = _dot(h, win_ref[:, C_VAL:C_Q])
    u = glu[:, :CONV_CH] * jax.nn.sigmoid(glu[:, CONV_CH:])
    u_ref[...] = u
    proj = _dot(h, win_ref[:, C_Q:C_END])
    if after_u is not None:
        after_u(u)
    qn = _rms(proj[:, 0:C_CKV - C_Q], qg_ref[...]).astype(BF16)
    ckv = _rms(proj[:, C_CKV - C_Q:C_KPE - C_Q], kvg_ref[...])
    ckv_ref[...] = ckv
    cq, sq, ck, sk = tab_ref[0], tab_ref[1], tab_ref[2], tab_ref[3]
    kpe_rot = proj[:, C_KPE - C_Q:C_KPE_SW - C_Q] * ck + proj[:, C_KPE_SW - C_Q:C_END - C_Q] * sk
    kpe_ref[...] = kpe_rot[:, :QK_ROPE]
    q = _dot(qn, wq_ref[...])
    qs = _dot(qn, wqsw_ref[...])
    if after_q is not None:
        after_q()
    q_heads = []
    for hd in range(N_HEADS):
        sl = slice(hd * HEAD_PAD, (hd + 1) * HEAD_PAD)
        q_heads.append(q[:, sl] * cq + qs[:, sl] * sq)
    return ckv, kpe_rot, q_heads


def _conv_fill_window(win_ref, halo, tile):
    tt = tile.shape[0]
    win_ref[0, 0:CONV_HALO, :] = halo
    win_ref[0, CONV_HALO:CONV_HALO + tt, :] = tile
    n = tt + CONV_HALO - SUBLANES
    for s in range(1, SUBLANES):
        win_ref[s, 0:n, :] = win_ref[0, s:s + n, :]


def _conv_chunk(win_ref, c, w_ref, b_ref, g_ref, lb_ref):
    acc = None
    for j in range(CONV_WIDTH):
        q, s = divmod(c * CONV_CHUNK + 2 + j, SUBLANES)
        term = win_ref[s, q * SUBLANES:q * SUBLANES + CONV_CHUNK, :] * w_ref[j:j + 1, :]
        acc = term if acc is None else acc + term
    return _ln_swish(acc + b_ref[...], g_ref[...], lb_ref[...])


def _inproj_prompt_kernel(tiles_per_seq, x_ref, gmix_ref, win_ref, qg_ref, wq_ref, wqsw_ref, kvg_ref, tab_ref,
                          wuk_ref, wuv_ref, vone_ref, prev_ref, cw_ref, cb_ref, cg_ref, clb_ref,
                          u_ref, ckv_ref, kpe_ref, q_ref, k_ref, v_ref, conv_ref, win_scr, carry_scr):
    tm = x_ref.shape[0]
    n_chunks = tm // CONV_CHUNK
    first = pl.program_id(0) % tiles_per_seq == 0

    def conv_chunks(lo, hi):
        for c in range(lo, hi):
            y = _conv_chunk(win_scr, c, cw_ref, cb_ref, cg_ref, clb_ref)
            conv_ref[c * CONV_CHUNK:(c + 1) * CONV_CHUNK, :] = y.astype(BF16)

    def after_u(u):
        _conv_fill_window(win_scr, jnp.where(first, prev_ref[0], carry_scr[...]), u)
        carry_scr[...] = u[tm - CONV_HALO:, :]
        conv_chunks(0, n_chunks * 3 // 4)

    ckv, kpe_rot, q_heads = _inproj_common(x_ref, gmix_ref, win_ref, qg_ref, wq_ref, wqsw_ref,
                                           kvg_ref, tab_ref, u_ref, ckv_ref, kpe_ref,
                                           after_u=after_u, after_q=lambda: conv_chunks(n_chunks * 3 // 4, n_chunks))
    ckv_b = ckv.astype(BF16)
    k_nope = _dot(ckv_b, wuk_ref[...])
    for hd in range(N_HEADS):
        sl = slice(hd * HEAD_PAD, (hd + 1) * HEAD_PAD)
        q_ref[:, sl] = q_heads[hd].astype(BF16)
        k_ref[:, sl] = (k_nope[:, sl] + kpe_rot).astype(BF16)
    v_ref[...] = (_dot(ckv_b, wuv_ref[...]) + vone_ref[...]).astype(BF16)


def _inproj_sample_kernel(x_ref, gmix_ref, win_ref, qg_ref, wq_ref, wqsw_ref, kvg_ref, tab_ref,
                          wukt_ref,
                          u_ref, ckv_ref, kpe_ref, qlat_ref, qpe_ref):
    _, _, q_heads = _inproj_common(x_ref, gmix_ref, win_ref, qg_ref, wq_ref, wqsw_ref,
                                   kvg_ref, tab_ref, u_ref, ckv_ref, kpe_ref)
    for hd in range(N_HEADS):
        qlat_ref[hd] = _dot(q_heads[hd].astype(BF16), wukt_ref[hd])
        qpe_ref[hd] = q_heads[hd][:, :QK_ROPE]


def _inproj(x2d, tab, w, sample, prev_pad=None):
    t = x2d.shape[0]
    tm = TOKEN_TILE
    n_tab = tab.shape[1] // tm
    row = lambda n: pl.BlockSpec((tm, n), lambda i: (i, 0))
    in_specs = [row(D_MODEL), _const_spec((1, D_MODEL)), _const_spec((D_MODEL, C_END)),
                _const_spec((1, Q_RANK)), _const_spec((Q_RANK, N_HEADS * HEAD_PAD)),
                _const_spec((Q_RANK, N_HEADS * HEAD_PAD)), _const_spec((1, KV_RANK)),
                pl.BlockSpec((4, tm, LANES), lambda i: (0, i % n_tab, 0))]
    args = [x2d, w['g_mix'], w['w_in_ext'], w['q_norm_g'], w['wq'], w['wq_sw'], w['kv_norm_g'], tab]
    out_shape = [jax.ShapeDtypeStruct((t, CONV_CH), F32), jax.ShapeDtypeStruct((t, KV_RANK), F32),
                 jax.ShapeDtypeStruct((t, QK_ROPE), F32)]
    out_specs = [row(CONV_CH), row(KV_RANK), row(QK_ROPE)]
    scratch = []
    if sample:
        body = _inproj_sample_kernel
        in_specs += [_const_spec((N_HEADS, HEAD_PAD, KV_RANK))]
        args += [w['wukt_pad']]
        out_shape += [jax.ShapeDtypeStruct((N_HEADS, t, KV_RANK), F32),
                      jax.ShapeDtypeStruct((N_HEADS, t, QK_ROPE), F32)]
        out_specs += [pl.BlockSpec((N_HEADS, tm, KV_RANK), lambda i: (0, i, 0)),
                      pl.BlockSpec((N_HEADS, tm, QK_ROPE), lambda i: (0, i, 0))]
    else:
        tiles_per_seq = t // prev_pad.shape[0] // tm
        body = functools.partial(_inproj_prompt_kernel, tiles_per_seq)
        in_specs += [_const_spec((KV_RANK, N_HEADS * HEAD_PAD)), _const_spec((KV_RANK, N_HEADS * HEAD_PAD)),
                     _const_spec((1, N_HEADS * HEAD_PAD)),
                     pl.BlockSpec((1, CONV_HALO, CONV_CH), lambda i: (i // tiles_per_seq, 0, 0)),
                     _const_spec((CONV_HALO, CONV_CH)), _const_spec((1, CONV_CH)),
                     _const_spec((1, CONV_CH)), _const_spec((1, CONV_CH))]
        args += [w['wuk_pad'], w['wuv_slot'], w['v_ones'], prev_pad,
                 w['conv_w'], w['conv_b'], w['conv_ln_g'], w['conv_ln_b']]
        out_shape += [jax.ShapeDtypeStruct((t, N_HEADS * HEAD_PAD), BF16)] * 3
        out_shape += [jax.ShapeDtypeStruct((t, CONV_CH), BF16)]
        out_specs += [row(N_HEADS * HEAD_PAD)] * 3 + [row(CONV_CH)]
        scratch = [pltpu.VMEM((SUBLANES, tm + CONV_HALO, CONV_CH), F32), pltpu.VMEM((CONV_HALO, CONV_CH), F32)]
    return pl.pallas_call(
        body, grid=(t // tm,), in_specs=in_specs, out_specs=out_specs, out_shape=out_shape,
        scratch_shapes=scratch,
        compiler_params=_params("parallel" if sample else "arbitrary", vmem=VMEM_LIMIT),
        name="inproj_sample" if sample else "inproj_prompt")(*args)


def _ln_swish(conv, g, b):
    mu = jnp.mean(conv, axis=-1, keepdims=True)
    xc = conv - mu
    var = jnp.mean(xc * xc, axis=-1, keepdims=True)
    y = xc * lax.rsqrt(var + EPS) * g + b
    return y * jax.nn.sigmoid(y)


def _conv_sample_kernel(win_ref, w_ref, b_ref, g_ref, lb_ref, o_ref):
    t = o_ref.shape[1]
    acc = win_ref[:, 2:2 + t, :] * w_ref[0:1, :]
    for j in range(1, CONV_WIDTH):
        acc = acc + win_ref[:, 2 + j:2 + j + t, :] * w_ref[j:j + 1, :]
    y = _ln_swish(acc + b_ref[...], g_ref[...], lb_ref[...])
    o_ref[...] = y


def _conv_sample(upad, w):
    b, s_pad, _ = upad.shape
    t = s_pad - CONV_HALO
    bb = 8
    return pl.pallas_call(
        _conv_sample_kernel, grid=(b // bb,),
        in_specs=[pl.BlockSpec((bb, s_pad, CONV_CH), lambda i: (i, 0, 0)),
                  _const_spec((CONV_HALO, CONV_CH)), _const_spec((1, CONV_CH)),
                  _const_spec((1, CONV_CH)), _const_spec((1, CONV_CH))],
        out_specs=pl.BlockSpec((bb, t, CONV_CH), lambda i: (i, 0, 0)),
        out_shape=jax.ShapeDtypeStruct((b, t, CONV_CH), F32),
        compiler_params=_params("parallel"),
        name="conv_sample")(upad, w['conv_w'], w['conv_b'], w['conv_ln_g'], w['conv_ln_b'])


def _attn_prompt_kernel(q_ref, k_ref, v_ref, o_ref, s_ref, m_ref):
    tq = q_ref.shape[1]
    qi = pl.program_id(2)
    nh = ATTN_HEADS
    head = lambda hd: slice(hd * HEAD_PAD, (hd + 1) * HEAD_PAD)
    rows = lax.broadcasted_iota(jnp.int32, (tq, tq), 0)
    cols = lax.broadcasted_iota(jnp.int32, (tq, tq), 1)
    lane = lax.broadcasted_iota(jnp.int32, (tq, LANES), 1)

    def scores(j, m_all, masked):
        start = pl.multiple_of(j * tq, tq)
        for hd in range(nh):
            s = _dot_t(q_ref[0, :, head(hd)], k_ref[0, pl.ds(start, tq), head(hd)])
            if masked:
                s = jnp.where(cols <= rows, s, NEG_INF)
            s_ref[hd, j] = s
            m_all = jnp.where(lane == hd, jnp.maximum(m_all, jnp.max(s, axis=-1, keepdims=True)), m_all)
        return m_all

    m_all = jnp.full((tq, LANES), NEG_INF, F32)
    m_all = lax.fori_loop(0, qi, functools.partial(scores, masked=False), m_all)
    m_all = scores(qi, m_all, True)
    for hd in range(nh):
        m_ref[hd] = jnp.broadcast_to(m_all[:, hd:hd + 1], (tq, LANES))

    def values(j, accs):
        start = pl.multiple_of(j * tq, tq)
        out = []
        for hd in range(nh):
            m = m_ref[hd]
            p = jnp.exp2(s_ref[hd, j] - jnp.concatenate([m] * (tq // LANES), axis=1)).astype(BF16)
            out.append(accs[hd] + _dot(p, v_ref[0, pl.ds(start, tq), head(hd)]))
        return tuple(out)

    accs = tuple(jnp.zeros((tq, HEAD_PAD), F32) for _ in range(nh))
    accs = lax.fori_loop(0, qi + 1, values, accs)
    for hp in range(nh // 2):
        even, odd = accs[2 * hp], accs[2 * hp + 1]
        o_even = even / even[:, V_DIM:V_DIM + 1]
        o_odd = odd / odd[:, 0:1]
        o_ref[0, :, hp * LANES:(hp + 1) * LANES] = jnp.where(lane < V_DIM, o_even, o_odd).astype(BF16)


def _attn_prompt(q, k, v):
    b, s, _ = q.shape
    tq = ATTN_TILE
    nh = ATTN_HEADS
    return pl.pallas_call(
        _attn_prompt_kernel, grid=(b, N_HEADS // nh, s // tq),
        in_specs=[pl.BlockSpec((1, tq, nh * HEAD_PAD), lambda bi, hq, qi: (bi, qi, hq)),
                  pl.BlockSpec((1, s, nh * HEAD_PAD), lambda bi, hq, qi: (bi, 0, hq)),
                  pl.BlockSpec((1, s, nh * HEAD_PAD), lambda bi, hq, qi: (bi, 0, hq))],
        out_specs=pl.BlockSpec((1, tq, nh * V_DIM), lambda bi, hq, qi: (bi, qi, hq)),
        out_shape=jax.ShapeDtypeStruct((b, s, N_HEADS * V_DIM), BF16),
        scratch_shapes=[pltpu.VMEM((nh, s // tq, tq, tq), F32), pltpu.VMEM((nh, tq, LANES), F32)],
        compiler_params=_params("parallel", "parallel", "arbitrary", vmem=VMEM_LIMIT),
        name="attn_prompt")(q, k, v)


def _attn_sample_kernel(pt_ref, ql_ref, qp_ref, cn_ref, kn_ref, ckv_hbm, kpe_hbm, o_ref,
                        ckv_buf, kpe_buf, sem):
    b = pl.program_id(0)
    n_pages = kpe_buf.shape[1]
    n_chunks = n_pages // PAGES_PER_CHUNK
    chunk = PAGES_PER_CHUNK * PAGE_SIZE
    t_new = cn_ref.shape[1]
    rows_q = N_HEADS * t_new

    def fetch(batch, slot):
        def one(p, _):
            page = pt_ref[batch * n_pages + p]
            rows = pl.ds(pl.multiple_of(p * PAGE_SIZE, PAGE_SIZE), PAGE_SIZE)
            pltpu.make_async_copy(ckv_hbm.at[page], ckv_buf.at[slot, rows], sem.at[0, slot]).start()
            pltpu.make_async_copy(kpe_hbm.at[page], kpe_buf.at[slot, p], sem.at[1, slot]).start()
            return 0
        lax.fori_loop(0, n_pages, one, 0, unroll=4)

    slot = b % 2

    @pl.when(b == 0)
    def _():
        fetch(0, 0)

    @pl.when(b + 1 < pl.num_programs(0))
    def _():
        fetch(b + 1, 1 - slot)

    pltpu.make_async_copy(ckv_buf.at[slot], ckv_buf.at[slot], sem.at[0, slot]).wait()
    pltpu.make_async_copy(kpe_buf.at[slot], kpe_buf.at[slot], sem.at[1, slot]).wait()

    ql = ql_ref[:, 0].reshape(rows_q, KV_RANK).astype(BF16)
    qp = qp_ref[:, 0].reshape(rows_q, QK_ROPE).astype(BF16)

    pad = PAGE_SIZE - t_new
    kc_new = jnp.concatenate([cn_ref[0], jnp.zeros((pad, KV_RANK), F32)], axis=0).astype(BF16)
    kp_new = jnp.concatenate([kn_ref[0], jnp.zeros((pad, QK_ROPE), F32)], axis=0).astype(BF16)
    s_new = _dot_t(ql, kc_new) + _dot_t(qp, kp_new)
    t_q = lax.broadcasted_iota(jnp.int32, s_new.shape, 0) % t_new
    cols = lax.broadcasted_iota(jnp.int32, s_new.shape, 1)
    s_new = jnp.where(cols <= t_q, s_new, NEG_INF)
    def part(s, values):
        m = jnp.max(s, axis=-1, keepdims=True)
        p = jnp.exp(s - m)
        return m, jnp.sum(p, axis=-1, keepdims=True), _dot(p.astype(BF16), values)

    def scores(c):
        kc = ckv_buf[slot, c * chunk:(c + 1) * chunk, :].astype(BF16)
        kpt = jnp.concatenate([kpe_buf[slot, c * PAGES_PER_CHUNK + i] for i in range(PAGES_PER_CHUNK)],
                              axis=1).astype(BF16)
        return _dot_t(ql, kc) + _dot(qp, kpt), kc

    parts = [part(s_new, kc_new)]
    nxt = scores(0)
    for c in range(n_chunks):
        cur = nxt
        if c + 1 < n_chunks:
            nxt = scores(c + 1)
        parts.append(part(*cur))
    m = parts[0][0]
    for mp, _, _ in parts[1:]:
        m = jnp.maximum(m, mp)
    l = jnp.zeros_like(m)
    acc = jnp.zeros((rows_q, KV_RANK), F32)
    for mp, lp, ap in parts:
        w = jnp.exp(mp - m)
        l = l + w * lp
        acc = acc + w * ap
    o_ref[:, 0] = (acc / l).reshape(N_HEADS, t_new, KV_RANK)


def _attn_sample(page_table, qlat, qpe, ckv_new, kpe_new, cache_ckv, cache_kpe_t):
    bs, n_pages = page_table.shape
    t_new = ckv_new.shape[1]
    qlat4 = qlat.reshape(N_HEADS, bs, t_new, KV_RANK)
    qpe4 = qpe.reshape(N_HEADS, bs, t_new, QK_ROPE)
    past = n_pages * PAGE_SIZE
    in_specs = [pl.BlockSpec((N_HEADS, 1, t_new, KV_RANK), lambda b, pt: (0, b, 0, 0)),
                pl.BlockSpec((N_HEADS, 1, t_new, QK_ROPE), lambda b, pt: (0, b, 0, 0)),
                pl.BlockSpec((1, t_new, KV_RANK), lambda b, pt: (b, 0, 0)),
                pl.BlockSpec((1, t_new, QK_ROPE), lambda b, pt: (b, 0, 0)),
                pl.BlockSpec(memory_space=pl.ANY), pl.BlockSpec(memory_space=pl.ANY)]
    grid_spec = pltpu.PrefetchScalarGridSpec(
        num_scalar_prefetch=1, grid=(bs,), in_specs=in_specs,
        out_specs=pl.BlockSpec((N_HEADS, 1, t_new, KV_RANK), lambda b, pt: (0, b, 0, 0)),
        scratch_shapes=[pltpu.VMEM((2, past, KV_RANK), F32),
                        pltpu.VMEM((2, n_pages, QK_ROPE, PAGE_SIZE), F32),
                        pltpu.SemaphoreType.DMA((2, 2))])
    o = pl.pallas_call(
        _attn_sample_kernel, grid_spec=grid_spec,
        out_shape=jax.ShapeDtypeStruct((N_HEADS, bs, t_new, KV_RANK), F32),
        compiler_params=_params("arbitrary", vmem=VMEM_LIMIT),
        name="attn_sample")(page_table.reshape(-1), qlat4, qpe4, ckv_new, kpe_new, cache_ckv, cache_kpe_t)
    return o.reshape(N_HEADS, bs * t_new, KV_RANK)


def _post_a_kernel(x_ref, conv_ref, olat_ref, wuv_ref, wout_ref, g_ref, wmq_ref, x1_ref, qm_ref):
    attn = None
    for hd in range(N_HEADS):
        d = _dot(olat_ref[hd].astype(BF16), wuv_ref[hd])
        attn = d if attn is None else attn + d
    mix = (_dot(conv_ref[...].astype(BF16), wout_ref[0:CONV_CH, :])
           + _dot(attn.astype(BF16), wout_ref[CONV_CH:, :]))
    x1 = x_ref[...] + mix
    x1_ref[...] = x1
    qm_ref[...] = _dot(_rms(x1, g_ref[...]).astype(BF16), wmq_ref[...])


def _post_a(x2d, conv2d, o_lat, w):
    t = x2d.shape[0]
    tm = TOKEN_TILE
    row = lambda n: pl.BlockSpec((tm, n), lambda i: (i, 0))
    mem_w = MEM_HEADS * MEM_HD
    return pl.pallas_call(
        _post_a_kernel, grid=(t // tm,),
        in_specs=[row(D_MODEL), row(CONV_CH), pl.BlockSpec((N_HEADS, tm, KV_RANK), lambda i: (0, i, 0)),
                  _const_spec((N_HEADS, KV_RANK, N_HEADS * V_DIM)),
                  _const_spec((D_MODEL, D_MODEL)), _const_spec((1, D_MODEL)), _const_spec((D_MODEL, mem_w))],
        out_specs=[row(D_MODEL), row(mem_w)],
        out_shape=[jax.ShapeDtypeStruct((t, D_MODEL), F32), jax.ShapeDtypeStruct((t, mem_w), F32)],
        compiler_params=_params("parallel", vmem=VMEM_LIMIT),
        name="post_a_sample")(x2d, conv2d, o_lat, w['wuv_pad'], w['w_out'], w['g_mem_q'], w['w_mq'])


def _mem_kv_kernel(m_ref, g_ref, wk_ref, wv_ref, k_ref, v_ref, kw_ref, vw_ref):
    tm = m_ref.shape[0]
    m = _rms(m_ref[...], g_ref[...]).astype(BF16)
    k = _dot(m, wk_ref[...])
    v = _dot(m, wv_ref[...])
    kw_ref[...] = k
    vw_ref[...] = v
    for hd in range(MEM_HEADS):
        sl = slice(hd * MEM_HD, (hd + 1) * MEM_HD)
        k_ref[pl.ds(hd, tm, stride=MEM_HEADS), :] = k[:, sl]
        v_ref[pl.ds(hd, tm, stride=MEM_HEADS), :] = v[:, sl]


def _mem_kv(mem2d, w):
    t = mem2d.shape[0]
    tm = TOKEN_TILE
    mem_w = MEM_HEADS * MEM_HD
    rows = pl.BlockSpec((tm * MEM_HEADS, MEM_HD), lambda i: (i, 0))
    wide = pl.BlockSpec((tm, mem_w), lambda i: (i, 0))
    return pl.pallas_call(
        _mem_kv_kernel, grid=(t // tm,),
        in_specs=[pl.BlockSpec((tm, D_MODEL), lambda i: (i, 0)), _const_spec((1, D_MODEL)),
                  _const_spec((D_MODEL, mem_w)), _const_spec((D_MODEL, mem_w))],
        out_specs=[rows, rows, wide, wide],
        out_shape=[jax.ShapeDtypeStruct((t * MEM_HEADS, MEM_HD), F32)] * 2
                  + [jax.ShapeDtypeStruct((t, mem_w), F32)] * 2,
        compiler_params=_params("parallel"),
        name="mem_kv")(mem2d, w['g_mem_kv'], w['w_mk'], w['w_mv'])


def _mem_attn_rows_kernel(q_ref, k_ref, v_ref, o_ref):
    bb, tq, _ = q_ref.shape
    rows = MEM_TOKENS * MEM_HEADS
    head = lambda hd: slice(hd * MEM_HD, (hd + 1) * MEM_HD)
    q_head = lax.broadcasted_iota(jnp.int32, (MEM_HEADS * tq, rows), 0) // tq
    k_head = lax.broadcasted_iota(jnp.int32, (MEM_HEADS * tq, rows), 1) % MEM_HEADS
    own = q_head == k_head
    for bi in range(bb):
        q = jnp.concatenate([q_ref[bi, :, head(hd)] for hd in range(MEM_HEADS)], axis=0).astype(BF16)
        k = k_ref[bi * rows:(bi + 1) * rows, :].astype(BF16)
        v = v_ref[bi * rows:(bi + 1) * rows, :].astype(BF16)
        s = jnp.where(own, _dot_t(q, k) * MEM_SCALE, NEG_INF)
        p = jnp.exp(s - jnp.max(s, axis=-1, keepdims=True))
        p = p / jnp.sum(p, axis=-1, keepdims=True)
        o = _dot(p.astype(BF16), v)
        for hd in range(MEM_HEADS):
            o_ref[bi, :, head(hd)] = o[hd * tq:(hd + 1) * tq, :]


def _mem_attn_rows(qm, mem_k, mem_v):
    b, s, mem_w = qm.shape
    bb = max(1, MEM_ATTN_ROWS // s)
    kv = pl.BlockSpec((bb * MEM_TOKENS * MEM_HEADS, MEM_HD), lambda bi: (bi, 0))
    return pl.pallas_call(
        _mem_attn_rows_kernel, grid=(b // bb,),
        in_specs=[pl.BlockSpec((bb, s, mem_w), lambda bi: (bi, 0, 0)), kv, kv],
        out_specs=pl.BlockSpec((bb, s, mem_w), lambda bi: (bi, 0, 0)),
        out_shape=jax.ShapeDtypeStruct((b, s, mem_w), F32),
        compiler_params=_params("parallel"),
        name="mem_attn_rows")(qm, mem_k, mem_v)


def _slab_rows(j, n):
    return pl.ds(j, n, stride=SLAB)


def _post_b_kernel(n_p, x_ref, conv_ref, attn_ref, mk_ref, mv_ref, wout_ref, gq_ref, wmq_ref,
                   x1s_ref, oms_ref, wmo_ref, g_ref, wr_ref, br_ref,
                   x2_ref, h3_ref, idx_ref, gate_ref, rank_ref, cnt_ref, carry_ref, x1_scr, om_scr):
    tm = x2_ref.shape[0]
    i = pl.program_id(0)

    @pl.when(i == 0)
    def _():
        carry_ref[...] = jnp.zeros(carry_ref.shape, F32)

    @pl.when(i < n_p)
    def _():
        mix = _dot(conv_ref[...], wout_ref[0:CONV_CH, :]) + _dot(attn_ref[...], wout_ref[CONV_CH:, :])
        x1p = x_ref[...] + mix
        x1_scr[...] = x1p
        qm = _dot(_rms(x1p, gq_ref[...]).astype(BF16), wmq_ref[...])
        for hd in range(MEM_HEADS):
            sl = slice(hd * MEM_HD, (hd + 1) * MEM_HD)
            s = _dot_t(qm[:, sl].astype(BF16), mk_ref[0, :, sl].astype(BF16)) * MEM_SCALE
            p = jnp.exp(s - jnp.max(s, axis=-1, keepdims=True))
            p = p / jnp.sum(p, axis=-1, keepdims=True)
            om_scr[:, sl] = _dot(p.astype(BF16), mv_ref[0, :, sl].astype(BF16))

    @pl.when(i >= n_p)
    def _():
        x1_scr[...] = x1s_ref[...]
        om_scr[...] = oms_ref[...]

    x1 = x1_scr[...]
    om = om_scr[...]
    x2 = x1 + _dot(om.astype(BF16), wmo_ref[...])
    x2_ref[...] = x2
    h3 = _rms(x2, g_ref[...])
    for j in range(SLAB):
        h3_ref[_slab_rows(j, tm), :] = h3[:, j * LANES:(j + 1) * LANES]
    logits = _dot(h3.astype(BF16), wr_ref[...]) + br_ref[...]
    lane = lax.broadcasted_iota(jnp.int32, logits.shape, 1)
    lane_f = lane.astype(F32)
    vals, hots = [], []
    idx_out = jnp.zeros(logits.shape, F32)
    for kk in range(TOP_K):
        mx = jnp.max(logits, axis=-1, keepdims=True)
        first = jnp.min(jnp.where(logits == mx, lane_f, float(LANES)), axis=-1, keepdims=True)
        hot = lane_f == first
        logits = jnp.where(hot, -jnp.inf, logits)
        vals.append(mx)
        hots.append(hot)
        idx_out = jnp.where(lane == kk, first, idx_out)
    exps = [jnp.exp(v - vals[0]) for v in vals]
    denom = exps[0] + exps[1] + exps[2] + exps[3]
    chosen = jnp.zeros(logits.shape, F32)
    gate_out = jnp.zeros(logits.shape, F32)
    for kk in range(TOP_K):
        chosen = chosen + hots[kk].astype(F32)
        gate_out = jnp.where(lane == kk, exps[kk] / denom, gate_out)
    r_i = lax.broadcasted_iota(jnp.int32, (tm, tm), 0)
    c_i = lax.broadcasted_iota(jnp.int32, (tm, tm), 1)
    tril = (c_i < r_i).astype(BF16)
    before = _dot(tril, chosen.astype(BF16)) + carry_ref[...]
    rank_out = jnp.zeros(logits.shape, F32)
    for kk in range(TOP_K):
        rk = jnp.sum(jnp.where(hots[kk], before, 0.0), axis=-1, keepdims=True)
        rank_out = jnp.where(lane == kk, rk, rank_out)
    carry = carry_ref[...] + jnp.sum(chosen, axis=0, keepdims=True)
    carry_ref[...] = carry
    idx_ref[...] = idx_out.astype(jnp.int32)
    gate_ref[...] = gate_out
    rank_ref[...] = rank_out.astype(jnp.int32)
    cnt_ref[...] = carry.astype(jnp.int32)


def _post_b(x_p, conv_p, attn_p, mk_wide, mv_wide, x1_s, om_s, w):
    tm = TOKEN_TILE
    n_p, n_s = x_p.shape[0] // tm, x1_s.shape[0] // tm
    tiles_per_seq = n_p // mk_wide.shape[0]
    t = (n_p + n_s) * tm
    mem_w = MEM_HEADS * MEM_HD
    row = lambda n: pl.BlockSpec((tm, n), lambda i: (i, 0))
    row_p = lambda n: pl.BlockSpec((tm, n), lambda i: (jnp.minimum(i, n_p - 1), 0))
    row_s = lambda n: pl.BlockSpec((tm, n), lambda i: (jnp.maximum(i - n_p, 0), 0))
    mem = pl.BlockSpec((1, MEM_TOKENS, mem_w), lambda i: (jnp.minimum(i, n_p - 1) // tiles_per_seq, 0, 0))
    return pl.pallas_call(
        functools.partial(_post_b_kernel, n_p), grid=(n_p + n_s,),
        in_specs=[row_p(D_MODEL), row_p(CONV_CH), row_p(N_HEADS * V_DIM), mem, mem,
                  _const_spec((D_MODEL, D_MODEL)), _const_spec((1, D_MODEL)), _const_spec((D_MODEL, mem_w)),
                  row_s(D_MODEL), row_s(mem_w),
                  _const_spec((mem_w, D_MODEL)), _const_spec((1, D_MODEL)),
                  _const_spec((D_MODEL, LANES)), _const_spec((1, LANES))],
        out_specs=[row(D_MODEL), pl.BlockSpec((tm * SLAB, LANES), lambda i: (i, 0)),
                   row(LANES), row(LANES), row(LANES), _const_spec((1, LANES))],
        out_shape=[jax.ShapeDtypeStruct((t, D_MODEL), F32), jax.ShapeDtypeStruct((t * SLAB, LANES), F32),
                   jax.ShapeDtypeStruct((t, LANES), jnp.int32), jax.ShapeDtypeStruct((t, LANES), F32),
                   jax.ShapeDtypeStruct((t, LANES), jnp.int32), jax.ShapeDtypeStruct((1, LANES), jnp.int32)],
        scratch_shapes=[pltpu.VMEM((1, LANES), F32), pltpu.VMEM((tm, D_MODEL), F32), pltpu.VMEM((tm, mem_w), F32)],
        compiler_params=_params("arbitrary", vmem=VMEM_LIMIT),
        name="post_b")(x_p, conv_p, attn_p, mk_wide, mv_wide, w['w_out'], w['g_mem_q'], w['w_mq'],
                       x1_s, om_s, w['w_mo'], w['g_ffn'], w['w_router'], w['b_router'])


def _slab(row):
    return pl.ds(pl.multiple_of(row * SLAB, SLAB), SLAB)


def _dest_kernel(start_ref, idx_ref, rank_ref, dest_ref):
    idx = idx_ref[...]
    table = [start_ref[e] for e in range(N_EXPERTS)]
    bit = 0
    while len(table) > 1:
        odd = ((idx >> bit) & 1) == 1
        table = [jnp.where(odd, table[2 * i + 1], table[2 * i]) for i in range(len(table) // 2)]
        bit += 1
    dest_ref[...] = rank_ref[...] + table[0]


def _dest_rows(idx128, rank128, pad_start):
    t = idx128.shape[0]
    tm = DEST_TILE if t % DEST_TILE == 0 else TOKEN_TILE
    spec = pl.BlockSpec((tm, LANES), lambda i, *_: (i, 0))
    grid_spec = pltpu.PrefetchScalarGridSpec(num_scalar_prefetch=1, grid=(t // tm,),
                                             in_specs=[spec, spec], out_specs=spec)
    return pl.pallas_call(
        _dest_kernel, grid_spec=grid_spec, out_shape=jax.ShapeDtypeStruct((t, LANES), jnp.int32),
        compiler_params=_params("parallel"), name="moe_dest")(pad_start, idx128, rank128)


def _dispatch_kernel(lo_ref, hi_ref, dest_ref, h_hbm, xs_hbm, hbuf, zero_ref, sem_in, sem_out, sem_z):
    i = pl.program_id(0)
    n = pl.num_programs(0)
    rows_per_tile = hbuf.shape[1]
    tm = rows_per_tile // SLAB

    def load(tile, slot):
        rows = pl.ds(pl.multiple_of(tile * rows_per_tile, SLAB), rows_per_tile)
        return pltpu.make_async_copy(h_hbm.at[rows], hbuf.at[slot], sem_in.at[slot])

    def wait_scatter(slot):
        for _ in range(TOP_K):
            pltpu.make_async_copy(hbuf.at[0], xs_hbm.at[pl.ds(0, rows_per_tile)], sem_out.at[slot]).wait()

    @pl.when(i == 0)
    def _():
        load(0, 0).start()

    @pl.when(i + 1 < n)
    def _():
        load(i + 1, (i + 1) % 3).start()

    cur = i % 3
    load(i, cur).wait()

    def issue(t, _):
        for k in range(TOP_K):
            d = dest_ref[t * TOP_K + k]
            pltpu.make_async_copy(hbuf.at[cur, _slab(t)], xs_hbm.at[_slab(d)],
                                  sem_out.at[i % 2]).start(priority=k % 2)
        return 0

    lax.fori_loop(0, tm, issue, 0, unroll=4)

    @pl.when(i > 0)
    def _():
        wait_scatter((i - 1) % 2)

    @pl.when(i == n - 1)
    def _():
        wait_scatter(i % 2)
        zero_ref[...] = jnp.zeros(zero_ref.shape, F32)
        bm = zero_ref.shape[0] // SLAB
        for e in range(N_EXPERTS):
            lo = lo_ref[e]
            pad = hi_ref[e] - lo
            for bit in range(bm.bit_length() - 1):
                size = 1 << bit

                @pl.when((pad >> bit) & 1 == 1)
                def _():
                    first = lo + (pad & (size - 1))
                    rows = pl.ds(pl.multiple_of(first * SLAB, SLAB), size * SLAB)
                    cp = pltpu.make_async_copy(zero_ref.at[pl.ds(0, size * SLAB)], xs_hbm.at[rows], sem_z)
                    cp.start()
                    cp.wait()

        def tail(blk, _):
            rows = pl.ds(pl.multiple_of(blk * bm * SLAB, SLAB), bm * SLAB)
            cp = pltpu.make_async_copy(zero_ref, xs_hbm.at[rows], sem_z)
            cp.start()
            cp.wait()
            return 0

        lax.fori_loop(hi_ref[N_EXPERTS - 1] // bm, xs_hbm.shape[0] // (bm * SLAB), tail, 0)


def _dispatch(h3s, dest_flat, pad_lo, pad_hi, n_rows):
    tm = TOKEN_TILE
    t = h3s.shape[0] // SLAB
    grid_spec = pltpu.PrefetchScalarGridSpec(
        num_scalar_prefetch=2, grid=(t // tm,),
        in_specs=[pl.BlockSpec((tm * TOP_K,), lambda i, *_: (i,), memory_space=pltpu.SMEM),
                  pl.BlockSpec(memory_space=pl.ANY)],
        out_specs=pl.BlockSpec(memory_space=pl.ANY),
        scratch_shapes=[pltpu.VMEM((3, tm * SLAB, LANES), F32), pltpu.VMEM((MOE_ROWS * SLAB, LANES), F32),
                        pltpu.SemaphoreType.DMA((3,)), pltpu.SemaphoreType.DMA((2,)), pltpu.SemaphoreType.DMA])
    return pl.pallas_call(
        _dispatch_kernel, grid_spec=grid_spec,
        out_shape=jax.ShapeDtypeStruct((n_rows * SLAB, LANES), F32),
        compiler_params=pltpu.CompilerParams(dimension_semantics=("arbitrary",), has_side_effects=True),
        name="moe_dispatch")(pad_lo, pad_hi, dest_flat, h3s)


def _expert_kernel(src_ref, exp_ref, first_ref, valid_ref, next_ref, x_ref, wg_hbm, wu_hbm, wd_hbm,
                   bg_ref, bu_ref, bd_ref, y_ref, wf_ref, wb_ref, xb_ref, sem):
    b = pl.program_id(0)
    bm = xb_ref.shape[0]
    w_hbm = (wg_hbm, wu_hbm, wd_hbm)

    def fetch(e):
        for i in range(3):
            pltpu.make_async_copy(w_hbm[i].at[e], wf_ref.at[i], sem).start()

    @pl.when(b == 0)
    def _():
        fetch(exp_ref[0])

    @pl.when(first_ref[b] == 1)
    def _():
        for i in range(3):
            pltpu.make_async_copy(w_hbm[i].at[0], wf_ref.at[i], sem).wait()
        def cast(r, _):
            rows = pl.ds(pl.multiple_of(r * CAST_ROWS, CAST_ROWS), CAST_ROWS)
            for i in range(3):
                wb_ref[i, rows, :] = wf_ref[i, rows, :].astype(BF16)
            return 0

        lax.fori_loop(0, D_MODEL // CAST_ROWS, cast, 0)

        @pl.when(next_ref[b] >= 0)
        def _():
            fetch(next_ref[b])

    @pl.when(valid_ref[b] == 1)
    def _():
        for j in range(SLAB):
            xb_ref[:, j * LANES:(j + 1) * LANES] = x_ref[_slab_rows(j, bm), :].astype(BF16)
        x = xb_ref[...]
        g = _dot(x, wb_ref[0]) + bg_ref[0]
        u = _dot(x, wb_ref[1]) + bu_ref[0]
        g = jnp.minimum(g, SWIGLU_LIMIT)
        u = jnp.clip(u, -SWIGLU_LIMIT, SWIGLU_LIMIT)
        a = (u + 1.0) * (g * jax.nn.sigmoid(SWIGLU_ALPHA * g))
        y = _dot(a.astype(BF16), wb_ref[2]) + bd_ref[0]
        for j in range(SLAB):
            y_ref[_slab_rows(j, bm), :] = y[:, j * LANES:(j + 1) * LANES]

    @pl.when(valid_ref[b] == 0)
    def _():
        y_ref[...] = jnp.zeros(y_ref.shape, F32)


def _experts(xs, blk_src, blk_exp, blk_first, blk_valid, blk_next, w):
    bm = MOE_ROWS
    n_blocks = xs.shape[0] // (bm * SLAB)
    hbm = pl.BlockSpec(memory_space=pl.ANY)
    bspec = pl.BlockSpec((1, 1, D_MODEL), lambda b, src, ex, *_: (ex[b], 0, 0))
    grid_spec = pltpu.PrefetchScalarGridSpec(
        num_scalar_prefetch=5, grid=(n_blocks,),
        in_specs=[pl.BlockSpec((bm * SLAB, LANES), lambda b, src, *_: (src[b], 0)),
                  hbm, hbm, hbm, bspec, bspec, bspec],
        out_specs=pl.BlockSpec((bm * SLAB, LANES), lambda b, *_: (b, 0)),
        scratch_shapes=[pltpu.VMEM((3, D_MODEL, D_MODEL), F32), pltpu.VMEM((3, D_MODEL, D_MODEL), BF16),
                        pltpu.VMEM((bm, D_MODEL), BF16), pltpu.SemaphoreType.DMA])
    return pl.pallas_call(
        _expert_kernel, grid_spec=grid_spec,
        out_shape=jax.ShapeDtypeStruct(xs.shape, F32),
        compiler_params=_params("arbitrary", vmem=VMEM_LIMIT),
        name="moe_experts")(blk_src, blk_exp, blk_first, blk_valid, blk_next, xs,
                            w['w_gate'], w['w_up'], w['w_down'], w['b_gate'], w['b_up'], w['b_down'])


def _combine_kernel(n_p, dest_ref, dest_nx_ref, gate_ref, yb_hbm, x2_ref, g_ref,
                    yp_ref, ys_ref, rows_ref, ysum_ref, y_ref, sem):
    i = pl.program_id(0)
    tm = x2_ref.shape[0]

    def gather(d_ref, slot):
        def issue(t, _):
            for k in range(TOP_K):
                r = t * TOP_K + k
                pltpu.make_async_copy(yb_hbm.at[_slab(d_ref[r])], rows_ref.at[slot, _slab(r)],
                                      sem.at[slot]).start(priority=k % 2)
            return 0
        lax.fori_loop(0, tm, issue, 0, unroll=4)

    slot = i % 2

    @pl.when(i == 0)
    def _():
        gather(dest_ref, 0)

    @pl.when(i + 1 < pl.num_programs(0))
    def _():
        gather(dest_nx_ref, 1 - slot)

    for _ in range(TOP_K):
        pltpu.make_async_copy(yb_hbm.at[pl.ds(0, tm * SLAB)], rows_ref.at[slot, pl.ds(0, tm * SLAB)],
                              sem.at[slot]).wait()

    def token(t, _):
        acc = rows_ref[slot, _slab(t * TOP_K)] * gate_ref[t * TOP_K]
        for k in range(1, TOP_K):
            acc = acc + rows_ref[slot, _slab(t * TOP_K + k)] * gate_ref[t * TOP_K + k]
        ysum_ref[_slab(t)] = acc
        return 0

    lax.fori_loop(0, tm, token, 0, unroll=4)

    ss = jnp.zeros((tm, 1), F32)
    for j in range(SLAB):
        y = x2_ref[:, j * LANES:(j + 1) * LANES] + ysum_ref[_slab_rows(j, tm), :]
        y_ref[:, j * LANES:(j + 1) * LANES] = y
        ss = ss + jnp.sum(y * y, axis=-1, keepdims=True)
    out = y_ref[...] * lax.rsqrt(ss * (1.0 / D_MODEL) + EPS) * g_ref[...]

    @pl.when(i < n_p)
    def _():
        yp_ref[...] = out

    @pl.when(i >= n_p)
    def _():
        ys_ref[...] = out


def _combine(yb, dest_flat, gate_flat, x2, g_final, n_p):
    t = x2.shape[0]
    tm = TOKEN_TILE
    n = t // tm
    n_s = n - n_p
    cur = lambda i: (i,)
    nxt = lambda i: (jnp.minimum(i + 1, n - 1),)
    smem = lambda index_map: pl.BlockSpec((tm * TOP_K,), index_map, memory_space=pltpu.SMEM)
    return pl.pallas_call(
        functools.partial(_combine_kernel, n_p), grid=(n,),
        in_specs=[smem(cur), smem(nxt), smem(cur),
                  pl.BlockSpec(memory_space=pl.ANY),
                  pl.BlockSpec((tm, D_MODEL), lambda i: (i, 0)),
                  pl.BlockSpec((1, D_MODEL), lambda i: (0, 0))],
        out_specs=[pl.BlockSpec((tm, D_MODEL), lambda i: (jnp.minimum(i, n_p - 1), 0)),
                   pl.BlockSpec((tm, D_MODEL), lambda i: (jnp.maximum(i - n_p, 0), 0))],
        out_shape=[jax.ShapeDtypeStruct((n_p * tm, D_MODEL), F32), jax.ShapeDtypeStruct((n_s * tm, D_MODEL), F32)],
        scratch_shapes=[pltpu.VMEM((2, tm * TOP_K * SLAB, LANES), F32), pltpu.VMEM((tm * SLAB, LANES), F32),
                        pltpu.VMEM((tm, D_MODEL), F32), pltpu.SemaphoreType.DMA((2,))],
        compiler_params=_params("arbitrary", vmem=VMEM_LIMIT),
        name="moe_combine")(dest_flat, dest_flat, gate_flat, yb, x2, g_final)


def _moe_and_final(x2, h3s, idx128, gate128, rank128, cnt128, w, n_p):
    t = x2.shape[0]
    bm = MOE_ROWS
    n_assign = t * TOP_K
    n_blocks = (n_assign + N_EXPERTS * (bm - 1) + bm - 1) // bm
    counts = cnt128[0, :N_EXPERTS]
    padded = ((counts + bm - 1) // bm) * bm
    pad_end = jnp.cumsum(padded).astype(jnp.int32)
    pad_start = pad_end - padded
    dest_flat = _dest_rows(idx128, rank128, pad_start)[:, :TOP_K].reshape(-1)
    blk = jnp.arange(n_blocks, dtype=jnp.int32)
    blk_valid = (blk * bm < pad_end[-1]).astype(jnp.int32)
    blk_src = jnp.minimum(blk, jnp.maximum(pad_end[-1] // bm - 1, 0))
    blk_exp = jnp.sum((pad_end[None, :] <= (blk_src * bm)[:, None]).astype(jnp.int32), axis=1)
    blk_exp = jnp.minimum(blk_exp, N_EXPERTS - 1)
    blk_first = jnp.concatenate([jnp.ones((1,), jnp.int32),
                                 (blk_exp[1:] != blk_exp[:-1]).astype(jnp.int32)])
    later_start = (blk[None, :] > blk[:, None]) & (blk_first[None, :] == 1)
    next_pos = jnp.min(jnp.where(later_start, blk[None, :], n_blocks), axis=1)
    blk_next = jnp.where(next_pos < n_blocks, blk_exp[jnp.minimum(next_pos, n_blocks - 1)], -1)
    xs = _dispatch(h3s, dest_flat, pad_start + counts, pad_end, n_blocks * bm)
    yb = _experts(xs, blk_src, blk_exp, blk_first, blk_valid, blk_next.astype(jnp.int32), w)
    gate_flat = gate128[:, :TOP_K].reshape(-1)
    return _combine(yb, dest_flat, gate_flat, x2, w['g_final'], n_p)


def _swap_halves(wcols):
    half = QK_ROPE // 2
    return jnp.concatenate([-wcols[..., half:], wcols[..., :half]], axis=-1)


def _prep_weights(g_mix, w_in, conv_w, conv_b, conv_ln_g, conv_ln_b, q_norm_g, w_q_up, kv_norm_g, w_kv_up,
                  w_out, g_mem_q, g_mem_kv, w_mq, w_mk, w_mv, w_mo, g_ffn, w_router, b_router,
                  w_gate, b_gate, w_up, b_up, w_down, b_down, g_final):
    l = 0
    w_kpe = w_in[l][:, C_KPE:]
    zpad = jnp.zeros((D_MODEL, HEAD_PAD - QK_ROPE), F32)
    w_in_ext = jnp.concatenate([w_in[l][:, :C_KPE], w_kpe, zpad, _swap_halves(w_kpe), zpad], axis=1)
    wq3 = w_q_up[l].reshape(Q_RANK, N_HEADS, QK_NOPE + QK_ROPE)
    q_nope, q_rope = wq3[..., :QK_NOPE], wq3[..., QK_NOPE:]
    z32 = jnp.zeros((Q_RANK, N_HEADS, HEAD_PAD - QK_NOPE - QK_ROPE), F32)
    wq = jnp.concatenate([q_rope, q_nope, z32], axis=-1).reshape(Q_RANK, N_HEADS * HEAD_PAD)
    wq_sw = jnp.concatenate([_swap_halves(q_rope), jnp.zeros_like(q_nope), z32], axis=-1)
    wq_sw = wq_sw.reshape(Q_RANK, N_HEADS * HEAD_PAD)
    w_uk = w_kv_up[l][:, :, :QK_NOPE]
    w_uv = w_kv_up[l][:, :, QK_NOPE:]
    wuk_pad = jnp.concatenate([jnp.zeros((KV_RANK, N_HEADS, QK_ROPE), F32), w_uk,
                               jnp.zeros((KV_RANK, N_HEADS, HEAD_PAD - QK_NOPE - QK_ROPE), F32)], axis=-1)
    wukt_pad = jnp.transpose(wuk_pad, (1, 2, 0))
    odd_head = (jnp.arange(N_HEADS) % 2 == 1)[None, :, None]
    zv = jnp.zeros_like(w_uv)
    wuv_slot = jnp.where(odd_head, jnp.concatenate([zv, w_uv], axis=-1), jnp.concatenate([w_uv, zv], axis=-1))
    lane_id = jnp.arange(HEAD_PAD)[None, :]
    v_ones = jnp.where(odd_head[0], lane_id == 0, lane_id == V_DIM).astype(F32)
    eye = jnp.eye(N_HEADS, dtype=F32)
    wuv_pad = (w_uv.transpose(1, 0, 2)[:, :, None, :] * eye[:, None, :, None])
    conv_w_pad = jnp.concatenate([conv_w[l], jnp.zeros((CONV_HALO - CONV_WIDTH, CONV_CH), F32)], axis=0)
    w_router_pad = jnp.concatenate([w_router[l], jnp.zeros((D_MODEL, LANES - N_EXPERTS), F32)], axis=1)
    b_router_pad = jnp.concatenate([b_router[l], jnp.full((LANES - N_EXPERTS,), NEG_INF, F32)])
    return {
        'g_mix': g_mix[l][None], 'w_in_ext': w_in_ext.astype(BF16),
        'q_norm_g': q_norm_g[l][None], 'wq': wq.astype(BF16), 'wq_sw': wq_sw.astype(BF16),
        'kv_norm_g': kv_norm_g[l][None],
        'wuk_pad': wuk_pad.reshape(KV_RANK, N_HEADS * HEAD_PAD).astype(BF16),
        'wuv_slot': wuv_slot.reshape(KV_RANK, N_HEADS * HEAD_PAD).astype(BF16),
        'v_ones': v_ones.reshape(1, N_HEADS * HEAD_PAD),
        'wukt_pad': wukt_pad.astype(BF16),
        'wuv_pad': wuv_pad.reshape(N_HEADS, KV_RANK, N_HEADS * V_DIM).astype(BF16),
        'conv_w': conv_w_pad, 'conv_b': conv_b[l][None],
        'conv_ln_g': conv_ln_g[l][None], 'conv_ln_b': conv_ln_b[l][None],
        'w_out': w_out[l].astype(BF16), 'g_mem_q': g_mem_q[l][None], 'w_mq': w_mq[l].astype(BF16),
        'g_mem_kv': g_mem_kv[l][None], 'w_mk': w_mk[l].astype(BF16), 'w_mv': w_mv[l].astype(BF16),
        'w_mo': w_mo[l].astype(BF16), 'g_ffn': g_ffn[l][None],
        'w_router': w_router_pad.astype(BF16), 'b_router': b_router_pad[None],
        'w_gate': w_gate[l], 'w_up': w_up[l], 'w_down': w_down[l],
        'b_gate': b_gate[l][:, None, :], 'b_up': b_up[l][:, None, :], 'b_down': b_down[l][:, None, :],
        'g_final': g_final[None],
    }


def _rope_table(pos, q_scale):
    half = QK_ROPE // 2
    inv = ROPE_THETA ** (-jnp.arange(half, dtype=F32) / half)
    ang = pos.astype(F32)[:, None] * inv[None, :]
    cos, sin = jnp.cos(ang), jnp.sin(ang)
    n = pos.shape[0]
    ones = jnp.ones((n, QK_NOPE), F32)
    z = lambda k: jnp.zeros((n, k), F32)
    cq = q_scale * jnp.concatenate([cos, cos, ones, z(HEAD_PAD - QK_NOPE - QK_ROPE)], axis=1)
    sq = q_scale * jnp.concatenate([sin, sin, z(HEAD_PAD - QK_ROPE)], axis=1)
    ck = jnp.concatenate([cos, cos, z(HEAD_PAD - QK_ROPE)], axis=1)
    sk = jnp.concatenate([sin, sin, z(HEAD_PAD - QK_ROPE)], axis=1)
    return jnp.stack([cq, sq, ck, sk])


def _front(x, conv_prev, mem_k, mem_v, pos, w, paged):
    b, s, _ = x.shape
    t = b * s
    x2d = x.reshape(t, D_MODEL)
    sample = paged is not None
    if sample:
        tab = _rope_table(jnp.tile(pos, TOKEN_TILE // s), MLA_SCALE)
    else:
        tab = _rope_table(pos, MLA_SCALE * LOG2_E)
    tail = CONV_WIDTH - 1
    prev_pad = jnp.concatenate([jnp.zeros((b, CONV_HALO - tail, CONV_CH), F32), conv_prev], axis=1)
    outs = _inproj(x2d, tab, w, sample, prev_pad)
    u, ckv, kpe = outs[0], outs[1], outs[2]
    u3 = u.reshape(b, s, CONV_CH)
    if s >= tail:
        conv_tail = u3[:, s - tail:]
    else:
        conv_tail = jnp.concatenate([conv_prev[:, s:], u3], axis=1)
    ckv3 = ckv.reshape(b, s, KV_RANK)
    kpe3 = kpe.reshape(b, s, QK_ROPE)
    if sample:
        page_table, cache_ckv, cache_kpe_t = paged
        conv_out = _conv_sample(jnp.concatenate([prev_pad, u3], axis=1), w)
        attn = _attn_sample(page_table, outs[3], outs[4], ckv3, kpe3, cache_ckv, cache_kpe_t)
    else:
        conv_out = outs[6]
        q, k, v = (a.reshape(b, s, -1) for a in outs[3:6])
        attn = _attn_prompt(q, k, v).reshape(t, N_HEADS * V_DIM)
    conv2d = conv_out.reshape(t, CONV_CH)
    if not sample:
        return (x2d, conv2d, attn), conv_tail, ckv3, kpe3
    x1, qm = _post_a(x2d, conv2d, attn, w)
    om = _mem_attn_rows(qm.reshape(b, s, -1), mem_k, mem_v)
    return (x1, om.reshape(t, -1)), conv_tail, ckv3, kpe3


def kernel(x_prompt, x_sample, mem_prompt, cache_ckv, cache_kpe, page_table, cache_mem_k, cache_mem_v, state_conv, g_mix, w_in, conv_w, conv_b, conv_ln_g, conv_ln_b, q_norm_g, w_q_up, kv_norm_g, w_kv_up, w_out, g_mem_q, g_mem_kv, w_mq, w_mk, w_mv, w_mo, g_ffn, w_router, b_router, w_gate, b_gate, w_up, b_up, w_down, b_down, g_final):
    assert g_mix.shape[0] == 1, "single-layer step"
    w = _prep_weights(g_mix, w_in, conv_w, conv_b, conv_ln_g, conv_ln_b, q_norm_g, w_q_up, kv_norm_g, w_kv_up,
                      w_out, g_mem_q, g_mem_kv, w_mq, w_mk, w_mv, w_mo, g_ffn, w_router, b_router,
                      w_gate, b_gate, w_up, b_up, w_down, b_down, g_final)
    b_p, s_p, _ = x_prompt.shape
    b_s, t_s, _ = x_sample.shape
    past = page_table.shape[1] * PAGE_SIZE

    mk, mv, mk_wide, mv_wide = _mem_kv(mem_prompt.reshape(-1, D_MODEL), w)
    conv0 = jnp.zeros((b_p, CONV_WIDTH - 1, CONV_CH), F32)
    (x_p, conv_out_p, attn_p), conv_p, ckv_p, kpe_p = _front(
        x_prompt, conv0, None, None, jnp.arange(s_p, dtype=jnp.int32), w, None)
    (x1_s, om_s), conv_s, ckv_s, kpe_s = _front(
        x_sample, state_conv[0], cache_mem_k[0].reshape(-1, MEM_HD), cache_mem_v[0].reshape(-1, MEM_HD),
        past + jnp.arange(t_s, dtype=jnp.int32), w,
        (page_table, cache_ckv[0], jnp.swapaxes(cache_kpe[0], 1, 2)))

    x2, h3s, idx128, gate128, rank128, cnt128 = _post_b(
        x_p, conv_out_p, attn_p, mk_wide.reshape(b_p, MEM_TOKENS, -1), mv_wide.reshape(b_p, MEM_TOKENS, -1),
        x1_s, om_s, w)
    y_p, y_s = _moe_and_final(x2, h3s, idx128, gate128, rank128, cnt128, w, x_p.shape[0] // TOKEN_TILE)

    mem_shape = (1, b_p, MEM_TOKENS, MEM_HEADS, MEM_HD)
    return (y_p.reshape(b_p, s_p, D_MODEL), y_s.reshape(b_s, t_s, D_MODEL), ckv_p[None], kpe_p[None],
            mk.reshape(mem_shape), mv.reshape(mem_shape), conv_p[None], ckv_s[None], kpe_s[None], conv_s[None])
```
